```python
import jax, jax.numpy as jnp
from jax import lax
import numpy as np

D_MODEL = 1024
BATCH = 8
SEQ = 8192
DEPTH = 4

CHUNK = 64
N_MIXERS = 2
MEM_LEN = 256
HEAD_DIM = 64
D_TOK = D_MODEL // 2
D_MEMH = D_MODEL // 4
D_MIX = D_TOK + D_MEMH
N_SB_HEADS = D_TOK // HEAD_DIM
N_MEM_HEADS = D_MEMH // HEAD_DIM
POOL_WINDOWS = (2, 4, 8, 16)
N_POOL_GROUPS = len(POOL_WINDOWS)
POOL_GROUP = D_TOK // N_POOL_GROUPS
D_FF = 2 * D_MODEL
Q_BLOCK = 128
EPS = 1e-6
N_A = (DEPTH + N_MIXERS - 1) // N_MIXERS
N_B = DEPTH // N_MIXERS

kernel_name = "hybrid_pool_stickbreak_memory_trunk"


def rmsnorm(x, g):
    xf = x.astype(jnp.float32)
    y = xf * lax.rsqrt(jnp.mean(xf * xf, axis=-1, keepdims=True) + EPS) * g.astype(jnp.float32)
    return y.astype(x.dtype)


def swiglu(h, w_gate, w_up, w_down):
    return (jax.nn.silu(h @ w_gate) * (h @ w_up)) @ w_down


def pool_mixer(u, pool_w, pool_scale):
    B, S, _ = u.shape
    uf = u.astype(jnp.float32)
    cs = jnp.cumsum(uf, axis=1)
    pos = jnp.arange(S)
    outs = []
    for gi, w in enumerate(POOL_WINDOWS):
        sl = slice(gi * POOL_GROUP, (gi + 1) * POOL_GROUP)
        c = cs[..., sl]
        lagged = jnp.pad(c, ((0, 0), (w, 0), (0, 0)))[:, :S]
        cnt = jnp.minimum(pos + 1, w).astype(jnp.float32)[None, :, None]
        outs.append((c - lagged) / cnt - uf[..., sl])
    d = jnp.stack(outs, axis=2)
    y = jnp.einsum('bsgc,gcd->bsgd', d, pool_w.astype(jnp.float32)).reshape(B, S, D_TOK)
    return (y * pool_scale.astype(jnp.float32)).astype(u.dtype)


def stick_breaking(q, k, v):
    B, S, H, dh = q.shape
    nb = S // Q_BLOCK
    scale = dh ** -0.5
    qt = q.transpose(0, 2, 1, 3)
    kt = k.transpose(0, 2, 1, 3)
    vt = v.transpose(0, 2, 1, 3)
    idx = jnp.arange(Q_BLOCK)
    later_mat = (idx[:, None] > idx[None, :]).astype(jnp.float32)
    outs = []
    for bi in range(nb):
        nk = bi + 1
        K = nk * Q_BLOCK
        qb = qt[:, :, bi * Q_BLOCK:(bi + 1) * Q_BLOCK]
        z = jnp.einsum('bhqd,bhkd->bhqk', qb, kt[:, :, :K]).astype(jnp.float32) * scale
        qpos = bi * Q_BLOCK + idx
        mask = jnp.arange(K)[None, :] < qpos[:, None]
        ls = jax.nn.log_sigmoid(z)
        lf = jnp.where(mask, ls - z, 0.0)
        lfb = lf.reshape(B, H, Q_BLOCK, nk, Q_BLOCK)
        within = jnp.einsum('bhqnj,js->bhqns', lfb, later_mat,
                            precision=lax.Precision.HIGHEST)
        bsum = jnp.sum(lfb, axis=-1)
        later = lax.cumsum(bsum, axis=3, reverse=True) - bsum
        logw = ls.reshape(B, H, Q_BLOCK, nk, Q_BLOCK) + within + later[..., None]
        a = jnp.where(mask, jnp.exp(logw).reshape(B, H, Q_BLOCK, K), 0.0)
        outs.append(jnp.einsum('bhqk,bhkd->bhqd', a.astype(vt.dtype), vt[:, :, :K]))
    o = jnp.concatenate(outs, axis=2)
    return o.transpose(0, 2, 1, 3).reshape(B, S, H * dh)


def mem_attention(qm, mem_n, w_kv):
    B, S, _ = qm.shape
    L = mem_n.shape[1]
    kv = mem_n @ w_kv
    km = kv[..., :D_MEMH].reshape(B, L, N_MEM_HEADS, HEAD_DIM)
    vm = kv[..., D_MEMH:].reshape(B, L, N_MEM_HEADS, HEAD_DIM)
    qh = qm.reshape(B, S, N_MEM_HEADS, HEAD_DIM)
    s = jnp.einsum('bqhd,bkhd->bhqk', qh, km).astype(jnp.float32) * (HEAD_DIM ** -0.5)
    p = jax.nn.softmax(s, axis=-1)
    o = jnp.einsum('bhqk,bkhd->bqhd', p.astype(vm.dtype), vm)
    return o.reshape(B, S, D_MEMH)


def _fwd_setup_inputs(seed: int = 0) -> dict:
    key = jax.random.key(seed)
    ks = jax.random.split(key, 17)
    f32 = jnp.float32

    def w(k, shape, fan_in):
        return jax.random.normal(k, shape, f32) * (fan_in ** -0.5)

    return {
        "x": jax.random.normal(ks[0], (BATCH, SEQ, D_MODEL), f32),
        "mem": jax.random.normal(ks[1], (BATCH, MEM_LEN, D_MODEL), f32),
        "g_pre": 1.0 + 0.05 * jax.random.normal(ks[2], (DEPTH, 3, D_MODEL), f32),
        "g_post": 1.0 + 0.05 * jax.random.normal(ks[3], (DEPTH, 3, D_MODEL), f32),
        "g_mem": 1.0 + 0.05 * jax.random.normal(ks[4], (DEPTH, D_MODEL), f32),
        "ffn1_gate": w(ks[5], (DEPTH, D_MODEL, D_FF), D_MODEL),
        "ffn1_up": w(ks[6], (DEPTH, D_MODEL, D_FF), D_MODEL),
        "ffn1_down": w(ks[7], (DEPTH, D_FF, D_MODEL), D_FF),
        "ffn2_gate": w(ks[8], (DEPTH, D_MODEL, D_FF), D_MODEL),
        "ffn2_up": w(ks[9], (DEPTH, D_MODEL, D_FF), D_MODEL),
        "ffn2_down": w(ks[10], (DEPTH, D_FF, D_MODEL), D_FF),
        "w_in_pool": w(ks[11], (N_A, D_MODEL, D_TOK + D_MEMH), D_MODEL),
        "pool_w": w(ks[12], (N_A, N_POOL_GROUPS, POOL_GROUP, POOL_GROUP), POOL_GROUP),
        "pool_scale": 1.0 + 0.1 * jax.random.normal(ks[13], (N_A, D_TOK), f32),
        "w_in_sb": w(ks[14], (N_B, D_MODEL, 3 * D_TOK + D_MEMH), D_MODEL),
        "w_mem_kv": w(ks[15], (DEPTH, D_MODEL, 2 * D_MEMH), D_MODEL),
        "w_out": w(ks[16], (DEPTH, D_MIX, D_MODEL), D_MIX),
    }


def _fwd_reference(x, mem, g_pre, g_post, g_mem, ffn1_gate, ffn1_up, ffn1_down,
              ffn2_gate, ffn2_up, ffn2_down, w_in_pool, pool_w, pool_scale,
              w_in_sb, w_mem_kv, w_out):
    B, S, _ = x.shape
    h = x
    for i in range(DEPTH):
        f = swiglu(rmsnorm(h, g_pre[i, 0]), ffn1_gate[i], ffn1_up[i], ffn1_down[i])
        h = h + 0.5 * rmsnorm(f, g_post[i, 0])

        u = rmsnorm(h, g_pre[i, 1])
        mem_n = rmsnorm(mem, g_mem[i])
        j = i // N_MIXERS
        if i % N_MIXERS == 0:
            proj = u @ w_in_pool[j]
            tok = pool_mixer(proj[..., :D_TOK], pool_w[j], pool_scale[j])
            qm = proj[..., D_TOK:]
        else:
            proj = u @ w_in_sb[j]
            q = proj[..., :D_TOK].reshape(B, S, N_SB_HEADS, HEAD_DIM)
            k = proj[..., D_TOK:2 * D_TOK].reshape(B, S, N_SB_HEADS, HEAD_DIM)
            v = proj[..., 2 * D_TOK:3 * D_TOK].reshape(B, S, N_SB_HEADS, HEAD_DIM)
            tok = stick_breaking(q, k, v)
            qm = proj[..., 3 * D_TOK:]
        mo = mem_attention(qm, mem_n, w_mem_kv[i])
        mix = jnp.concatenate([tok, mo], axis=-1) @ w_out[i]
        h = h + rmsnorm(mix, g_post[i, 1])

        f = swiglu(rmsnorm(h, g_pre[i, 2]), ffn2_gate[i], ffn2_up[i], ffn2_down[i])
        h = h + 0.5 * rmsnorm(f, g_post[i, 2])
    return h


import jax as _jax
import jax.numpy as _jnp

TWIN_FORMAT = 'train_step'
FWD_PARAMS = ['x', 'mem', 'g_pre', 'g_post', 'g_mem', 'ffn1_gate', 'ffn1_up', 'ffn1_down', 'ffn2_gate', 'ffn2_up', 'ffn2_down', 'w_in_pool', 'pool_w', 'pool_scale', 'w_in_sb', 'w_mem_kv', 'w_out']
TWIN_WEIGHTS = ['g_pre', 'g_post', 'g_mem', 'ffn1_gate', 'ffn1_up', 'ffn1_down', 'ffn2_gate', 'ffn2_up', 'ffn2_down', 'w_in_pool', 'pool_w', 'pool_scale', 'w_in_sb', 'w_mem_kv', 'w_out']
TWIN_DIFF_INPUT = 'x'
TWIN_INPUTS = ['x', 'mem', 'g_pre', 'g_post', 'g_mem', 'ffn1_gate', 'ffn1_up', 'ffn1_down', 'ffn2_gate', 'ffn2_up', 'ffn2_down', 'w_in_pool', 'pool_w', 'pool_scale', 'w_in_sb', 'w_mem_kv', 'w_out', 'loss_target', 'm_g_pre', 'm_g_post', 'm_g_mem', 'm_ffn1_gate', 'm_ffn1_up', 'm_ffn1_down', 'm_ffn2_gate', 'm_ffn2_up', 'm_ffn2_down', 'm_w_in_pool', 'm_pool_w', 'm_pool_scale', 'm_w_in_sb', 'm_w_mem_kv', 'm_w_out', 'v_g_pre', 'v_g_post', 'v_g_mem', 'v_ffn1_gate', 'v_ffn1_up', 'v_ffn1_down', 'v_ffn2_gate', 'v_ffn2_up', 'v_ffn2_down', 'v_w_in_pool', 'v_pool_w', 'v_pool_scale', 'v_w_in_sb', 'v_w_mem_kv', 'v_w_out']
TWIN_OUTPUTS = ['loss', 'grad_x', 'grad_g_pre', 'grad_g_post', 'grad_g_mem', 'grad_ffn1_gate', 'grad_ffn1_up', 'grad_ffn1_down', 'grad_ffn2_gate', 'grad_ffn2_up', 'grad_ffn2_down', 'grad_w_in_pool', 'grad_pool_w', 'grad_pool_scale', 'grad_w_in_sb', 'grad_w_mem_kv', 'grad_w_out', 'delta_g_pre', 'delta_g_post', 'delta_g_mem', 'delta_ffn1_gate', 'delta_ffn1_up', 'delta_ffn1_down', 'delta_ffn2_gate', 'delta_ffn2_up', 'delta_ffn2_down', 'delta_w_in_pool', 'delta_pool_w', 'delta_pool_scale', 'delta_w_in_sb', 'delta_w_mem_kv', 'delta_w_out', 'new_m_g_pre', 'new_m_g_post', 'new_m_g_mem', 'new_m_ffn1_gate', 'new_m_ffn1_up', 'new_m_ffn1_down', 'new_m_ffn2_gate', 'new_m_ffn2_up', 'new_m_ffn2_down', 'new_m_w_in_pool', 'new_m_pool_w', 'new_m_pool_scale', 'new_m_w_in_sb', 'new_m_w_mem_kv', 'new_m_w_out', 'new_v_g_pre', 'new_v_g_post', 'new_v_g_mem', 'new_v_ffn1_gate', 'new_v_ffn1_up', 'new_v_ffn1_down', 'new_v_ffn2_gate', 'new_v_ffn2_up', 'new_v_ffn2_down', 'new_v_w_in_pool', 'new_v_pool_w', 'new_v_pool_scale', 'new_v_w_in_sb', 'new_v_w_mem_kv', 'new_v_w_out']
TWIN_LEAF_KINDS = {'loss': 'loss', 'grad_x': 'grad_x', 'grad_g_pre': 'grad_w', 'grad_g_post': 'grad_w', 'grad_g_mem': 'grad_w', 'grad_ffn1_gate': 'grad_w', 'grad_ffn1_up': 'grad_w', 'grad_ffn1_down': 'grad_w', 'grad_ffn2_gate': 'grad_w', 'grad_ffn2_up': 'grad_w', 'grad_ffn2_down': 'grad_w', 'grad_w_in_pool': 'grad_w', 'grad_pool_w': 'grad_w', 'grad_pool_scale': 'grad_w', 'grad_w_in_sb': 'grad_w', 'grad_w_mem_kv': 'grad_w', 'grad_w_out': 'grad_w', 'delta_g_pre': 'delta_w', 'delta_g_post': 'delta_w', 'delta_g_mem': 'delta_w', 'delta_ffn1_gate': 'delta_w', 'delta_ffn1_up': 'delta_w', 'delta_ffn1_down': 'delta_w', 'delta_ffn2_gate': 'delta_w', 'delta_ffn2_up': 'delta_w', 'delta_ffn2_down': 'delta_w', 'delta_w_in_pool': 'delta_w', 'delta_pool_w': 'delta_w', 'delta_pool_scale': 'delta_w', 'delta_w_in_sb': 'delta_w', 'delta_w_mem_kv': 'delta_w', 'delta_w_out': 'delta_w', 'new_m_g_pre': 'new_m', 'new_m_g_post': 'new_m', 'new_m_g_mem': 'new_m', 'new_m_ffn1_gate': 'new_m', 'new_m_ffn1_up': 'new_m', 'new_m_ffn1_down': 'new_m', 'new_m_ffn2_gate': 'new_m', 'new_m_ffn2_up': 'new_m', 'new_m_ffn2_down': 'new_m', 'new_m_w_in_pool': 'new_m', 'new_m_pool_w': 'new_m', 'new_m_pool_scale': 'new_m', 'new_m_w_in_sb': 'new_m', 'new_m_w_mem_kv': 'new_m', 'new_m_w_out': 'new_m', 'new_v_g_pre': 'new_v', 'new_v_g_post': 'new_v', 'new_v_g_mem': 'new_v', 'new_v_ffn1_gate': 'new_v', 'new_v_ffn1_up': 'new_v', 'new_v_ffn1_down': 'new_v', 'new_v_ffn2_gate': 'new_v', 'new_v_ffn2_up': 'new_v', 'new_v_ffn2_down': 'new_v', 'new_v_w_in_pool': 'new_v', 'new_v_pool_w': 'new_v', 'new_v_pool_scale': 'new_v', 'new_v_w_in_sb': 'new_v', 'new_v_w_mem_kv': 'new_v', 'new_v_w_out': 'new_v'}


def _forward(args):
    return _fwd_reference(*[args[k] for k in FWD_PARAMS])


def _output_shape():
    out = _jax.eval_shape(lambda: _forward(_fwd_setup_inputs(0)))
    return out.shape, out.dtype

N_MICROBATCH = 1
ADAM_LR = 0.001
ADAM_B1 = 0.9
ADAM_B2 = 0.999
ADAM_EPS = 1e-08
ADAM_WD = 0.01
ADAM_STEP = 10
PER_EXAMPLE_BATCH_AXIS = {'x': 0, 'mem': 0, 'loss_target': 0}
SHARED_INPUTS = []
_WEIGHT_DTYPES = {'g_pre': _jnp.float32, 'g_post': _jnp.float32, 'g_mem': _jnp.float32, 'ffn1_gate': _jnp.float32, 'ffn1_up': _jnp.float32, 'ffn1_down': _jnp.float32, 'ffn2_gate': _jnp.float32, 'ffn2_up': _jnp.float32, 'ffn2_down': _jnp.float32, 'w_in_pool': _jnp.float32, 'pool_w': _jnp.float32, 'pool_scale': _jnp.float32, 'w_in_sb': _jnp.float32, 'w_mem_kv': _jnp.float32, 'w_out': _jnp.float32}
MOMENT_SCALE = {'g_pre': 1.457206e+00, 'g_post': 3.880857e+01, 'g_mem': 2.339870e-01, 'ffn1_gate': 5.332818e-01, 'ffn1_up': 5.999425e-01, 'ffn1_down': 8.638460e-01, 'ffn2_gate': 3.964236e-01, 'ffn2_up': 5.199572e-01, 'ffn2_down': 7.435881e-01, 'w_in_pool': 2.809686e+00, 'pool_w': 3.749439e+00, 'pool_scale': 4.300468e+00, 'w_in_sb': 1.160600e+00, 'w_mem_kv': 3.117768e-01, 'w_out': 2.324270e+00}


def _to_microbatches(a, axis):
    t = _jnp.moveaxis(a, axis, 0)
    t = t.reshape((N_MICROBATCH, t.shape[0] // N_MICROBATCH) + t.shape[1:])
    return _jnp.moveaxis(t, 1, axis + 1)


def setup_inputs(seed: int = 0) -> dict:
    inp = _fwd_setup_inputs(seed)
    key = _jax.random.fold_in(_jax.random.key(seed), 7919)
    shape, _ = _output_shape()
    out = dict(inp)
    out["loss_target"] = _jax.random.normal(_jax.random.fold_in(key, 0), shape, _jnp.float32)
    for i, name in enumerate(TWIN_WEIGHTS):
        w = inp[name].astype(_jnp.float32)
        if MOMENT_SCALE is None:
            s = _jnp.sqrt(_jnp.mean(_jnp.square(w)) + 1e-30)
        else:
            s = MOMENT_SCALE[name]
        km, kv = _jax.random.split(_jax.random.fold_in(key, i + 1))
        out[name] = w
        out["m_" + name] = s * _jax.random.normal(km, w.shape, _jnp.float32)
        out["v_" + name] = (s * s) * _jax.random.uniform(kv, w.shape, _jnp.float32, 0.5, 1.5)
    if N_MICROBATCH > 1:
        for name, axis in PER_EXAMPLE_BATCH_AXIS.items():
            out[name] = _to_microbatches(out[name], axis)
    return {'x': out['x'], 'mem': out['mem'], 'g_pre': out['g_pre'], 'g_post': out['g_post'], 'g_mem': out['g_mem'], 'ffn1_gate': out['ffn1_gate'], 'ffn1_up': out['ffn1_up'], 'ffn1_down': out['ffn1_down'], 'ffn2_gate': out['ffn2_gate'], 'ffn2_up': out['ffn2_up'], 'ffn2_down': out['ffn2_down'], 'w_in_pool': out['w_in_pool'], 'pool_w': out['pool_w'], 'pool_scale': out['pool_scale'], 'w_in_sb': out['w_in_sb'], 'w_mem_kv': out['w_mem_kv'], 'w_out': out['w_out'], 'loss_target': out['loss_target'], 'm_g_pre': out['m_g_pre'], 'm_g_post': out['m_g_post'], 'm_g_mem': out['m_g_mem'], 'm_ffn1_gate': out['m_ffn1_gate'], 'm_ffn1_up': out['m_ffn1_up'], 'm_ffn1_down': out['m_ffn1_down'], 'm_ffn2_gate': out['m_ffn2_gate'], 'm_ffn2_up': out['m_ffn2_up'], 'm_ffn2_down': out['m_ffn2_down'], 'm_w_in_pool': out['m_w_in_pool'], 'm_pool_w': out['m_pool_w'], 'm_pool_scale': out['m_pool_scale'], 'm_w_in_sb': out['m_w_in_sb'], 'm_w_mem_kv': out['m_w_mem_kv'], 'm_w_out': out['m_w_out'], 'v_g_pre': out['v_g_pre'], 'v_g_post': out['v_g_post'], 'v_g_mem': out['v_g_mem'], 'v_ffn1_gate': out['v_ffn1_gate'], 'v_ffn1_up': out['v_ffn1_up'], 'v_ffn1_down': out['v_ffn1_down'], 'v_ffn2_gate': out['v_ffn2_gate'], 'v_ffn2_up': out['v_ffn2_up'], 'v_ffn2_down': out['v_ffn2_down'], 'v_w_in_pool': out['v_w_in_pool'], 'v_pool_w': out['v_pool_w'], 'v_pool_scale': out['v_pool_scale'], 'v_w_in_sb': out['v_w_in_sb'], 'v_w_mem_kv': out['v_w_mem_kv'], 'v_w_out': out['v_w_out']}


def _loss(weights, diff, rest, loss_target):
    with _jax.named_scope("forward"):
        args = {**rest, TWIN_DIFF_INPUT: diff, **{k: w.astype(_WEIGHT_DTYPES[k]) for k, w in weights.items()}}
        y = _forward(args)
    with _jax.named_scope("loss_head"):
        err = _jnp.square(y.astype(_jnp.float32) - loss_target)
        return 0.5 * _jnp.sum(_jnp.mean(err, axis=-1)) if err.ndim else 0.5 * err


def _adamw(w, g, m, v):
    m = ADAM_B1 * m + (1.0 - ADAM_B1) * g
    v = ADAM_B2 * v + (1.0 - ADAM_B2) * _jnp.square(g)
    m_hat = m / (1.0 - ADAM_B1 ** ADAM_STEP)
    v_hat = v / (1.0 - ADAM_B2 ** ADAM_STEP)
    delta = -ADAM_LR * (m_hat / (_jnp.sqrt(v_hat) + ADAM_EPS) + ADAM_WD * w)
    return delta, m, v


def reference(x, mem, g_pre, g_post, g_mem, ffn1_gate, ffn1_up, ffn1_down, ffn2_gate, ffn2_up, ffn2_down, w_in_pool, pool_w, pool_scale, w_in_sb, w_mem_kv, w_out, loss_target, m_g_pre, m_g_post, m_g_mem, m_ffn1_gate, m_ffn1_up, m_ffn1_down, m_ffn2_gate, m_ffn2_up, m_ffn2_down, m_w_in_pool, m_pool_w, m_pool_scale, m_w_in_sb, m_w_mem_kv, m_w_out, v_g_pre, v_g_post, v_g_mem, v_ffn1_gate, v_ffn1_up, v_ffn1_down, v_ffn2_gate, v_ffn2_up, v_ffn2_down, v_w_in_pool, v_pool_w, v_pool_scale, v_w_in_sb, v_w_mem_kv, v_w_out):
    given = dict(x=x, mem=mem, g_pre=g_pre, g_post=g_post, g_mem=g_mem, ffn1_gate=ffn1_gate, ffn1_up=ffn1_up, ffn1_down=ffn1_down, ffn2_gate=ffn2_gate, ffn2_up=ffn2_up, ffn2_down=ffn2_down, w_in_pool=w_in_pool, pool_w=pool_w, pool_scale=pool_scale, w_in_sb=w_in_sb, w_mem_kv=w_mem_kv, w_out=w_out, loss_target=loss_target, m_g_pre=m_g_pre, m_g_post=m_g_post, m_g_mem=m_g_mem, m_ffn1_gate=m_ffn1_gate, m_ffn1_up=m_ffn1_up, m_ffn1_down=m_ffn1_down, m_ffn2_gate=m_ffn2_gate, m_ffn2_up=m_ffn2_up, m_ffn2_down=m_ffn2_down, m_w_in_pool=m_w_in_pool, m_pool_w=m_pool_w, m_pool_scale=m_pool_scale, m_w_in_sb=m_w_in_sb, m_w_mem_kv=m_w_mem_kv, m_w_out=m_w_out, v_g_pre=v_g_pre, v_g_post=v_g_post, v_g_mem=v_g_mem, v_ffn1_gate=v_ffn1_gate, v_ffn1_up=v_ffn1_up, v_ffn1_down=v_ffn1_down, v_ffn2_gate=v_ffn2_gate, v_ffn2_up=v_ffn2_up, v_ffn2_down=v_ffn2_down, v_w_in_pool=v_w_in_pool, v_pool_w=v_pool_w, v_pool_scale=v_pool_scale, v_w_in_sb=v_w_in_sb, v_w_mem_kv=v_w_mem_kv, v_w_out=v_w_out)
    weights = {n: given[n] for n in TWIN_WEIGHTS}
    shared = {n: given[n] for n in SHARED_INPUTS}
    per_example = {n: given[n] for n in ['x', 'mem']}
    grad_fn = _jax.value_and_grad(_loss, argnums=(0, 1))

    def one_microbatch(ex, loss_target):
        ex = dict(ex)
        diff = ex.pop(TWIN_DIFF_INPUT)
        return grad_fn(weights, diff, {**shared, **ex}, loss_target)

    if N_MICROBATCH == 1:
        loss, (grad_w, grad_x) = one_microbatch(per_example, given["loss_target"])
    else:
        def body(carry, xs):
            loss_sum, grad_sum = carry
            l_k, (gw_k, gx_k) = one_microbatch(xs[0], xs[1])
            with _jax.named_scope("update"):
                return (loss_sum + l_k, _jax.tree.map(_jnp.add, grad_sum, gw_k)), gx_k

        init = (_jnp.zeros((), _jnp.float32), _jax.tree.map(_jnp.zeros_like, weights))
        (loss, grad_w), grad_x = _jax.lax.scan(body, init, (per_example, given["loss_target"]))
    with _jax.named_scope("update"):
        delta_w, new_m, new_v = {}, {}, {}
        for n in TWIN_WEIGHTS:
            delta_w[n], new_m[n], new_v[n] = _adamw(weights[n], grad_w[n], given["m_" + n], given["v_" + n])
    return (loss, grad_x, *[grad_w[n] for n in TWIN_WEIGHTS], *[delta_w[n] for n in TWIN_WEIGHTS],
            *[new_m[n] for n in TWIN_WEIGHTS], *[new_v[n] for n in TWIN_WEIGHTS])
```

```python
import jax
import jax.numpy as jnp
from jax import lax
from jax.experimental import pallas as pl
from jax.experimental.pallas import tpu as pltpu

F32 = jnp.float32
BF16 = jnp.bfloat16
_MM = jnp.bfloat16

D = 1024
FF = 2048
DTOK = 512
DMEM = 256
DMIX = DTOK + DMEM
DSB = 3 * DTOK + DMEM
HD = 64
MEM_LEN = 256
DEPTH = 4
EPS = 1e-6
WINDOWS = (2, 4, 8, 16)
HALO = 16
PG = 128
QB = 128
SCALE = HD ** -0.5
FC = 512
NEG_CUT = -104.0

ADAM_LR, ADAM_B1, ADAM_B2, ADAM_EPS, ADAM_WD, ADAM_STEP = 0.001, 0.9, 0.999, 1e-08, 0.01, 10

VMEM_LIMIT = 56 * 2 ** 20
MESH = pl.DeviceIdType.MESH
N_CHIPS = 4
N_DEV = 8


def _cparams(sem):
    return pltpu.CompilerParams(dimension_semantics=sem, vmem_limit_bytes=VMEM_LIMIT)


def _mm(a, b):
    return jnp.dot(a.astype(_MM), b.astype(_MM), preferred_element_type=F32)


def _mm_nt(a, b):
    return lax.dot_general(a.astype(_MM), b.astype(_MM), (((1,), (1,)), ((), ())), preferred_element_type=F32)


def _mm_tn(a, b):
    return lax.dot_general(a.astype(_MM), b.astype(_MM), (((0,), (0,)), ((), ())), preferred_element_type=F32)


def _mm2(a, b):
    hi = a.astype(_MM)
    lo = (a - hi.astype(F32)).astype(_MM)
    return jnp.dot(hi, b, preferred_element_type=F32) + jnp.dot(lo, b, preferred_element_type=F32)


def _rstd(x):
    return lax.rsqrt(jnp.mean(x * x, axis=-1, keepdims=True) + EPS)


def _rms_bwd(x, g, dy):
    r = _rstd(x)
    xh = x * r
    t = dy * g
    dx = r * (t - xh * jnp.mean(t * xh, axis=-1, keepdims=True))
    return dx, jnp.sum(dy * xh, axis=0, keepdims=True)


def _sigmoid(a):
    return 1.0 / (1.0 + jnp.exp(-a))


def _row_tile(n, want):
    t = min(n, want)
    assert n % t == 0, (n, t)
    return t


def _ffn_fwd(h, g1, g2, wg, wu, wd, li):
    S = h.shape[0]
    ts = _row_tile(S, 256)

    def body(h_ref, g1_ref, g2_ref, wg_ref, wu_ref, wd_ref, ho_ref, a_ref, b_ref, f_ref):
        hv = h_ref[...]
        n = (hv * _rstd(hv) * g1_ref[...]).astype(_MM)
        f = jnp.zeros((ts, D), F32)
        for j in range(FF // FC):
            cs = slice(j * FC, (j + 1) * FC)
            a = _mm(n, wg_ref[:, cs])
            b = _mm(n, wu_ref[:, cs])
            a_ref[:, cs] = a.astype(a_ref.dtype)
            b_ref[:, cs] = b.astype(b_ref.dtype)
            f = f + _mm(a * _sigmoid(a) * b, wd_ref[cs, :])
        f_ref[...] = f
        ho_ref[...] = hv + 0.5 * (f * _rstd(f) * g2_ref[...])

    row = lambda i: (i, 0)
    fix = lambda i: (0, 0)
    lay = lambda i: (li, 0, 0)
    return pl.pallas_call(
        body, name="ffn_fwd", grid=(S // ts,),
        in_specs=[pl.BlockSpec((ts, D), row), pl.BlockSpec((1, D), fix), pl.BlockSpec((1, D), fix),
                  pl.BlockSpec((None, D, FF), lay), pl.BlockSpec((None, D, FF), lay), pl.BlockSpec((None, FF, D), lay)],
        out_specs=[pl.BlockSpec((ts, D), row), pl.BlockSpec((ts, FF), row), pl.BlockSpec((ts, FF), row),
                   pl.BlockSpec((ts, D), row)],
        out_shape=[jax.ShapeDtypeStruct((S, D), F32), jax.ShapeDtypeStruct((S, FF), _MM),
                   jax.ShapeDtypeStruct((S, FF), _MM), jax.ShapeDtypeStruct((S, D), F32)],
        compiler_params=_cparams(("parallel",)),
    )(h, g1, g2, wg, wu, wd)


def _ffn_bwd(h, f, a, b, dho, g1, g2, wg, wu, wd, li):
    S = h.shape[0]
    ts = _row_tile(S, 256)

    def body(h_ref, f_ref, a_ref, b_ref, dho_ref, g1_ref, g2_ref, wg_ref, wu_ref, wd_ref,
             dh_ref, n_ref, s_ref, da_ref, db_ref, df_ref, dg1_ref, dg2_ref):
        @pl.when(pl.program_id(0) == 0)
        def _():
            dg1_ref[...] = jnp.zeros_like(dg1_ref)
            dg2_ref[...] = jnp.zeros_like(dg2_ref)

        hv = h_ref[...]
        dho = dho_ref[...]
        df, dg2 = _rms_bwd(f_ref[...], g2_ref[...], 0.5 * dho)
        dg2_ref[...] += dg2
        dfb = df.astype(_MM)
        df_ref[...] = dfb
        rh = _rstd(hv)
        hh = hv * rh
        g1 = g1_ref[...]
        n_ref[...] = (hh * g1).astype(_MM)
        dn = jnp.zeros((ts, D), F32)
        for j in range(FF // FC):
            cs = slice(j * FC, (j + 1) * FC)
            av = a_ref[:, cs].astype(F32)
            bv = b_ref[:, cs].astype(F32)
            sig = _sigmoid(av)
            sl = av * sig
            s_ref[:, cs] = (sl * bv).astype(_MM)
            ds = _mm_nt(dfb, wd_ref[cs, :])
            da = (ds * bv * (sig * (1.0 + av * (1.0 - sig)))).astype(_MM)
            db = (ds * sl).astype(_MM)
            da_ref[:, cs] = da
            db_ref[:, cs] = db
            dn = dn + _mm_nt(da, wg_ref[:, cs]) + _mm_nt(db, wu_ref[:, cs])
        t = dn * g1
        dh_ref[...] = dho + rh * (t - hh * jnp.mean(t * hh, axis=-1, keepdims=True))
        dg1_ref[...] += jnp.sum(dn * hh, axis=0, keepdims=True)

    row = lambda i: (i, 0)
    fix = lambda i: (0, 0)
    lay = lambda i: (li, 0, 0)
    td = pl.BlockSpec((ts, D), row)
    tf = pl.BlockSpec((ts, FF), row)
    gd = pl.BlockSpec((1, D), fix)
    return pl.pallas_call(
        body, name="ffn_bwd", grid=(S // ts,),
        in_specs=[td, td, tf, tf, td, gd, gd,
                  pl.BlockSpec((None, D, FF), lay), pl.BlockSpec((None, D, FF), lay), pl.BlockSpec((None, FF, D), lay)],
        out_specs=[td, td, tf, tf, tf, td, gd, gd],
        out_shape=[jax.ShapeDtypeStruct((S, D), F32), jax.ShapeDtypeStruct((S, D), _MM),
                   jax.ShapeDtypeStruct((S, FF), _MM), jax.ShapeDtypeStruct((S, FF), _MM),
                   jax.ShapeDtypeStruct((S, FF), _MM), jax.ShapeDtypeStruct((S, D), _MM),
                   jax.ShapeDtypeStruct((1, D), F32), jax.ShapeDtypeStruct((1, D), F32)],
        compiler_params=_cparams(("arbitrary",)),
    )(h, f, a, b, dho, g1, g2, wg, wu, wd)


def _wgrad(x, y):
    T, M = x.shape
    N = y.shape[1]
    tt = _row_tile(T, 512)
    bn = 512 if N % 512 == 0 else 256
    assert N % bn == 0

    def body(x_ref, y_ref, o_ref):
        @pl.when(pl.program_id(1) == 0)
        def _():
            o_ref[...] = jnp.zeros_like(o_ref)

        o_ref[...] += _mm_tn(x_ref[...], y_ref[...])

    return pl.pallas_call(
        body, name="wgrad", grid=(N // bn, T // tt),
        in_specs=[pl.BlockSpec((tt, M), lambda j, t: (t, 0)), pl.BlockSpec((tt, bn), lambda j, t: (t, j))],
        out_specs=pl.BlockSpec((M, bn), lambda j, t: (0, j)),
        out_shape=jax.ShapeDtypeStruct((M, N), F32),
        compiler_params=_cparams(("parallel", "arbitrary")),
    )(x, y)


def _proj_fwd(h, g, w, li, out_dtype):
    S = h.shape[0]
    N = w.shape[2]
    ts = _row_tile(S, 256)

    def body(h_ref, g_ref, w_ref, u_ref, p_ref):
        hv = h_ref[...]
        u = (hv * _rstd(hv) * g_ref[...]).astype(_MM)
        u_ref[...] = u
        p_ref[...] = _mm(u, w_ref[...]).astype(p_ref.dtype)

    return pl.pallas_call(
        body, name="proj_fwd", grid=(S // ts,),
        in_specs=[pl.BlockSpec((ts, D), lambda i: (i, 0)), pl.BlockSpec((1, D), lambda i: (0, 0)),
                  pl.BlockSpec((None, D, N), lambda i: (li, 0, 0))],
        out_specs=[pl.BlockSpec((ts, D), lambda i: (i, 0)), pl.BlockSpec((ts, N), lambda i: (i, 0))],
        out_shape=[jax.ShapeDtypeStruct((S, D), _MM), jax.ShapeDtypeStruct((S, N), out_dtype)],
        compiler_params=_cparams(("parallel",)),
    )(h, g, w)


def _proj_bwd(h, dho, g, w, li, pieces):
    S = h.shape[0]
    N = w.shape[2]
    ts = _row_tile(S, 256)
    widths = [p.shape[1] for p in pieces]
    assert sum(widths) == N
    npc = len(pieces)

    def body(*refs):
        h_ref, dho_ref, g_ref, w_ref = refs[:4]
        p_refs = refs[4:4 + npc]
        dh_ref, dp_ref, dg_ref = refs[4 + npc:]

        @pl.when(pl.program_id(0) == 0)
        def _():
            dg_ref[...] = jnp.zeros_like(dg_ref)

        du = jnp.zeros((ts, D), F32)
        off = 0
        for p_ref, n in zip(p_refs, widths):
            pv = p_ref[...].astype(_MM)
            dp_ref[:, off:off + n] = pv
            du = du + _mm_nt(pv, w_ref[:, off:off + n])
            off += n
        dx, dg = _rms_bwd(h_ref[...], g_ref[...], du)
        dh_ref[...] = dho_ref[...] + dx
        dg_ref[...] += dg

    row = lambda i: (i, 0)
    return pl.pallas_call(
        body, name="proj_bwd", grid=(S // ts,),
        in_specs=[pl.BlockSpec((ts, D), row), pl.BlockSpec((ts, D), row), pl.BlockSpec((1, D), lambda i: (0, 0)),
                  pl.BlockSpec((None, D, N), lambda i: (li, 0, 0))] + [pl.BlockSpec((ts, n), row) for n in widths],
        out_specs=[pl.BlockSpec((ts, D), row), pl.BlockSpec((ts, N), row), pl.BlockSpec((1, D), lambda i: (0, 0))],
        out_shape=[jax.ShapeDtypeStruct((S, D), F32), jax.ShapeDtypeStruct((S, N), _MM),
                   jax.ShapeDtypeStruct((1, D), F32)],
        compiler_params=_cparams(("arbitrary",)),
    )(h, dho, g, w, *pieces)


def _pool_counts(first_row, ts):
    pos = (first_row + lax.broadcasted_iota(jnp.int32, (ts, 1), 0) + 1).astype(F32)
    return [jnp.minimum(pos, float(w)) for w in WINDOWS]


def _pool_delta(x, prev, cnts, ts):
    xe = jnp.concatenate([prev, x], axis=0)
    sums = []
    cur = xe
    for sh in (1, 2, 4, 8):
        cur = cur + pltpu.roll(cur, sh, 0)
        sums.append(cur)
    return [sums[gi][HALO:, gi * PG:(gi + 1) * PG] / cnts[gi] - x[:, gi * PG:(gi + 1) * PG]
            for gi in range(len(WINDOWS))]


def _pool_fwd(proj, pw, ps):
    S = proj.shape[0]
    ts = _row_tile(S, 256)

    def body(x_ref, pw_ref, ps_ref, tok_ref, carry_ref):
        i = pl.program_id(0)

        @pl.when(i == 0)
        def _():
            carry_ref[...] = jnp.zeros_like(carry_ref)

        x = x_ref[...]
        ds = _pool_delta(x, carry_ref[...], _pool_counts(i * ts, ts), ts)
        y = jnp.concatenate([_mm(ds[gi], pw_ref[gi]) for gi in range(len(WINDOWS))], axis=-1)
        tok_ref[...] = (y * ps_ref[...]).astype(tok_ref.dtype)
        carry_ref[...] = x[ts - HALO:, :]

    return pl.pallas_call(
        body, name="pool_fwd", grid=(S // ts,),
        in_specs=[pl.BlockSpec((ts, DTOK), lambda i: (i, 0)), pl.BlockSpec((len(WINDOWS), PG, PG), lambda i: (0, 0, 0)),
                  pl.BlockSpec((1, DTOK), lambda i: (0, 0))],
        out_specs=pl.BlockSpec((ts, DTOK), lambda i: (i, 0)),
        out_shape=jax.ShapeDtypeStruct((S, DTOK), _MM),
        scratch_shapes=[pltpu.VMEM((HALO, DTOK), F32)],
        compiler_params=_cparams(("arbitrary",)),
    )(proj, pw, ps)


def _pool_bwd(proj, dtok, pw, ps):
    S = proj.shape[0]
    ts = _row_tile(S, 256)
    nt = S // ts
    per = ts // HALO
    ng = len(WINDOWS)

    def body(x_ref, xp_ref, dt_ref, pw_ref, ps_ref, dx_ref, dpw_ref, dps_ref, carry_ref):
        i = pl.program_id(0)
        idx = nt - 1 - i

        @pl.when(i == 0)
        def _():
            carry_ref[...] = jnp.zeros_like(carry_ref)
            dpw_ref[...] = jnp.zeros_like(dpw_ref)
            dps_ref[...] = jnp.zeros_like(dps_ref)

        x = x_ref[...]
        prev = jnp.where(idx > 0, xp_ref[...], 0.0)
        cnts = _pool_counts(idx * ts, ts)
        ds = _pool_delta(x, prev, cnts, ts)
        dt = dt_ref[...]
        y = jnp.concatenate([_mm(ds[gi], pw_ref[gi]) for gi in range(ng)], axis=-1)
        dps_ref[...] += jnp.sum(dt * y, axis=0, keepdims=True)
        dy = (dt * ps_ref[...]).astype(_MM)
        dds = []
        for gi in range(ng):
            dyg = dy[:, gi * PG:(gi + 1) * PG]
            dpw_ref[gi] += _mm_tn(ds[gi], dyg)
            dds.append(_mm_nt(dyg, pw_ref[gi]))
        e = jnp.concatenate([dds[gi] / cnts[gi] for gi in range(ng)], axis=-1)
        ee = jnp.concatenate([e, carry_ref[...]], axis=0)
        rows = ts + HALO
        cur = ee
        outs = []
        for gi, sh in enumerate((1, 2, 4, 8)):
            cur = cur + pltpu.roll(cur, rows - sh, 0)
            outs.append(cur[:ts, gi * PG:(gi + 1) * PG] - dds[gi])
        dx_ref[...] = jnp.concatenate(outs, axis=-1)
        carry_ref[...] = e[:HALO, :]

    rev = lambda i: (nt - 1 - i, 0)
    return pl.pallas_call(
        body, name="pool_bwd", grid=(nt,),
        in_specs=[pl.BlockSpec((ts, DTOK), rev),
                  pl.BlockSpec((HALO, DTOK), lambda i: (jnp.maximum((nt - 1 - i) * per - 1, 0), 0)),
                  pl.BlockSpec((ts, DTOK), rev), pl.BlockSpec((ng, PG, PG), lambda i: (0, 0, 0)),
                  pl.BlockSpec((1, DTOK), lambda i: (0, 0))],
        out_specs=[pl.BlockSpec((ts, DTOK), rev), pl.BlockSpec((ng, PG, PG), lambda i: (0, 0, 0)),
                   pl.BlockSpec((1, DTOK), lambda i: (0, 0))],
        out_shape=[jax.ShapeDtypeStruct((S, DTOK), F32), jax.ShapeDtypeStruct((ng, PG, PG), F32),
                   jax.ShapeDtypeStruct((1, DTOK), F32)],
        scratch_shapes=[pltpu.VMEM((HALO, DTOK), F32)],
        compiler_params=_cparams(("arbitrary",)),
    )(proj, proj, dtok, pw, ps)


def _tri(strict):
    r = lax.broadcasted_iota(jnp.int32, (QB, QB), 0)
    c = lax.broadcasted_iota(jnp.int32, (QB, QB), 1)
    return jnp.where(r > c if strict else r >= c, 1.0, 0.0).astype(_MM)


def _sb_logits(qh, k2, valid):
    z = _mm_nt(qh, k2) * SCALE
    e = jnp.exp(-jnp.abs(z))
    sp = jnp.log(1.0 + e)
    ls = jnp.minimum(z, 0.0) - sp
    lf = jnp.where(valid, jnp.minimum(-z, 0.0) - sp, 0.0)
    return ls, lf, e, z


def _sb_fwd(proj):
    S = proj.shape[0]
    nq = S // QB
    npair = DTOK // 128

    def body(q_ref, k_ref, v_ref, tri_ref, o_ref):
        qb = pl.program_id(1)
        q2 = q_ref[...]
        head_a = lax.broadcasted_iota(jnp.int32, (QB, 128), 1) < HD
        zq = jnp.zeros_like(q2)
        qs = (jnp.where(head_a, q2, zq), jnp.where(head_a, zq, q2))
        row = lax.broadcasted_iota(jnp.int32, (QB, QB), 0)
        col = lax.broadcasted_iota(jnp.int32, (QB, QB), 1)
        tri = tri_ref[...]

        def step(i, carry):
            cs, accs = carry
            kb = qb - i
            off = pl.multiple_of(kb * QB, QB)
            k2 = k_ref[pl.ds(off, QB), :]
            v2 = v_ref[pl.ds(off, QB), :]
            valid = (col - row) < i * QB
            new_c, new_acc = [], []
            for hd in range(2):
                ls, lf, _, _ = _sb_logits(qs[hd], k2, valid)
                within = _mm2(lf, tri)
                a = jnp.where(valid, jnp.exp(ls + within + cs[hd]), 0.0)
                new_acc.append(accs[hd] + _mm2(a, v2))
                new_c.append(cs[hd] + jnp.sum(lf, axis=-1, keepdims=True))
            return tuple(new_c), tuple(new_acc)

        zc = jnp.zeros((QB, 1), F32)
        za = jnp.zeros((QB, 128), F32)
        _, accs = lax.fori_loop(0, qb + 1, step, ((zc, zc), (za, za)))
        o_ref[...] = jnp.where(head_a, accs[0], accs[1])

    return pl.pallas_call(
        body, name="sb_fwd", grid=(npair, nq),
        in_specs=[pl.BlockSpec((QB, 128), lambda p, i: (i, p)),
                  pl.BlockSpec((S, 128), lambda p, i: (0, npair + p)),
                  pl.BlockSpec((S, 128), lambda p, i: (0, 2 * npair + p)),
                  pl.BlockSpec((QB, QB), lambda p, i: (0, 0))],
        out_specs=pl.BlockSpec((QB, 128), lambda p, i: (i, p)),
        out_shape=jax.ShapeDtypeStruct((S, DTOK), F32),
        compiler_params=_cparams(("parallel", "parallel")),
    )(proj, proj, proj, _tri(True))


def _sb_bwd(proj, o, do):
    S = proj.shape[0]
    nq = S // QB
    npair = DTOK // 128

    def body(q_ref, k_ref, v_ref, o_ref, do_ref, tri_ref, tri2_ref, dq_ref, dk_ref, dv_ref):
        qb = pl.program_id(1)

        @pl.when(qb == 0)
        def _():
            dk_ref[...] = jnp.zeros_like(dk_ref)
            dv_ref[...] = jnp.zeros_like(dv_ref)

        q2 = q_ref[...]
        head_a = lax.broadcasted_iota(jnp.int32, (QB, 128), 1) < HD
        zq = jnp.zeros_like(q2)
        qs = (jnp.where(head_a, q2, zq), jnp.where(head_a, zq, q2))
        dob = do_ref[...].astype(_MM)
        zd = jnp.zeros_like(dob)
        dos = (jnp.where(head_a, dob, zd), jnp.where(head_a, zd, dob))
        go = dob.astype(F32) * o_ref[...]
        gs = (jnp.sum(jnp.where(head_a, go, 0.0), axis=-1, keepdims=True),
              jnp.sum(jnp.where(head_a, 0.0, go), axis=-1, keepdims=True))
        row = lax.broadcasted_iota(jnp.int32, (QB, QB), 0)
        col = lax.broadcasted_iota(jnp.int32, (QB, QB), 1)
        tri = tri_ref[...]
        tri2 = tri2_ref[...]

        def step(i, carry):
            cs, rs, dqs = carry
            kb = qb - i
            off = pl.multiple_of(kb * QB, QB)
            k2 = k_ref[pl.ds(off, QB), :]
            v2 = v_ref[pl.ds(off, QB), :]
            valid = (col - row) < i * QB
            new_c, new_r, new_dq = [], [], []
            dk_acc = jnp.zeros((QB, 128), F32)
            dv_acc = jnp.zeros((QB, 128), F32)
            for hd in range(2):
                ls, lf, e, z = _sb_logits(qs[hd], k2, valid)
                inv = 1.0 / (1.0 + e)
                pos = z >= 0.0
                beta = jnp.where(pos, 1.0, e) * inv
                omb = jnp.where(pos, e, 1.0) * inv
                within = _mm2(lf, tri)
                a = jnp.where(valid, jnp.exp(ls + within + cs[hd]), 0.0)
                dlw = _mm_nt(dos[hd], v2) * a
                prefix = gs[hd] - rs[hd] - _mm2(dlw, tri2)
                dz = (jnp.where(valid, dlw * omb - beta * prefix, 0.0) * SCALE).astype(_MM)
                new_dq.append(dqs[hd] + _mm(dz, k2))
                dk_acc = dk_acc + _mm_tn(dz, qs[hd])
                hi = a.astype(_MM)
                lo = (a - hi.astype(F32)).astype(_MM)
                dv_acc = dv_acc + _mm_tn(hi, dos[hd]) + _mm_tn(lo, dos[hd])
                new_c.append(cs[hd] + jnp.sum(lf, axis=-1, keepdims=True))
                new_r.append(rs[hd] + jnp.sum(dlw, axis=-1, keepdims=True))
            dk_ref[pl.ds(off, QB), :] += dk_acc
            dv_ref[pl.ds(off, QB), :] += dv_acc
            return tuple(new_c), tuple(new_r), tuple(new_dq)

        zc = jnp.zeros((QB, 1), F32)
        za = jnp.zeros((QB, 128), F32)
        _, _, dqs = lax.fori_loop(0, qb + 1, step, ((zc, zc), (zc, zc), (za, za)))
        dq_ref[...] = jnp.where(head_a, dqs[0], dqs[1])

    qspec = pl.BlockSpec((QB, 128), lambda p, i: (i, p))
    full = lambda base: pl.BlockSpec((S, 128), lambda p, i: (0, base + p))
    tspec = pl.BlockSpec((QB, QB), lambda p, i: (0, 0))
    return pl.pallas_call(
        body, name="sb_bwd", grid=(npair, nq),
        in_specs=[qspec, full(npair), full(2 * npair), qspec, qspec, tspec, tspec],
        out_specs=[qspec, full(0), full(0)],
        out_shape=[jax.ShapeDtypeStruct((S, DTOK), F32)] * 3,
        compiler_params=_cparams(("parallel", "arbitrary")),
    )(proj, proj, proj, o, do, _tri(True), _tri(False))


def _memkv_fwd(mem, gm, wkv):
    nl = wkv.shape[0]

    def body(m_ref, g_ref, w_ref, kv_ref):
        m = m_ref[...]
        mn = (m * _rstd(m) * g_ref[...]).astype(_MM)
        kv_ref[...] = _mm(mn, w_ref[...]).astype(kv_ref.dtype)

    return pl.pallas_call(
        body, name="memkv_fwd", grid=(nl,),
        in_specs=[pl.BlockSpec((MEM_LEN, D), lambda l: (0, 0)), pl.BlockSpec((None, 1, D), lambda l: (l, 0, 0)),
                  pl.BlockSpec((None, D, 2 * DMEM), lambda l: (l, 0, 0))],
        out_specs=pl.BlockSpec((None, MEM_LEN, 2 * DMEM), lambda l: (l, 0, 0)),
        out_shape=jax.ShapeDtypeStruct((nl, MEM_LEN, 2 * DMEM), _MM),
        compiler_params=_cparams(("parallel",)),
    )(mem, gm, wkv)


def _memkv_bwd(mem, gm, wkv, dkv):
    nl = wkv.shape[0]

    def body(m_ref, g_ref, w_ref, dkv_ref, dw_ref, dg_ref):
        m = m_ref[...]
        mh = m * _rstd(m)
        mn = (mh * g_ref[...]).astype(_MM)
        dkv = dkv_ref[...].astype(_MM)
        dw_ref[...] = _mm_tn(mn, dkv)
        dmn = _mm_nt(dkv, w_ref[...])
        dg_ref[...] = jnp.sum(dmn * mh, axis=0, keepdims=True)

    return pl.pallas_call(
        body, name="memkv_bwd", grid=(nl,),
        in_specs=[pl.BlockSpec((MEM_LEN, D), lambda l: (0, 0)), pl.BlockSpec((None, 1, D), lambda l: (l, 0, 0)),
                  pl.BlockSpec((None, D, 2 * DMEM), lambda l: (l, 0, 0)),
                  pl.BlockSpec((None, MEM_LEN, 2 * DMEM), lambda l: (l, 0, 0))],
        out_specs=[pl.BlockSpec((None, D, 2 * DMEM), lambda l: (l, 0, 0)), pl.BlockSpec((None, 1, D), lambda l: (l, 0, 0))],
        out_shape=[jax.ShapeDtypeStruct((nl, D, 2 * DMEM), F32), jax.ShapeDtypeStruct((nl, 1, D), F32)],
        compiler_params=_cparams(("parallel",)),
    )(mem, gm, wkv, dkv)


def _mem_probs(qh, kp):
    s = _mm_nt(qh, kp) * SCALE
    p = jnp.exp(s - jnp.max(s, axis=-1, keepdims=True))
    return p / jnp.sum(p, axis=-1, keepdims=True)


def _mixout_fwd(h, tok, proj, kv, wo, g, li, qm_block):
    S = h.shape[0]
    ts = _row_tile(S, 256)

    def body(h_ref, tok_ref, qm_ref, kv_ref, wo_ref, g_ref, ho_ref, cat_ref, mix_ref):
        head_a = lax.broadcasted_iota(jnp.int32, (ts, 128), 1) < HD
        qm = qm_ref[...].astype(_MM)
        zq = jnp.zeros((ts, 128), _MM)
        mos = []
        for pi in range(DMEM // 128):
            cs = slice(pi * 128, (pi + 1) * 128)
            qp = qm[:, cs]
            kp = kv_ref[:, cs]
            vp = kv_ref[:, DMEM + pi * 128:DMEM + (pi + 1) * 128]
            pa = _mem_probs(jnp.where(head_a, qp, zq), kp)
            pb = _mem_probs(jnp.where(head_a, zq, qp), kp)
            mos.append(jnp.where(head_a, _mm(pa, vp), _mm(pb, vp)))
        cat = jnp.concatenate([tok_ref[...].astype(_MM)] + [m.astype(_MM) for m in mos], axis=-1)
        cat_ref[...] = cat
        mix = _mm(cat, wo_ref[...])
        mix_ref[...] = mix
        ho_ref[...] = h_ref[...] + mix * _rstd(mix) * g_ref[...]

    row = lambda i: (i, 0)
    return pl.pallas_call(
        body, name="mixout_fwd", grid=(S // ts,),
        in_specs=[pl.BlockSpec((ts, D), row), pl.BlockSpec((ts, DTOK), row),
                  pl.BlockSpec((ts, DMEM), lambda i: (i, qm_block)),
                  pl.BlockSpec((None, MEM_LEN, 2 * DMEM), lambda i: (li, 0, 0)),
                  pl.BlockSpec((None, DMIX, D), lambda i: (li, 0, 0)), pl.BlockSpec((1, D), lambda i: (0, 0))],
        out_specs=[pl.BlockSpec((ts, D), row), pl.BlockSpec((ts, DMIX), row), pl.BlockSpec((ts, D), row)],
        out_shape=[jax.ShapeDtypeStruct((S, D), F32), jax.ShapeDtypeStruct((S, DMIX), _MM),
                   jax.ShapeDtypeStruct((S, D), F32)],
        compiler_params=_cparams(("parallel",)),
    )(h, tok, proj, kv, wo, g)


def _mixout_bwd(dho, mix, proj, kv, wo, g, li, qm_block):
    S = dho.shape[0]
    ts = _row_tile(S, 256)

    def body(dho_ref, mix_ref, qm_ref, kv_ref, wo_ref, g_ref, dmix_ref, dtok_ref, dqm_ref, dkv_ref, dg_ref):
        @pl.when(pl.program_id(0) == 0)
        def _():
            dkv_ref[...] = jnp.zeros_like(dkv_ref)
            dg_ref[...] = jnp.zeros_like(dg_ref)

        dmix, dg = _rms_bwd(mix_ref[...], g_ref[...], dho_ref[...])
        dg_ref[...] += dg
        dmb = dmix.astype(_MM)
        dmix_ref[...] = dmb
        dcat = _mm_nt(dmb, wo_ref[...])
        dtok_ref[...] = dcat[:, :DTOK]
        head_a = lax.broadcasted_iota(jnp.int32, (ts, 128), 1) < HD
        qm = qm_ref[...].astype(_MM)
        zq = jnp.zeros((ts, 128), _MM)
        for pi in range(DMEM // 128):
            cs = slice(pi * 128, (pi + 1) * 128)
            vs = slice(DMEM + pi * 128, DMEM + (pi + 1) * 128)
            kp = kv_ref[:, cs]
            vp = kv_ref[:, vs]
            dmo = dcat[:, DTOK + pi * 128:DTOK + (pi + 1) * 128].astype(_MM)
            dqs = []
            dk_acc = jnp.zeros((MEM_LEN, 128), F32)
            dv_acc = jnp.zeros((MEM_LEN, 128), F32)
            for hd in range(2):
                qh = jnp.where(head_a, qm[:, cs], zq) if hd == 0 else jnp.where(head_a, zq, qm[:, cs])
                dmh = jnp.where(head_a, dmo, zq) if hd == 0 else jnp.where(head_a, zq, dmo)
                p = _mem_probs(qh, kp)
                dp = _mm_nt(dmh, vp)
                ds = (p * (dp - jnp.sum(dp * p, axis=-1, keepdims=True)) * SCALE).astype(_MM)
                dqs.append(_mm(ds, kp))
                dk_acc = dk_acc + _mm_tn(ds, qh)
                dv_acc = dv_acc + _mm_tn(p, dmh)
            dqm_ref[:, cs] = jnp.where(head_a, dqs[0], dqs[1]).astype(_MM)
            dkv_ref[:, cs] += dk_acc
            dkv_ref[:, vs] += dv_acc

    row = lambda i: (i, 0)
    fix = lambda i: (0, 0)
    return pl.pallas_call(
        body, name="mixout_bwd", grid=(S // ts,),
        in_specs=[pl.BlockSpec((ts, D), row), pl.BlockSpec((ts, D), row),
                  pl.BlockSpec((ts, DMEM), lambda i: (i, qm_block)),
                  pl.BlockSpec((None, MEM_LEN, 2 * DMEM), lambda i: (li, 0, 0)),
                  pl.BlockSpec((None, DMIX, D), lambda i: (li, 0, 0)), pl.BlockSpec((1, D), fix)],
        out_specs=[pl.BlockSpec((ts, D), row), pl.BlockSpec((ts, DTOK), row), pl.BlockSpec((ts, DMEM), row),
                   pl.BlockSpec((MEM_LEN, 2 * DMEM), fix), pl.BlockSpec((1, D), fix)],
        out_shape=[jax.ShapeDtypeStruct((S, D), _MM), jax.ShapeDtypeStruct((S, DTOK), F32),
                   jax.ShapeDtypeStruct((S, DMEM), _MM), jax.ShapeDtypeStruct((MEM_LEN, 2 * DMEM), F32),
                   jax.ShapeDtypeStruct((1, D), F32)],
        compiler_params=_cparams(("arbitrary",)),
    )(dho, mix, proj, kv, wo, g)


def _loss_grad(y, target):
    S = y.shape[0]
    ts = _row_tile(S, 512)

    def body(y_ref, t_ref, acc_ref, dy_ref):
        @pl.when(pl.program_id(0) == 0)
        def _():
            acc_ref[...] = jnp.zeros_like(acc_ref)

        err = y_ref[...] - t_ref[...]
        acc_ref[...] += jnp.sum(err * err, axis=0, keepdims=True)
        dy_ref[...] = err * (1.0 / D)

    row = lambda i: (i, 0)
    return pl.pallas_call(
        body, name="loss_grad", grid=(S // ts,),
        in_specs=[pl.BlockSpec((ts, D), row), pl.BlockSpec((ts, D), row)],
        out_specs=[pl.BlockSpec((1, D), lambda i: (0, 0)), pl.BlockSpec((ts, D), row)],
        out_shape=[jax.ShapeDtypeStruct((1, D), F32), jax.ShapeDtypeStruct((S, D), F32)],
        compiler_params=_cparams(("arbitrary",)),
    )(y, target)


def _local_step(x, mem, target, W):
    gpre, gpost = W["g_pre"], W["g_post"]
    gm = W["g_mem"].reshape(DEPTH, 1, D)
    g_at = lambda g, i, k: g[i, k].reshape(1, D)
    kv = _memkv_fwd(mem, gm, W["w_mem_kv"])

    saved = []
    h = x
    for i in range(DEPTH):
        j = i // 2
        sv = {"h0": h}
        h, sv["a1"], sv["b1"], sv["f1"] = _ffn_fwd(h, g_at(gpre, i, 0), g_at(gpost, i, 0),
                                                   W["ffn1_gate"], W["ffn1_up"], W["ffn1_down"], i)
        sv["h1"] = h
        if i % 2 == 0:
            sv["u"], sv["proj"] = _proj_fwd(h, g_at(gpre, i, 1), W["w_in_pool"], j, F32)
            tok = _pool_fwd(sv["proj"], W["pool_w"][j], W["pool_scale"][j].reshape(1, DTOK))
            qm_block = DTOK // DMEM
        else:
            sv["u"], sv["proj"] = _proj_fwd(h, g_at(gpre, i, 1), W["w_in_sb"], j, _MM)
            tok = sv["o"] = _sb_fwd(sv["proj"])
            qm_block = 3 * DTOK // DMEM
        h, sv["cat"], sv["mix"] = _mixout_fwd(h, tok, sv["proj"], kv, W["w_out"], g_at(gpost, i, 1), i, qm_block)
        sv["h2"] = h
        h, sv["a2"], sv["b2"], sv["f2"] = _ffn_fwd(h, g_at(gpre, i, 2), g_at(gpost, i, 2),
                                                   W["ffn2_gate"], W["ffn2_up"], W["ffn2_down"], i)
        saved.append(sv)

    acc, dh = _loss_grad(h, target)
    loss = 0.5 / D * jnp.sum(acc)

    per_layer = {k: [None] * DEPTH for k in ("ffn1_gate", "ffn1_up", "ffn1_down", "ffn2_gate", "ffn2_up",
                                              "ffn2_down", "w_out")}
    d_in_pool, d_in_sb = [None] * 2, [None] * 2
    d_pool_w, d_pool_scale = [None] * 2, [None] * 2
    dgpre = [[None] * 3 for _ in range(DEPTH)]
    dgpost = [[None] * 3 for _ in range(DEPTH)]
    dkv = [None] * DEPTH

    def ffn_back(i, which, dh, h_in, sv):
        k = 0 if which == 1 else 2
        n = f"ffn{which}"
        dh, nb, sb, da, db, df, dg1, dg2 = _ffn_bwd(
            h_in, sv[f"f{which}"], sv[f"a{which}"], sv[f"b{which}"], dh, g_at(gpre, i, k), g_at(gpost, i, k),
            W[n + "_gate"], W[n + "_up"], W[n + "_down"], i)
        per_layer[n + "_gate"][i] = _wgrad(nb, da)
        per_layer[n + "_up"][i] = _wgrad(nb, db)
        per_layer[n + "_down"][i] = _wgrad(sb, df)
        dgpre[i][k], dgpost[i][k] = dg1, dg2
        return dh

    for i in reversed(range(DEPTH)):
        j = i // 2
        sv = saved[i]
        dh = ffn_back(i, 2, dh, sv["h2"], sv)
        qm_block = DTOK // DMEM if i % 2 == 0 else 3 * DTOK // DMEM
        dmix, dtok, dqm, dkv[i], dgpost[i][1] = _mixout_bwd(dh, sv["mix"], sv["proj"], kv, W["w_out"],
                                                            g_at(gpost, i, 1), i, qm_block)
        per_layer["w_out"][i] = _wgrad(sv["cat"], dmix)
        if i % 2 == 0:
            dx, d_pool_w[j], dps = _pool_bwd(sv["proj"], dtok, W["pool_w"][j], W["pool_scale"][j].reshape(1, DTOK))
            d_pool_scale[j] = dps.reshape(DTOK)
            dh, dproj, dgpre[i][1] = _proj_bwd(sv["h1"], dh, g_at(gpre, i, 1), W["w_in_pool"], j, [dx, dqm])
            d_in_pool[j] = _wgrad(sv["u"], dproj)
        else:
            dq, dk, dv = _sb_bwd(sv["proj"], sv["o"], dtok)
            dh, dproj, dgpre[i][1] = _proj_bwd(sv["h1"], dh, g_at(gpre, i, 1), W["w_in_sb"], j, [dq, dk, dv, dqm])
            d_in_sb[j] = _wgrad(sv["u"], dproj)
        dh = ffn_back(i, 1, dh, sv["h0"], sv)

    dwkv, dgm = _memkv_bwd(mem, gm, W["w_mem_kv"], jnp.stack(dkv))
    grads = {k: jnp.stack(v) for k, v in per_layer.items()}
    grads["w_in_pool"] = jnp.stack(d_in_pool)
    grads["w_in_sb"] = jnp.stack(d_in_sb)
    grads["w_mem_kv"] = dwkv
    grads["pool_w"] = jnp.stack(d_pool_w)
    grads["pool_scale"] = jnp.stack(d_pool_scale)
    grads["g_mem"] = dgm.reshape(DEPTH, D)
    grads["g_pre"] = jnp.stack([jnp.concatenate(r, axis=0) for r in dgpre])
    grads["g_post"] = jnp.stack([jnp.concatenate(r, axis=0) for r in dgpost])
    return loss, dh, grads


ANY = pl.BlockSpec(memory_space=pl.ANY)


def _place():
    return lax.axis_index("x"), lax.axis_index("y"), lax.axis_index("c")


def _flip(v, bit):
    return 1 - v if bit else v


def _small_allgather(buf):
    R = buf.shape[0]

    def body(x_ref, o_ref, send_sems, recv_sems, loc_sem):
        x, y, c = _place()
        me = 4 * x + 2 * y + c
        loc = pltpu.make_async_copy(x_ref, o_ref.at[me], loc_sem)
        loc.start()
        sends = []
        for k in range(1, N_DEV):
            peer = (_flip(x, k & 4), _flip(y, k & 2), _flip(c, k & 1))
            cp = pltpu.make_async_remote_copy(src_ref=x_ref, dst_ref=o_ref.at[me], send_sem=send_sems.at[k - 1],
                                              recv_sem=recv_sems.at[k - 1], device_id=peer, device_id_type=MESH)
            cp.start()
            sends.append(cp)
        for k in range(1, N_DEV):
            peer = (_flip(x, k & 4), _flip(y, k & 2), _flip(c, k & 1))
            slot = 4 * peer[0] + 2 * peer[1] + peer[2]
            pltpu.make_async_remote_copy(src_ref=x_ref, dst_ref=o_ref.at[slot], send_sem=send_sems.at[k - 1],
                                         recv_sem=recv_sems.at[k - 1], device_id=peer, device_id_type=MESH).wait_recv()
        for cp in sends:
            cp.wait_send()
        loc.wait()

    return pl.pallas_call(
        body, name="small_allgather", in_specs=[ANY], out_specs=ANY,
        out_shape=jax.ShapeDtypeStruct((N_DEV, R, 128), buf.dtype),
        scratch_shapes=[pltpu.SemaphoreType.DMA((N_DEV - 1,)), pltpu.SemaphoreType.DMA((N_DEV - 1,)),
                        pltpu.SemaphoreType.DMA],
    )(buf)


def _gather_weights(shards):
    n = len(shards)
    halves = [s.shape[0] // 2 for s in shards]

    def body(*refs):
        ins, outs = refs[:n], refs[n:2 * n]
        ici_send, ici_recv, d2d_send, d2d_recv, loc_sems = refs[2 * n:]
        x, y, c = _place()
        me = 2 * x + y
        sib = (x, y, 1 - c)
        chips = [(_flip(x, k & 2), _flip(y, k & 1)) for k in (1, 2, 3)]
        locs = []
        for t in range(n):
            cp = pltpu.make_async_copy(ins[t], outs[t].at[me], loc_sems.at[t])
            cp.start()
            locs.append(cp)

        def half(t, chip_idx, hc):
            return outs[t].at[chip_idx, pl.ds(hc * halves[t], halves[t])]

        sends = []
        for t in range(n):
            for k, chip in enumerate(chips):
                cp = pltpu.make_async_remote_copy(
                    src_ref=ins[t].at[pl.ds(c * halves[t], halves[t])], dst_ref=half(t, me, c),
                    send_sem=ici_send.at[3 * t + k], recv_sem=ici_recv.at[3 * t + k],
                    device_id=(chip[0], chip[1], c), device_id_type=MESH)
                cp.start()
                sends.append(cp)
        for t in range(n):
            for k, chip in enumerate(chips):
                idx = 2 * chip[0] + chip[1]
                pltpu.make_async_remote_copy(
                    src_ref=half(t, idx, c), dst_ref=half(t, idx, c), send_sem=ici_send.at[3 * t + k],
                    recv_sem=ici_recv.at[3 * t + k], device_id=(chip[0], chip[1], c), device_id_type=MESH).wait_recv()
                cp = pltpu.make_async_remote_copy(
                    src_ref=half(t, idx, c), dst_ref=half(t, idx, c), send_sem=d2d_send.at[3 * t + k],
                    recv_sem=d2d_recv.at[3 * t + k], device_id=sib, device_id_type=MESH)
                cp.start()
                sends.append(cp)
        for t in range(n):
            for k, chip in enumerate(chips):
                idx = 2 * chip[0] + chip[1]
                pltpu.make_async_remote_copy(
                    src_ref=half(t, idx, 1 - c), dst_ref=half(t, idx, 1 - c), send_sem=d2d_send.at[3 * t + k],
                    recv_sem=d2d_recv.at[3 * t + k], device_id=sib, device_id_type=MESH).wait_recv()
        for cp in sends:
            cp.wait_send()
        for cp in locs:
            cp.wait()

    dma = lambda m: pltpu.SemaphoreType.DMA((m,))
    return pl.pallas_call(
        body, name="gather_weights", in_specs=[ANY] * n, out_specs=[ANY] * n,
        out_shape=[jax.ShapeDtypeStruct((N_CHIPS,) + s.shape, s.dtype) for s in shards],
        scratch_shapes=[dma(3 * n), dma(3 * n), dma(3 * n), dma(3 * n), dma(n)],
    )(*shards)


def _rs_to_sibling(gs):
    n = len(gs)
    halves = [g.shape[1] // 2 for g in gs]

    def body(*refs):
        ins, outs = refs[:n], refs[n:2 * n]
        send_sems, recv_sems = refs[2 * n:]
        x, y, c = _place()
        cps = []
        for t in range(n):
            cp = pltpu.make_async_remote_copy(
                src_ref=ins[t].at[:, pl.ds((1 - c) * halves[t], halves[t])], dst_ref=outs[t],
                send_sem=send_sems.at[t], recv_sem=recv_sems.at[t], device_id=(x, y, 1 - c), device_id_type=MESH)
            cp.start()
            cps.append(cp)
        for cp in cps:
            cp.wait()

    dma = lambda m: pltpu.SemaphoreType.DMA((m,))
    return pl.pallas_call(
        body, name="rs_to_sibling", in_specs=[ANY] * n, out_specs=[ANY] * n,
        out_shape=[jax.ShapeDtypeStruct((N_CHIPS, g.shape[1] // 2) + g.shape[2:], g.dtype) for g in gs],
        scratch_shapes=[dma(n), dma(n)],
    )(*gs)


def _rs_to_chips(ss):
    n = len(ss)

    def body(*refs):
        ins, outs = refs[:n], refs[n:2 * n]
        send_sems, recv_sems = refs[2 * n:]
        x, y, c = _place()
        chips = [(_flip(x, k & 2), _flip(y, k & 1)) for k in (1, 2, 3)]
        cps = []
        for t in range(n):
            for k, chip in enumerate(chips):
                cp = pltpu.make_async_remote_copy(
                    src_ref=ins[t].at[2 * chip[0] + chip[1]], dst_ref=outs[t].at[k],
                    send_sem=send_sems.at[3 * t + k], recv_sem=recv_sems.at[3 * t + k],
                    device_id=(chip[0], chip[1], c), device_id_type=MESH)
                cp.start()
                cps.append(cp)
        for cp in cps:
            cp.wait()

    dma = lambda m: pltpu.SemaphoreType.DMA((m,))
    return pl.pallas_call(
        body, name="rs_to_chips", in_specs=[ANY] * n, out_specs=[ANY] * n,
        out_shape=[jax.ShapeDtypeStruct((3,) + s.shape[1:], s.dtype) for s in ss],
        scratch_shapes=[dma(3 * n), dma(3 * n)],
    )(*ss)


def _share_halves(ts):
    n = len(ts)

    def body(*refs):
        ins, outs = refs[:n], refs[n:2 * n]
        send_sems, recv_sems, loc_sems = refs[2 * n:]
        x, y, c = _place()
        cps, locs = [], []
        for t in range(n):
            loc = pltpu.make_async_copy(ins[t], outs[t].at[c], loc_sems.at[t])
            loc.start()
            locs.append(loc)
            cp = pltpu.make_async_remote_copy(
                src_ref=ins[t], dst_ref=outs[t].at[c], send_sem=send_sems.at[t], recv_sem=recv_sems.at[t],
                device_id=(x, y, 1 - c), device_id_type=MESH)
            cp.start()
            cps.append(cp)
        for t in range(n):
            pltpu.make_async_remote_copy(
                src_ref=ins[t], dst_ref=outs[t].at[1 - c], send_sem=send_sems.at[t], recv_sem=recv_sems.at[t],
                device_id=(x, y, 1 - c), device_id_type=MESH).wait_recv()
        for cp in cps:
            cp.wait_send()
        for loc in locs:
            loc.wait()

    dma = lambda m: pltpu.SemaphoreType.DMA((m,))
    return pl.pallas_call(
        body, name="share_halves", in_specs=[ANY] * n, out_specs=[ANY] * n,
        out_shape=[jax.ShapeDtypeStruct((2,) + t.shape, t.dtype) for t in ts],
        scratch_shapes=[dma(n), dma(n), dma(n)],
    )(*ts)


def _rows(a, lead):
    return a.reshape(a.shape[:lead] + (-1, a.shape[-1]))


def _add_sibling(g, r, c):
    L_ = g.shape[1]
    g3 = _rows(g.reshape((N_CHIPS, 2, L_ // 2) + g.shape[2:]), 2)
    r3 = _rows(r, 1)
    R_, C = r3.shape[1:]
    tr = _row_tile(R_, 512)

    def body(c_ref, g_ref, r_ref, o_ref):
        o_ref[...] = (g_ref[...] + r_ref[...]).astype(o_ref.dtype)

    out = pl.pallas_call(
        body, name="add_sibling",
        grid_spec=pltpu.PrefetchScalarGridSpec(
            num_scalar_prefetch=1, grid=(N_CHIPS, R_ // tr),
            in_specs=[pl.BlockSpec((None, None, tr, C), lambda s, i, c_ref: (s, c_ref[0], i, 0)),
                      pl.BlockSpec((None, tr, C), lambda s, i, c_ref: (s, i, 0))],
            out_specs=pl.BlockSpec((None, tr, C), lambda s, i, c_ref: (s, i, 0))),
        out_shape=jax.ShapeDtypeStruct((N_CHIPS, R_, C), _MM),
        compiler_params=_cparams(("parallel", "parallel")),
    )(c.reshape(1), g3, r3)
    return out.reshape((N_CHIPS, L_ // 2) + g.shape[2:])


def _add_chips(s, r, me):
    s3, r3 = _rows(s, 1), _rows(r, 1)
    R_, C = s3.shape[1:]
    tr = _row_tile(R_, 512)

    def body(me_ref, s_ref, r0_ref, r1_ref, r2_ref, o_ref):
        f = lambda ref: ref[...].astype(F32)
        o_ref[...] = ((f(s_ref) + f(r0_ref)) + f(r1_ref)) + f(r2_ref)

    rk = lambda k: pl.BlockSpec((None, tr, C), lambda i, me_ref: (k, i, 0))
    out = pl.pallas_call(
        body, name="add_chips",
        grid_spec=pltpu.PrefetchScalarGridSpec(
            num_scalar_prefetch=1, grid=(R_ // tr,),
            in_specs=[pl.BlockSpec((None, tr, C), lambda i, me_ref: (me_ref[0], i, 0)), rk(0), rk(1), rk(2)],
            out_specs=pl.BlockSpec((tr, C), lambda i, me_ref: (i, 0))),
        out_shape=jax.ShapeDtypeStruct((R_, C), F32),
        compiler_params=_cparams(("parallel",)),
    )(me.reshape(1), s3, r3, r3, r3)
    return out.reshape(s.shape[1:])


def _sum_devices(a):
    R_ = a.shape[1]

    def body(a_ref, o_ref):
        acc = a_ref[0]
        for k in range(1, N_DEV):
            acc = acc + a_ref[k]
        o_ref[...] = acc

    return pl.pallas_call(
        body, name="sum_devices", out_shape=jax.ShapeDtypeStruct((R_, 128), F32),
        compiler_params=pltpu.CompilerParams(vmem_limit_bytes=VMEM_LIMIT),
    )(a)


def _adamw(w, g, m, v):
    shape = w.shape
    to2 = lambda a: a.reshape(-1, shape[-1])
    w2, g2, m2, v2 = to2(w), to2(g), to2(m), to2(v)
    R_, C = w2.shape
    tr = 256 if R_ % 256 == 0 else R_

    def body(w_ref, g_ref, m_ref, v_ref, d_ref, mo_ref, vo_ref):
        gv = g_ref[...]
        mn = ADAM_B1 * m_ref[...] + (1.0 - ADAM_B1) * gv
        vn = ADAM_B2 * v_ref[...] + (1.0 - ADAM_B2) * (gv * gv)
        m_hat = mn / (1.0 - ADAM_B1 ** ADAM_STEP)
        v_hat = vn / (1.0 - ADAM_B2 ** ADAM_STEP)
        d_ref[...] = -ADAM_LR * (m_hat / (jnp.sqrt(v_hat) + ADAM_EPS) + ADAM_WD * w_ref[...])
        mo_ref[...] = mn
        vo_ref[...] = vn

    spec = pl.BlockSpec((tr, C), lambda i: (i, 0))
    outs = pl.pallas_call(
        body, name="adamw", grid=(R_ // tr,), in_specs=[spec] * 4, out_specs=[spec] * 3,
        out_shape=[jax.ShapeDtypeStruct((R_, C), F32)] * 3,
        compiler_params=_cparams(("parallel",)),
    )(w2, g2, m2, v2)
    return [o.reshape(shape) for o in outs]


SHARD_AXIS = {"g_pre": 2, "g_post": 2, "g_mem": None, "ffn1_gate": 2, "ffn1_up": 2, "ffn1_down": 1,
              "ffn2_gate": 2, "ffn2_up": 2, "ffn2_down": 1, "w_in_pool": 1, "pool_w": None, "pool_scale": None,
              "w_in_sb": 2, "w_mem_kv": 1, "w_out": 2}
WEIGHTS = list(SHARD_AXIS)
BIG = [k for k in WEIGHTS if SHARD_AXIS[k] is not None and k not in ("g_pre", "g_post")]
SMALL = [k for k in WEIGHTS if k not in BIG]


def _unshard(g, axis):
    _, L_, r, N = g.shape
    if axis == 1:
        return jnp.transpose(g, (1, 0, 2, 3)).reshape(L_, N_CHIPS * r, N)
    return jnp.transpose(g, (1, 2, 0, 3)).reshape(L_, r, N_CHIPS * N)


def _to_shards(a, axis):
    L_, r, N = a.shape
    if axis == 1:
        return jnp.transpose(a.reshape(L_, N_CHIPS, r // N_CHIPS, N), (1, 0, 2, 3))
    return jnp.transpose(a.reshape(L_, r, N_CHIPS, N // N_CHIPS), (2, 0, 1, 3))


def _pack128(arrs):
    flat = jnp.concatenate([a.reshape(-1) for a in arrs])
    pad = (-flat.shape[0]) % (8 * 128)
    return jnp.pad(flat, (0, pad)).reshape(-1, 128)


def _unpack128(buf, shapes):
    flat = buf.reshape(-1)
    out, off = [], 0
    for shp in shapes:
        size = 1
        for d in shp:
            size *= d
        out.append(flat[off:off + size].reshape(shp))
        off += size
    return out


def kernel(x, mem, g_pre, g_post, g_mem, ffn1_gate, ffn1_up, ffn1_down, ffn2_gate, ffn2_up, ffn2_down, w_in_pool, pool_w, pool_scale, w_in_sb, w_mem_kv, w_out, loss_target, m_g_pre, m_g_post, m_g_mem, m_ffn1_gate, m_ffn1_up, m_ffn1_down, m_ffn2_gate, m_ffn2_up, m_ffn2_down, m_w_in_pool, m_pool_w, m_pool_scale, m_w_in_sb, m_w_mem_kv, m_w_out, v_g_pre, v_g_post, v_g_mem, v_ffn1_gate, v_ffn1_up, v_ffn1_down, v_ffn2_gate, v_ffn2_up, v_ffn2_down, v_w_in_pool, v_pool_w, v_pool_scale, v_w_in_sb, v_w_mem_kv, v_w_out):
    w = dict(g_pre=g_pre, g_post=g_post, g_mem=g_mem, ffn1_gate=ffn1_gate, ffn1_up=ffn1_up, ffn1_down=ffn1_down,
             ffn2_gate=ffn2_gate, ffn2_up=ffn2_up, ffn2_down=ffn2_down, w_in_pool=w_in_pool, pool_w=pool_w,
             pool_scale=pool_scale, w_in_sb=w_in_sb, w_mem_kv=w_mem_kv, w_out=w_out)
    m = dict(g_pre=m_g_pre, g_post=m_g_post, g_mem=m_g_mem, ffn1_gate=m_ffn1_gate, ffn1_up=m_ffn1_up,
             ffn1_down=m_ffn1_down, ffn2_gate=m_ffn2_gate, ffn2_up=m_ffn2_up, ffn2_down=m_ffn2_down,
             w_in_pool=m_w_in_pool, pool_w=m_pool_w, pool_scale=m_pool_scale, w_in_sb=m_w_in_sb,
             w_mem_kv=m_w_mem_kv, w_out=m_w_out)
    v = dict(g_pre=v_g_pre, g_post=v_g_post, g_mem=v_g_mem, ffn1_gate=v_ffn1_gate, ffn1_up=v_ffn1_up,
             ffn1_down=v_ffn1_down, ffn2_gate=v_ffn2_gate, ffn2_up=v_ffn2_up, ffn2_down=v_ffn2_down,
             w_in_pool=v_w_in_pool, pool_w=v_pool_w, pool_scale=v_pool_scale, w_in_sb=v_w_in_sb,
             w_mem_kv=v_w_mem_kv, w_out=v_w_out)
    cx, cy, cc = _place()
    chip = (2 * cx + cy).astype(jnp.int32)
    core = cc.astype(jnp.int32)

    gathered = _gather_weights([w[k].astype(_MM) for k in BIG])
    full = {k: _unshard(g, SHARD_AXIS[k]) for k, g in zip(BIG, gathered)}
    gains = _small_allgather(_pack128([g_pre, g_post]))
    per_chip = [_unpack128(gains[2 * j], [g_pre.shape, g_post.shape]) for j in range(N_CHIPS)]
    full["g_pre"] = jnp.concatenate([p[0] for p in per_chip], axis=2)
    full["g_post"] = jnp.concatenate([p[1] for p in per_chip], axis=2)
    full["g_mem"], full["pool_w"], full["pool_scale"] = g_mem, pool_w, pool_scale

    loss, dx, grads = _local_step(x[0], mem[0], loss_target[0], full)
    loss = lax.psum(loss, ("x", "y", "c"))

    gs = [_to_shards(grads[k], SHARD_AXIS[k]) for k in BIG]
    from_sib = _rs_to_sibling(gs)
    chip_sums = [_add_sibling(g, r, core) for g, r in zip(gs, from_sib)]
    from_chips = _rs_to_chips(chip_sums)
    mine = [_add_chips(s, r, chip) for s, r in zip(chip_sums, from_chips)]
    shared = _share_halves(mine)
    red = {k: s.reshape(w[k].shape) for k, s in zip(BIG, shared)}

    small_shapes = [grads[k].shape for k in SMALL]
    summed = _unpack128(_sum_devices(_small_allgather(_pack128([grads[k] for k in SMALL]))), small_shapes)
    for k, s in zip(SMALL, summed):
        if SHARD_AXIS[k] is None:
            red[k] = s
        else:
            red[k] = lax.dynamic_slice_in_dim(s, chip * w[k].shape[2], w[k].shape[2], axis=2)

    deltas, new_m, new_v = {}, {}, {}
    for k in WEIGHTS:
        deltas[k], new_m[k], new_v[k] = _adamw(w[k], red[k], m[k], v[k])
    return (loss, dx.reshape(x.shape), *[red[k] for k in WEIGHTS], *[deltas[k] for k in WEIGHTS],
            *[new_m[k] for k in WEIGHTS], *[new_v[k] for k in WEIGHTS])
```

```python
import jax
import jax.numpy as jnp
from jax import lax
from jax.experimental import pallas as pl
from jax.experimental.pallas import tpu as pltpu

F32 = jnp.float32
BF16 = jnp.bfloat16
_MM = jnp.bfloat16

D = 1024
FF = 2048
DTOK = 512
DMEM = 256
DMIX = DTOK + DMEM
DSB = 3 * DTOK + DMEM
HD = 64
MEM_LEN = 256
DEPTH = 4
EPS = 1e-6
WINDOWS = (2, 4, 8, 16)
HALO = 16
PG = 128
QB = 128
SCALE = HD ** -0.5
FC = 512
NEG_CUT = -104.0

ADAM_LR, ADAM_B1, ADAM_B2, ADAM_EPS, ADAM_WD, ADAM_STEP = 0.001, 0.9, 0.999, 1e-08, 0.01, 10

VMEM_LIMIT = 56 * 2 ** 20
MESH = pl.DeviceIdType.MESH
N_CHIPS = 4
N_DEV = 8


def _cparams(sem):
    return pltpu.CompilerParams(dimension_semantics=sem, vmem_limit_bytes=VMEM_LIMIT)


def _mm(a, b):
    return jnp.dot(a.astype(_MM), b.astype(_MM), preferred_element_type=F32)


def _mm_nt(a, b):
    return lax.dot_general(a.astype(_MM), b.astype(_MM), (((1,), (1,)), ((), ())), preferred_element_type=F32)


def _mm_tn(a, b):
    return lax.dot_general(a.astype(_MM), b.astype(_MM), (((0,), (0,)), ((), ())), preferred_element_type=F32)


def _mm2(a, b):
    hi = a.astype(_MM)
    lo = (a - hi.astype(F32)).astype(_MM)
    return jnp.dot(hi, b, preferred_element_type=F32) + jnp.dot(lo, b, preferred_element_type=F32)


def _rstd(x):
    return lax.rsqrt(jnp.mean(x * x, axis=-1, keepdims=True) + EPS)


def _rms_bwd(x, g, dy):
    r = _rstd(x)
    xh = x * r
    t = dy * g
    dx = r * (t - xh * jnp.mean(t * xh, axis=-1, keepdims=True))
    return dx, jnp.sum(dy * xh, axis=0, keepdims=True)


def _sigmoid(a):
    return 1.0 / (1.0 + jnp.exp(-a))


def _row_tile(n, want):
    t = min(n, want)
    assert n % t == 0, (n, t)
    return t


def _ffn_fwd(h, g1, g2, wg, wu, wd, li):
    S = h.shape[0]
    ts = _row_tile(S, 256)

    def body(h_ref, g1_ref, g2_ref, wg_ref, wu_ref, wd_ref, ho_ref, a_ref, b_ref, f_ref):
        hv = h_ref[...]
        n = (hv * _rstd(hv) * g1_ref[...]).astype(_MM)
        f = jnp.zeros((ts, D), F32)
        for j in range(FF // FC):
            cs = slice(j * FC, (j + 1) * FC)
            a = _mm(n, wg_ref[:, cs])
            b = _mm(n, wu_ref[:, cs])
            a_ref[:, cs] = a.astype(a_ref.dtype)
            b_ref[:, cs] = b.astype(b_ref.dtype)
            f = f + _mm(a * _sigmoid(a) * b, wd_ref[cs, :])
        f_ref[...] = f
        ho_ref[...] = hv + 0.5 * (f * _rstd(f) * g2_ref[...])

    row = lambda i: (i, 0)
    fix = lambda i: (0, 0)
    lay = lambda i: (li, 0, 0)
    return pl.pallas_call(
        body, name="ffn_fwd", grid=(S // ts,),
        in_specs=[pl.BlockSpec((ts, D), row), pl.BlockSpec((1, D), fix), pl.BlockSpec((1, D), fix),
                  pl.BlockSpec((None, D, FF), lay), pl.BlockSpec((None, D, FF), lay), pl.BlockSpec((None, FF, D), lay)],
        out_specs=[pl.BlockSpec((ts, D), row), pl.BlockSpec((ts, FF), row), pl.BlockSpec((ts, FF), row),
                   pl.BlockSpec((ts, D), row)],
        out_shape=[jax.ShapeDtypeStruct((S, D), F32), jax.ShapeDtypeStruct((S, FF), _MM),
                   jax.ShapeDtypeStruct((S, FF), _MM), jax.ShapeDtypeStruct((S, D), F32)],
        compiler_params=_cparams(("parallel",)),
    )(h, g1, g2, wg, wu, wd)


def _ffn_bwd(h, f, a, b, dho, g1, g2, wg, wu, wd, li):
    S = h.shape[0]
    ts = _row_tile(S, 256)

    def body(h_ref, f_ref, a_ref, b_ref, dho_ref, g1_ref, g2_ref, wg_ref, wu_ref, wd_ref,
             dh_ref, n_ref, s_ref, da_ref, db_ref, df_ref, dg1_ref, dg2_ref):
        @pl.when(pl.program_id(0) == 0)
        def _():
            dg1_ref[...] = jnp.zeros_like(dg1_ref)
            dg2_ref[...] = jnp.zeros_like(dg2_ref)

        hv = h_ref[...]
        dho = dho_ref[...]
        df, dg2 = _rms_bwd(f_ref[...], g2_ref[...], 0.5 * dho)
        dg2_ref[...] += dg2
        dfb = df.astype(_MM)
        df_ref[...] = dfb
        rh = _rstd(hv)
        hh = hv * rh
        g1 = g1_ref[...]
        n_ref[...] = (hh * g1).astype(_MM)
        dn = jnp.zeros((ts, D), F32)
        for j in range(FF // FC):
            cs = slice(j * FC, (j + 1) * FC)
            av = a_ref[:, cs].astype(F32)
            bv = b_ref[:, cs].astype(F32)
            sig = _sigmoid(av)
            sl = av * sig
            s_ref[:, cs] = (sl * bv).astype(_MM)
            ds = _mm_nt(dfb, wd_ref[cs, :])
            da = (ds * bv * (sig * (1.0 + av * (1.0 - sig)))).astype(_MM)
            db = (ds * sl).astype(_MM)
            da_ref[:, cs] = da
            db_ref[:, cs] = db
            dn = dn + _mm_nt(da, wg_ref[:, cs]) + _mm_nt(db, wu_ref[:, cs])
        t = dn * g1
        dh_ref[...] = dho + rh * (t - hh * jnp.mean(t * hh, axis=-1, keepdims=True))
        dg1_ref[...] += jnp.sum(dn * hh, axis=0, keepdims=True)

    row = lambda i: (i, 0)
    fix = lambda i: (0, 0)
    lay = lambda i: (li, 0, 0)
    td = pl.BlockSpec((ts, D), row)
    tf = pl.BlockSpec((ts, FF), row)
    gd = pl.BlockSpec((1, D), fix)
    return pl.pallas_call(
        body, name="ffn_bwd", grid=(S // ts,),
        in_specs=[td, td, tf, tf, td, gd, gd,
                  pl.BlockSpec((None, D, FF), lay), pl.BlockSpec((None, D, FF), lay), pl.BlockSpec((None, FF, D), lay)],
        out_specs=[td, td, tf, tf, tf, td, gd, gd],
        out_shape=[jax.ShapeDtypeStruct((S, D), F32), jax.ShapeDtypeStruct((S, D), _MM),
                   jax.ShapeDtypeStruct((S, FF), _MM), jax.ShapeDtypeStruct((S, FF), _MM),
                   jax.ShapeDtypeStruct((S, FF), _MM), jax.ShapeDtypeStruct((S, D), _MM),
                   jax.ShapeDtypeStruct((1, D), F32), jax.ShapeDtypeStruct((1, D), F32)],
        compiler_params=_cparams(("arbitrary",)),
    )(h, f, a, b, dho, g1, g2, wg, wu, wd)


def _wgrad(x, y):
    T, M = x.shape
    N = y.shape[1]
    tt = _row_tile(T, 512)
    bn = 512 if N % 512 == 0 else 256
    assert N % bn == 0

    def body(x_ref, y_ref, o_ref):
        @pl.when(pl.program_id(1) == 0)
        def _():
            o_ref[...] = jnp.zeros_like(o_ref)

        o_ref[...] += _mm_tn(x_ref[...], y_ref[...])

    return pl.pallas_call(
        body, name="wgrad", grid=(N // bn, T // tt),
        in_specs=[pl.BlockSpec((tt, M), lambda j, t: (t, 0)), pl.BlockSpec((tt, bn), lambda j, t: (t, j))],
        out_specs=pl.BlockSpec((M, bn), lambda j, t: (0, j)),
        out_shape=jax.ShapeDtypeStruct((M, N), F32),
        compiler_params=_cparams(("parallel", "arbitrary")),
    )(x, y)


def _proj_fwd(h, g, w, li, out_dtype):
    S = h.shape[0]
    N = w.shape[2]
    ts = _row_tile(S, 256)

    def body(h_ref, g_ref, w_ref, u_ref, p_ref):
        hv = h_ref[...]
        u = (hv * _rstd(hv) * g_ref[...]).astype(_MM)
        u_ref[...] = u
        p_ref[...] = _mm(u, w_ref[...]).astype(p_ref.dtype)

    return pl.pallas_call(
        body, name="proj_fwd", grid=(S // ts,),
        in_specs=[pl.BlockSpec((ts, D), lambda i: (i, 0)), pl.BlockSpec((1, D), lambda i: (0, 0)),
                  pl.BlockSpec((None, D, N), lambda i: (li, 0, 0))],
        out_specs=[pl.BlockSpec((ts, D), lambda i: (i, 0)), pl.BlockSpec((ts, N), lambda i: (i, 0))],
        out_shape=[jax.ShapeDtypeStruct((S, D), _MM), jax.ShapeDtypeStruct((S, N), out_dtype)],
        compiler_params=_cparams(("parallel",)),
    )(h, g, w)


def _proj_bwd(h, dho, g, w, li, pieces):
    S = h.shape[0]
    N = w.shape[2]
    ts = _row_tile(S, 256)
    widths = [p.shape[1] for p in pieces]
    assert sum(widths) == N
    npc = len(pieces)

    def body(*refs):
        h_ref, dho_ref, g_ref, w_ref = refs[:4]
        p_refs = refs[4:4 + npc]
        dh_ref, dp_ref, dg_ref = refs[4 + npc:]

        @pl.when(pl.program_id(0) == 0)
        def _():
            dg_ref[...] = jnp.zeros_like(dg_ref)

        du = jnp.zeros((ts, D), F32)
        off = 0
        for p_ref, n in zip(p_refs, widths):
            pv = p_ref[...].astype(_MM)
            dp_ref[:, off:off + n] = pv
            du = du + _mm_nt(pv, w_ref[:, off:off + n])
            off += n
        dx, dg = _rms_bwd(h_ref[...], g_ref[...], du)
        dh_ref[...] = dho_ref[...] + dx
        dg_ref[...] += dg

    row = lambda i: (i, 0)
    return pl.pallas_call(
        body, name="proj_bwd", grid=(S // ts,),
        in_specs=[pl.BlockSpec((ts, D), row), pl.BlockSpec((ts, D), row), pl.BlockSpec((1, D), lambda i: (0, 0)),
                  pl.BlockSpec((None, D, N), lambda i: (li, 0, 0))] + [pl.BlockSpec((ts, n), row) for n in widths],
        out_specs=[pl.BlockSpec((ts, D), row), pl.BlockSpec((ts, N), row), pl.BlockSpec((1, D), lambda i: (0, 0))],
        out_shape=[jax.ShapeDtypeStruct((S, D), F32), jax.ShapeDtypeStruct((S, N), _MM),
                   jax.ShapeDtypeStruct((1, D), F32)],
        compiler_params=_cparams(("arbitrary",)),
    )(h, dho, g, w, *pieces)


def _pool_counts(first_row, ts):
    pos = (first_row + lax.broadcasted_iota(jnp.int32, (ts, 1), 0) + 1).astype(F32)
    return [jnp.minimum(pos, float(w)) for w in WINDOWS]


def _pool_delta(x, prev, cnts, ts):
    xe = jnp.concatenate([prev, x], axis=0)
    sums = []
    cur = xe
    for sh in (1, 2, 4, 8):
        cur = cur + pltpu.roll(cur, sh, 0)
        sums.append(cur)
    return [sums[gi][HALO:, gi * PG:(gi + 1) * PG] / cnts[gi] - x[:, gi * PG:(gi + 1) * PG]
            for gi in range(len(WINDOWS))]


def _pool_fwd(proj, pw, ps):
    S = proj.shape[0]
    ts = _row_tile(S, 256)

    def body(x_ref, pw_ref, ps_ref, tok_ref, carry_ref):
        i = pl.program_id(0)

        @pl.when(i == 0)
        def _():
            carry_ref[...] = jnp.zeros_like(carry_ref)

        x = x_ref[...]
        ds = _pool_delta(x, carry_ref[...], _pool_counts(i * ts, ts), ts)
        y = jnp.concatenate([_mm(ds[gi], pw_ref[gi]) for gi in range(len(WINDOWS))], axis=-1)
        tok_ref[...] = (y * ps_ref[...]).astype(tok_ref.dtype)
        carry_ref[...] = x[ts - HALO:, :]

    return pl.pallas_call(
        body, name="pool_fwd", grid=(S // ts,),
        in_specs=[pl.BlockSpec((ts, DTOK), lambda i: (i, 0)), pl.BlockSpec((len(WINDOWS), PG, PG), lambda i: (0, 0, 0)),
                  pl.BlockSpec((1, DTOK), lambda i: (0, 0))],
        out_specs=pl.BlockSpec((ts, DTOK), lambda i: (i, 0)),
        out_shape=jax.ShapeDtypeStruct((S, DTOK), _MM),
        scratch_shapes=[pltpu.VMEM((HALO, DTOK), F32)],
        compiler_params=_cparams(("arbitrary",)),
    )(proj, pw, ps)


def _pool_bwd(proj, dtok, pw, ps):
    S = proj.shape[0]
    ts = _row_tile(S, 256)
    nt = S // ts
    per = ts // HALO
    ng = len(WINDOWS)

    def body(x_ref, xp_ref, dt_ref, pw_ref, ps_ref, dx_ref, dpw_ref, dps_ref, carry_ref):
        i = pl.program_id(0)
        idx = nt - 1 - i

        @pl.when(i == 0)
        def _():
            carry_ref[...] = jnp.zeros_like(carry_ref)
            dpw_ref[...] = jnp.zeros_like(dpw_ref)
            dps_ref[...] = jnp.zeros_like(dps_ref)

        x = x_ref[...]
        prev = jnp.where(idx > 0, xp_ref[...], 0.0)
        cnts = _pool_counts(idx * ts, ts)
        ds = _pool_delta(x, prev, cnts, ts)
        dt = dt_ref[...]
        y = jnp.concatenate([_mm(ds[gi], pw_ref[gi]) for gi in range(ng)], axis=-1)
        dps_ref[...] += jnp.sum(dt * y, axis=0, keepdims=True)
        dy = (dt * ps_ref[...]).astype(_MM)
        dds = []
        for gi in range(ng):
            dyg = dy[:, gi * PG:(gi + 1) * PG]
            dpw_ref[gi] += _mm_tn(ds[gi], dyg)
            dds.append(_mm_nt(dyg, pw_ref[gi]))
        e = jnp.concatenate([dds[gi] / cnts[gi] for gi in range(ng)], axis=-1)
        ee = jnp.concatenate([e, carry_ref[...]], axis=0)
        rows = ts + HALO
        cur = ee
        outs = []
        for gi, sh in enumerate((1, 2, 4, 8)):
            cur = cur + pltpu.roll(cur, rows - sh, 0)
            outs.append(cur[:ts, gi * PG:(gi + 1) * PG] - dds[gi])
        dx_ref[...] = jnp.concatenate(outs, axis=-1)
        carry_ref[...] = e[:HALO, :]

    rev = lambda i: (nt - 1 - i, 0)
    return pl.pallas_call(
        body, name="pool_bwd", grid=(nt,),
        in_specs=[pl.BlockSpec((ts, DTOK), rev),
                  pl.BlockSpec((HALO, DTOK), lambda i: (jnp.maximum((nt - 1 - i) * per - 1, 0), 0)),
                  pl.BlockSpec((ts, DTOK), rev), pl.BlockSpec((ng, PG, PG), lambda i: (0, 0, 0)),
                  pl.BlockSpec((1, DTOK), lambda i: (0, 0))],
        out_specs=[pl.BlockSpec((ts, DTOK), rev), pl.BlockSpec((ng, PG, PG), lambda i: (0, 0, 0)),
                   pl.BlockSpec((1, DTOK), lambda i: (0, 0))],
        out_shape=[jax.ShapeDtypeStruct((S, DTOK), F32), jax.ShapeDtypeStruct((ng, PG, PG), F32),
                   jax.ShapeDtypeStruct((1, DTOK), F32)],
        scratch_shapes=[pltpu.VMEM((HALO, DTOK), F32)],
        compiler_params=_cparams(("arbitrary",)),
    )(proj, proj, dtok, pw, ps)


def _tri(strict):
    r = lax.broadcasted_iota(jnp.int32, (QB, QB), 0)
    c = lax.broadcasted_iota(jnp.int32, (QB, QB), 1)
    return jnp.where(r > c if strict else r >= c, 1.0, 0.0).astype(_MM)


def _sb_terms(z, valid):
    e = jnp.exp(-jnp.abs(z))
    sp = jnp.log(1.0 + e)
    ls = jnp.minimum(z, 0.0) - sp
    lf = jnp.where(valid, jnp.minimum(-z, 0.0) - sp, 0.0)
    return ls, lf, e


def _sb_alive(cs):
    top = cs[0]
    for c in cs[1:]:
        top = jnp.maximum(top, c)
    return (jnp.max(top) > NEG_CUT).astype(jnp.int32)


def _sb_split(x2, head_a):
    zero = jnp.zeros_like(x2)
    return jnp.where(head_a, x2, zero), jnp.where(head_a, zero, x2)


SB_GROUPS_FWD = 4
SB_GROUPS_BWD = 4


def _sb_fwd(proj):
    S = proj.shape[0]
    G = min(SB_GROUPS_FWD, S // QB)
    nstep = S // (G * QB)
    npair = DTOK // 128
    nch = 2 * G

    def body(q_ref, k_ref, v_ref, tri_ref, o_ref):
        qb = pl.program_id(1)
        head_a = lax.broadcasted_iota(jnp.int32, (QB, 128), 1) < HD
        qs = []
        for g in range(G):
            qs.extend(_sb_split(q_ref[g * QB:(g + 1) * QB, :], head_a))
        row = lax.broadcasted_iota(jnp.int32, (QB, QB), 0)
        col = lax.broadcasted_iota(jnp.int32, (QB, QB), 1)
        tri = tri_ref[...]
        last = G * qb + G - 1

        def step(carry):
            i, _, cs, accs = carry
            causal = (col - row) < i * QB
            ks, vs, valids = [], [], []
            for g in range(G):
                kb = G * qb + g - i
                off = pl.multiple_of(jnp.maximum(kb, 0) * QB, QB)
                ks.append(k_ref[pl.ds(off, QB), :])
                vs.append(v_ref[pl.ds(off, QB), :])
                valids.append(jnp.logical_and(causal, kb >= 0))
            zs = [_mm_nt(qs[ch], ks[ch // 2]) * SCALE for ch in range(nch)]
            terms = [_sb_terms(zs[ch], valids[ch // 2]) for ch in range(nch)]
            withins = [_mm2(terms[ch][1], tri) for ch in range(nch)]
            new_c, new_acc = [], []
            for ch in range(nch):
                ls, lf, _ = terms[ch]
                a = jnp.where(valids[ch // 2], jnp.exp(ls + withins[ch] + cs[ch]), 0.0)
                new_acc.append(accs[ch] + _mm2(a, vs[ch // 2]))
                new_c.append(cs[ch] + jnp.sum(lf, axis=-1, keepdims=True))
            return i + 1, _sb_alive(new_c), tuple(new_c), tuple(new_acc)

        zc = jnp.zeros((QB, 1), F32)
        za = jnp.zeros((QB, 128), F32)
        more = lambda cr: jnp.logical_and(cr[0] <= last, cr[1] > 0)
        accs = lax.while_loop(more, step, (0, 1, (zc,) * nch, (za,) * nch))[3]
        for g in range(G):
            o_ref[g * QB:(g + 1) * QB, :] = jnp.where(head_a, accs[2 * g], accs[2 * g + 1])

    return pl.pallas_call(
        body, name="sb_fwd", grid=(npair, nstep),
        in_specs=[pl.BlockSpec((G * QB, 128), lambda p, i: (i, p)),
                  pl.BlockSpec((S, 128), lambda p, i: (0, npair + p)),
                  pl.BlockSpec((S, 128), lambda p, i: (0, 2 * npair + p)),
                  pl.BlockSpec((QB, QB), lambda p, i: (0, 0))],
        out_specs=pl.BlockSpec((G * QB, 128), lambda p, i: (i, p)),
        out_shape=jax.ShapeDtypeStruct((S, DTOK), F32),
        compiler_params=_cparams(("parallel", "parallel")),
    )(proj, proj, proj, _tri(True))


def _sb_bwd(proj, o, do):
    S = proj.shape[0]
    G = min(SB_GROUPS_BWD, S // QB)
    nstep = S // (G * QB)
    npair = DTOK // 128
    nch = 2 * G

    def body(q_ref, k_ref, v_ref, o_ref, do_ref, tri_ref, tri2_ref, dq_ref, dk_ref, dv_ref):
        qb = pl.program_id(1)

        @pl.when(qb == 0)
        def _():
            dk_ref[...] = jnp.zeros_like(dk_ref)
            dv_ref[...] = jnp.zeros_like(dv_ref)

        head_a = lax.broadcasted_iota(jnp.int32, (QB, 128), 1) < HD
        qs, dos, gs = [], [], []
        for g in range(G):
            rows = slice(g * QB, (g + 1) * QB)
            qs.extend(_sb_split(q_ref[rows, :], head_a))
            dob = do_ref[rows, :].astype(_MM)
            dos.extend(_sb_split(dob, head_a))
            go = dob.astype(F32) * o_ref[rows, :]
            gs.append(jnp.sum(jnp.where(head_a, go, 0.0), axis=-1, keepdims=True))
            gs.append(jnp.sum(jnp.where(head_a, 0.0, go), axis=-1, keepdims=True))
        row = lax.broadcasted_iota(jnp.int32, (QB, QB), 0)
        col = lax.broadcasted_iota(jnp.int32, (QB, QB), 1)
        tri = tri_ref[...]
        tri2 = tri2_ref[...]
        last = G * qb + G - 1

        def step(carry):
            i, _, cs, rs, dqs = carry
            causal = (col - row) < i * QB
            offs, ks, vs, valids = [], [], [], []
            for g in range(G):
                kb = G * qb + g - i
                off = pl.multiple_of(jnp.maximum(kb, 0) * QB, QB)
                offs.append(off)
                ks.append(k_ref[pl.ds(off, QB), :])
                vs.append(v_ref[pl.ds(off, QB), :])
                valids.append(jnp.logical_and(causal, kb >= 0))
            zs = [_mm_nt(qs[ch], ks[ch // 2]) * SCALE for ch in range(nch)]
            das = [_mm_nt(dos[ch], vs[ch // 2]) for ch in range(nch)]
            terms = [_sb_terms(zs[ch], valids[ch // 2]) for ch in range(nch)]
            withins = [_mm2(terms[ch][1], tri) for ch in range(nch)]
            a_s, dlws = [], []
            for ch in range(nch):
                a = jnp.where(valids[ch // 2], jnp.exp(terms[ch][0] + withins[ch] + cs[ch]), 0.0)
                a_s.append(a)
                dlws.append(das[ch] * a)
            sfx = [_mm2(dlws[ch], tri2) for ch in range(nch)]
            dzs = []
            for ch in range(nch):
                e = terms[ch][2]
                inv = 1.0 / (1.0 + e)
                pos = zs[ch] >= 0.0
                beta = jnp.where(pos, 1.0, e) * inv
                omb = jnp.where(pos, e, 1.0) * inv
                prefix = gs[ch] - rs[ch] - sfx[ch]
                dzs.append((jnp.where(valids[ch // 2], dlws[ch] * omb - beta * prefix, 0.0) * SCALE).astype(_MM))
            new_dq = [dqs[ch] + _mm(dzs[ch], ks[ch // 2]) for ch in range(nch)]
            for g in range(G):
                dk_ref[pl.ds(offs[g], QB), :] += (_mm_tn(dzs[2 * g], qs[2 * g]) + _mm_tn(dzs[2 * g + 1], qs[2 * g + 1]))
                dv_ref[pl.ds(offs[g], QB), :] += (_mm_tn(a_s[2 * g], dos[2 * g]) + _mm_tn(a_s[2 * g + 1], dos[2 * g + 1]))
            new_c = [cs[ch] + jnp.sum(terms[ch][1], axis=-1, keepdims=True) for ch in range(nch)]
            new_r = [rs[ch] + jnp.sum(dlws[ch], axis=-1, keepdims=True) for ch in range(nch)]
            return i + 1, _sb_alive(new_c), tuple(new_c), tuple(new_r), tuple(new_dq)

        zc = jnp.zeros((QB, 1), F32)
        za = jnp.zeros((QB, 128), F32)
        more = lambda cr: jnp.logical_and(cr[0] <= last, cr[1] > 0)
        dqs = lax.while_loop(more, step, (0, 1, (zc,) * nch, (zc,) * nch, (za,) * nch))[4]
        for g in range(G):
            dq_ref[g * QB:(g + 1) * QB, :] = jnp.where(head_a, dqs[2 * g], dqs[2 * g + 1])

    qspec = pl.BlockSpec((G * QB, 128), lambda p, i: (i, p))
    full = lambda base: pl.BlockSpec((S, 128), lambda p, i: (0, base + p))
    tspec = pl.BlockSpec((QB, QB), lambda p, i: (0, 0))
    return pl.pallas_call(
        body, name="sb_bwd", grid=(npair, nstep),
        in_specs=[qspec, full(npair), full(2 * npair), qspec, qspec, tspec, tspec],
        out_specs=[qspec, full(0), full(0)],
        out_shape=[jax.ShapeDtypeStruct((S, DTOK), F32)] * 3,
        compiler_params=_cparams(("parallel", "arbitrary")),
    )(proj, proj, proj, o, do, _tri(True), _tri(False))


def _memkv_fwd(mem, gm, wkv):
    nl = wkv.shape[0]

    def body(m_ref, g_ref, w_ref, kv_ref):
        m = m_ref[...]
        mn = (m * _rstd(m) * g_ref[...]).astype(_MM)
        kv_ref[...] = _mm(mn, w_ref[...]).astype(kv_ref.dtype)

    return pl.pallas_call(
        body, name="memkv_fwd", grid=(nl,),
        in_specs=[pl.BlockSpec((MEM_LEN, D), lambda l: (0, 0)), pl.BlockSpec((None, 1, D), lambda l: (l, 0, 0)),
                  pl.BlockSpec((None, D, 2 * DMEM), lambda l: (l, 0, 0))],
        out_specs=pl.BlockSpec((None, MEM_LEN, 2 * DMEM), lambda l: (l, 0, 0)),
        out_shape=jax.ShapeDtypeStruct((nl, MEM_LEN, 2 * DMEM), _MM),
        compiler_params=_cparams(("parallel",)),
    )(mem, gm, wkv)


def _memkv_bwd(mem, gm, wkv, dkv):
    nl = wkv.shape[0]

    def body(m_ref, g_ref, w_ref, dkv_ref, dw_ref, dg_ref):
        m = m_ref[...]
        mh = m * _rstd(m)
        mn = (mh * g_ref[...]).astype(_MM)
        dkv = dkv_ref[...].astype(_MM)
        dw_ref[...] = _mm_tn(mn, dkv)
        dmn = _mm_nt(dkv, w_ref[...])
        dg_ref[...] = jnp.sum(dmn * mh, axis=0, keepdims=True)

    return pl.pallas_call(
        body, name="memkv_bwd", grid=(nl,),
        in_specs=[pl.BlockSpec((MEM_LEN, D), lambda l: (0, 0)), pl.BlockSpec((None, 1, D), lambda l: (l, 0, 0)),
                  pl.BlockSpec((None, D, 2 * DMEM), lambda l: (l, 0, 0)),
                  pl.BlockSpec((None, MEM_LEN, 2 * DMEM), lambda l: (l, 0, 0))],
        out_specs=[pl.BlockSpec((None, D, 2 * DMEM), lambda l: (l, 0, 0)), pl.BlockSpec((None, 1, D), lambda l: (l, 0, 0))],
        out_shape=[jax.ShapeDtypeStruct((nl, D, 2 * DMEM), F32), jax.ShapeDtypeStruct((nl, 1, D), F32)],
        compiler_params=_cparams(("parallel",)),
    )(mem, gm, wkv, dkv)


def _mem_probs(qh, kp):
    s = _mm_nt(qh, kp) * SCALE
    p = jnp.exp(s - jnp.max(s, axis=-1, keepdims=True))
    return p / jnp.sum(p, axis=-1, keepdims=True)


def _mixout_fwd(h, tok, proj, kv, wo, g, li, qm_block):
    S = h.shape[0]
    ts = _row_tile(S, 256)

    def body(h_ref, tok_ref, qm_ref, kv_ref, wo_ref, g_ref, ho_ref, cat_ref, mix_ref):
        head_a = lax.broadcasted_iota(jnp.int32, (ts, 128), 1) < HD
        qm = qm_ref[...].astype(_MM)
        zq = jnp.zeros((ts, 128), _MM)
        mos = []
        for pi in range(DMEM // 128):
            cs = slice(pi * 128, (pi + 1) * 128)
            qp = qm[:, cs]
            kp = kv_ref[:, cs]
            vp = kv_ref[:, DMEM + pi * 128:DMEM + (pi + 1) * 128]
            pa = _mem_probs(jnp.where(head_a, qp, zq), kp)
            pb = _mem_probs(jnp.where(head_a, zq, qp), kp)
            mos.append(jnp.where(head_a, _mm(pa, vp), _mm(pb, vp)))
        cat = jnp.concatenate([tok_ref[...].astype(_MM)] + [m.astype(_MM) for m in mos], axis=-1)
        cat_ref[...] = cat
        mix = _mm(cat, wo_ref[...])
        mix_ref[...] = mix
        ho_ref[...] = h_ref[...] + mix * _rstd(mix) * g_ref[...]

    row = lambda i: (i, 0)
    return pl.pallas_call(
        body, name="mixout_fwd", grid=(S // ts,),
        in_specs=[pl.BlockSpec((ts, D), row), pl.BlockSpec((ts, DTOK), row),
                  pl.BlockSpec((ts, DMEM), lambda i: (i, qm_block)),
                  pl.BlockSpec((None, MEM_LEN, 2 * DMEM), lambda i: (li, 0, 0)),
                  pl.BlockSpec((None, DMIX, D), lambda i: (li, 0, 0)), pl.BlockSpec((1, D), lambda i: (0, 0))],
        out_specs=[pl.BlockSpec((ts, D), row), pl.BlockSpec((ts, DMIX), row), pl.BlockSpec((ts, D), row)],
        out_shape=[jax.ShapeDtypeStruct((S, D), F32), jax.ShapeDtypeStruct((S, DMIX), _MM),
                   jax.ShapeDtypeStruct((S, D), F32)],
        compiler_params=_cparams(("parallel",)),
    )(h, tok, proj, kv, wo, g)


def _mixout_bwd(dho, mix, proj, kv, wo, g, li, qm_block):
    S = dho.shape[0]
    ts = _row_tile(S, 256)

    def body(dho_ref, mix_ref, qm_ref, kv_ref, wo_ref, g_ref, dmix_ref, dtok_ref, dqm_ref, dkv_ref, dg_ref):
        @pl.when(pl.program_id(0) == 0)
        def _():
            dkv_ref[...] = jnp.zeros_like(dkv_ref)
            dg_ref[...] = jnp.zeros_like(dg_ref)

        dmix, dg = _rms_bwd(mix_ref[...], g_ref[...], dho_ref[...])
        dg_ref[...] += dg
        dmb = dmix.astype(_MM)
        dmix_ref[...] = dmb
        dcat = _mm_nt(dmb, wo_ref[...])
        dtok_ref[...] = dcat[:, :DTOK]
        head_a = lax.broadcasted_iota(jnp.int32, (ts, 128), 1) < HD
        qm = qm_ref[...].astype(_MM)
        zq = jnp.zeros((ts, 128), _MM)
        for pi in range(DMEM // 128):
            cs = slice(pi * 128, (pi + 1) * 128)
            vs = slice(DMEM + pi * 128, DMEM + (pi + 1) * 128)
            kp = kv_ref[:, cs]
            vp = kv_ref[:, vs]
            dmo = dcat[:, DTOK + pi * 128:DTOK + (pi + 1) * 128].astype(_MM)
            dqs = []
            dk_acc = jnp.zeros((MEM_LEN, 128), F32)
            dv_acc = jnp.zeros((MEM_LEN, 128), F32)
            for hd in range(2):
                qh = jnp.where(head_a, qm[:, cs], zq) if hd == 0 else jnp.where(head_a, zq, qm[:, cs])
                dmh = jnp.where(head_a, dmo, zq) if hd == 0 else jnp.where(head_a, zq, dmo)
                p = _mem_probs(qh, kp)
                dp = _mm_nt(dmh, vp)
                ds = (p * (dp - jnp.sum(dp * p, axis=-1, keepdims=True)) * SCALE).astype(_MM)
                dqs.append(_mm(ds, kp))
                dk_acc = dk_acc + _mm_tn(ds, qh)
                dv_acc = dv_acc + _mm_tn(p, dmh)
            dqm_ref[:, cs] = jnp.where(head_a, dqs[0], dqs[1]).astype(_MM)
            dkv_ref[:, cs] += dk_acc
            dkv_ref[:, vs] += dv_acc

    row = lambda i: (i, 0)
    fix = lambda i: (0, 0)
    return pl.pallas_call(
        body, name="mixout_bwd", grid=(S // ts,),
        in_specs=[pl.BlockSpec((ts, D), row), pl.BlockSpec((ts, D), row),
                  pl.BlockSpec((ts, DMEM), lambda i: (i, qm_block)),
                  pl.BlockSpec((None, MEM_LEN, 2 * DMEM), lambda i: (li, 0, 0)),
                  pl.BlockSpec((None, DMIX, D), lambda i: (li, 0, 0)), pl.BlockSpec((1, D), fix)],
        out_specs=[pl.BlockSpec((ts, D), row), pl.BlockSpec((ts, DTOK), row), pl.BlockSpec((ts, DMEM), row),
                   pl.BlockSpec((MEM_LEN, 2 * DMEM), fix), pl.BlockSpec((1, D), fix)],
        out_shape=[jax.ShapeDtypeStruct((S, D), _MM), jax.ShapeDtypeStruct((S, DTOK), F32),
                   jax.ShapeDtypeStruct((S, DMEM), _MM), jax.ShapeDtypeStruct((MEM_LEN, 2 * DMEM), F32),
                   jax.ShapeDtypeStruct((1, D), F32)],
        compiler_params=_cparams(("arbitrary",)),
    )(dho, mix, proj, kv, wo, g)


def _loss_grad(y, target):
    S = y.shape[0]
    ts = _row_tile(S, 512)

    def body(y_ref, t_ref, acc_ref, dy_ref):
        @pl.when(pl.program_id(0) == 0)
        def _():
            acc_ref[...] = jnp.zeros_like(acc_ref)

        err = y_ref[...] - t_ref[...]
        acc_ref[...] += jnp.sum(err * err, axis=0, keepdims=True)
        dy_ref[...] = err * (1.0 / D)

    row = lambda i: (i, 0)
    return pl.pallas_call(
        body, name="loss_grad", grid=(S // ts,),
        in_specs=[pl.BlockSpec((ts, D), row), pl.BlockSpec((ts, D), row)],
        out_specs=[pl.BlockSpec((1, D), lambda i: (0, 0)), pl.BlockSpec((ts, D), row)],
        out_shape=[jax.ShapeDtypeStruct((1, D), F32), jax.ShapeDtypeStruct((S, D), F32)],
        compiler_params=_cparams(("arbitrary",)),
    )(y, target)


def _local_step(x, mem, target, W):
    gpre, gpost = W["g_pre"], W["g_post"]
    gm = W["g_mem"].reshape(DEPTH, 1, D)
    g_at = lambda g, i, k: g[i, k].reshape(1, D)
    kv = _memkv_fwd(mem, gm, W["w_mem_kv"])

    saved = []
    h = x
    for i in range(DEPTH):
        j = i // 2
        sv = {"h0": h}
        h, sv["a1"], sv["b1"], sv["f1"] = _ffn_fwd(h, g_at(gpre, i, 0), g_at(gpost, i, 0),
                                                   W["ffn1_gate"], W["ffn1_up"], W["ffn1_down"], i)
        sv["h1"] = h
        if i % 2 == 0:
            sv["u"], sv["proj"] = _proj_fwd(h, g_at(gpre, i, 1), W["w_in_pool"], j, F32)
            tok = _pool_fwd(sv["proj"], W["pool_w"][j], W["pool_scale"][j].reshape(1, DTOK))
            qm_block = DTOK // DMEM
        else:
            sv["u"], sv["proj"] = _proj_fwd(h, g_at(gpre, i, 1), W["w_in_sb"], j, _MM)
            tok = sv["o"] = _sb_fwd(sv["proj"])
            qm_block = 3 * DTOK // DMEM
        h, sv["cat"], sv["mix"] = _mixout_fwd(h, tok, sv["proj"], kv, W["w_out"], g_at(gpost, i, 1), i, qm_block)
        sv["h2"] = h
        h, sv["a2"], sv["b2"], sv["f2"] = _ffn_fwd(h, g_at(gpre, i, 2), g_at(gpost, i, 2),
                                                   W["ffn2_gate"], W["ffn2_up"], W["ffn2_down"], i)
        saved.append(sv)

    acc, dh = _loss_grad(h, target)
    loss = 0.5 / D * jnp.sum(acc)

    per_layer = {k: [None] * DEPTH for k in ("ffn1_gate", "ffn1_up", "ffn1_down", "ffn2_gate", "ffn2_up",
                                              "ffn2_down", "w_out")}
    d_in_pool, d_in_sb = [None] * 2, [None] * 2
    d_pool_w, d_pool_scale = [None] * 2, [None] * 2
    dgpre = [[None] * 3 for _ in range(DEPTH)]
    dgpost = [[None] * 3 for _ in range(DEPTH)]
    dkv = [None] * DEPTH

    def ffn_back(i, which, dh, h_in, sv):
        k = 0 if which == 1 else 2
        n = f"ffn{which}"
        dh, nb, sb, da, db, df, dg1, dg2 = _ffn_bwd(
            h_in, sv[f"f{which}"], sv[f"a{which}"], sv[f"b{which}"], dh, g_at(gpre, i, k), g_at(gpost, i, k),
            W[n + "_gate"], W[n + "_up"], W[n + "_down"], i)
        per_layer[n + "_gate"][i] = _wgrad(nb, da)
        per_layer[n + "_up"][i] = _wgrad(nb, db)
        per_layer[n + "_down"][i] = _wgrad(sb, df)
        dgpre[i][k], dgpost[i][k] = dg1, dg2
        return dh

    for i in reversed(range(DEPTH)):
        j = i // 2
        sv = saved[i]
        dh = ffn_back(i, 2, dh, sv["h2"], sv)
        qm_block = DTOK // DMEM if i % 2 == 0 else 3 * DTOK // DMEM
        dmix, dtok, dqm, dkv[i], dgpost[i][1] = _mixout_bwd(dh, sv["mix"], sv["proj"], kv, W["w_out"],
                                                            g_at(gpost, i, 1), i, qm_block)
        per_layer["w_out"][i] = _wgrad(sv["cat"], dmix)
        if i % 2 == 0:
            dx, d_pool_w[j], dps = _pool_bwd(sv["proj"], dtok, W["pool_w"][j], W["pool_scale"][j].reshape(1, DTOK))
            d_pool_scale[j] = dps.reshape(DTOK)
            dh, dproj, dgpre[i][1] = _proj_bwd(sv["h1"], dh, g_at(gpre, i, 1), W["w_in_pool"], j, [dx, dqm])
            d_in_pool[j] = _wgrad(sv["u"], dproj)
        else:
            dq, dk, dv = _sb_bwd(sv["proj"], sv["o"], dtok)
            dh, dproj, dgpre[i][1] = _proj_bwd(sv["h1"], dh, g_at(gpre, i, 1), W["w_in_sb"], j, [dq, dk, dv, dqm])
            d_in_sb[j] = _wgrad(sv["u"], dproj)
        dh = ffn_back(i, 1, dh, sv["h0"], sv)

    dwkv, dgm = _memkv_bwd(mem, gm, W["w_mem_kv"], jnp.stack(dkv))
    grads = {k: jnp.stack(v) for k, v in per_layer.items()}
    grads["w_in_pool"] = jnp.stack(d_in_pool)
    grads["w_in_sb"] = jnp.stack(d_in_sb)
    grads["w_mem_kv"] = dwkv
    grads["pool_w"] = jnp.stack(d_pool_w)
    grads["pool_scale"] = jnp.stack(d_pool_scale)
    grads["g_mem"] = dgm.reshape(DEPTH, D)
    grads["g_pre"] = jnp.stack([jnp.concatenate(r, axis=0) for r in dgpre])
    grads["g_post"] = jnp.stack([jnp.concatenate(r, axis=0) for r in dgpost])
    return loss, dh, grads


ANY = pl.BlockSpec(memory_space=pl.ANY)


def _place():
    return lax.axis_index("x"), lax.axis_index("y"), lax.axis_index("c")


def _flip(v, bit):
    return 1 - v if bit else v


def _small_allgather(buf):
    R = buf.shape[0]

    def body(x_ref, o_ref, send_sems, recv_sems, loc_sem):
        x, y, c = _place()
        me = 4 * x + 2 * y + c
        loc = pltpu.make_async_copy(x_ref, o_ref.at[me], loc_sem)
        loc.start()
        sends = []
        for k in range(1, N_DEV):
            peer = (_flip(x, k & 4), _flip(y, k & 2), _flip(c, k & 1))
            cp = pltpu.make_async_remote_copy(src_ref=x_ref, dst_ref=o_ref.at[me], send_sem=send_sems.at[k - 1],
                                              recv_sem=recv_sems.at[k - 1], device_id=peer, device_id_type=MESH)
            cp.start()
            sends.append(cp)
        for k in range(1, N_DEV):
            peer = (_flip(x, k & 4), _flip(y, k & 2), _flip(c, k & 1))
            slot = 4 * peer[0] + 2 * peer[1] + peer[2]
            pltpu.make_async_remote_copy(src_ref=x_ref, dst_ref=o_ref.at[slot], send_sem=send_sems.at[k - 1],
                                         recv_sem=recv_sems.at[k - 1], device_id=peer, device_id_type=MESH).wait_recv()
        for cp in sends:
            cp.wait_send()
        loc.wait()

    return pl.pallas_call(
        body, name="small_allgather", in_specs=[ANY], out_specs=ANY,
        out_shape=jax.ShapeDtypeStruct((N_DEV, R, 128), buf.dtype),
        scratch_shapes=[pltpu.SemaphoreType.DMA((N_DEV - 1,)), pltpu.SemaphoreType.DMA((N_DEV - 1,)),
                        pltpu.SemaphoreType.DMA],
    )(buf)


def _gather_weights(shards):
    n = len(shards)
    halves = [s.shape[0] // 2 for s in shards]

    def body(*refs):
        ins, outs = refs[:n], refs[n:2 * n]
        ici_send, ici_recv, d2d_send, d2d_recv = refs[2 * n:]
        x, y, c = _place()
        me = 2 * x + y
        sib = (x, y, 1 - c)
        chips = [(_flip(x, k & 2), _flip(y, k & 1)) for k in (1, 2, 3)]

        def half(t, chip_idx, hc):
            return outs[t].at[chip_idx, pl.ds(hc * halves[t], halves[t])]

        sends = []
        for t in range(n):
            for k, chip in enumerate(chips):
                cp = pltpu.make_async_remote_copy(
                    src_ref=ins[t].at[pl.ds(c * halves[t], halves[t])], dst_ref=half(t, me, c),
                    send_sem=ici_send.at[3 * t + k], recv_sem=ici_recv.at[3 * t + k],
                    device_id=(chip[0], chip[1], c), device_id_type=MESH)
                cp.start()
                sends.append(cp)
        for t in range(n):
            for k, chip in enumerate(chips):
                idx = 2 * chip[0] + chip[1]
                pltpu.make_async_remote_copy(
                    src_ref=half(t, idx, c), dst_ref=half(t, idx, c), send_sem=ici_send.at[3 * t + k],
                    recv_sem=ici_recv.at[3 * t + k], device_id=(chip[0], chip[1], c), device_id_type=MESH).wait_recv()
                cp = pltpu.make_async_remote_copy(
                    src_ref=half(t, idx, c), dst_ref=half(t, idx, c), send_sem=d2d_send.at[3 * t + k],
                    recv_sem=d2d_recv.at[3 * t + k], device_id=sib, device_id_type=MESH)
                cp.start()
                sends.append(cp)
        for t in range(n):
            for k, chip in enumerate(chips):
                idx = 2 * chip[0] + chip[1]
                pltpu.make_async_remote_copy(
                    src_ref=half(t, idx, 1 - c), dst_ref=half(t, idx, 1 - c), send_sem=d2d_send.at[3 * t + k],
                    recv_sem=d2d_recv.at[3 * t + k], device_id=sib, device_id_type=MESH).wait_recv()
        for cp in sends:
            cp.wait_send()

    dma = lambda m: pltpu.SemaphoreType.DMA((m,))
    return pl.pallas_call(
        body, name="gather_weights", in_specs=[ANY] * n, out_specs=[ANY] * n,
        out_shape=[jax.ShapeDtypeStruct((N_CHIPS,) + s.shape, s.dtype) for s in shards],
        scratch_shapes=[dma(3 * n), dma(3 * n), dma(3 * n), dma(3 * n)],
    )(*shards)


def _rs_to_sibling(gs):
    n = len(gs)
    halves = [g.shape[1] // 2 for g in gs]

    def body(*refs):
        ins, outs = refs[:n], refs[n:2 * n]
        send_sems, recv_sems = refs[2 * n:]
        x, y, c = _place()
        cps = []
        for t in range(n):
            cp = pltpu.make_async_remote_copy(
                src_ref=ins[t].at[:, pl.ds((1 - c) * halves[t], halves[t])], dst_ref=outs[t],
                send_sem=send_sems.at[t], recv_sem=recv_sems.at[t], device_id=(x, y, 1 - c), device_id_type=MESH)
            cp.start()
            cps.append(cp)
        for cp in cps:
            cp.wait()

    dma = lambda m: pltpu.SemaphoreType.DMA((m,))
    return pl.pallas_call(
        body, name="rs_to_sibling", in_specs=[ANY] * n, out_specs=[ANY] * n,
        out_shape=[jax.ShapeDtypeStruct((N_CHIPS, g.shape[1] // 2) + g.shape[2:], g.dtype) for g in gs],
        scratch_shapes=[dma(n), dma(n)],
    )(*gs)


def _rs_to_chips(ss):
    n = len(ss)

    def body(*refs):
        ins, outs = refs[:n], refs[n:2 * n]
        send_sems, recv_sems = refs[2 * n:]
        x, y, c = _place()
        chips = [(_flip(x, k & 2), _flip(y, k & 1)) for k in (1, 2, 3)]
        cps = []
        for t in range(n):
            for k, chip in enumerate(chips):
                cp = pltpu.make_async_remote_copy(
                    src_ref=ins[t].at[2 * chip[0] + chip[1]], dst_ref=outs[t].at[k],
                    send_sem=send_sems.at[3 * t + k], recv_sem=recv_sems.at[3 * t + k],
                    device_id=(chip[0], chip[1], c), device_id_type=MESH)
                cp.start()
                cps.append(cp)
        for cp in cps:
            cp.wait()

    dma = lambda m: pltpu.SemaphoreType.DMA((m,))
    return pl.pallas_call(
        body, name="rs_to_chips", in_specs=[ANY] * n, out_specs=[ANY] * n,
        out_shape=[jax.ShapeDtypeStruct((3,) + s.shape[1:], s.dtype) for s in ss],
        scratch_shapes=[dma(3 * n), dma(3 * n)],
    )(*ss)


def _share_halves(ts):
    n = len(ts)

    def body(*refs):
        outs = refs[n:2 * n]
        send_sems, recv_sems = refs[2 * n:]
        x, y, c = _place()
        cps = []
        for t in range(n):
            cp = pltpu.make_async_remote_copy(
                src_ref=outs[t].at[c], dst_ref=outs[t].at[c], send_sem=send_sems.at[t], recv_sem=recv_sems.at[t],
                device_id=(x, y, 1 - c), device_id_type=MESH)
            cp.start()
            cps.append(cp)
        for t in range(n):
            pltpu.make_async_remote_copy(
                src_ref=outs[t].at[1 - c], dst_ref=outs[t].at[1 - c], send_sem=send_sems.at[t],
                recv_sem=recv_sems.at[t], device_id=(x, y, 1 - c), device_id_type=MESH).wait_recv()
        for cp in cps:
            cp.wait_send()

    dma = lambda m: pltpu.SemaphoreType.DMA((m,))
    return pl.pallas_call(
        body, name="share_halves", in_specs=[ANY] * n, out_specs=[ANY] * n,
        out_shape=[jax.ShapeDtypeStruct(t.shape, t.dtype) for t in ts],
        input_output_aliases={t: t for t in range(n)},
        scratch_shapes=[dma(n), dma(n)],
    )(*ts)


def _rows(a, lead):
    return a.reshape(a.shape[:lead] + (-1, a.shape[-1]))


def _add_sibling(g, r, c):
    L_ = g.shape[1]
    g3 = _rows(g.reshape((N_CHIPS, 2, L_ // 2) + g.shape[2:]), 2)
    r3 = _rows(r, 1)
    R_, C = r3.shape[1:]
    tr = _row_tile(R_, 512)

    def body(c_ref, g_ref, r_ref, o_ref):
        o_ref[...] = (g_ref[...] + r_ref[...]).astype(o_ref.dtype)

    out = pl.pallas_call(
        body, name="add_sibling",
        grid_spec=pltpu.PrefetchScalarGridSpec(
            num_scalar_prefetch=1, grid=(N_CHIPS, R_ // tr),
            in_specs=[pl.BlockSpec((None, None, tr, C), lambda s, i, c_ref: (s, c_ref[0], i, 0)),
                      pl.BlockSpec((None, tr, C), lambda s, i, c_ref: (s, i, 0))],
            out_specs=pl.BlockSpec((None, tr, C), lambda s, i, c_ref: (s, i, 0))),
        out_shape=jax.ShapeDtypeStruct((N_CHIPS, R_, C), _MM),
        compiler_params=_cparams(("parallel", "parallel")),
    )(c.reshape(1), g3, r3)
    return out.reshape((N_CHIPS, L_ // 2) + g.shape[2:])


def _add_chips(s, r, me, c):
    s3, r3 = _rows(s, 1), _rows(r, 1)
    R_, C = s3.shape[1:]
    tr = _row_tile(R_, 512)

    def body(at_ref, s_ref, r0_ref, r1_ref, r2_ref, o_ref):
        f = lambda ref: ref[...].astype(F32)
        o_ref[...] = ((f(s_ref) + f(r0_ref)) + f(r1_ref)) + f(r2_ref)

    rk = lambda k: pl.BlockSpec((None, tr, C), lambda i, at_ref: (k, i, 0))
    out = pl.pallas_call(
        body, name="add_chips",
        grid_spec=pltpu.PrefetchScalarGridSpec(
            num_scalar_prefetch=1, grid=(R_ // tr,),
            in_specs=[pl.BlockSpec((None, tr, C), lambda i, at_ref: (at_ref[0], i, 0)), rk(0), rk(1), rk(2)],
            out_specs=pl.BlockSpec((None, tr, C), lambda i, at_ref: (at_ref[1], i, 0))),
        out_shape=jax.ShapeDtypeStruct((2, R_, C), F32),
        compiler_params=_cparams(("parallel",)),
    )(jnp.stack([me, c]), s3, r3, r3, r3)
    return out.reshape((2,) + s.shape[1:])


def _sum_devices(a):
    R_ = a.shape[1]

    def body(a_ref, o_ref):
        acc = a_ref[0]
        for k in range(1, N_DEV):
            acc = acc + a_ref[k]
        o_ref[...] = acc

    return pl.pallas_call(
        body, name="sum_devices", out_shape=jax.ShapeDtypeStruct((R_, 128), F32),
        compiler_params=pltpu.CompilerParams(vmem_limit_bytes=VMEM_LIMIT),
    )(a)


def _adamw(w, g, m, v):
    shape = w.shape
    to2 = lambda a: a.reshape(-1, shape[-1])
    w2, g2, m2, v2 = to2(w), to2(g), to2(m), to2(v)
    R_, C = w2.shape
    tr = 256 if R_ % 256 == 0 else R_

    def body(w_ref, g_ref, m_ref, v_ref, d_ref, mo_ref, vo_ref):
        gv = g_ref[...]
        mn = ADAM_B1 * m_ref[...] + (1.0 - ADAM_B1) * gv
        vn = ADAM_B2 * v_ref[...] + (1.0 - ADAM_B2) * (gv * gv)
        m_hat = mn / (1.0 - ADAM_B1 ** ADAM_STEP)
        v_hat = vn / (1.0 - ADAM_B2 ** ADAM_STEP)
        d_ref[...] = -ADAM_LR * (m_hat / (jnp.sqrt(v_hat) + ADAM_EPS) + ADAM_WD * w_ref[...])
        mo_ref[...] = mn
        vo_ref[...] = vn

    spec = pl.BlockSpec((tr, C), lambda i: (i, 0))
    outs = pl.pallas_call(
        body, name="adamw", grid=(R_ // tr,), in_specs=[spec] * 4, out_specs=[spec] * 3,
        out_shape=[jax.ShapeDtypeStruct((R_, C), F32)] * 3,
        compiler_params=_cparams(("parallel",)),
    )(w2, g2, m2, v2)
    return [o.reshape(shape) for o in outs]


SHARD_AXIS = {"g_pre": 2, "g_post": 2, "g_mem": None, "ffn1_gate": 2, "ffn1_up": 2, "ffn1_down": 1,
              "ffn2_gate": 2, "ffn2_up": 2, "ffn2_down": 1, "w_in_pool": 1, "pool_w": None, "pool_scale": None,
              "w_in_sb": 2, "w_mem_kv": 1, "w_out": 2}
WEIGHTS = list(SHARD_AXIS)
BIG = [k for k in WEIGHTS if SHARD_AXIS[k] is not None and k not in ("g_pre", "g_post")]
SMALL = [k for k in WEIGHTS if k not in BIG]


def _unshard(g, axis):
    _, L_, r, N = g.shape
    if axis == 1:
        return jnp.transpose(g, (1, 0, 2, 3)).reshape(L_, N_CHIPS * r, N)
    return jnp.transpose(g, (1, 2, 0, 3)).reshape(L_, r, N_CHIPS * N)


def _to_shards(a, axis):
    L_, r, N = a.shape
    if axis == 1:
        return jnp.transpose(a.reshape(L_, N_CHIPS, r // N_CHIPS, N), (1, 0, 2, 3))
    return jnp.transpose(a.reshape(L_, r, N_CHIPS, N // N_CHIPS), (2, 0, 1, 3))


def _pack128(arrs):
    flat = jnp.concatenate([a.reshape(-1) for a in arrs])
    pad = (-flat.shape[0]) % (8 * 128)
    return jnp.pad(flat, (0, pad)).reshape(-1, 128)


def _unpack128(buf, shapes):
    flat = buf.reshape(-1)
    out, off = [], 0
    for shp in shapes:
        size = 1
        for d in shp:
            size *= d
        out.append(flat[off:off + size].reshape(shp))
        off += size
    return out


def kernel(x, mem, g_pre, g_post, g_mem, ffn1_gate, ffn1_up, ffn1_down, ffn2_gate, ffn2_up, ffn2_down, w_in_pool, pool_w, pool_scale, w_in_sb, w_mem_kv, w_out, loss_target, m_g_pre, m_g_post, m_g_mem, m_ffn1_gate, m_ffn1_up, m_ffn1_down, m_ffn2_gate, m_ffn2_up, m_ffn2_down, m_w_in_pool, m_pool_w, m_pool_scale, m_w_in_sb, m_w_mem_kv, m_w_out, v_g_pre, v_g_post, v_g_mem, v_ffn1_gate, v_ffn1_up, v_ffn1_down, v_ffn2_gate, v_ffn2_up, v_ffn2_down, v_w_in_pool, v_pool_w, v_pool_scale, v_w_in_sb, v_w_mem_kv, v_w_out):
    w = dict(g_pre=g_pre, g_post=g_post, g_mem=g_mem, ffn1_gate=ffn1_gate, ffn1_up=ffn1_up, ffn1_down=ffn1_down,
             ffn2_gate=ffn2_gate, ffn2_up=ffn2_up, ffn2_down=ffn2_down, w_in_pool=w_in_pool, pool_w=pool_w,
             pool_scale=pool_scale, w_in_sb=w_in_sb, w_mem_kv=w_mem_kv, w_out=w_out)
    m = dict(g_pre=m_g_pre, g_post=m_g_post, g_mem=m_g_mem, ffn1_gate=m_ffn1_gate, ffn1_up=m_ffn1_up,
             ffn1_down=m_ffn1_down, ffn2_gate=m_ffn2_gate, ffn2_up=m_ffn2_up, ffn2_down=m_ffn2_down,
             w_in_pool=m_w_in_pool, pool_w=m_pool_w, pool_scale=m_pool_scale, w_in_sb=m_w_in_sb,
             w_mem_kv=m_w_mem_kv, w_out=m_w_out)
    v = dict(g_pre=v_g_pre, g_post=v_g_post, g_mem=v_g_mem, ffn1_gate=v_ffn1_gate, ffn1_up=v_ffn1_up,
             ffn1_down=v_ffn1_down, ffn2_gate=v_ffn2_gate, ffn2_up=v_ffn2_up, ffn2_down=v_ffn2_down,
             w_in_pool=v_w_in_pool, pool_w=v_pool_w, pool_scale=v_pool_scale, w_in_sb=v_w_in_sb,
             w_mem_kv=v_w_mem_kv, w_out=v_w_out)
    cx, cy, cc = _place()
    chip = (2 * cx + cy).astype(jnp.int32)
    core = cc.astype(jnp.int32)

    own = [w[k].astype(_MM) for k in BIG]
    gathered = _gather_weights(own)
    slot = lambda s: lax.broadcasted_iota(jnp.int32, (N_CHIPS,) + (1,) * s.ndim, 0)
    full = {k: _unshard(jnp.where(slot(s) == chip, s[None], g), SHARD_AXIS[k]) for k, s, g in zip(BIG, own, gathered)}
    gains = _small_allgather(_pack128([g_pre, g_post]))
    per_chip = [_unpack128(gains[2 * j], [g_pre.shape, g_post.shape]) for j in range(N_CHIPS)]
    full["g_pre"] = jnp.concatenate([p[0] for p in per_chip], axis=2)
    full["g_post"] = jnp.concatenate([p[1] for p in per_chip], axis=2)
    full["g_mem"], full["pool_w"], full["pool_scale"] = g_mem, pool_w, pool_scale

    loss, dx, grads = _local_step(x[0], mem[0], loss_target[0], full)
    loss = lax.psum(loss, ("x", "y", "c"))

    gs = [_to_shards(grads[k], SHARD_AXIS[k]) for k in BIG]
    from_sib = _rs_to_sibling(gs)
    chip_sums = [_add_sibling(g, r, core) for g, r in zip(gs, from_sib)]
    from_chips = _rs_to_chips(chip_sums)
    mine = [_add_chips(s, r, chip, core) for s, r in zip(chip_sums, from_chips)]
    shared = _share_halves(mine)
    red = {k: s.reshape(w[k].shape) for k, s in zip(BIG, shared)}

    small_shapes = [grads[k].shape for k in SMALL]
    summed = _unpack128(_sum_devices(_small_allgather(_pack128([grads[k] for k in SMALL]))), small_shapes)
    for k, s in zip(SMALL, summed):
        if SHARD_AXIS[k] is None:
            red[k] = s
        else:
            red[k] = lax.dynamic_slice_in_dim(s, chip * w[k].shape[2], w[k].shape[2], axis=2)

    deltas, new_m, new_v = {}, {}, {}
    for k in WEIGHTS:
        deltas[k], new_m[k], new_v[k] = _adamw(w[k], red[k], m[k], v[k])
    return (loss, dx.reshape(x.shape), *[red[k] for k in WEIGHTS], *[deltas[k] for k in WEIGHTS],
            *[new_m[k] for k in WEIGHTS], *[new_v[k] for k in WEIGHTS])
```

```python
import jax
import jax.numpy as jnp
from jax import lax
from jax.experimental import pallas as pl
from jax.experimental.pallas import tpu as pltpu

F32 = jnp.float32
BF16 = jnp.bfloat16
_MM = jnp.bfloat16

D = 1024
FF = 2048
DTOK = 512
DMEM = 256
DMIX = DTOK + DMEM
DSB = 3 * DTOK + DMEM
HD = 64
MEM_LEN = 256
DEPTH = 4
EPS = 1e-6
WINDOWS = (2, 4, 8, 16)
HALO = 16
PG = 128
QB = 128
SCALE = HD ** -0.5
FC = 512
NEG_CUT = -104.0

ADAM_LR, ADAM_B1, ADAM_B2, ADAM_EPS, ADAM_WD, ADAM_STEP = 0.001, 0.9, 0.999, 1e-08, 0.01, 10

VMEM_LIMIT = 56 * 2 ** 20
MESH = pl.DeviceIdType.MESH
N_CHIPS = 4
N_DEV = 8


def _cparams(sem):
    return pltpu.CompilerParams(dimension_semantics=sem, vmem_limit_bytes=VMEM_LIMIT)


def _mm(a, b):
    return jnp.dot(a.astype(_MM), b.astype(_MM), preferred_element_type=F32)


def _mm_nt(a, b):
    return lax.dot_general(a.astype(_MM), b.astype(_MM), (((1,), (1,)), ((), ())), preferred_element_type=F32)


def _mm_tn(a, b):
    return lax.dot_general(a.astype(_MM), b.astype(_MM), (((0,), (0,)), ((), ())), preferred_element_type=F32)


def _mm2(a, b):
    hi = a.astype(_MM)
    lo = (a - hi.astype(F32)).astype(_MM)
    return jnp.dot(hi, b, preferred_element_type=F32) + jnp.dot(lo, b, preferred_element_type=F32)


def _rstd(x):
    return lax.rsqrt(jnp.mean(x * x, axis=-1, keepdims=True) + EPS)


def _rms_bwd(x, g, dy):
    r = _rstd(x)
    xh = x * r
    t = dy * g
    dx = r * (t - xh * jnp.mean(t * xh, axis=-1, keepdims=True))
    return dx, jnp.sum(dy * xh, axis=0, keepdims=True)


def _sigmoid(a):
    return 1.0 / (1.0 + jnp.exp(-a))


def _row_tile(n, want):
    t = min(n, want)
    assert n % t == 0, (n, t)
    return t


def _ffn_fwd(h, g1, g2, wg, wu, wd):
    S = h.shape[0]
    ts = _row_tile(S, 256)

    def body(h_ref, g1_ref, g2_ref, wg_ref, wu_ref, wd_ref, ho_ref, a_ref, b_ref, f_ref):
        hv = h_ref[...]
        n = (hv * _rstd(hv) * g1_ref[...]).astype(_MM)
        f = jnp.zeros((ts, D), F32)
        for j in range(FF // FC):
            cs = slice(j * FC, (j + 1) * FC)
            a = _mm(n, wg_ref[j])
            b = _mm(n, wu_ref[j])
            a_ref[:, cs] = a.astype(a_ref.dtype)
            b_ref[:, cs] = b.astype(b_ref.dtype)
            f = f + _mm(a * _sigmoid(a) * b, wd_ref[j])
        f_ref[...] = f
        ho_ref[...] = hv + 0.5 * (f * _rstd(f) * g2_ref[...])

    row = lambda i: (i, 0)
    fix = lambda i: (0, 0)
    all3 = lambda i: (0, 0, 0)
    ns = FF // FC
    return pl.pallas_call(
        body, name="ffn_fwd", grid=(S // ts,),
        in_specs=[pl.BlockSpec((ts, D), row), pl.BlockSpec((1, D), fix), pl.BlockSpec((1, D), fix),
                  pl.BlockSpec((ns, D, FC), all3), pl.BlockSpec((ns, D, FC), all3), pl.BlockSpec((ns, FC, D), all3)],
        out_specs=[pl.BlockSpec((ts, D), row), pl.BlockSpec((ts, FF), row), pl.BlockSpec((ts, FF), row),
                   pl.BlockSpec((ts, D), row)],
        out_shape=[jax.ShapeDtypeStruct((S, D), F32), jax.ShapeDtypeStruct((S, FF), _MM),
                   jax.ShapeDtypeStruct((S, FF), _MM), jax.ShapeDtypeStruct((S, D), F32)],
        compiler_params=_cparams(("parallel",)),
    )(h, g1, g2, wg, wu, wd)


def _ffn_bwd(h, f, a, b, dho, g1, g2, wg, wu, wd):
    S = h.shape[0]
    ts = _row_tile(S, 256)

    def body(h_ref, f_ref, a_ref, b_ref, dho_ref, g1_ref, g2_ref, wg_ref, wu_ref, wd_ref,
             dh_ref, n_ref, s_ref, da_ref, db_ref, df_ref, dg1_ref, dg2_ref):
        @pl.when(pl.program_id(0) == 0)
        def _():
            dg1_ref[...] = jnp.zeros_like(dg1_ref)
            dg2_ref[...] = jnp.zeros_like(dg2_ref)

        hv = h_ref[...]
        dho = dho_ref[...]
        df, dg2 = _rms_bwd(f_ref[...], g2_ref[...], 0.5 * dho)
        dg2_ref[...] += dg2
        dfb = df.astype(_MM)
        df_ref[...] = dfb
        rh = _rstd(hv)
        hh = hv * rh
        g1 = g1_ref[...]
        n_ref[...] = (hh * g1).astype(_MM)
        dn = jnp.zeros((ts, D), F32)
        for j in range(FF // FC):
            cs = slice(j * FC, (j + 1) * FC)
            av = a_ref[:, cs].astype(F32)
            bv = b_ref[:, cs].astype(F32)
            sig = _sigmoid(av)
            sl = av * sig
            s_ref[:, cs] = (sl * bv).astype(_MM)
            ds = _mm_nt(dfb, wd_ref[j])
            da = (ds * bv * (sig * (1.0 + av * (1.0 - sig)))).astype(_MM)
            db = (ds * sl).astype(_MM)
            da_ref[:, cs] = da
            db_ref[:, cs] = db
            dn = dn + _mm_nt(da, wg_ref[j]) + _mm_nt(db, wu_ref[j])
        t = dn * g1
        dh_ref[...] = dho + rh * (t - hh * jnp.mean(t * hh, axis=-1, keepdims=True))
        dg1_ref[...] += jnp.sum(dn * hh, axis=0, keepdims=True)

    row = lambda i: (i, 0)
    fix = lambda i: (0, 0)
    all3 = lambda i: (0, 0, 0)
    ns = FF // FC
    td = pl.BlockSpec((ts, D), row)
    tf = pl.BlockSpec((ts, FF), row)
    gd = pl.BlockSpec((1, D), fix)
    return pl.pallas_call(
        body, name="ffn_bwd", grid=(S // ts,),
        in_specs=[td, td, tf, tf, td, gd, gd,
                  pl.BlockSpec((ns, D, FC), all3), pl.BlockSpec((ns, D, FC), all3), pl.BlockSpec((ns, FC, D), all3)],
        out_specs=[td, td, tf, tf, tf, td, gd, gd],
        out_shape=[jax.ShapeDtypeStruct((S, D), F32), jax.ShapeDtypeStruct((S, D), _MM),
                   jax.ShapeDtypeStruct((S, FF), _MM), jax.ShapeDtypeStruct((S, FF), _MM),
                   jax.ShapeDtypeStruct((S, FF), _MM), jax.ShapeDtypeStruct((S, D), _MM),
                   jax.ShapeDtypeStruct((1, D), F32), jax.ShapeDtypeStruct((1, D), F32)],
        compiler_params=_cparams(("arbitrary",)),
    )(h, f, a, b, dho, g1, g2, wg, wu, wd)


def _wgrad(x, y, nslot=0):
    T, M = x.shape
    N = y.shape[1]
    tt = _row_tile(T, 1024)
    cw = N // nslot if nslot else (512 if N % 512 == 0 else 256)
    nc = N // cw
    assert cw * nc == N and cw % 128 == 0, (N, nc)

    def body(x_ref, y_ref, o_ref):
        @pl.when(pl.program_id(0) == 0)
        def _():
            o_ref[...] = jnp.zeros_like(o_ref)

        xt = x_ref[...].T
        for j in range(nc):
            part = jnp.dot(xt, y_ref[:, j * cw:(j + 1) * cw], preferred_element_type=F32)
            if nslot:
                o_ref[j] += part
            else:
                o_ref[:, j * cw:(j + 1) * cw] += part

    oshape = (nslot, M, cw) if nslot else (M, N)
    return pl.pallas_call(
        body, name="wgrad", grid=(T // tt,),
        in_specs=[pl.BlockSpec((tt, M), lambda t: (t, 0)), pl.BlockSpec((tt, N), lambda t: (t, 0))],
        out_specs=pl.BlockSpec(oshape, lambda t: (0,) * len(oshape)),
        out_shape=jax.ShapeDtypeStruct(oshape, F32),
        compiler_params=_cparams(("arbitrary",)),
    )(x, y)


def _proj_fwd(h, g, w, out_dtype):
    S = h.shape[0]
    nk, kw, N = w.shape
    ts = _row_tile(S, 256)

    def body(h_ref, g_ref, w_ref, u_ref, p_ref):
        hv = h_ref[...]
        u = (hv * _rstd(hv) * g_ref[...]).astype(_MM)
        u_ref[...] = u
        acc = _mm(u[:, :kw], w_ref[0])
        for j in range(1, nk):
            acc = acc + _mm(u[:, j * kw:(j + 1) * kw], w_ref[j])
        p_ref[...] = acc.astype(p_ref.dtype)

    return pl.pallas_call(
        body, name="proj_fwd", grid=(S // ts,),
        in_specs=[pl.BlockSpec((ts, D), lambda i: (i, 0)), pl.BlockSpec((1, D), lambda i: (0, 0)),
                  pl.BlockSpec((nk, kw, N), lambda i: (0, 0, 0))],
        out_specs=[pl.BlockSpec((ts, D), lambda i: (i, 0)), pl.BlockSpec((ts, N), lambda i: (i, 0))],
        out_shape=[jax.ShapeDtypeStruct((S, D), _MM), jax.ShapeDtypeStruct((S, N), out_dtype)],
        compiler_params=_cparams(("parallel",)),
    )(h, g, w)


def _proj_bwd(h, dho, g, w, pieces):
    S = h.shape[0]
    nk, kw, N = w.shape
    ts = _row_tile(S, 256)
    widths = [p.shape[1] for p in pieces]
    assert sum(widths) == N
    npc = len(pieces)

    def body(*refs):
        h_ref, dho_ref, g_ref, w_ref = refs[:4]
        p_refs = refs[4:4 + npc]
        dh_ref, dp_ref, dg_ref = refs[4 + npc:]

        @pl.when(pl.program_id(0) == 0)
        def _():
            dg_ref[...] = jnp.zeros_like(dg_ref)

        dus = [jnp.zeros((ts, kw), F32) for _ in range(nk)]
        off = 0
        for p_ref, n in zip(p_refs, widths):
            pv = p_ref[...].astype(_MM)
            dp_ref[:, off:off + n] = pv
            for j in range(nk):
                dus[j] = dus[j] + _mm_nt(pv, w_ref[j, :, off:off + n])
            off += n
        du = dus[0] if nk == 1 else jnp.concatenate(dus, axis=-1)
        dx, dg = _rms_bwd(h_ref[...], g_ref[...], du)
        dh_ref[...] = dho_ref[...] + dx
        dg_ref[...] += dg

    row = lambda i: (i, 0)
    return pl.pallas_call(
        body, name="proj_bwd", grid=(S // ts,),
        in_specs=[pl.BlockSpec((ts, D), row), pl.BlockSpec((ts, D), row), pl.BlockSpec((1, D), lambda i: (0, 0)),
                  pl.BlockSpec((nk, kw, N), lambda i: (0, 0, 0))] + [pl.BlockSpec((ts, n), row) for n in widths],
        out_specs=[pl.BlockSpec((ts, D), row), pl.BlockSpec((ts, N), row), pl.BlockSpec((1, D), lambda i: (0, 0))],
        out_shape=[jax.ShapeDtypeStruct((S, D), F32), jax.ShapeDtypeStruct((S, N), _MM),
                   jax.ShapeDtypeStruct((1, D), F32)],
        compiler_params=_cparams(("arbitrary",)),
    )(h, dho, g, w, *pieces)


def _pool_counts(first_row, ts):
    pos = (first_row + lax.broadcasted_iota(jnp.int32, (ts, 1), 0) + 1).astype(F32)
    return [jnp.minimum(pos, float(w)) for w in WINDOWS]


def _pool_delta(x, prev, cnts, ts):
    xe = jnp.concatenate([prev, x], axis=0)
    sums = []
    cur = xe
    for sh in (1, 2, 4, 8):
        cur = cur + pltpu.roll(cur, sh, 0)
        sums.append(cur)
    return [sums[gi][HALO:, gi * PG:(gi + 1) * PG] / cnts[gi] - x[:, gi * PG:(gi + 1) * PG]
            for gi in range(len(WINDOWS))]


def _pool_fwd(proj, pw, ps):
    S = proj.shape[0]
    ts = _row_tile(S, 256)

    def body(x_ref, pw_ref, ps_ref, tok_ref, carry_ref):
        i = pl.program_id(0)

        @pl.when(i == 0)
        def _():
            carry_ref[...] = jnp.zeros_like(carry_ref)

        x = x_ref[...]
        ds = _pool_delta(x, carry_ref[...], _pool_counts(i * ts, ts), ts)
        y = jnp.concatenate([_mm(ds[gi], pw_ref[gi]) for gi in range(len(WINDOWS))], axis=-1)
        tok_ref[...] = (y * ps_ref[...]).astype(tok_ref.dtype)
        carry_ref[...] = x[ts - HALO:, :]

    return pl.pallas_call(
        body, name="pool_fwd", grid=(S // ts,),
        in_specs=[pl.BlockSpec((ts, DTOK), lambda i: (i, 0)), pl.BlockSpec((len(WINDOWS), PG, PG), lambda i: (0, 0, 0)),
                  pl.BlockSpec((1, DTOK), lambda i: (0, 0))],
        out_specs=pl.BlockSpec((ts, DTOK), lambda i: (i, 0)),
        out_shape=jax.ShapeDtypeStruct((S, DTOK), _MM),
        scratch_shapes=[pltpu.VMEM((HALO, DTOK), F32)],
        compiler_params=_cparams(("arbitrary",)),
    )(proj, pw, ps)


def _pool_bwd(proj, dtok, pw, ps):
    S = proj.shape[0]
    ts = _row_tile(S, 256)
    nt = S // ts
    per = ts // HALO
    ng = len(WINDOWS)

    def body(x_ref, xp_ref, dt_ref, pw_ref, ps_ref, dx_ref, dpw_ref, dps_ref, carry_ref):
        i = pl.program_id(0)
        idx = nt - 1 - i

        @pl.when(i == 0)
        def _():
            carry_ref[...] = jnp.zeros_like(carry_ref)
            dpw_ref[...] = jnp.zeros_like(dpw_ref)
            dps_ref[...] = jnp.zeros_like(dps_ref)

        x = x_ref[...]
        prev = jnp.where(idx > 0, xp_ref[...], 0.0)
        cnts = _pool_counts(idx * ts, ts)
        ds = _pool_delta(x, prev, cnts, ts)
        dt = dt_ref[...]
        y = jnp.concatenate([_mm(ds[gi], pw_ref[gi]) for gi in range(ng)], axis=-1)
        dps_ref[...] += jnp.sum(dt * y, axis=0, keepdims=True)
        dy = (dt * ps_ref[...]).astype(_MM)
        dds = []
        for gi in range(ng):
            dyg = dy[:, gi * PG:(gi + 1) * PG]
            dpw_ref[gi] += _mm_tn(ds[gi], dyg)
            dds.append(_mm_nt(dyg, pw_ref[gi]))
        e = jnp.concatenate([dds[gi] / cnts[gi] for gi in range(ng)], axis=-1)
        ee = jnp.concatenate([e, carry_ref[...]], axis=0)
        rows = ts + HALO
        cur = ee
        outs = []
        for gi, sh in enumerate((1, 2, 4, 8)):
            cur = cur + pltpu.roll(cur, rows - sh, 0)
            outs.append(cur[:ts, gi * PG:(gi + 1) * PG] - dds[gi])
        dx_ref[...] = jnp.concatenate(outs, axis=-1)
        carry_ref[...] = e[:HALO, :]

    rev = lambda i: (nt - 1 - i, 0)
    return pl.pallas_call(
        body, name="pool_bwd", grid=(nt,),
        in_specs=[pl.BlockSpec((ts, DTOK), rev),
                  pl.BlockSpec((HALO, DTOK), lambda i: (jnp.maximum((nt - 1 - i) * per - 1, 0), 0)),
                  pl.BlockSpec((ts, DTOK), rev), pl.BlockSpec((ng, PG, PG), lambda i: (0, 0, 0)),
                  pl.BlockSpec((1, DTOK), lambda i: (0, 0))],
        out_specs=[pl.BlockSpec((ts, DTOK), rev), pl.BlockSpec((ng, PG, PG), lambda i: (0, 0, 0)),
                   pl.BlockSpec((1, DTOK), lambda i: (0, 0))],
        out_shape=[jax.ShapeDtypeStruct((S, DTOK), F32), jax.ShapeDtypeStruct((ng, PG, PG), F32),
                   jax.ShapeDtypeStruct((1, DTOK), F32)],
        scratch_shapes=[pltpu.VMEM((HALO, DTOK), F32)],
        compiler_params=_cparams(("arbitrary",)),
    )(proj, proj, dtok, pw, ps)


def _tri(strict):
    r = lax.broadcasted_iota(jnp.int32, (QB, QB), 0)
    c = lax.broadcasted_iota(jnp.int32, (QB, QB), 1)
    return jnp.where(r > c if strict else r >= c, 1.0, 0.0).astype(_MM)


def _sb_terms(z, valid):
    e = jnp.exp(-jnp.abs(z))
    sp = jnp.log(1.0 + e)
    ls = jnp.minimum(z, 0.0) - sp
    lf = jnp.where(valid, jnp.minimum(-z, 0.0) - sp, 0.0)
    return ls, lf, e


def _sb_alive(cs):
    top = cs[0]
    for c in cs[1:]:
        top = jnp.maximum(top, c)
    return (jnp.max(top) > NEG_CUT).astype(jnp.int32)


def _sb_split(x2, head_a):
    zero = jnp.zeros_like(x2)
    return jnp.where(head_a, x2, zero), jnp.where(head_a, zero, x2)


SB_GROUPS_FWD = 4
SB_GROUPS_BWD = 4


def _sb_fwd(proj):
    S = proj.shape[0]
    G = min(SB_GROUPS_FWD, S // QB)
    nstep = S // (G * QB)
    npair = DTOK // 128
    nch = 2 * G

    def body(q_ref, k_ref, v_ref, tri_ref, o_ref):
        qb = pl.program_id(1)
        head_a = lax.broadcasted_iota(jnp.int32, (QB, 128), 1) < HD
        qs = []
        for g in range(G):
            qs.extend(_sb_split(q_ref[g * QB:(g + 1) * QB, :], head_a))
        row = lax.broadcasted_iota(jnp.int32, (QB, QB), 0)
        col = lax.broadcasted_iota(jnp.int32, (QB, QB), 1)
        tri = tri_ref[...]
        last = G * qb + G - 1

        def step(carry):
            i, _, cs, accs = carry
            causal = (col - row) < i * QB
            ks, vs, valids = [], [], []
            for g in range(G):
                kb = G * qb + g - i
                off = pl.multiple_of(jnp.maximum(kb, 0) * QB, QB)
                ks.append(k_ref[pl.ds(off, QB), :])
                vs.append(v_ref[pl.ds(off, QB), :])
                valids.append(jnp.logical_and(causal, kb >= 0))
            zs = [_mm_nt(qs[ch], ks[ch // 2]) * SCALE for ch in range(nch)]
            terms = [_sb_terms(zs[ch], valids[ch // 2]) for ch in range(nch)]
            withins = [_mm2(terms[ch][1], tri) for ch in range(nch)]
            new_c, new_acc = [], []
            for ch in range(nch):
                ls, lf, _ = terms[ch]
                a = jnp.where(valids[ch // 2], jnp.exp(ls + withins[ch] + cs[ch]), 0.0)
                new_acc.append(accs[ch] + _mm2(a, vs[ch // 2]))
                new_c.append(cs[ch] + jnp.sum(lf, axis=-1, keepdims=True))
            return i + 1, _sb_alive(new_c), tuple(new_c), tuple(new_acc)

        zc = jnp.zeros((QB, 1), F32)
        za = jnp.zeros((QB, 128), F32)
        more = lambda cr: jnp.logical_and(cr[0] <= last, cr[1] > 0)
        accs = lax.while_loop(more, step, (0, 1, (zc,) * nch, (za,) * nch))[3]
        for g in range(G):
            o_ref[g * QB:(g + 1) * QB, :] = jnp.where(head_a, accs[2 * g], accs[2 * g + 1])

    return pl.pallas_call(
        body, name="sb_fwd", grid=(npair, nstep),
        in_specs=[pl.BlockSpec((G * QB, 128), lambda p, i: (i, p)),
                  pl.BlockSpec((S, 128), lambda p, i: (0, npair + p)),
                  pl.BlockSpec((S, 128), lambda p, i: (0, 2 * npair + p)),
                  pl.BlockSpec((QB, QB), lambda p, i: (0, 0))],
        out_specs=pl.BlockSpec((G * QB, 128), lambda p, i: (i, p)),
        out_shape=jax.ShapeDtypeStruct((S, DTOK), F32),
        compiler_params=_cparams(("parallel", "parallel")),
    )(proj, proj, proj, _tri(True))


def _sb_bwd(proj, o, do):
    S = proj.shape[0]
    G = min(SB_GROUPS_BWD, S // QB)
    nstep = S // (G * QB)
    npair = DTOK // 128
    nch = 2 * G

    def body(q_ref, k_ref, v_ref, o_ref, do_ref, tri_ref, tri2_ref, dq_ref, dk_ref, dv_ref):
        qb = pl.program_id(1)

        @pl.when(qb == 0)
        def _():
            dk_ref[...] = jnp.zeros_like(dk_ref)
            dv_ref[...] = jnp.zeros_like(dv_ref)

        head_a = lax.broadcasted_iota(jnp.int32, (QB, 128), 1) < HD
        qs, dos, gs = [], [], []
        for g in range(G):
            rows = slice(g * QB, (g + 1) * QB)
            qs.extend(_sb_split(q_ref[rows, :], head_a))
            dob = do_ref[rows, :].astype(_MM)
            dos.extend(_sb_split(dob, head_a))
            go = dob.astype(F32) * o_ref[rows, :]
            gs.append(jnp.sum(jnp.where(head_a, go, 0.0), axis=-1, keepdims=True))
            gs.append(jnp.sum(jnp.where(head_a, 0.0, go), axis=-1, keepdims=True))
        row = lax.broadcasted_iota(jnp.int32, (QB, QB), 0)
        col = lax.broadcasted_iota(jnp.int32, (QB, QB), 1)
        tri = tri_ref[...]
        tri2 = tri2_ref[...]
        last = G * qb + G - 1

        def step(carry):
            i, _, cs, rs, dqs = carry
            causal = (col - row) < i * QB
            offs, ks, vs, valids = [], [], [], []
            for g in range(G):
                kb = G * qb + g - i
                off = pl.multiple_of(jnp.maximum(kb, 0) * QB, QB)
                offs.append(off)
                ks.append(k_ref[pl.ds(off, QB), :])
                vs.append(v_ref[pl.ds(off, QB), :])
                valids.append(jnp.logical_and(causal, kb >= 0))
            zs = [_mm_nt(qs[ch], ks[ch // 2]) * SCALE for ch in range(nch)]
            das = [_mm_nt(dos[ch], vs[ch // 2]) for ch in range(nch)]
            terms = [_sb_terms(zs[ch], valids[ch // 2]) for ch in range(nch)]
            withins = [_mm2(terms[ch][1], tri) for ch in range(nch)]
            a_s, dlws = [], []
            for ch in range(nch):
                a = jnp.where(valids[ch // 2], jnp.exp(terms[ch][0] + withins[ch] + cs[ch]), 0.0)
                a_s.append(a)
                dlws.append(das[ch] * a)
            sfx = [_mm2(dlws[ch], tri2) for ch in range(nch)]
            dzs = []
            for ch in range(nch):
                e = terms[ch][2]
                inv = 1.0 / (1.0 + e)
                pos = zs[ch] >= 0.0
                beta = jnp.where(pos, 1.0, e) * inv
                omb = jnp.where(pos, e, 1.0) * inv
                prefix = gs[ch] - rs[ch] - sfx[ch]
                dzs.append((jnp.where(valids[ch // 2], dlws[ch] * omb - beta * prefix, 0.0) * SCALE).astype(_MM))
            new_dq = [dqs[ch] + _mm(dzs[ch], ks[ch // 2]) for ch in range(nch)]
            for g in range(G):
                dk_ref[pl.ds(offs[g], QB), :] += (_mm_tn(dzs[2 * g], qs[2 * g]) + _mm_tn(dzs[2 * g + 1], qs[2 * g + 1]))
                dv_ref[pl.ds(offs[g], QB), :] += (_mm_tn(a_s[2 * g], dos[2 * g]) + _mm_tn(a_s[2 * g + 1], dos[2 * g + 1]))
            new_c = [cs[ch] + jnp.sum(terms[ch][1], axis=-1, keepdims=True) for ch in range(nch)]
            new_r = [rs[ch] + jnp.sum(dlws[ch], axis=-1, keepdims=True) for ch in range(nch)]
            return i + 1, _sb_alive(new_c), tuple(new_c), tuple(new_r), tuple(new_dq)

        zc = jnp.zeros((QB, 1), F32)
        za = jnp.zeros((QB, 128), F32)
        more = lambda cr: jnp.logical_and(cr[0] <= last, cr[1] > 0)
        dqs = lax.while_loop(more, step, (0, 1, (zc,) * nch, (zc,) * nch, (za,) * nch))[4]
        for g in range(G):
            dq_ref[g * QB:(g + 1) * QB, :] = jnp.where(head_a, dqs[2 * g], dqs[2 * g + 1])

    qspec = pl.BlockSpec((G * QB, 128), lambda p, i: (i, p))
    full = lambda base: pl.BlockSpec((S, 128), lambda p, i: (0, base + p))
    tspec = pl.BlockSpec((QB, QB), lambda p, i: (0, 0))
    return pl.pallas_call(
        body, name="sb_bwd", grid=(npair, nstep),
        in_specs=[qspec, full(npair), full(2 * npair), qspec, qspec, tspec, tspec],
        out_specs=[qspec, full(0), full(0)],
        out_shape=[jax.ShapeDtypeStruct((S, DTOK), F32)] * 3,
        compiler_params=_cparams(("parallel", "arbitrary")),
    )(proj, proj, proj, o, do, _tri(True), _tri(False))


def _memkv_fwd(mem, gm, wkv):
    nk, kw, _ = wkv.shape

    def body(m_ref, g_ref, w_ref, kv_ref):
        m = m_ref[...]
        mn = (m * _rstd(m) * g_ref[...]).astype(_MM)
        acc = _mm(mn[:, :kw], w_ref[0])
        for j in range(1, nk):
            acc = acc + _mm(mn[:, j * kw:(j + 1) * kw], w_ref[j])
        kv_ref[...] = acc.astype(kv_ref.dtype)

    return pl.pallas_call(
        body, name="memkv_fwd", out_shape=jax.ShapeDtypeStruct((MEM_LEN, 2 * DMEM), _MM),
        compiler_params=pltpu.CompilerParams(vmem_limit_bytes=VMEM_LIMIT),
    )(mem, gm, wkv)


def _memkv_bwd(mem, gm, wkv, dkv):
    nk, kw, _ = wkv.shape

    def body(m_ref, g_ref, w_ref, dkv_ref, dw_ref, dg_ref):
        m = m_ref[...]
        mh = m * _rstd(m)
        mn = (mh * g_ref[...]).astype(_MM)
        dkv = dkv_ref[...].astype(_MM)
        dw_ref[...] = _mm_tn(mn, dkv)
        dmn = jnp.concatenate([_mm_nt(dkv, w_ref[j]) for j in range(nk)], axis=-1)
        dg_ref[...] = jnp.sum(dmn * mh, axis=0, keepdims=True)

    return pl.pallas_call(
        body, name="memkv_bwd",
        out_shape=[jax.ShapeDtypeStruct((D, 2 * DMEM), F32), jax.ShapeDtypeStruct((1, D), F32)],
        compiler_params=pltpu.CompilerParams(vmem_limit_bytes=VMEM_LIMIT),
    )(mem, gm, wkv, dkv)


def _mem_probs(qh, kp):
    s = _mm_nt(qh, kp) * SCALE
    p = jnp.exp(s - jnp.max(s, axis=-1, keepdims=True))
    return p / jnp.sum(p, axis=-1, keepdims=True)


def _mixout_fwd(h, tok, proj, kv, wo, g, qm_block):
    S = h.shape[0]
    ts = _row_tile(S, 256)

    def body(h_ref, tok_ref, qm_ref, kv_ref, wo_ref, g_ref, ho_ref, cat_ref, mix_ref):
        head_a = lax.broadcasted_iota(jnp.int32, (ts, 128), 1) < HD
        qm = qm_ref[...].astype(_MM)
        zq = jnp.zeros((ts, 128), _MM)
        mos = []
        for pi in range(DMEM // 128):
            cs = slice(pi * 128, (pi + 1) * 128)
            qp = qm[:, cs]
            kp = kv_ref[:, cs]
            vp = kv_ref[:, DMEM + pi * 128:DMEM + (pi + 1) * 128]
            pa = _mem_probs(jnp.where(head_a, qp, zq), kp)
            pb = _mem_probs(jnp.where(head_a, zq, qp), kp)
            mos.append(jnp.where(head_a, _mm(pa, vp), _mm(pb, vp)))
        cat = jnp.concatenate([tok_ref[...].astype(_MM)] + [m.astype(_MM) for m in mos], axis=-1)
        cat_ref[...] = cat
        mix = jnp.concatenate([_mm(cat, wo_ref[j]) for j in range(N_CHIPS)], axis=-1)
        mix_ref[...] = mix
        ho_ref[...] = h_ref[...] + mix * _rstd(mix) * g_ref[...]

    row = lambda i: (i, 0)
    return pl.pallas_call(
        body, name="mixout_fwd", grid=(S // ts,),
        in_specs=[pl.BlockSpec((ts, D), row), pl.BlockSpec((ts, DTOK), row),
                  pl.BlockSpec((ts, DMEM), lambda i: (i, qm_block)),
                  pl.BlockSpec((MEM_LEN, 2 * DMEM), lambda i: (0, 0)),
                  pl.BlockSpec((N_CHIPS, DMIX, D // N_CHIPS), lambda i: (0, 0, 0)), pl.BlockSpec((1, D), lambda i: (0, 0))],
        out_specs=[pl.BlockSpec((ts, D), row), pl.BlockSpec((ts, DMIX), row), pl.BlockSpec((ts, D), row)],
        out_shape=[jax.ShapeDtypeStruct((S, D), F32), jax.ShapeDtypeStruct((S, DMIX), _MM),
                   jax.ShapeDtypeStruct((S, D), F32)],
        compiler_params=_cparams(("parallel",)),
    )(h, tok, proj, kv, wo, g)


def _mixout_bwd(dho, mix, proj, kv, wo, g, qm_block):
    S = dho.shape[0]
    ts = _row_tile(S, 256)

    def body(dho_ref, mix_ref, qm_ref, kv_ref, wo_ref, g_ref, dmix_ref, dtok_ref, dqm_ref, dkv_ref, dg_ref):
        @pl.when(pl.program_id(0) == 0)
        def _():
            dkv_ref[...] = jnp.zeros_like(dkv_ref)
            dg_ref[...] = jnp.zeros_like(dg_ref)

        dmix, dg = _rms_bwd(mix_ref[...], g_ref[...], dho_ref[...])
        dg_ref[...] += dg
        dmb = dmix.astype(_MM)
        dmix_ref[...] = dmb
        cw = D // N_CHIPS
        dcat = _mm_nt(dmb[:, :cw], wo_ref[0])
        for j in range(1, N_CHIPS):
            dcat = dcat + _mm_nt(dmb[:, j * cw:(j + 1) * cw], wo_ref[j])
        dtok_ref[...] = dcat[:, :DTOK]
        head_a = lax.broadcasted_iota(jnp.int32, (ts, 128), 1) < HD
        qm = qm_ref[...].astype(_MM)
        zq = jnp.zeros((ts, 128), _MM)
        for pi in range(DMEM // 128):
            cs = slice(pi * 128, (pi + 1) * 128)
            vs = slice(DMEM + pi * 128, DMEM + (pi + 1) * 128)
            kp = kv_ref[:, cs]
            vp = kv_ref[:, vs]
            dmo = dcat[:, DTOK + pi * 128:DTOK + (pi + 1) * 128].astype(_MM)
            dqs = []
            dk_acc = jnp.zeros((MEM_LEN, 128), F32)
            dv_acc = jnp.zeros((MEM_LEN, 128), F32)
            for hd in range(2):
                qh = jnp.where(head_a, qm[:, cs], zq) if hd == 0 else jnp.where(head_a, zq, qm[:, cs])
                dmh = jnp.where(head_a, dmo, zq) if hd == 0 else jnp.where(head_a, zq, dmo)
                p = _mem_probs(qh, kp)
                dp = _mm_nt(dmh, vp)
                ds = (p * (dp - jnp.sum(dp * p, axis=-1, keepdims=True)) * SCALE).astype(_MM)
                dqs.append(_mm(ds, kp))
                dk_acc = dk_acc + _mm_tn(ds, qh)
                dv_acc = dv_acc + _mm_tn(p, dmh)
            dqm_ref[:, cs] = jnp.where(head_a, dqs[0], dqs[1]).astype(_MM)
            dkv_ref[:, cs] += dk_acc
            dkv_ref[:, vs] += dv_acc

    row = lambda i: (i, 0)
    fix = lambda i: (0, 0)
    return pl.pallas_call(
        body, name="mixout_bwd", grid=(S // ts,),
        in_specs=[pl.BlockSpec((ts, D), row), pl.BlockSpec((ts, D), row),
                  pl.BlockSpec((ts, DMEM), lambda i: (i, qm_block)),
                  pl.BlockSpec((MEM_LEN, 2 * DMEM), fix),
                  pl.BlockSpec((N_CHIPS, DMIX, D // N_CHIPS), lambda i: (0, 0, 0)), pl.BlockSpec((1, D), fix)],
        out_specs=[pl.BlockSpec((ts, D), row), pl.BlockSpec((ts, DTOK), row), pl.BlockSpec((ts, DMEM), row),
                   pl.BlockSpec((MEM_LEN, 2 * DMEM), fix), pl.BlockSpec((1, D), fix)],
        out_shape=[jax.ShapeDtypeStruct((S, D), _MM), jax.ShapeDtypeStruct((S, DTOK), F32),
                   jax.ShapeDtypeStruct((S, DMEM), _MM), jax.ShapeDtypeStruct((MEM_LEN, 2 * DMEM), F32),
                   jax.ShapeDtypeStruct((1, D), F32)],
        compiler_params=_cparams(("arbitrary",)),
    )(dho, mix, proj, kv, wo, g)


def _loss_grad(y, target):
    S = y.shape[0]
    ts = _row_tile(S, 512)

    def body(y_ref, t_ref, acc_ref, dy_ref):
        @pl.when(pl.program_id(0) == 0)
        def _():
            acc_ref[...] = jnp.zeros_like(acc_ref)

        err = y_ref[...] - t_ref[...]
        acc_ref[...] += jnp.sum(err * err, axis=0, keepdims=True)
        dy_ref[...] = err * (1.0 / D)

    row = lambda i: (i, 0)
    return pl.pallas_call(
        body, name="loss_grad", grid=(S // ts,),
        in_specs=[pl.BlockSpec((ts, D), row), pl.BlockSpec((ts, D), row)],
        out_specs=[pl.BlockSpec((1, D), lambda i: (0, 0)), pl.BlockSpec((ts, D), row)],
        out_shape=[jax.ShapeDtypeStruct((1, D), F32), jax.ShapeDtypeStruct((S, D), F32)],
        compiler_params=_cparams(("arbitrary",)),
    )(y, target)


LAYER_TENSORS = ("ffn1_gate", "ffn1_up", "ffn1_down", "ffn2_gate", "ffn2_up", "ffn2_down", "w_in", "w_mem_kv", "w_out")
SPLITS = {"ffn1_gate": "c", "ffn1_up": "c", "ffn1_down": "r", "ffn2_gate": "c", "ffn2_up": "c", "ffn2_down": "r",
          "w_in_pool": "r", "w_in_sb": "c", "w_mem_kv": "r", "w_out": "c"}


def _is_pool(i):
    return i % 2 == 0


def _qm_block(i):
    return (DTOK if _is_pool(i) else 3 * DTOK) // DMEM


def _layer_forward(h, mem, lw, i):
    g_at = lambda g, k: g[k].reshape(1, D)
    sv = {"h0": h}
    h, sv["a1"], sv["b1"], sv["f1"] = _ffn_fwd(h, g_at(lw["g_pre"], 0), g_at(lw["g_post"], 0),
                                               lw["ffn1_gate"], lw["ffn1_up"], lw["ffn1_down"])
    sv["h1"] = h
    sv["kv"] = _memkv_fwd(mem, lw["g_mem"], lw["w_mem_kv"])
    if _is_pool(i):
        sv["u"], sv["proj"] = _proj_fwd(h, g_at(lw["g_pre"], 1), lw["w_in"], F32)
        tok = _pool_fwd(sv["proj"], lw["pool_w"], lw["pool_scale"])
    else:
        sv["u"], sv["proj"] = _proj_fwd(h, g_at(lw["g_pre"], 1), lw["w_in"], _MM)
        tok = sv["o"] = _sb_fwd(sv["proj"])
    h, sv["cat"], sv["mix"] = _mixout_fwd(h, tok, sv["proj"], sv["kv"], lw["w_out"], g_at(lw["g_post"], 1),
                                          _qm_block(i))
    sv["h2"] = h
    h, sv["a2"], sv["b2"], sv["f2"] = _ffn_fwd(h, g_at(lw["g_pre"], 2), g_at(lw["g_post"], 2),
                                               lw["ffn2_gate"], lw["ffn2_up"], lw["ffn2_down"])
    return h, sv


def _layer_backward(dh, mem, lw, sv, i):
    g_at = lambda g, k: g[k].reshape(1, D)
    big, small = {}, {}
    dgpre, dgpost = [None] * 3, [None] * 3

    def ffn_back(which, dh, h_in):
        k = 0 if which == 1 else 2
        n = f"ffn{which}"
        dh, nb, sb, da, db, df, dgpre[k], dgpost[k] = _ffn_bwd(
            h_in, sv[f"f{which}"], sv[f"a{which}"], sv[f"b{which}"], dh, g_at(lw["g_pre"], k), g_at(lw["g_post"], k),
            lw[n + "_gate"], lw[n + "_up"], lw[n + "_down"])
        big[n + "_gate"] = _wgrad(nb, da, N_CHIPS)
        big[n + "_up"] = _wgrad(nb, db, N_CHIPS)
        big[n + "_down"] = _wgrad(sb, df).reshape(N_CHIPS, FF // N_CHIPS, D)
        return dh

    dh = ffn_back(2, dh, sv["h2"])
    dmix, dtok, dqm, dkv, dgpost[1] = _mixout_bwd(dh, sv["mix"], sv["proj"], sv["kv"], lw["w_out"],
                                                  g_at(lw["g_post"], 1), _qm_block(i))
    big["w_out"] = _wgrad(sv["cat"], dmix, N_CHIPS)
    dwkv, small["g_mem"] = _memkv_bwd(mem, lw["g_mem"], lw["w_mem_kv"], dkv)
    big["w_mem_kv"] = dwkv.reshape(N_CHIPS, D // N_CHIPS, 2 * DMEM)
    if _is_pool(i):
        dx, small["pool_w"], small["pool_scale"] = _pool_bwd(sv["proj"], dtok, lw["pool_w"], lw["pool_scale"])
        dh, dproj, dgpre[1] = _proj_bwd(sv["h1"], dh, g_at(lw["g_pre"], 1), lw["w_in"], [dx, dqm])
        big["w_in"] = _wgrad(sv["u"], dproj).reshape(N_CHIPS, D // N_CHIPS, DMIX)
    else:
        dq, dk, dv = _sb_bwd(sv["proj"], sv["o"], dtok)
        dh, dproj, dgpre[1] = _proj_bwd(sv["h1"], dh, g_at(lw["g_pre"], 1), lw["w_in"], [dq, dk, dv, dqm])
        big["w_in"] = jnp.transpose(_wgrad(sv["u"], dproj).reshape(D, N_CHIPS, DSB // N_CHIPS), (1, 0, 2))
    dh = ffn_back(1, dh, sv["h0"])
    small["g_pre"] = jnp.concatenate(dgpre, axis=0)
    small["g_post"] = jnp.concatenate(dgpost, axis=0)
    return dh, [big[k] for k in LAYER_TENSORS], small


ANY = pl.BlockSpec(memory_space=pl.ANY)


def _place():
    return lax.axis_index("x"), lax.axis_index("y"), lax.axis_index("c")


def _flip(v, bit):
    return 1 - v if bit else v


def _small_allgather(buf):
    R = buf.shape[0]

    def body(x_ref, o_ref, send_sems, recv_sems, loc_sem):
        x, y, c = _place()
        me = 4 * x + 2 * y + c
        loc = pltpu.make_async_copy(x_ref, o_ref.at[me], loc_sem)
        loc.start()
        sends = []
        for k in range(1, N_DEV):
            peer = (_flip(x, k & 4), _flip(y, k & 2), _flip(c, k & 1))
            cp = pltpu.make_async_remote_copy(src_ref=x_ref, dst_ref=o_ref.at[me], send_sem=send_sems.at[k - 1],
                                              recv_sem=recv_sems.at[k - 1], device_id=peer, device_id_type=MESH)
            cp.start()
            sends.append(cp)
        for k in range(1, N_DEV):
            peer = (_flip(x, k & 4), _flip(y, k & 2), _flip(c, k & 1))
            slot = 4 * peer[0] + 2 * peer[1] + peer[2]
            pltpu.make_async_remote_copy(src_ref=x_ref, dst_ref=o_ref.at[slot], send_sem=send_sems.at[k - 1],
                                         recv_sem=recv_sems.at[k - 1], device_id=peer, device_id_type=MESH).wait_recv()
        for cp in sends:
            cp.wait_send()
        loc.wait()

    return pl.pallas_call(
        body, name="small_allgather", in_specs=[ANY], out_specs=ANY,
        out_shape=jax.ShapeDtypeStruct((N_DEV, R, 128), buf.dtype),
        scratch_shapes=[pltpu.SemaphoreType.DMA((N_DEV - 1,)), pltpu.SemaphoreType.DMA((N_DEV - 1,)),
                        pltpu.SemaphoreType.DMA],
    )(buf)


def _gather_weights(shards):
    n = len(shards)
    halves = [s.shape[0] // 2 for s in shards]

    def body(*refs):
        ins, outs = refs[:n], refs[n:2 * n]
        ici_send, ici_recv, d2d_send, d2d_recv, loc_sems = refs[2 * n:]
        x, y, c = _place()
        me = 2 * x + y
        sib = (x, y, 1 - c)
        chips = [(_flip(x, k & 2), _flip(y, k & 1)) for k in (1, 2, 3)]
        locs = []
        for t in range(n):
            for hc in range(2):
                rows = pl.ds(hc * halves[t], halves[t])
                cp = pltpu.make_async_copy(ins[t].at[rows], outs[t].at[me, rows], loc_sems.at[2 * t + hc])
                cp.start()
                locs.append(cp)

        def half(t, chip_idx, hc):
            return outs[t].at[chip_idx, pl.ds(hc * halves[t], halves[t])]

        sends = []
        for t in range(n):
            for k, chip in enumerate(chips):
                cp = pltpu.make_async_remote_copy(
                    src_ref=ins[t].at[pl.ds(c * halves[t], halves[t])], dst_ref=half(t, me, c),
                    send_sem=ici_send.at[3 * t + k], recv_sem=ici_recv.at[3 * t + k],
                    device_id=(chip[0], chip[1], c), device_id_type=MESH)
                cp.start()
                sends.append(cp)
        for t in range(n):
            for k, chip in enumerate(chips):
                idx = 2 * chip[0] + chip[1]
                pltpu.make_async_remote_copy(
                    src_ref=half(t, idx, c), dst_ref=half(t, idx, c), send_sem=ici_send.at[3 * t + k],
                    recv_sem=ici_recv.at[3 * t + k], device_id=(chip[0], chip[1], c), device_id_type=MESH).wait_recv()
                cp = pltpu.make_async_remote_copy(
                    src_ref=half(t, idx, c), dst_ref=half(t, idx, c), send_sem=d2d_send.at[3 * t + k],
                    recv_sem=d2d_recv.at[3 * t + k], device_id=sib, device_id_type=MESH)
                cp.start()
                sends.append(cp)
        for t in range(n):
            for k, chip in enumerate(chips):
                idx = 2 * chip[0] + chip[1]
                pltpu.make_async_remote_copy(
                    src_ref=half(t, idx, 1 - c), dst_ref=half(t, idx, 1 - c), send_sem=d2d_send.at[3 * t + k],
                    recv_sem=d2d_recv.at[3 * t + k], device_id=sib, device_id_type=MESH).wait_recv()
        for cp in sends:
            cp.wait_send()
        for cp in locs:
            cp.wait()

    dma = lambda m: pltpu.SemaphoreType.DMA((m,))
    return pl.pallas_call(
        body, name="gather_weights", in_specs=[ANY] * n, out_specs=[ANY] * n,
        out_shape=[jax.ShapeDtypeStruct((N_CHIPS,) + s.shape, s.dtype) for s in shards],
        scratch_shapes=[dma(3 * n), dma(3 * n), dma(3 * n), dma(3 * n), dma(2 * n)],
    )(*shards)


def _rs_to_sibling(gs):
    n = len(gs)
    halves = [g.shape[1] // 2 for g in gs]

    def body(*refs):
        ins, outs = refs[:n], refs[n:2 * n]
        send_sems, recv_sems = refs[2 * n:]
        x, y, c = _place()
        cps = []
        for t in range(n):
            cp = pltpu.make_async_remote_copy(
                src_ref=ins[t].at[:, pl.ds((1 - c) * halves[t], halves[t])], dst_ref=outs[t],
                send_sem=send_sems.at[t], recv_sem=recv_sems.at[t], device_id=(x, y, 1 - c), device_id_type=MESH)
            cp.start()
            cps.append(cp)
        for cp in cps:
            cp.wait()

    dma = lambda m: pltpu.SemaphoreType.DMA((m,))
    return pl.pallas_call(
        body, name="rs_to_sibling", in_specs=[ANY] * n, out_specs=[ANY] * n,
        out_shape=[jax.ShapeDtypeStruct((N_CHIPS, g.shape[1] // 2) + g.shape[2:], g.dtype) for g in gs],
        scratch_shapes=[dma(n), dma(n)],
    )(*gs)


def _rs_to_chips(ss):
    n = len(ss)

    def body(*refs):
        ins, outs = refs[:n], refs[n:2 * n]
        send_sems, recv_sems = refs[2 * n:]
        x, y, c = _place()
        chips = [(_flip(x, k & 2), _flip(y, k & 1)) for k in (1, 2, 3)]
        cps = []
        for t in range(n):
            for k, chip in enumerate(chips):
                cp = pltpu.make_async_remote_copy(
                    src_ref=ins[t].at[2 * chip[0] + chip[1]], dst_ref=outs[t].at[k],
                    send_sem=send_sems.at[3 * t + k], recv_sem=recv_sems.at[3 * t + k],
                    device_id=(chip[0], chip[1], c), device_id_type=MESH)
                cp.start()
                cps.append(cp)
        for cp in cps:
            cp.wait()

    dma = lambda m: pltpu.SemaphoreType.DMA((m,))
    return pl.pallas_call(
        body, name="rs_to_chips", in_specs=[ANY] * n, out_specs=[ANY] * n,
        out_shape=[jax.ShapeDtypeStruct((3,) + s.shape[1:], s.dtype) for s in ss],
        scratch_shapes=[dma(3 * n), dma(3 * n)],
    )(*ss)


def _share_halves(ts):
    n = len(ts)

    def body(*refs):
        outs = refs[n:2 * n]
        send_sems, recv_sems = refs[2 * n:]
        x, y, c = _place()
        cps = []
        for t in range(n):
            cp = pltpu.make_async_remote_copy(
                src_ref=outs[t].at[c], dst_ref=outs[t].at[c], send_sem=send_sems.at[t], recv_sem=recv_sems.at[t],
                device_id=(x, y, 1 - c), device_id_type=MESH)
            cp.start()
            cps.append(cp)
        for t in range(n):
            pltpu.make_async_remote_copy(
                src_ref=outs[t].at[1 - c], dst_ref=outs[t].at[1 - c], send_sem=send_sems.at[t],
                recv_sem=recv_sems.at[t], device_id=(x, y, 1 - c), device_id_type=MESH).wait_recv()
        for cp in cps:
            cp.wait_send()

    dma = lambda m: pltpu.SemaphoreType.DMA((m,))
    return pl.pallas_call(
        body, name="share_halves", in_specs=[ANY] * n, out_specs=[ANY] * n,
        out_shape=[jax.ShapeDtypeStruct(t.shape, t.dtype) for t in ts],
        input_output_aliases={t: t for t in range(n)},
        scratch_shapes=[dma(n), dma(n)],
    )(*ts)


def _rows(a, lead):
    return a.reshape(a.shape[:lead] + (-1, a.shape[-1]))


def _add_sibling(g, r, c):
    L_ = g.shape[1]
    g3 = _rows(g.reshape((N_CHIPS, 2, L_ // 2) + g.shape[2:]), 2)
    r3 = _rows(r, 1)
    R_, C = r3.shape[1:]
    tr = _row_tile(R_, 512)

    def body(c_ref, g_ref, r_ref, o_ref):
        o_ref[...] = (g_ref[...] + r_ref[...]).astype(o_ref.dtype)

    out = pl.pallas_call(
        body, name="add_sibling",
        grid_spec=pltpu.PrefetchScalarGridSpec(
            num_scalar_prefetch=1, grid=(N_CHIPS, R_ // tr),
            in_specs=[pl.BlockSpec((None, None, tr, C), lambda s, i, c_ref: (s, c_ref[0], i, 0)),
                      pl.BlockSpec((None, tr, C), lambda s, i, c_ref: (s, i, 0))],
            out_specs=pl.BlockSpec((None, tr, C), lambda s, i, c_ref: (s, i, 0))),
        out_shape=jax.ShapeDtypeStruct((N_CHIPS, R_, C), _MM),
        compiler_params=_cparams(("parallel", "parallel")),
    )(c.reshape(1), g3, r3)
    return out.reshape((N_CHIPS, L_ // 2) + g.shape[2:])


def _add_chips(s, r, me, c):
    s3, r3 = _rows(s, 1), _rows(r, 1)
    R_, C = s3.shape[1:]
    tr = _row_tile(R_, 512)

    def body(at_ref, s_ref, r0_ref, r1_ref, r2_ref, o_ref):
        f = lambda ref: ref[...].astype(F32)
        o_ref[...] = ((f(s_ref) + f(r0_ref)) + f(r1_ref)) + f(r2_ref)

    rk = lambda k: pl.BlockSpec((None, tr, C), lambda i, at_ref: (k, i, 0))
    out = pl.pallas_call(
        body, name="add_chips",
        grid_spec=pltpu.PrefetchScalarGridSpec(
            num_scalar_prefetch=1, grid=(R_ // tr,),
            in_specs=[pl.BlockSpec((None, tr, C), lambda i, at_ref: (at_ref[0], i, 0)), rk(0), rk(1), rk(2)],
            out_specs=pl.BlockSpec((None, tr, C), lambda i, at_ref: (at_ref[1], i, 0))),
        out_shape=jax.ShapeDtypeStruct((2, R_, C), F32),
        compiler_params=_cparams(("parallel",)),
    )(jnp.stack([me, c]), s3, r3, r3, r3)
    return out.reshape((2,) + s.shape[1:])


def _sum_devices(a):
    R_ = a.shape[1]

    def body(a_ref, o_ref):
        acc = a_ref[0]
        for k in range(1, N_DEV):
            acc = acc + a_ref[k]
        o_ref[...] = acc

    return pl.pallas_call(
        body, name="sum_devices", out_shape=jax.ShapeDtypeStruct((R_, 128), F32),
        compiler_params=pltpu.CompilerParams(vmem_limit_bytes=VMEM_LIMIT),
    )(a)


def _adamw(w, g, m, v):
    shape = w.shape
    to2 = lambda a: a.reshape(-1, shape[-1])
    w2, g2, m2, v2 = to2(w), to2(g), to2(m), to2(v)
    R_, C = w2.shape
    tr = 256 if R_ % 256 == 0 else R_

    def body(w_ref, g_ref, m_ref, v_ref, d_ref, mo_ref, vo_ref):
        gv = g_ref[...]
        mn = ADAM_B1 * m_ref[...] + (1.0 - ADAM_B1) * gv
        vn = ADAM_B2 * v_ref[...] + (1.0 - ADAM_B2) * (gv * gv)
        m_hat = mn / (1.0 - ADAM_B1 ** ADAM_STEP)
        v_hat = vn / (1.0 - ADAM_B2 ** ADAM_STEP)
        d_ref[...] = -ADAM_LR * (m_hat / (jnp.sqrt(v_hat) + ADAM_EPS) + ADAM_WD * w_ref[...])
        mo_ref[...] = mn
        vo_ref[...] = vn

    spec = pl.BlockSpec((tr, C), lambda i: (i, 0))
    outs = pl.pallas_call(
        body, name="adamw", grid=(R_ // tr,), in_specs=[spec] * 4, out_specs=[spec] * 3,
        out_shape=[jax.ShapeDtypeStruct((R_, C), F32)] * 3,
        compiler_params=_cparams(("parallel",)),
    )(w2, g2, m2, v2)
    return [o.reshape(shape) for o in outs]


SHARD_AXIS = {"g_pre": 2, "g_post": 2, "g_mem": None, "ffn1_gate": 2, "ffn1_up": 2, "ffn1_down": 1,
              "ffn2_gate": 2, "ffn2_up": 2, "ffn2_down": 1, "w_in_pool": 1, "pool_w": None, "pool_scale": None,
              "w_in_sb": 2, "w_mem_kv": 1, "w_out": 2}
WEIGHTS = list(SHARD_AXIS)
BIG = [k for k in WEIGHTS if SHARD_AXIS[k] is not None and k not in ("g_pre", "g_post")]
SMALL = [k for k in WEIGHTS if k not in BIG]


def _unshard(g, axis):
    _, L_, r, N = g.shape
    if axis == 1:
        return jnp.transpose(g, (1, 0, 2, 3)).reshape(L_, N_CHIPS * r, N)
    return jnp.transpose(g, (1, 2, 0, 3)).reshape(L_, r, N_CHIPS * N)


def _to_shards(a, axis):
    L_, r, N = a.shape
    if axis == 1:
        return jnp.transpose(a.reshape(L_, N_CHIPS, r // N_CHIPS, N), (1, 0, 2, 3))
    return jnp.transpose(a.reshape(L_, r, N_CHIPS, N // N_CHIPS), (2, 0, 1, 3))


def _pack128(arrs):
    flat = jnp.concatenate([a.reshape(-1) for a in arrs])
    pad = (-flat.shape[0]) % (8 * 128)
    return jnp.pad(flat, (0, pad)).reshape(-1, 128)


def _unpack128(buf, shapes):
    flat = buf.reshape(-1)
    out, off = [], 0
    for shp in shapes:
        size = 1
        for d in shp:
            size *= d
        out.append(flat[off:off + size].reshape(shp))
        off += size
    return out


def kernel(x, mem, g_pre, g_post, g_mem, ffn1_gate, ffn1_up, ffn1_down, ffn2_gate, ffn2_up, ffn2_down, w_in_pool, pool_w, pool_scale, w_in_sb, w_mem_kv, w_out, loss_target, m_g_pre, m_g_post, m_g_mem, m_ffn1_gate, m_ffn1_up, m_ffn1_down, m_ffn2_gate, m_ffn2_up, m_ffn2_down, m_w_in_pool, m_pool_w, m_pool_scale, m_w_in_sb, m_w_mem_kv, m_w_out, v_g_pre, v_g_post, v_g_mem, v_ffn1_gate, v_ffn1_up, v_ffn1_down, v_ffn2_gate, v_ffn2_up, v_ffn2_down, v_w_in_pool, v_pool_w, v_pool_scale, v_w_in_sb, v_w_mem_kv, v_w_out):
    w = dict(g_pre=g_pre, g_post=g_post, g_mem=g_mem, ffn1_gate=ffn1_gate, ffn1_up=ffn1_up, ffn1_down=ffn1_down,
             ffn2_gate=ffn2_gate, ffn2_up=ffn2_up, ffn2_down=ffn2_down, w_in_pool=w_in_pool, pool_w=pool_w,
             pool_scale=pool_scale, w_in_sb=w_in_sb, w_mem_kv=w_mem_kv, w_out=w_out)
    m = dict(g_pre=m_g_pre, g_post=m_g_post, g_mem=m_g_mem, ffn1_gate=m_ffn1_gate, ffn1_up=m_ffn1_up,
             ffn1_down=m_ffn1_down, ffn2_gate=m_ffn2_gate, ffn2_up=m_ffn2_up, ffn2_down=m_ffn2_down,
             w_in_pool=m_w_in_pool, pool_w=m_pool_w, pool_scale=m_pool_scale, w_in_sb=m_w_in_sb,
             w_mem_kv=m_w_mem_kv, w_out=m_w_out)
    v = dict(g_pre=v_g_pre, g_post=v_g_post, g_mem=v_g_mem, ffn1_gate=v_ffn1_gate, ffn1_up=v_ffn1_up,
             ffn1_down=v_ffn1_down, ffn2_gate=v_ffn2_gate, ffn2_up=v_ffn2_up, ffn2_down=v_ffn2_down,
             w_in_pool=v_w_in_pool, pool_w=v_pool_w, pool_scale=v_pool_scale, w_in_sb=v_w_in_sb,
             w_mem_kv=v_w_mem_kv, w_out=v_w_out)
    cx, cy, cc = _place()
    chip = (2 * cx + cy).astype(jnp.int32)
    core = cc.astype(jnp.int32)

    gains = _small_allgather(_pack128([g_pre, g_post]))
    per_chip = [_unpack128(gains[2 * j], [g_pre.shape, g_post.shape]) for j in range(N_CHIPS)]
    full_pre = jnp.concatenate([p[0] for p in per_chip], axis=2)
    full_post = jnp.concatenate([p[1] for p in per_chip], axis=2)

    def shard_name(k, i):
        return ("w_in_pool" if _is_pool(i) else "w_in_sb") if k == "w_in" else k

    def layer_weights(i):
        own = [w[shard_name(k, i)][i // 2 if k == "w_in" else i].astype(_MM) for k in LAYER_TENSORS]
        lw = dict(zip(LAYER_TENSORS, _gather_weights(own)))
        if not _is_pool(i):
            lw["w_in"] = jnp.transpose(lw["w_in"], (1, 0, 2)).reshape(1, D, DSB)
        else:
            lw["pool_w"], lw["pool_scale"] = pool_w[i // 2], pool_scale[i // 2].reshape(1, DTOK)
        lw["g_pre"], lw["g_post"], lw["g_mem"] = full_pre[i], full_post[i], g_mem[i].reshape(1, D)
        return lw

    def reduce_layer(gs):
        chip_sums = [_add_sibling(g, r, core) for g, r in zip(gs, _rs_to_sibling(gs))]
        mine = [_add_chips(s, r, chip, core) for s, r in zip(chip_sums, _rs_to_chips(chip_sums))]
        return [s.reshape(-1, s.shape[-1]) for s in _share_halves(mine)]

    xs, mems = x[0], mem[0]
    h, lws, saved = xs, [], []
    for i in range(DEPTH):
        lws.append(layer_weights(i))
        h, sv = _layer_forward(h, mems, lws[i], i)
        saved.append(sv)
    acc, dh = _loss_grad(h, loss_target[0])
    loss = lax.psum(0.5 / D * jnp.sum(acc), ("x", "y", "c"))

    reduced, smalls = [None] * DEPTH, [None] * DEPTH
    for i in reversed(range(DEPTH)):
        dh, gs, smalls[i] = _layer_backward(dh, mems, lws[i], saved[i], i)
        reduced[i] = dict(zip(LAYER_TENSORS, reduce_layer(gs)))
    dx = dh
    red = {}
    for k in BIG:
        if k.startswith("w_in_"):
            layers = [i for i in range(DEPTH) if _is_pool(i) == (k == "w_in_pool")]
            red[k] = jnp.stack([reduced[i]["w_in"] for i in layers])
        else:
            red[k] = jnp.stack([reduced[i][k] for i in range(DEPTH)])

    grads = {"g_pre": jnp.stack([s["g_pre"] for s in smalls]), "g_post": jnp.stack([s["g_post"] for s in smalls]),
             "g_mem": jnp.concatenate([s["g_mem"] for s in smalls], axis=0),
             "pool_w": jnp.stack([smalls[i]["pool_w"] for i in range(DEPTH) if _is_pool(i)]),
             "pool_scale": jnp.concatenate([smalls[i]["pool_scale"] for i in range(DEPTH) if _is_pool(i)], axis=0)}
    small_shapes = [grads[k].shape for k in SMALL]
    summed = _unpack128(_sum_devices(_small_allgather(_pack128([grads[k] for k in SMALL]))), small_shapes)
    for k, s in zip(SMALL, summed):
        if SHARD_AXIS[k] is None:
            red[k] = s
        else:
            red[k] = lax.dynamic_slice_in_dim(s, chip * w[k].shape[2], w[k].shape[2], axis=2)

    deltas, new_m, new_v = {}, {}, {}
    for k in WEIGHTS:
        deltas[k], new_m[k], new_v[k] = _adamw(w[k], red[k], m[k], v[k])
    return (loss, dx.reshape(x.shape), *[red[k] for k in WEIGHTS], *[deltas[k] for k in WEIGHTS],
            *[new_m[k] for k in WEIGHTS], *[new_v[k] for k in WEIGHTS])
```

```python
import jax
import jax.numpy as jnp
from jax import lax
from jax.experimental import pallas as pl
from jax.experimental.pallas import tpu as pltpu

F32 = jnp.float32
BF16 = jnp.bfloat16
_MM = jnp.bfloat16

D = 1024
FF = 2048
DTOK = 512
DMEM = 256
DMIX = DTOK + DMEM
DSB = 3 * DTOK + DMEM
HD = 64
MEM_LEN = 256
DEPTH = 4
EPS = 1e-6
WINDOWS = (2, 4, 8, 16)
HALO = 16
PG = 128
QB = 128
SCALE = HD ** -0.5
FC = 512
NEG_CUT = -104.0

ADAM_LR, ADAM_B1, ADAM_B2, ADAM_EPS, ADAM_WD, ADAM_STEP = 0.001, 0.9, 0.999, 1e-08, 0.01, 10

VMEM_LIMIT = 56 * 2 ** 20
MESH = pl.DeviceIdType.MESH
N_CHIPS = 4
N_DEV = 8


def _cparams(sem):
    return pltpu.CompilerParams(dimension_semantics=sem, vmem_limit_bytes=VMEM_LIMIT)


def _mm(a, b):
    return jnp.dot(a.astype(_MM), b.astype(_MM), preferred_element_type=F32)


def _mm_nt(a, b):
    return lax.dot_general(a.astype(_MM), b.astype(_MM), (((1,), (1,)), ((), ())), preferred_element_type=F32)


def _mm_tn(a, b):
    return lax.dot_general(a.astype(_MM), b.astype(_MM), (((0,), (0,)), ((), ())), preferred_element_type=F32)


def _mm2(a, b):
    hi = a.astype(_MM)
    lo = (a - hi.astype(F32)).astype(_MM)
    return jnp.dot(hi, b, preferred_element_type=F32) + jnp.dot(lo, b, preferred_element_type=F32)


def _rstd(x):
    return lax.rsqrt(jnp.mean(x * x, axis=-1, keepdims=True) + EPS)


def _rms_bwd(x, g, dy):
    r = _rstd(x)
    xh = x * r
    t = dy * g
    dx = r * (t - xh * jnp.mean(t * xh, axis=-1, keepdims=True))
    return dx, jnp.sum(dy * xh, axis=0, keepdims=True)


def _sigmoid(a):
    return 1.0 / (1.0 + jnp.exp(-a))


def _row_tile(n, want):
    t = min(n, want)
    assert n % t == 0, (n, t)
    return t


def _ffn_fwd(h, g1, g2, wg, wu, wd, ride=None):
    S = h.shape[0]
    ts = _row_tile(S, 256)

    def body(h_ref, g1_ref, g2_ref, wg_ref, wu_ref, wd_ref, ho_ref, a_ref, b_ref, f_ref):
        hv = h_ref[...]
        n = (hv * _rstd(hv) * g1_ref[...]).astype(_MM)
        f = jnp.zeros((ts, D), F32)
        for j in range(FF // FC):
            cs = slice(j * FC, (j + 1) * FC)
            a = _mm(n, wg_ref[j])
            b = _mm(n, wu_ref[j])
            a_ref[:, cs] = a.astype(a_ref.dtype)
            b_ref[:, cs] = b.astype(b_ref.dtype)
            f = f + _mm(a * _sigmoid(a) * b, wd_ref[j])
        f_ref[...] = f
        ho_ref[...] = hv + 0.5 * (f * _rstd(f) * g2_ref[...])

    row = lambda i: (i, 0)
    fix = lambda i: (0, 0)
    all3 = lambda i: (0, 0, 0)
    ns = FF // FC
    return _call(
        body, "ffn_fwd", S // ts,
        [pl.BlockSpec((ts, D), row), pl.BlockSpec((1, D), fix), pl.BlockSpec((1, D), fix),
         pl.BlockSpec((ns, D, FC), all3), pl.BlockSpec((ns, D, FC), all3), pl.BlockSpec((ns, FC, D), all3)],
        [pl.BlockSpec((ts, D), row), pl.BlockSpec((ts, FF), row), pl.BlockSpec((ts, FF), row),
         pl.BlockSpec((ts, D), row)],
        [jax.ShapeDtypeStruct((S, D), F32), jax.ShapeDtypeStruct((S, FF), _MM),
         jax.ShapeDtypeStruct((S, FF), _MM), jax.ShapeDtypeStruct((S, D), F32)],
        (h, g1, g2, wg, wu, wd), ride)


def _ffn_bwd(h, f, a, b, dho, g1, g2, wg, wu, wd, ride=None):
    S = h.shape[0]
    ts = _row_tile(S, 256)

    def body(h_ref, f_ref, a_ref, b_ref, dho_ref, g1_ref, g2_ref, wg_ref, wu_ref, wd_ref,
             dh_ref, n_ref, s_ref, da_ref, db_ref, df_ref, dg1_ref, dg2_ref):
        @pl.when(pl.program_id(0) == 0)
        def _():
            dg1_ref[...] = jnp.zeros_like(dg1_ref)
            dg2_ref[...] = jnp.zeros_like(dg2_ref)

        hv = h_ref[...]
        dho = dho_ref[...]
        df, dg2 = _rms_bwd(f_ref[...], g2_ref[...], 0.5 * dho)
        dg2_ref[...] += dg2
        dfb = df.astype(_MM)
        df_ref[...] = dfb
        rh = _rstd(hv)
        hh = hv * rh
        g1 = g1_ref[...]
        n_ref[...] = (hh * g1).astype(_MM)
        dn = jnp.zeros((ts, D), F32)
        for j in range(FF // FC):
            cs = slice(j * FC, (j + 1) * FC)
            av = a_ref[:, cs].astype(F32)
            bv = b_ref[:, cs].astype(F32)
            sig = _sigmoid(av)
            sl = av * sig
            s_ref[:, cs] = (sl * bv).astype(_MM)
            ds = _mm_nt(dfb, wd_ref[j])
            da = (ds * bv * (sig * (1.0 + av * (1.0 - sig)))).astype(_MM)
            db = (ds * sl).astype(_MM)
            da_ref[:, cs] = da
            db_ref[:, cs] = db
            dn = dn + _mm_nt(da, wg_ref[j]) + _mm_nt(db, wu_ref[j])
        t = dn * g1
        dh_ref[...] = dho + rh * (t - hh * jnp.mean(t * hh, axis=-1, keepdims=True))
        dg1_ref[...] += jnp.sum(dn * hh, axis=0, keepdims=True)

    row = lambda i: (i, 0)
    fix = lambda i: (0, 0)
    all3 = lambda i: (0, 0, 0)
    ns = FF // FC
    td = pl.BlockSpec((ts, D), row)
    tf = pl.BlockSpec((ts, FF), row)
    gd = pl.BlockSpec((1, D), fix)
    return _call(
        body, "ffn_bwd", S // ts,
        [td, td, tf, tf, td, gd, gd,
         pl.BlockSpec((ns, D, FC), all3), pl.BlockSpec((ns, D, FC), all3), pl.BlockSpec((ns, FC, D), all3)],
        [td, td, tf, tf, tf, td, gd, gd],
        [jax.ShapeDtypeStruct((S, D), F32), jax.ShapeDtypeStruct((S, D), _MM),
         jax.ShapeDtypeStruct((S, FF), _MM), jax.ShapeDtypeStruct((S, FF), _MM),
         jax.ShapeDtypeStruct((S, FF), _MM), jax.ShapeDtypeStruct((S, D), _MM),
         jax.ShapeDtypeStruct((1, D), F32), jax.ShapeDtypeStruct((1, D), F32)],
        (h, f, a, b, dho, g1, g2, wg, wu, wd), ride)


def _wgrad(x, y, nslot=0):
    T, M = x.shape
    N = y.shape[1]
    tt = _row_tile(T, 1024)
    cw = N // nslot if nslot else (512 if N % 512 == 0 else 256)
    nc = N // cw
    assert cw * nc == N and cw % 128 == 0, (N, nc)

    def body(x_ref, y_ref, o_ref):
        @pl.when(pl.program_id(0) == 0)
        def _():
            o_ref[...] = jnp.zeros_like(o_ref)

        xt = x_ref[...].T
        for j in range(nc):
            part = jnp.dot(xt, y_ref[:, j * cw:(j + 1) * cw], preferred_element_type=F32)
            if nslot:
                o_ref[j] += part
            else:
                o_ref[:, j * cw:(j + 1) * cw] += part

    oshape = (nslot, M, cw) if nslot else (M, N)
    return pl.pallas_call(
        body, name="wgrad", grid=(T // tt,),
        in_specs=[pl.BlockSpec((tt, M), lambda t: (t, 0)), pl.BlockSpec((tt, N), lambda t: (t, 0))],
        out_specs=pl.BlockSpec(oshape, lambda t: (0,) * len(oshape)),
        out_shape=jax.ShapeDtypeStruct(oshape, F32),
        compiler_params=_cparams(("arbitrary",)),
    )(x, y)


def _proj_fwd(h, g, w, out_dtype):
    S = h.shape[0]
    nk, kw, N = w.shape
    ts = _row_tile(S, 256)

    def body(h_ref, g_ref, w_ref, u_ref, p_ref):
        hv = h_ref[...]
        u = (hv * _rstd(hv) * g_ref[...]).astype(_MM)
        u_ref[...] = u
        acc = _mm(u[:, :kw], w_ref[0])
        for j in range(1, nk):
            acc = acc + _mm(u[:, j * kw:(j + 1) * kw], w_ref[j])
        p_ref[...] = acc.astype(p_ref.dtype)

    return pl.pallas_call(
        body, name="proj_fwd", grid=(S // ts,),
        in_specs=[pl.BlockSpec((ts, D), lambda i: (i, 0)), pl.BlockSpec((1, D), lambda i: (0, 0)),
                  pl.BlockSpec((nk, kw, N), lambda i: (0, 0, 0))],
        out_specs=[pl.BlockSpec((ts, D), lambda i: (i, 0)), pl.BlockSpec((ts, N), lambda i: (i, 0))],
        out_shape=[jax.ShapeDtypeStruct((S, D), _MM), jax.ShapeDtypeStruct((S, N), out_dtype)],
        compiler_params=_cparams(("parallel",)),
    )(h, g, w)


def _proj_bwd(h, dho, g, w, pieces):
    S = h.shape[0]
    nk, kw, N = w.shape
    ts = _row_tile(S, 256)
    widths = [p.shape[1] for p in pieces]
    assert sum(widths) == N
    npc = len(pieces)

    def body(*refs):
        h_ref, dho_ref, g_ref, w_ref = refs[:4]
        p_refs = refs[4:4 + npc]
        dh_ref, dp_ref, dg_ref = refs[4 + npc:]

        @pl.when(pl.program_id(0) == 0)
        def _():
            dg_ref[...] = jnp.zeros_like(dg_ref)

        dus = [jnp.zeros((ts, kw), F32) for _ in range(nk)]
        off = 0
        for p_ref, n in zip(p_refs, widths):
            pv = p_ref[...].astype(_MM)
            dp_ref[:, off:off + n] = pv
            for j in range(nk):
                dus[j] = dus[j] + _mm_nt(pv, w_ref[j, :, off:off + n])
            off += n
        du = dus[0] if nk == 1 else jnp.concatenate(dus, axis=-1)
        dx, dg = _rms_bwd(h_ref[...], g_ref[...], du)
        dh_ref[...] = dho_ref[...] + dx
        dg_ref[...] += dg

    row = lambda i: (i, 0)
    return pl.pallas_call(
        body, name="proj_bwd", grid=(S // ts,),
        in_specs=[pl.BlockSpec((ts, D), row), pl.BlockSpec((ts, D), row), pl.BlockSpec((1, D), lambda i: (0, 0)),
                  pl.BlockSpec((nk, kw, N), lambda i: (0, 0, 0))] + [pl.BlockSpec((ts, n), row) for n in widths],
        out_specs=[pl.BlockSpec((ts, D), row), pl.BlockSpec((ts, N), row), pl.BlockSpec((1, D), lambda i: (0, 0))],
        out_shape=[jax.ShapeDtypeStruct((S, D), F32), jax.ShapeDtypeStruct((S, N), _MM),
                   jax.ShapeDtypeStruct((1, D), F32)],
        compiler_params=_cparams(("arbitrary",)),
    )(h, dho, g, w, *pieces)


def _pool_counts(first_row, ts):
    pos = (first_row + lax.broadcasted_iota(jnp.int32, (ts, 1), 0) + 1).astype(F32)
    return [jnp.minimum(pos, float(w)) for w in WINDOWS]


def _pool_delta(x, prev, cnts, ts):
    xe = jnp.concatenate([prev, x], axis=0)
    sums = []
    cur = xe
    for sh in (1, 2, 4, 8):
        cur = cur + pltpu.roll(cur, sh, 0)
        sums.append(cur)
    return [sums[gi][HALO:, gi * PG:(gi + 1) * PG] / cnts[gi] - x[:, gi * PG:(gi + 1) * PG]
            for gi in range(len(WINDOWS))]


def _pool_fwd(proj, pw, ps):
    S = proj.shape[0]
    ts = _row_tile(S, 256)

    def body(x_ref, pw_ref, ps_ref, tok_ref, carry_ref):
        i = pl.program_id(0)

        @pl.when(i == 0)
        def _():
            carry_ref[...] = jnp.zeros_like(carry_ref)

        x = x_ref[...]
        ds = _pool_delta(x, carry_ref[...], _pool_counts(i * ts, ts), ts)
        y = jnp.concatenate([_mm(ds[gi], pw_ref[gi]) for gi in range(len(WINDOWS))], axis=-1)
        tok_ref[...] = (y * ps_ref[...]).astype(tok_ref.dtype)
        carry_ref[...] = x[ts - HALO:, :]

    return pl.pallas_call(
        body, name="pool_fwd", grid=(S // ts,),
        in_specs=[pl.BlockSpec((ts, DTOK), lambda i: (i, 0)), pl.BlockSpec((len(WINDOWS), PG, PG), lambda i: (0, 0, 0)),
                  pl.BlockSpec((1, DTOK), lambda i: (0, 0))],
        out_specs=pl.BlockSpec((ts, DTOK), lambda i: (i, 0)),
        out_shape=jax.ShapeDtypeStruct((S, DTOK), _MM),
        scratch_shapes=[pltpu.VMEM((HALO, DTOK), F32)],
        compiler_params=_cparams(("arbitrary",)),
    )(proj, pw, ps)


def _pool_bwd(proj, dtok, pw, ps):
    S = proj.shape[0]
    ts = _row_tile(S, 256)
    nt = S // ts
    per = ts // HALO
    ng = len(WINDOWS)

    def body(x_ref, xp_ref, dt_ref, pw_ref, ps_ref, dx_ref, dpw_ref, dps_ref, carry_ref):
        i = pl.program_id(0)
        idx = nt - 1 - i

        @pl.when(i == 0)
        def _():
            carry_ref[...] = jnp.zeros_like(carry_ref)
            dpw_ref[...] = jnp.zeros_like(dpw_ref)
            dps_ref[...] = jnp.zeros_like(dps_ref)

        x = x_ref[...]
        prev = jnp.where(idx > 0, xp_ref[...], 0.0)
        cnts = _pool_counts(idx * ts, ts)
        ds = _pool_delta(x, prev, cnts, ts)
        dt = dt_ref[...]
        y = jnp.concatenate([_mm(ds[gi], pw_ref[gi]) for gi in range(ng)], axis=-1)
        dps_ref[...] += jnp.sum(dt * y, axis=0, keepdims=True)
        dy = (dt * ps_ref[...]).astype(_MM)
        dds = []
        for gi in range(ng):
            dyg = dy[:, gi * PG:(gi + 1) * PG]
            dpw_ref[gi] += _mm_tn(ds[gi], dyg)
            dds.append(_mm_nt(dyg, pw_ref[gi]))
        e = jnp.concatenate([dds[gi] / cnts[gi] for gi in range(ng)], axis=-1)
        ee = jnp.concatenate([e, carry_ref[...]], axis=0)
        rows = ts + HALO
        cur = ee
        outs = []
        for gi, sh in enumerate((1, 2, 4, 8)):
            cur = cur + pltpu.roll(cur, rows - sh, 0)
            outs.append(cur[:ts, gi * PG:(gi + 1) * PG] - dds[gi])
        dx_ref[...] = jnp.concatenate(outs, axis=-1)
        carry_ref[...] = e[:HALO, :]

    rev = lambda i: (nt - 1 - i, 0)
    return pl.pallas_call(
        body, name="pool_bwd", grid=(nt,),
        in_specs=[pl.BlockSpec((ts, DTOK), rev),
                  pl.BlockSpec((HALO, DTOK), lambda i: (jnp.maximum((nt - 1 - i) * per - 1, 0), 0)),
                  pl.BlockSpec((ts, DTOK), rev), pl.BlockSpec((ng, PG, PG), lambda i: (0, 0, 0)),
                  pl.BlockSpec((1, DTOK), lambda i: (0, 0))],
        out_specs=[pl.BlockSpec((ts, DTOK), rev), pl.BlockSpec((ng, PG, PG), lambda i: (0, 0, 0)),
                   pl.BlockSpec((1, DTOK), lambda i: (0, 0))],
        out_shape=[jax.ShapeDtypeStruct((S, DTOK), F32), jax.ShapeDtypeStruct((ng, PG, PG), F32),
                   jax.ShapeDtypeStruct((1, DTOK), F32)],
        scratch_shapes=[pltpu.VMEM((HALO, DTOK), F32)],
        compiler_params=_cparams(("arbitrary",)),
    )(proj, proj, dtok, pw, ps)


def _tri(strict):
    r = lax.broadcasted_iota(jnp.int32, (QB, QB), 0)
    c = lax.broadcasted_iota(jnp.int32, (QB, QB), 1)
    return jnp.where(r > c if strict else r >= c, 1.0, 0.0).astype(_MM)


def _sb_terms(z, valid):
    e = jnp.exp(-jnp.abs(z))
    sp = jnp.log(1.0 + e)
    ls = jnp.minimum(z, 0.0) - sp
    lf = jnp.where(valid, jnp.minimum(-z, 0.0) - sp, 0.0)
    return ls, lf, e


def _sb_alive(cs):
    top = cs[0]
    for c in cs[1:]:
        top = jnp.maximum(top, c)
    return (jnp.max(top) > NEG_CUT).astype(jnp.int32)


def _sb_split(x2, head_a):
    zero = jnp.zeros_like(x2)
    return jnp.where(head_a, x2, zero), jnp.where(head_a, zero, x2)


SB_GROUPS_FWD = 4
SB_GROUPS_BWD = 4


def _sb_fwd(proj):
    S = proj.shape[0]
    G = min(SB_GROUPS_FWD, S // QB)
    nstep = S // (G * QB)
    npair = DTOK // 128
    nch = 2 * G

    def body(q_ref, k_ref, v_ref, tri_ref, o_ref):
        qb = pl.program_id(1)
        head_a = lax.broadcasted_iota(jnp.int32, (QB, 128), 1) < HD
        qs = []
        for g in range(G):
            qs.extend(_sb_split(q_ref[g * QB:(g + 1) * QB, :], head_a))
        row = lax.broadcasted_iota(jnp.int32, (QB, QB), 0)
        col = lax.broadcasted_iota(jnp.int32, (QB, QB), 1)
        tri = tri_ref[...]
        last = G * qb + G - 1

        def step(carry):
            i, _, cs, accs = carry
            causal = (col - row) < i * QB
            ks, vs, valids = [], [], []
            for g in range(G):
                kb = G * qb + g - i
                off = pl.multiple_of(jnp.maximum(kb, 0) * QB, QB)
                ks.append(k_ref[pl.ds(off, QB), :])
                vs.append(v_ref[pl.ds(off, QB), :])
                valids.append(jnp.logical_and(causal, kb >= 0))
            zs = [_mm_nt(qs[ch], ks[ch // 2]) * SCALE for ch in range(nch)]
            terms = [_sb_terms(zs[ch], valids[ch // 2]) for ch in range(nch)]
            withins = [_mm2(terms[ch][1], tri) for ch in range(nch)]
            new_c, new_acc = [], []
            for ch in range(nch):
                ls, lf, _ = terms[ch]
                a = jnp.where(valids[ch // 2], jnp.exp(ls + withins[ch] + cs[ch]), 0.0)
                new_acc.append(accs[ch] + _mm2(a, vs[ch // 2]))
                new_c.append(cs[ch] + jnp.sum(lf, axis=-1, keepdims=True))
            return i + 1, _sb_alive(new_c), tuple(new_c), tuple(new_acc)

        zc = jnp.zeros((QB, 1), F32)
        za = jnp.zeros((QB, 128), F32)
        more = lambda cr: jnp.logical_and(cr[0] <= last, cr[1] > 0)
        accs = lax.while_loop(more, step, (0, 1, (zc,) * nch, (za,) * nch))[3]
        for g in range(G):
            o_ref[g * QB:(g + 1) * QB, :] = jnp.where(head_a, accs[2 * g], accs[2 * g + 1])

    return pl.pallas_call(
        body, name="sb_fwd", grid=(npair, nstep),
        in_specs=[pl.BlockSpec((G * QB, 128), lambda p, i: (i, p)),
                  pl.BlockSpec((S, 128), lambda p, i: (0, npair + p)),
                  pl.BlockSpec((S, 128), lambda p, i: (0, 2 * npair + p)),
                  pl.BlockSpec((QB, QB), lambda p, i: (0, 0))],
        out_specs=pl.BlockSpec((G * QB, 128), lambda p, i: (i, p)),
        out_shape=jax.ShapeDtypeStruct((S, DTOK), F32),
        compiler_params=_cparams(("parallel", "parallel")),
    )(proj, proj, proj, _tri(True))


def _sb_bwd(proj, o, do):
    S = proj.shape[0]
    G = min(SB_GROUPS_BWD, S // QB)
    nstep = S // (G * QB)
    npair = DTOK // 128
    nch = 2 * G

    def body(q_ref, k_ref, v_ref, o_ref, do_ref, tri_ref, tri2_ref, dq_ref, dk_ref, dv_ref):
        qb = pl.program_id(1)

        @pl.when(qb == 0)
        def _():
            dk_ref[...] = jnp.zeros_like(dk_ref)
            dv_ref[...] = jnp.zeros_like(dv_ref)

        head_a = lax.broadcasted_iota(jnp.int32, (QB, 128), 1) < HD
        qs, dos, gs = [], [], []
        for g in range(G):
            rows = slice(g * QB, (g + 1) * QB)
            qs.extend(_sb_split(q_ref[rows, :], head_a))
            dob = do_ref[rows, :].astype(_MM)
            dos.extend(_sb_split(dob, head_a))
            go = dob.astype(F32) * o_ref[rows, :]
            gs.append(jnp.sum(jnp.where(head_a, go, 0.0), axis=-1, keepdims=True))
            gs.append(jnp.sum(jnp.where(head_a, 0.0, go), axis=-1, keepdims=True))
        row = lax.broadcasted_iota(jnp.int32, (QB, QB), 0)
        col = lax.broadcasted_iota(jnp.int32, (QB, QB), 1)
        tri = tri_ref[...]
        tri2 = tri2_ref[...]
        last = G * qb + G - 1

        def step(carry):
            i, _, cs, rs, dqs = carry
            causal = (col - row) < i * QB
            offs, ks, vs, valids = [], [], [], []
            for g in range(G):
                kb = G * qb + g - i
                off = pl.multiple_of(jnp.maximum(kb, 0) * QB, QB)
                offs.append(off)
                ks.append(k_ref[pl.ds(off, QB), :])
                vs.append(v_ref[pl.ds(off, QB), :])
                valids.append(jnp.logical_and(causal, kb >= 0))
            zs = [_mm_nt(qs[ch], ks[ch // 2]) * SCALE for ch in range(nch)]
            das = [_mm_nt(dos[ch], vs[ch // 2]) for ch in range(nch)]
            terms = [_sb_terms(zs[ch], valids[ch // 2]) for ch in range(nch)]
            withins = [_mm2(terms[ch][1], tri) for ch in range(nch)]
            a_s, dlws = [], []
            for ch in range(nch):
                a = jnp.where(valids[ch // 2], jnp.exp(terms[ch][0] + withins[ch] + cs[ch]), 0.0)
                a_s.append(a)
                dlws.append(das[ch] * a)
            sfx = [_mm2(dlws[ch], tri2) for ch in range(nch)]
            dzs = []
            for ch in range(nch):
                e = terms[ch][2]
                inv = 1.0 / (1.0 + e)
                pos = zs[ch] >= 0.0
                beta = jnp.where(pos, 1.0, e) * inv
                omb = jnp.where(pos, e, 1.0) * inv
                prefix = gs[ch] - rs[ch] - sfx[ch]
                dzs.append((jnp.where(valids[ch // 2], dlws[ch] * omb - beta * prefix, 0.0) * SCALE).astype(_MM))
            new_dq = [dqs[ch] + _mm(dzs[ch], ks[ch // 2]) for ch in range(nch)]
            for g in range(G):
                dk_ref[pl.ds(offs[g], QB), :] += (_mm_tn(dzs[2 * g], qs[2 * g]) + _mm_tn(dzs[2 * g + 1], qs[2 * g + 1]))
                dv_ref[pl.ds(offs[g], QB), :] += (_mm_tn(a_s[2 * g], dos[2 * g]) + _mm_tn(a_s[2 * g + 1], dos[2 * g + 1]))
            new_c = [cs[ch] + jnp.sum(terms[ch][1], axis=-1, keepdims=True) for ch in range(nch)]
            new_r = [rs[ch] + jnp.sum(dlws[ch], axis=-1, keepdims=True) for ch in range(nch)]
            return i + 1, _sb_alive(new_c), tuple(new_c), tuple(new_r), tuple(new_dq)

        zc = jnp.zeros((QB, 1), F32)
        za = jnp.zeros((QB, 128), F32)
        more = lambda cr: jnp.logical_and(cr[0] <= last, cr[1] > 0)
        dqs = lax.while_loop(more, step, (0, 1, (zc,) * nch, (zc,) * nch, (za,) * nch))[4]
        for g in range(G):
            dq_ref[g * QB:(g + 1) * QB, :] = jnp.where(head_a, dqs[2 * g], dqs[2 * g + 1])

    qspec = pl.BlockSpec((G * QB, 128), lambda p, i: (i, p))
    full = lambda base: pl.BlockSpec((S, 128), lambda p, i: (0, base + p))
    tspec = pl.BlockSpec((QB, QB), lambda p, i: (0, 0))
    return pl.pallas_call(
        body, name="sb_bwd", grid=(npair, nstep),
        in_specs=[qspec, full(npair), full(2 * npair), qspec, qspec, tspec, tspec],
        out_specs=[qspec, full(0), full(0)],
        out_shape=[jax.ShapeDtypeStruct((S, DTOK), F32)] * 3,
        compiler_params=_cparams(("parallel", "arbitrary")),
    )(proj, proj, proj, o, do, _tri(True), _tri(False))


def _memkv_fwd(mem, gm, wkv):
    nk, kw, _ = wkv.shape

    def body(m_ref, g_ref, w_ref, kv_ref):
        m = m_ref[...]
        mn = (m * _rstd(m) * g_ref[...]).astype(_MM)
        acc = _mm(mn[:, :kw], w_ref[0])
        for j in range(1, nk):
            acc = acc + _mm(mn[:, j * kw:(j + 1) * kw], w_ref[j])
        kv_ref[...] = acc.astype(kv_ref.dtype)

    return pl.pallas_call(
        body, name="memkv_fwd", out_shape=jax.ShapeDtypeStruct((MEM_LEN, 2 * DMEM), _MM),
        compiler_params=pltpu.CompilerParams(vmem_limit_bytes=VMEM_LIMIT),
    )(mem, gm, wkv)


def _memkv_bwd(mem, gm, wkv, dkv):
    nk, kw, _ = wkv.shape

    def body(m_ref, g_ref, w_ref, dkv_ref, dw_ref, dg_ref):
        m = m_ref[...]
        mh = m * _rstd(m)
        mn = (mh * g_ref[...]).astype(_MM)
        dkv = dkv_ref[...].astype(_MM)
        dw_ref[...] = _mm_tn(mn, dkv)
        dmn = jnp.concatenate([_mm_nt(dkv, w_ref[j]) for j in range(nk)], axis=-1)
        dg_ref[...] = jnp.sum(dmn * mh, axis=0, keepdims=True)

    return pl.pallas_call(
        body, name="memkv_bwd",
        out_shape=[jax.ShapeDtypeStruct((D, 2 * DMEM), F32), jax.ShapeDtypeStruct((1, D), F32)],
        compiler_params=pltpu.CompilerParams(vmem_limit_bytes=VMEM_LIMIT),
    )(mem, gm, wkv, dkv)


def _mem_probs(qh, kp):
    s = _mm_nt(qh, kp) * SCALE
    p = jnp.exp(s - jnp.max(s, axis=-1, keepdims=True))
    return p / jnp.sum(p, axis=-1, keepdims=True)


def _mixout_fwd(h, tok, proj, kv, wo, g, qm_block):
    S = h.shape[0]
    ts = _row_tile(S, 256)

    def body(h_ref, tok_ref, qm_ref, kv_ref, wo_ref, g_ref, ho_ref, cat_ref, mix_ref):
        head_a = lax.broadcasted_iota(jnp.int32, (ts, 128), 1) < HD
        qm = qm_ref[...].astype(_MM)
        zq = jnp.zeros((ts, 128), _MM)
        mos = []
        for pi in range(DMEM // 128):
            cs = slice(pi * 128, (pi + 1) * 128)
            qp = qm[:, cs]
            kp = kv_ref[:, cs]
            vp = kv_ref[:, DMEM + pi * 128:DMEM + (pi + 1) * 128]
            pa = _mem_probs(jnp.where(head_a, qp, zq), kp)
            pb = _mem_probs(jnp.where(head_a, zq, qp), kp)
            mos.append(jnp.where(head_a, _mm(pa, vp), _mm(pb, vp)))
        cat = jnp.concatenate([tok_ref[...].astype(_MM)] + [m.astype(_MM) for m in mos], axis=-1)
        cat_ref[...] = cat
        mix = jnp.concatenate([_mm(cat, wo_ref[j]) for j in range(N_CHIPS)], axis=-1)
        mix_ref[...] = mix
        ho_ref[...] = h_ref[...] + mix * _rstd(mix) * g_ref[...]

    row = lambda i: (i, 0)
    return pl.pallas_call(
        body, name="mixout_fwd", grid=(S // ts,),
        in_specs=[pl.BlockSpec((ts, D), row), pl.BlockSpec((ts, DTOK), row),
                  pl.BlockSpec((ts, DMEM), lambda i: (i, qm_block)),
                  pl.BlockSpec((MEM_LEN, 2 * DMEM), lambda i: (0, 0)),
                  pl.BlockSpec((N_CHIPS, DMIX, D // N_CHIPS), lambda i: (0, 0, 0)), pl.BlockSpec((1, D), lambda i: (0, 0))],
        out_specs=[pl.BlockSpec((ts, D), row), pl.BlockSpec((ts, DMIX), row), pl.BlockSpec((ts, D), row)],
        out_shape=[jax.ShapeDtypeStruct((S, D), F32), jax.ShapeDtypeStruct((S, DMIX), _MM),
                   jax.ShapeDtypeStruct((S, D), F32)],
        compiler_params=_cparams(("parallel",)),
    )(h, tok, proj, kv, wo, g)


def _mixout_bwd(dho, mix, proj, kv, wo, g, qm_block):
    S = dho.shape[0]
    ts = _row_tile(S, 256)

    def body(dho_ref, mix_ref, qm_ref, kv_ref, wo_ref, g_ref, dmix_ref, dtok_ref, dqm_ref, dkv_ref, dg_ref):
        @pl.when(pl.program_id(0) == 0)
        def _():
            dkv_ref[...] = jnp.zeros_like(dkv_ref)
            dg_ref[...] = jnp.zeros_like(dg_ref)

        dmix, dg = _rms_bwd(mix_ref[...], g_ref[...], dho_ref[...])
        dg_ref[...] += dg
        dmb = dmix.astype(_MM)
        dmix_ref[...] = dmb
        cw = D // N_CHIPS
        dcat = _mm_nt(dmb[:, :cw], wo_ref[0])
        for j in range(1, N_CHIPS):
            dcat = dcat + _mm_nt(dmb[:, j * cw:(j + 1) * cw], wo_ref[j])
        dtok_ref[...] = dcat[:, :DTOK]
        head_a = lax.broadcasted_iota(jnp.int32, (ts, 128), 1) < HD
        qm = qm_ref[...].astype(_MM)
        zq = jnp.zeros((ts, 128), _MM)
        for pi in range(DMEM // 128):
            cs = slice(pi * 128, (pi + 1) * 128)
            vs = slice(DMEM + pi * 128, DMEM + (pi + 1) * 128)
            kp = kv_ref[:, cs]
            vp = kv_ref[:, vs]
            dmo = dcat[:, DTOK + pi * 128:DTOK + (pi + 1) * 128].astype(_MM)
            dqs = []
            dk_acc = jnp.zeros((MEM_LEN, 128), F32)
            dv_acc = jnp.zeros((MEM_LEN, 128), F32)
            for hd in range(2):
                qh = jnp.where(head_a, qm[:, cs], zq) if hd == 0 else jnp.where(head_a, zq, qm[:, cs])
                dmh = jnp.where(head_a, dmo, zq) if hd == 0 else jnp.where(head_a, zq, dmo)
                p = _mem_probs(qh, kp)
                dp = _mm_nt(dmh, vp)
                ds = (p * (dp - jnp.sum(dp * p, axis=-1, keepdims=True)) * SCALE).astype(_MM)
                dqs.append(_mm(ds, kp))
                dk_acc = dk_acc + _mm_tn(ds, qh)
                dv_acc = dv_acc + _mm_tn(p, dmh)
            dqm_ref[:, cs] = jnp.where(head_a, dqs[0], dqs[1]).astype(_MM)
            dkv_ref[:, cs] += dk_acc
            dkv_ref[:, vs] += dv_acc

    row = lambda i: (i, 0)
    fix = lambda i: (0, 0)
    return pl.pallas_call(
        body, name="mixout_bwd", grid=(S // ts,),
        in_specs=[pl.BlockSpec((ts, D), row), pl.BlockSpec((ts, D), row),
                  pl.BlockSpec((ts, DMEM), lambda i: (i, qm_block)),
                  pl.BlockSpec((MEM_LEN, 2 * DMEM), fix),
                  pl.BlockSpec((N_CHIPS, DMIX, D // N_CHIPS), lambda i: (0, 0, 0)), pl.BlockSpec((1, D), fix)],
        out_specs=[pl.BlockSpec((ts, D), row), pl.BlockSpec((ts, DTOK), row), pl.BlockSpec((ts, DMEM), row),
                   pl.BlockSpec((MEM_LEN, 2 * DMEM), fix), pl.BlockSpec((1, D), fix)],
        out_shape=[jax.ShapeDtypeStruct((S, D), _MM), jax.ShapeDtypeStruct((S, DTOK), F32),
                   jax.ShapeDtypeStruct((S, DMEM), _MM), jax.ShapeDtypeStruct((MEM_LEN, 2 * DMEM), F32),
                   jax.ShapeDtypeStruct((1, D), F32)],
        compiler_params=_cparams(("arbitrary",)),
    )(dho, mix, proj, kv, wo, g)


def _loss_grad(y, target):
    S = y.shape[0]
    ts = _row_tile(S, 512)

    def body(y_ref, t_ref, acc_ref, dy_ref):
        @pl.when(pl.program_id(0) == 0)
        def _():
            acc_ref[...] = jnp.zeros_like(acc_ref)

        err = y_ref[...] - t_ref[...]
        acc_ref[...] += jnp.sum(err * err, axis=0, keepdims=True)
        dy_ref[...] = err * (1.0 / D)

    row = lambda i: (i, 0)
    return pl.pallas_call(
        body, name="loss_grad", grid=(S // ts,),
        in_specs=[pl.BlockSpec((ts, D), row), pl.BlockSpec((ts, D), row)],
        out_specs=[pl.BlockSpec((1, D), lambda i: (0, 0)), pl.BlockSpec((ts, D), row)],
        out_shape=[jax.ShapeDtypeStruct((1, D), F32), jax.ShapeDtypeStruct((S, D), F32)],
        compiler_params=_cparams(("arbitrary",)),
    )(y, target)


LAYER_TENSORS = ("ffn1_gate", "ffn1_up", "ffn1_down", "ffn2_gate", "ffn2_up", "ffn2_down", "w_in", "w_mem_kv", "w_out")
SPLITS = {"ffn1_gate": "c", "ffn1_up": "c", "ffn1_down": "r", "ffn2_gate": "c", "ffn2_up": "c", "ffn2_down": "r",
          "w_in_pool": "r", "w_in_sb": "c", "w_mem_kv": "r", "w_out": "c"}


def _is_pool(i):
    return i % 2 == 0


def _qm_block(i):
    return (DTOK if _is_pool(i) else 3 * DTOK) // DMEM


FIRST_FFN = LAYER_TENSORS[:3]
REST = LAYER_TENSORS[3:]


def _layer_forward(h, mem, lw, i, rest_shards, next_shards):
    g_at = lambda g, k: g[k].reshape(1, D)
    sv = {"h0": h}
    (h, sv["a1"], sv["b1"], sv["f1"]), rest = _ffn_fwd(
        h, g_at(lw["g_pre"], 0), g_at(lw["g_post"], 0), lw["ffn1_gate"], lw["ffn1_up"], lw["ffn1_down"],
        _GatherExchange(rest_shards))
    lw.update(zip(REST, rest))
    if not _is_pool(i):
        lw["w_in"] = jnp.transpose(lw["w_in"], (1, 0, 2)).reshape(1, D, DSB)
    sv["h1"] = h
    sv["kv"] = _memkv_fwd(mem, lw["g_mem"], lw["w_mem_kv"])
    if _is_pool(i):
        sv["u"], sv["proj"] = _proj_fwd(h, g_at(lw["g_pre"], 1), lw["w_in"], F32)
        tok = _pool_fwd(sv["proj"], lw["pool_w"], lw["pool_scale"])
    else:
        sv["u"], sv["proj"] = _proj_fwd(h, g_at(lw["g_pre"], 1), lw["w_in"], _MM)
        tok = sv["o"] = _sb_fwd(sv["proj"])
    h, sv["cat"], sv["mix"] = _mixout_fwd(h, tok, sv["proj"], sv["kv"], lw["w_out"], g_at(lw["g_post"], 1),
                                          _qm_block(i))
    sv["h2"] = h
    ffn2 = (h, g_at(lw["g_pre"], 2), g_at(lw["g_post"], 2), lw["ffn2_gate"], lw["ffn2_up"], lw["ffn2_down"])
    if next_shards is None:
        h, sv["a2"], sv["b2"], sv["f2"] = _ffn_fwd(*ffn2)
        return h, sv, None
    (h, sv["a2"], sv["b2"], sv["f2"]), nxt = _ffn_fwd(*ffn2, _GatherExchange(next_shards))
    return h, sv, nxt


def _layer_backward(dh, mem, lw, sv, i, ride2=None, make_ride1=None):
    g_at = lambda g, k: g[k].reshape(1, D)
    big, small = {}, {}
    dgpre, dgpost = [None] * 3, [None] * 3
    rode = {}

    def ffn_back(which, dh, h_in, ride):
        k = 0 if which == 1 else 2
        n = f"ffn{which}"
        outs = _ffn_bwd(
            h_in, sv[f"f{which}"], sv[f"a{which}"], sv[f"b{which}"], dh, g_at(lw["g_pre"], k), g_at(lw["g_post"], k),
            lw[n + "_gate"], lw[n + "_up"], lw[n + "_down"], ride)
        if ride is not None:
            outs, rode[which] = outs
        dh, nb, sb, da, db, df, dgpre[k], dgpost[k] = outs
        big[n + "_gate"] = _wgrad(nb, da, N_CHIPS)
        big[n + "_up"] = _wgrad(nb, db, N_CHIPS)
        big[n + "_down"] = _wgrad(sb, df).reshape(N_CHIPS, FF // N_CHIPS, D)
        return dh

    dh = ffn_back(2, dh, sv["h2"], ride2)
    ride1 = make_ride1(rode.get(2)) if make_ride1 is not None else None
    dmix, dtok, dqm, dkv, dgpost[1] = _mixout_bwd(dh, sv["mix"], sv["proj"], sv["kv"], lw["w_out"],
                                                  g_at(lw["g_post"], 1), _qm_block(i))
    big["w_out"] = _wgrad(sv["cat"], dmix, N_CHIPS)
    dwkv, small["g_mem"] = _memkv_bwd(mem, lw["g_mem"], lw["w_mem_kv"], dkv)
    big["w_mem_kv"] = dwkv.reshape(N_CHIPS, D // N_CHIPS, 2 * DMEM)
    if _is_pool(i):
        dx, small["pool_w"], small["pool_scale"] = _pool_bwd(sv["proj"], dtok, lw["pool_w"], lw["pool_scale"])
        dh, dproj, dgpre[1] = _proj_bwd(sv["h1"], dh, g_at(lw["g_pre"], 1), lw["w_in"], [dx, dqm])
        big["w_in"] = _wgrad(sv["u"], dproj).reshape(N_CHIPS, D // N_CHIPS, DMIX)
    else:
        dq, dk, dv = _sb_bwd(sv["proj"], sv["o"], dtok)
        dh, dproj, dgpre[1] = _proj_bwd(sv["h1"], dh, g_at(lw["g_pre"], 1), lw["w_in"], [dq, dk, dv, dqm])
        big["w_in"] = jnp.transpose(_wgrad(sv["u"], dproj).reshape(D, N_CHIPS, DSB // N_CHIPS), (1, 0, 2))
    dh = ffn_back(1, dh, sv["h0"], ride1)
    small["g_pre"] = jnp.concatenate(dgpre, axis=0)
    small["g_post"] = jnp.concatenate(dgpost, axis=0)
    return dh, [big[k] for k in LAYER_TENSORS], small, rode.get(1)


ANY = pl.BlockSpec(memory_space=pl.ANY)


def _place():
    return lax.axis_index("x"), lax.axis_index("y"), lax.axis_index("c")


def _flip(v, bit):
    return 1 - v if bit else v


def _small_allgather(buf):
    R = buf.shape[0]

    def body(x_ref, o_ref, send_sems, recv_sems, loc_sem):
        x, y, c = _place()
        me = 4 * x + 2 * y + c
        loc = pltpu.make_async_copy(x_ref, o_ref.at[me], loc_sem)
        loc.start()
        sends = []
        for k in range(1, N_DEV):
            peer = (_flip(x, k & 4), _flip(y, k & 2), _flip(c, k & 1))
            cp = pltpu.make_async_remote_copy(src_ref=x_ref, dst_ref=o_ref.at[me], send_sem=send_sems.at[k - 1],
                                              recv_sem=recv_sems.at[k - 1], device_id=peer, device_id_type=MESH)
            cp.start()
            sends.append(cp)
        for k in range(1, N_DEV):
            peer = (_flip(x, k & 4), _flip(y, k & 2), _flip(c, k & 1))
            slot = 4 * peer[0] + 2 * peer[1] + peer[2]
            pltpu.make_async_remote_copy(src_ref=x_ref, dst_ref=o_ref.at[slot], send_sem=send_sems.at[k - 1],
                                         recv_sem=recv_sems.at[k - 1], device_id=peer, device_id_type=MESH).wait_recv()
        for cp in sends:
            cp.wait_send()
        loc.wait()

    return pl.pallas_call(
        body, name="small_allgather", in_specs=[ANY], out_specs=ANY,
        out_shape=jax.ShapeDtypeStruct((N_DEV, R, 128), buf.dtype),
        scratch_shapes=[pltpu.SemaphoreType.DMA((N_DEV - 1,)), pltpu.SemaphoreType.DMA((N_DEV - 1,)),
                        pltpu.SemaphoreType.DMA],
    )(buf)


def _dma_sems(m):
    return pltpu.SemaphoreType.DMA((m,))


class _GatherExchange:
    name = "gather_weights"

    def __init__(self, shards):
        n = len(shards)
        self.n, self.inputs = n, list(shards)
        self.halves = [s.shape[0] // 2 for s in shards]
        self.out_shapes = [jax.ShapeDtypeStruct((N_CHIPS,) + s.shape, s.dtype) for s in shards]
        self.sems = [_dma_sems(3 * n), _dma_sems(3 * n), _dma_sems(3 * n), _dma_sems(3 * n), _dma_sems(2 * n)]

    def _copies(self, ins, outs, sems):
        ici_send, ici_recv, d2d_send, d2d_recv, loc_sems = sems
        x, y, c = _place()
        me = 2 * x + y
        sib = (x, y, 1 - c)
        chips = [(_flip(x, k & 2), _flip(y, k & 1)) for k in (1, 2, 3)]
        half = lambda t, slot, hc: outs[t].at[slot, pl.ds(hc * self.halves[t], self.halves[t])]
        locs, ici_out, ici_in, fwd_out, fwd_in = [], [], [], [], []
        for t in range(self.n):
            for hc in range(2):
                rows = pl.ds(hc * self.halves[t], self.halves[t])
                locs.append(pltpu.make_async_copy(ins[t].at[rows], outs[t].at[me, rows], loc_sems.at[2 * t + hc]))
            for k, chip in enumerate(chips):
                idx = 2 * chip[0] + chip[1]
                peer = (chip[0], chip[1], c)
                ici = dict(send_sem=ici_send.at[3 * t + k], recv_sem=ici_recv.at[3 * t + k], device_id=peer,
                           device_id_type=MESH)
                d2d = dict(send_sem=d2d_send.at[3 * t + k], recv_sem=d2d_recv.at[3 * t + k], device_id=sib,
                           device_id_type=MESH)
                ici_out.append(pltpu.make_async_remote_copy(
                    src_ref=ins[t].at[pl.ds(c * self.halves[t], self.halves[t])], dst_ref=half(t, me, c), **ici))
                ici_in.append(pltpu.make_async_remote_copy(src_ref=half(t, idx, c), dst_ref=half(t, idx, c), **ici))
                fwd_out.append(pltpu.make_async_remote_copy(src_ref=half(t, idx, c), dst_ref=half(t, idx, c), **d2d))
                fwd_in.append(pltpu.make_async_remote_copy(
                    src_ref=half(t, idx, 1 - c), dst_ref=half(t, idx, 1 - c), **d2d))
        return locs, ici_out, ici_in, fwd_out, fwd_in

    def start(self, ins, outs, sems):
        locs, ici_out, _, _, _ = self._copies(ins, outs, sems)
        for cp in ici_out + locs:
            cp.start()

    def mid(self, ins, outs, sems):
        _, _, ici_in, fwd_out, _ = self._copies(ins, outs, sems)
        for arrived, onward in zip(ici_in, fwd_out):
            arrived.wait_recv()
            onward.start()

    def finish(self, ins, outs, sems):
        locs, ici_out, _, fwd_out, fwd_in = self._copies(ins, outs, sems)
        for cp in fwd_in:
            cp.wait_recv()
        for cp in ici_out + fwd_out:
            cp.wait_send()
        for cp in locs:
            cp.wait()


class _SiblingExchange:
    name = "rs_to_sibling"

    def __init__(self, gs):
        n = len(gs)
        self.n, self.inputs = n, list(gs)
        self.halves = [g.shape[1] // 2 for g in gs]
        self.out_shapes = [jax.ShapeDtypeStruct((N_CHIPS, g.shape[1] // 2) + g.shape[2:], g.dtype) for g in gs]
        self.sems = [_dma_sems(n), _dma_sems(n)]

    def _copies(self, ins, outs, sems):
        x, y, c = _place()
        return [pltpu.make_async_remote_copy(
            src_ref=ins[t].at[:, pl.ds((1 - c) * self.halves[t], self.halves[t])], dst_ref=outs[t],
            send_sem=sems[0].at[t], recv_sem=sems[1].at[t], device_id=(x, y, 1 - c), device_id_type=MESH)
            for t in range(self.n)]

    def start(self, ins, outs, sems):
        for cp in self._copies(ins, outs, sems):
            cp.start()

    def mid(self, ins, outs, sems):
        pass

    def finish(self, ins, outs, sems):
        for cp in self._copies(ins, outs, sems):
            cp.wait()


class _ChipsExchange:
    name = "rs_to_chips"

    def __init__(self, ss):
        n = len(ss)
        self.n, self.inputs = n, list(ss)
        self.out_shapes = [jax.ShapeDtypeStruct((3,) + s.shape[1:], s.dtype) for s in ss]
        self.sems = [_dma_sems(3 * n), _dma_sems(3 * n)]

    def _copies(self, ins, outs, sems):
        x, y, c = _place()
        chips = [(_flip(x, k & 2), _flip(y, k & 1)) for k in (1, 2, 3)]
        return [pltpu.make_async_remote_copy(
            src_ref=ins[t].at[2 * chip[0] + chip[1]], dst_ref=outs[t].at[k], send_sem=sems[0].at[3 * t + k],
            recv_sem=sems[1].at[3 * t + k], device_id=(chip[0], chip[1], c), device_id_type=MESH)
            for t in range(self.n) for k, chip in enumerate(chips)]

    def start(self, ins, outs, sems):
        for cp in self._copies(ins, outs, sems):
            cp.start()

    def mid(self, ins, outs, sems):
        pass

    def finish(self, ins, outs, sems):
        for cp in self._copies(ins, outs, sems):
            cp.wait()


def _run_exchange(ex):
    ni, no = len(ex.inputs), len(ex.out_shapes)

    def body(*refs):
        ins, outs, sems = refs[:ni], refs[ni:ni + no], refs[ni + no:]
        ex.start(ins, outs, sems)
        ex.mid(ins, outs, sems)
        ex.finish(ins, outs, sems)

    return pl.pallas_call(
        body, name=ex.name, in_specs=[ANY] * ni, out_specs=[ANY] * no, out_shape=ex.out_shapes,
        scratch_shapes=ex.sems,
    )(*ex.inputs)


def _call(body, name, nsteps, in_specs, out_specs, out_shape, args, ride=None):
    if ride is None:
        return pl.pallas_call(body, name=name, grid=(nsteps,), in_specs=in_specs, out_specs=out_specs,
                              out_shape=out_shape, compiler_params=_cparams(("arbitrary",)))(*args)
    ni, no = len(args), len(out_shape)
    ri, ro = len(ride.inputs), len(ride.out_shapes)
    mid_step = (3 * nsteps) // 4 if nsteps > 1 else 0

    def hosted(*refs):
        a, ra = refs[:ni], refs[ni:ni + ri]
        o, ro_refs = refs[ni + ri:ni + ri + no], refs[ni + ri + no:ni + ri + no + ro]
        sems = refs[ni + ri + no + ro:]
        step = pl.program_id(0)

        @pl.when(step == 0)
        def _():
            ride.start(ra, ro_refs, sems)

        body(*a, *o)

        @pl.when(step == mid_step)
        def _():
            ride.mid(ra, ro_refs, sems)

        @pl.when(step == nsteps - 1)
        def _():
            ride.finish(ra, ro_refs, sems)

    outs = pl.pallas_call(
        hosted, name=name + "_" + ride.name, grid=(nsteps,), in_specs=list(in_specs) + [ANY] * ri,
        out_specs=list(out_specs) + [ANY] * ro, out_shape=list(out_shape) + ride.out_shapes,
        scratch_shapes=ride.sems, compiler_params=_cparams(("arbitrary",)),
    )(*args, *ride.inputs)
    return outs[:no], outs[no:]


def _share_halves(ts):
    n = len(ts)

    def body(*refs):
        outs = refs[n:2 * n]
        send_sems, recv_sems = refs[2 * n:]
        x, y, c = _place()
        cps = []
        for t in range(n):
            cp = pltpu.make_async_remote_copy(
                src_ref=outs[t].at[c], dst_ref=outs[t].at[c], send_sem=send_sems.at[t], recv_sem=recv_sems.at[t],
                device_id=(x, y, 1 - c), device_id_type=MESH)
            cp.start()
            cps.append(cp)
        for t in range(n):
            pltpu.make_async_remote_copy(
                src_ref=outs[t].at[1 - c], dst_ref=outs[t].at[1 - c], send_sem=send_sems.at[t],
                recv_sem=recv_sems.at[t], device_id=(x, y, 1 - c), device_id_type=MESH).wait_recv()
        for cp in cps:
            cp.wait_send()

    dma = lambda m: pltpu.SemaphoreType.DMA((m,))
    return pl.pallas_call(
        body, name="share_halves", in_specs=[ANY] * n, out_specs=[ANY] * n,
        out_shape=[jax.ShapeDtypeStruct(t.shape, t.dtype) for t in ts],
        input_output_aliases={t: t for t in range(n)},
        scratch_shapes=[dma(n), dma(n)],
    )(*ts)


def _rows(a, lead):
    return a.reshape(a.shape[:lead] + (-1, a.shape[-1]))


def _add_sibling(g, r, c):
    L_ = g.shape[1]
    g3 = _rows(g.reshape((N_CHIPS, 2, L_ // 2) + g.shape[2:]), 2)
    r3 = _rows(r, 1)
    R_, C = r3.shape[1:]
    tr = _row_tile(R_, 512)

    def body(c_ref, g_ref, r_ref, o_ref):
        o_ref[...] = (g_ref[...] + r_ref[...]).astype(o_ref.dtype)

    out = pl.pallas_call(
        body, name="add_sibling",
        grid_spec=pltpu.PrefetchScalarGridSpec(
            num_scalar_prefetch=1, grid=(N_CHIPS, R_ // tr),
            in_specs=[pl.BlockSpec((None, None, tr, C), lambda s, i, c_ref: (s, c_ref[0], i, 0)),
                      pl.BlockSpec((None, tr, C), lambda s, i, c_ref: (s, i, 0))],
            out_specs=pl.BlockSpec((None, tr, C), lambda s, i, c_ref: (s, i, 0))),
        out_shape=jax.ShapeDtypeStruct((N_CHIPS, R_, C), _MM),
        compiler_params=_cparams(("parallel", "parallel")),
    )(c.reshape(1), g3, r3)
    return out.reshape((N_CHIPS, L_ // 2) + g.shape[2:])


def _add_chips(s, r, me, c):
    s3, r3 = _rows(s, 1), _rows(r, 1)
    R_, C = s3.shape[1:]
    tr = _row_tile(R_, 512)

    def body(at_ref, s_ref, r0_ref, r1_ref, r2_ref, o_ref):
        f = lambda ref: ref[...].astype(F32)
        o_ref[...] = ((f(s_ref) + f(r0_ref)) + f(r1_ref)) + f(r2_ref)

    rk = lambda k: pl.BlockSpec((None, tr, C), lambda i, at_ref: (k, i, 0))
    out = pl.pallas_call(
        body, name="add_chips",
        grid_spec=pltpu.PrefetchScalarGridSpec(
            num_scalar_prefetch=1, grid=(R_ // tr,),
            in_specs=[pl.BlockSpec((None, tr, C), lambda i, at_ref: (at_ref[0], i, 0)), rk(0), rk(1), rk(2)],
            out_specs=pl.BlockSpec((None, tr, C), lambda i, at_ref: (at_ref[1], i, 0))),
        out_shape=jax.ShapeDtypeStruct((2, R_, C), F32),
        compiler_params=_cparams(("parallel",)),
    )(jnp.stack([me, c]), s3, r3, r3, r3)
    return out.reshape((2,) + s.shape[1:])


def _sum_devices(a):
    R_ = a.shape[1]

    def body(a_ref, o_ref):
        acc = a_ref[0]
        for k in range(1, N_DEV):
            acc = acc + a_ref[k]
        o_ref[...] = acc

    return pl.pallas_call(
        body, name="sum_devices", out_shape=jax.ShapeDtypeStruct((R_, 128), F32),
        compiler_params=pltpu.CompilerParams(vmem_limit_bytes=VMEM_LIMIT),
    )(a)


def _adamw(w, g, m, v):
    shape = w.shape
    to2 = lambda a: a.reshape(-1, shape[-1])
    w2, g2, m2, v2 = to2(w), to2(g), to2(m), to2(v)
    R_, C = w2.shape
    tr = 256 if R_ % 256 == 0 else R_

    def body(w_ref, g_ref, m_ref, v_ref, d_ref, mo_ref, vo_ref):
        gv = g_ref[...]
        mn = ADAM_B1 * m_ref[...] + (1.0 - ADAM_B1) * gv
        vn = ADAM_B2 * v_ref[...] + (1.0 - ADAM_B2) * (gv * gv)
        m_hat = mn / (1.0 - ADAM_B1 ** ADAM_STEP)
        v_hat = vn / (1.0 - ADAM_B2 ** ADAM_STEP)
        d_ref[...] = -ADAM_LR * (m_hat / (jnp.sqrt(v_hat) + ADAM_EPS) + ADAM_WD * w_ref[...])
        mo_ref[...] = mn
        vo_ref[...] = vn

    spec = pl.BlockSpec((tr, C), lambda i: (i, 0))
    outs = pl.pallas_call(
        body, name="adamw", grid=(R_ // tr,), in_specs=[spec] * 4, out_specs=[spec] * 3,
        out_shape=[jax.ShapeDtypeStruct((R_, C), F32)] * 3,
        compiler_params=_cparams(("parallel",)),
    )(w2, g2, m2, v2)
    return [o.reshape(shape) for o in outs]


SHARD_AXIS = {"g_pre": 2, "g_post": 2, "g_mem": None, "ffn1_gate": 2, "ffn1_up": 2, "ffn1_down": 1,
              "ffn2_gate": 2, "ffn2_up": 2, "ffn2_down": 1, "w_in_pool": 1, "pool_w": None, "pool_scale": None,
              "w_in_sb": 2, "w_mem_kv": 1, "w_out": 2}
WEIGHTS = list(SHARD_AXIS)
BIG = [k for k in WEIGHTS if SHARD_AXIS[k] is not None and k not in ("g_pre", "g_post")]
SMALL = [k for k in WEIGHTS if k not in BIG]


def _unshard(g, axis):
    _, L_, r, N = g.shape
    if axis == 1:
        return jnp.transpose(g, (1, 0, 2, 3)).reshape(L_, N_CHIPS * r, N)
    return jnp.transpose(g, (1, 2, 0, 3)).reshape(L_, r, N_CHIPS * N)


def _to_shards(a, axis):
    L_, r, N = a.shape
    if axis == 1:
        return jnp.transpose(a.reshape(L_, N_CHIPS, r // N_CHIPS, N), (1, 0, 2, 3))
    return jnp.transpose(a.reshape(L_, r, N_CHIPS, N // N_CHIPS), (2, 0, 1, 3))


def _pack128(arrs):
    flat = jnp.concatenate([a.reshape(-1) for a in arrs])
    pad = (-flat.shape[0]) % (8 * 128)
    return jnp.pad(flat, (0, pad)).reshape(-1, 128)


def _unpack128(buf, shapes):
    flat = buf.reshape(-1)
    out, off = [], 0
    for shp in shapes:
        size = 1
        for d in shp:
            size *= d
        out.append(flat[off:off + size].reshape(shp))
        off += size
    return out


def kernel(x, mem, g_pre, g_post, g_mem, ffn1_gate, ffn1_up, ffn1_down, ffn2_gate, ffn2_up, ffn2_down, w_in_pool, pool_w, pool_scale, w_in_sb, w_mem_kv, w_out, loss_target, m_g_pre, m_g_post, m_g_mem, m_ffn1_gate, m_ffn1_up, m_ffn1_down, m_ffn2_gate, m_ffn2_up, m_ffn2_down, m_w_in_pool, m_pool_w, m_pool_scale, m_w_in_sb, m_w_mem_kv, m_w_out, v_g_pre, v_g_post, v_g_mem, v_ffn1_gate, v_ffn1_up, v_ffn1_down, v_ffn2_gate, v_ffn2_up, v_ffn2_down, v_w_in_pool, v_pool_w, v_pool_scale, v_w_in_sb, v_w_mem_kv, v_w_out):
    w = dict(g_pre=g_pre, g_post=g_post, g_mem=g_mem, ffn1_gate=ffn1_gate, ffn1_up=ffn1_up, ffn1_down=ffn1_down,
             ffn2_gate=ffn2_gate, ffn2_up=ffn2_up, ffn2_down=ffn2_down, w_in_pool=w_in_pool, pool_w=pool_w,
             pool_scale=pool_scale, w_in_sb=w_in_sb, w_mem_kv=w_mem_kv, w_out=w_out)
    m = dict(g_pre=m_g_pre, g_post=m_g_post, g_mem=m_g_mem, ffn1_gate=m_ffn1_gate, ffn1_up=m_ffn1_up,
             ffn1_down=m_ffn1_down, ffn2_gate=m_ffn2_gate, ffn2_up=m_ffn2_up, ffn2_down=m_ffn2_down,
             w_in_pool=m_w_in_pool, pool_w=m_pool_w, pool_scale=m_pool_scale, w_in_sb=m_w_in_sb,
             w_mem_kv=m_w_mem_kv, w_out=m_w_out)
    v = dict(g_pre=v_g_pre, g_post=v_g_post, g_mem=v_g_mem, ffn1_gate=v_ffn1_gate, ffn1_up=v_ffn1_up,
             ffn1_down=v_ffn1_down, ffn2_gate=v_ffn2_gate, ffn2_up=v_ffn2_up, ffn2_down=v_ffn2_down,
             w_in_pool=v_w_in_pool, pool_w=v_pool_w, pool_scale=v_pool_scale, w_in_sb=v_w_in_sb,
             w_mem_kv=v_w_mem_kv, w_out=v_w_out)
    cx, cy, cc = _place()
    chip = (2 * cx + cy).astype(jnp.int32)
    core = cc.astype(jnp.int32)

    gains = _small_allgather(_pack128([g_pre, g_post]))
    per_chip = [_unpack128(gains[2 * j], [g_pre.shape, g_post.shape]) for j in range(N_CHIPS)]
    full_pre = jnp.concatenate([p[0] for p in per_chip], axis=2)
    full_post = jnp.concatenate([p[1] for p in per_chip], axis=2)

    def shard_name(k, i):
        return ("w_in_pool" if _is_pool(i) else "w_in_sb") if k == "w_in" else k

    def own(i, names):
        return [w[shard_name(k, i)][i // 2 if k == "w_in" else i].astype(_MM) for k in names]

    xs, mems = x[0], mem[0]
    h, lws, saved = xs, [], []
    first = _run_exchange(_GatherExchange(own(0, FIRST_FFN)))
    for i in range(DEPTH):
        lw = dict(zip(FIRST_FFN, first))
        lw["g_pre"], lw["g_post"], lw["g_mem"] = full_pre[i], full_post[i], g_mem[i].reshape(1, D)
        if _is_pool(i):
            lw["pool_w"], lw["pool_scale"] = pool_w[i // 2], pool_scale[i // 2].reshape(1, DTOK)
        h, sv, first = _layer_forward(h, mems, lw, i, own(i, REST), own(i + 1, FIRST_FFN) if i + 1 < DEPTH else None)
        lws.append(lw)
        saved.append(sv)
    acc, dh = _loss_grad(h, loss_target[0])
    loss = lax.psum(0.5 / D * jnp.sum(acc), ("x", "y", "c"))

    mine, smalls = [None] * DEPTH, [None] * DEPTH
    above, sums = None, {}

    def chips_ride(from_sib, layer):
        sums[layer] = [_add_sibling(g, r, core) for g, r in zip(above, from_sib)]
        return _ChipsExchange(sums[layer])

    for i in reversed(range(DEPTH)):
        if above is None:
            dh, gs, smalls[i], _ = _layer_backward(dh, mems, lws[i], saved[i], i)
        else:
            dh, gs, smalls[i], from_chips = _layer_backward(
                dh, mems, lws[i], saved[i], i, _SiblingExchange(above), lambda fs, layer=i + 1: chips_ride(fs, layer))
            mine[i + 1] = [_add_chips(s, r, chip, core) for s, r in zip(sums[i + 1], from_chips)]
        above = gs
    from_chips = _run_exchange(chips_ride(_run_exchange(_SiblingExchange(above)), 0))
    mine[0] = [_add_chips(s, r, chip, core) for s, r in zip(sums[0], from_chips)]
    shared = _share_halves([t for layer in mine for t in layer])
    nt = len(LAYER_TENSORS)
    reduced = [dict(zip(LAYER_TENSORS, [s.reshape(-1, s.shape[-1]) for s in shared[nt * i:nt * (i + 1)]]))
               for i in range(DEPTH)]
    dx = dh
    red = {}
    for k in BIG:
        if k.startswith("w_in_"):
            layers = [i for i in range(DEPTH) if _is_pool(i) == (k == "w_in_pool")]
            red[k] = jnp.stack([reduced[i]["w_in"] for i in layers])
        else:
            red[k] = jnp.stack([reduced[i][k] for i in range(DEPTH)])

    grads = {"g_pre": jnp.stack([s["g_pre"] for s in smalls]), "g_post": jnp.stack([s["g_post"] for s in smalls]),
             "g_mem": jnp.concatenate([s["g_mem"] for s in smalls], axis=0),
             "pool_w": jnp.stack([smalls[i]["pool_w"] for i in range(DEPTH) if _is_pool(i)]),
             "pool_scale": jnp.concatenate([smalls[i]["pool_scale"] for i in range(DEPTH) if _is_pool(i)], axis=0)}
    small_shapes = [grads[k].shape for k in SMALL]
    summed = _unpack128(_sum_devices(_small_allgather(_pack128([grads[k] for k in SMALL]))), small_shapes)
    for k, s in zip(SMALL, summed):
        if SHARD_AXIS[k] is None:
            red[k] = s
        else:
            red[k] = lax.dynamic_slice_in_dim(s, chip * w[k].shape[2], w[k].shape[2], axis=2)

    deltas, new_m, new_v = {}, {}, {}
    for k in WEIGHTS:
        deltas[k], new_m[k], new_v[k] = _adamw(w[k], red[k], m[k], v[k])
    return (loss, dx.reshape(x.shape), *[red[k] for k in WEIGHTS], *[deltas[k] for k in WEIGHTS],
            *[new_m[k] for k in WEIGHTS], *[new_v[k] for k in WEIGHTS])
```

```python
import jax
import jax.numpy as jnp
from jax import lax
from jax.experimental import pallas as pl
from jax.experimental.pallas import tpu as pltpu

F32 = jnp.float32
BF16 = jnp.bfloat16
_MM = jnp.bfloat16

D = 1024
FF = 2048
DTOK = 512
DMEM = 256
DMIX = DTOK + DMEM
DSB = 3 * DTOK + DMEM
HD = 64
MEM_LEN = 256
DEPTH = 4
EPS = 1e-6
WINDOWS = (2, 4, 8, 16)
HALO = 16
PG = 128
QB = 128
SCALE = HD ** -0.5
FC = 512
NEG_CUT = -104.0

ADAM_LR, ADAM_B1, ADAM_B2, ADAM_EPS, ADAM_WD, ADAM_STEP = 0.001, 0.9, 0.999, 1e-08, 0.01, 10

VMEM_LIMIT = 56 * 2 ** 20
MESH = pl.DeviceIdType.MESH
N_CHIPS = 4
N_DEV = 8


def _cparams(sem):
    return pltpu.CompilerParams(dimension_semantics=sem, vmem_limit_bytes=VMEM_LIMIT)


def _mm(a, b):
    return jnp.dot(a.astype(_MM), b.astype(_MM), preferred_element_type=F32)


def _mm_nt(a, b):
    return lax.dot_general(a.astype(_MM), b.astype(_MM), (((1,), (1,)), ((), ())), preferred_element_type=F32)


def _mm_tn(a, b):
    return lax.dot_general(a.astype(_MM), b.astype(_MM), (((0,), (0,)), ((), ())), preferred_element_type=F32)


def _mm2(a, b):
    hi = a.astype(_MM)
    lo = (a - hi.astype(F32)).astype(_MM)
    return jnp.dot(hi, b, preferred_element_type=F32) + jnp.dot(lo, b, preferred_element_type=F32)


def _rstd(x):
    return lax.rsqrt(jnp.mean(x * x, axis=-1, keepdims=True) + EPS)


def _rms_bwd(x, g, dy):
    r = _rstd(x)
    xh = x * r
    t = dy * g
    dx = r * (t - xh * jnp.mean(t * xh, axis=-1, keepdims=True))
    return dx, jnp.sum(dy * xh, axis=0, keepdims=True)


def _sigmoid(a):
    return 1.0 / (1.0 + jnp.exp(-a))


def _row_tile(n, want):
    t = min(n, want)
    assert n % t == 0, (n, t)
    return t


def _ffn_fwd(h, g1, g2, wg, wu, wd, ride=None):
    S = h.shape[0]
    ts = _row_tile(S, 512)

    def body(h_ref, g1_ref, g2_ref, wg_ref, wu_ref, wd_ref, ho_ref, a_ref, b_ref, f_ref):
        hv = h_ref[...]
        n = (hv * _rstd(hv) * g1_ref[...]).astype(_MM)
        f = jnp.zeros((ts, D), F32)
        for j in range(FF // FC):
            cs = slice(j * FC, (j + 1) * FC)
            a = _mm(n, wg_ref[j])
            b = _mm(n, wu_ref[j])
            a_ref[:, cs] = a.astype(a_ref.dtype)
            b_ref[:, cs] = b.astype(b_ref.dtype)
            f = f + _mm(a * _sigmoid(a) * b, wd_ref[j])
        f_ref[...] = f
        ho_ref[...] = hv + 0.5 * (f * _rstd(f) * g2_ref[...])

    row = lambda i: (i, 0)
    fix = lambda i: (0, 0)
    all3 = lambda i: (0, 0, 0)
    ns = FF // FC
    return _call(
        body, "ffn_fwd", S // ts,
        [pl.BlockSpec((ts, D), row), pl.BlockSpec((1, D), fix), pl.BlockSpec((1, D), fix),
         pl.BlockSpec((ns, D, FC), all3), pl.BlockSpec((ns, D, FC), all3), pl.BlockSpec((ns, FC, D), all3)],
        [pl.BlockSpec((ts, D), row), pl.BlockSpec((ts, FF), row), pl.BlockSpec((ts, FF), row),
         pl.BlockSpec((ts, D), row)],
        [jax.ShapeDtypeStruct((S, D), F32), jax.ShapeDtypeStruct((S, FF), _MM),
         jax.ShapeDtypeStruct((S, FF), _MM), jax.ShapeDtypeStruct((S, D), F32)],
        (h, g1, g2, wg, wu, wd), ride)


def _ffn_bwd(h, f, a, b, dho, g1, g2, wg, wu, wd, ride=None):
    S = h.shape[0]
    ts = _row_tile(S, 256)

    def body(h_ref, f_ref, a_ref, b_ref, dho_ref, g1_ref, g2_ref, wg_ref, wu_ref, wd_ref,
             dh_ref, n_ref, s_ref, da_ref, db_ref, df_ref, dg1_ref, dg2_ref):
        @pl.when(pl.program_id(0) == 0)
        def _():
            dg1_ref[...] = jnp.zeros_like(dg1_ref)
            dg2_ref[...] = jnp.zeros_like(dg2_ref)

        hv = h_ref[...]
        dho = dho_ref[...]
        df, dg2 = _rms_bwd(f_ref[...], g2_ref[...], 0.5 * dho)
        dg2_ref[...] += dg2
        dfb = df.astype(_MM)
        df_ref[...] = dfb
        rh = _rstd(hv)
        hh = hv * rh
        g1 = g1_ref[...]
        n_ref[...] = (hh * g1).astype(_MM)
        dss = [_mm_nt(dfb, wd_ref[j]) for j in range(FF // FC)]
        das, dbs = [], []
        for j in range(FF // FC):
            cs = slice(j * FC, (j + 1) * FC)
            av = a_ref[:, cs].astype(F32)
            bv = b_ref[:, cs].astype(F32)
            sig = _sigmoid(av)
            sl = av * sig
            s_ref[:, cs] = (sl * bv).astype(_MM)
            da = (dss[j] * bv * (sig * (1.0 + av * (1.0 - sig)))).astype(_MM)
            db = (dss[j] * sl).astype(_MM)
            da_ref[:, cs] = da
            db_ref[:, cs] = db
            das.append(da)
            dbs.append(db)
        dn = jnp.zeros((ts, D), F32)
        for j in range(FF // FC):
            dn = dn + _mm_nt(das[j], wg_ref[j]) + _mm_nt(dbs[j], wu_ref[j])
        t = dn * g1
        dh_ref[...] = dho + rh * (t - hh * jnp.mean(t * hh, axis=-1, keepdims=True))
        dg1_ref[...] += jnp.sum(dn * hh, axis=0, keepdims=True)

    row = lambda i: (i, 0)
    fix = lambda i: (0, 0)
    all3 = lambda i: (0, 0, 0)
    ns = FF // FC
    td = pl.BlockSpec((ts, D), row)
    tf = pl.BlockSpec((ts, FF), row)
    gd = pl.BlockSpec((1, D), fix)
    return _call(
        body, "ffn_bwd", S // ts,
        [td, td, tf, tf, td, gd, gd,
         pl.BlockSpec((ns, D, FC), all3), pl.BlockSpec((ns, D, FC), all3), pl.BlockSpec((ns, FC, D), all3)],
        [td, td, tf, tf, tf, td, gd, gd],
        [jax.ShapeDtypeStruct((S, D), F32), jax.ShapeDtypeStruct((S, D), _MM),
         jax.ShapeDtypeStruct((S, FF), _MM), jax.ShapeDtypeStruct((S, FF), _MM),
         jax.ShapeDtypeStruct((S, FF), _MM), jax.ShapeDtypeStruct((S, D), _MM),
         jax.ShapeDtypeStruct((1, D), F32), jax.ShapeDtypeStruct((1, D), F32)],
        (h, f, a, b, dho, g1, g2, wg, wu, wd), ride)


def _wgrad(x, y, nslot=0, ride=None):
    T, M = x.shape
    N = y.shape[1]
    tt = _row_tile(T, 1024)
    cw = N // nslot if nslot else (512 if N % 512 == 0 else 256)
    nc = N // cw
    assert cw * nc == N and cw % 128 == 0, (N, nc)

    def body(x_ref, y_ref, o_ref):
        @pl.when(pl.program_id(0) == 0)
        def _():
            o_ref[...] = jnp.zeros_like(o_ref)

        xt = x_ref[...].T
        for j in range(nc):
            part = jnp.dot(xt, y_ref[:, j * cw:(j + 1) * cw], preferred_element_type=F32)
            if nslot:
                o_ref[j] += part
            else:
                o_ref[:, j * cw:(j + 1) * cw] += part

    oshape = (nslot, M, cw) if nslot else (M, N)
    res = _call(body, "wgrad", T // tt,
                [pl.BlockSpec((tt, M), lambda t: (t, 0)), pl.BlockSpec((tt, N), lambda t: (t, 0))],
                [pl.BlockSpec(oshape, lambda t: (0,) * len(oshape))], [jax.ShapeDtypeStruct(oshape, F32)], (x, y), ride)
    return res[0] if ride is None else (res[0][0], res[1])


def _proj_fwd(h, g, w, out_dtype):
    S = h.shape[0]
    nk, kw, N = w.shape
    ts = _row_tile(S, 256)

    def body(h_ref, g_ref, w_ref, u_ref, p_ref):
        hv = h_ref[...]
        u = (hv * _rstd(hv) * g_ref[...]).astype(_MM)
        u_ref[...] = u
        acc = _mm(u[:, :kw], w_ref[0])
        for j in range(1, nk):
            acc = acc + _mm(u[:, j * kw:(j + 1) * kw], w_ref[j])
        p_ref[...] = acc.astype(p_ref.dtype)

    return pl.pallas_call(
        body, name="proj_fwd", grid=(S // ts,),
        in_specs=[pl.BlockSpec((ts, D), lambda i: (i, 0)), pl.BlockSpec((1, D), lambda i: (0, 0)),
                  pl.BlockSpec((nk, kw, N), lambda i: (0, 0, 0))],
        out_specs=[pl.BlockSpec((ts, D), lambda i: (i, 0)), pl.BlockSpec((ts, N), lambda i: (i, 0))],
        out_shape=[jax.ShapeDtypeStruct((S, D), _MM), jax.ShapeDtypeStruct((S, N), out_dtype)],
        compiler_params=_cparams(("parallel",)),
    )(h, g, w)


def _proj_bwd(h, dho, g, w, pieces):
    S = h.shape[0]
    nk, kw, N = w.shape
    ts = _row_tile(S, 256)
    widths = [p.shape[1] for p in pieces]
    assert sum(widths) == N
    npc = len(pieces)

    def body(*refs):
        h_ref, dho_ref, g_ref, w_ref = refs[:4]
        p_refs = refs[4:4 + npc]
        dh_ref, dp_ref, dg_ref = refs[4 + npc:]

        @pl.when(pl.program_id(0) == 0)
        def _():
            dg_ref[...] = jnp.zeros_like(dg_ref)

        dus = [jnp.zeros((ts, kw), F32) for _ in range(nk)]
        off = 0
        for p_ref, n in zip(p_refs, widths):
            pv = p_ref[...].astype(_MM)
            dp_ref[:, off:off + n] = pv
            for j in range(nk):
                dus[j] = dus[j] + _mm_nt(pv, w_ref[j, :, off:off + n])
            off += n
        du = dus[0] if nk == 1 else jnp.concatenate(dus, axis=-1)
        dx, dg = _rms_bwd(h_ref[...], g_ref[...], du)
        dh_ref[...] = dho_ref[...] + dx
        dg_ref[...] += dg

    row = lambda i: (i, 0)
    return pl.pallas_call(
        body, name="proj_bwd", grid=(S // ts,),
        in_specs=[pl.BlockSpec((ts, D), row), pl.BlockSpec((ts, D), row), pl.BlockSpec((1, D), lambda i: (0, 0)),
                  pl.BlockSpec((nk, kw, N), lambda i: (0, 0, 0))] + [pl.BlockSpec((ts, n), row) for n in widths],
        out_specs=[pl.BlockSpec((ts, D), row), pl.BlockSpec((ts, N), row), pl.BlockSpec((1, D), lambda i: (0, 0))],
        out_shape=[jax.ShapeDtypeStruct((S, D), F32), jax.ShapeDtypeStruct((S, N), _MM),
                   jax.ShapeDtypeStruct((1, D), F32)],
        compiler_params=_cparams(("arbitrary",)),
    )(h, dho, g, w, *pieces)


def _pool_counts(first_row, ts):
    pos = (first_row + lax.broadcasted_iota(jnp.int32, (ts, 1), 0) + 1).astype(F32)
    return [jnp.minimum(pos, float(w)) for w in WINDOWS]


def _pool_delta(x, prev, cnts, ts):
    xe = jnp.concatenate([prev, x], axis=0)
    sums = []
    cur = xe
    for sh in (1, 2, 4, 8):
        cur = cur + pltpu.roll(cur, sh, 0)
        sums.append(cur)
    return [sums[gi][HALO:, gi * PG:(gi + 1) * PG] / cnts[gi] - x[:, gi * PG:(gi + 1) * PG]
            for gi in range(len(WINDOWS))]


def _pool_fwd(proj, pw, ps):
    S = proj.shape[0]
    ts = _row_tile(S, 256)

    def body(x_ref, pw_ref, ps_ref, tok_ref, carry_ref):
        i = pl.program_id(0)

        @pl.when(i == 0)
        def _():
            carry_ref[...] = jnp.zeros_like(carry_ref)

        x = x_ref[...]
        ds = _pool_delta(x, carry_ref[...], _pool_counts(i * ts, ts), ts)
        y = jnp.concatenate([_mm(ds[gi], pw_ref[gi]) for gi in range(len(WINDOWS))], axis=-1)
        tok_ref[...] = (y * ps_ref[...]).astype(tok_ref.dtype)
        carry_ref[...] = x[ts - HALO:, :]

    return pl.pallas_call(
        body, name="pool_fwd", grid=(S // ts,),
        in_specs=[pl.BlockSpec((ts, DTOK), lambda i: (i, 0)), pl.BlockSpec((len(WINDOWS), PG, PG), lambda i: (0, 0, 0)),
                  pl.BlockSpec((1, DTOK), lambda i: (0, 0))],
        out_specs=pl.BlockSpec((ts, DTOK), lambda i: (i, 0)),
        out_shape=jax.ShapeDtypeStruct((S, DTOK), _MM),
        scratch_shapes=[pltpu.VMEM((HALO, DTOK), F32)],
        compiler_params=_cparams(("arbitrary",)),
    )(proj, pw, ps)


def _pool_bwd(proj, dtok, pw, ps):
    S = proj.shape[0]
    ts = _row_tile(S, 256)
    nt = S // ts
    per = ts // HALO
    ng = len(WINDOWS)

    def body(x_ref, xp_ref, dt_ref, pw_ref, ps_ref, dx_ref, dpw_ref, dps_ref, carry_ref):
        i = pl.program_id(0)
        idx = nt - 1 - i

        @pl.when(i == 0)
        def _():
            carry_ref[...] = jnp.zeros_like(carry_ref)
            dpw_ref[...] = jnp.zeros_like(dpw_ref)
            dps_ref[...] = jnp.zeros_like(dps_ref)

        x = x_ref[...]
        prev = jnp.where(idx > 0, xp_ref[...], 0.0)
        cnts = _pool_counts(idx * ts, ts)
        ds = _pool_delta(x, prev, cnts, ts)
        dt = dt_ref[...]
        y = jnp.concatenate([_mm(ds[gi], pw_ref[gi]) for gi in range(ng)], axis=-1)
        dps_ref[...] += jnp.sum(dt * y, axis=0, keepdims=True)
        dy = (dt * ps_ref[...]).astype(_MM)
        dds = []
        for gi in range(ng):
            dyg = dy[:, gi * PG:(gi + 1) * PG]
            dpw_ref[gi] += _mm_tn(ds[gi], dyg)
            dds.append(_mm_nt(dyg, pw_ref[gi]))
        e = jnp.concatenate([dds[gi] / cnts[gi] for gi in range(ng)], axis=-1)
        ee = jnp.concatenate([e, carry_ref[...]], axis=0)
        rows = ts + HALO
        cur = ee
        outs = []
        for gi, sh in enumerate((1, 2, 4, 8)):
            cur = cur + pltpu.roll(cur, rows - sh, 0)
            outs.append(cur[:ts, gi * PG:(gi + 1) * PG] - dds[gi])
        dx_ref[...] = jnp.concatenate(outs, axis=-1)
        carry_ref[...] = e[:HALO, :]

    rev = lambda i: (nt - 1 - i, 0)
    return pl.pallas_call(
        body, name="pool_bwd", grid=(nt,),
        in_specs=[pl.BlockSpec((ts, DTOK), rev),
                  pl.BlockSpec((HALO, DTOK), lambda i: (jnp.maximum((nt - 1 - i) * per - 1, 0), 0)),
                  pl.BlockSpec((ts, DTOK), rev), pl.BlockSpec((ng, PG, PG), lambda i: (0, 0, 0)),
                  pl.BlockSpec((1, DTOK), lambda i: (0, 0))],
        out_specs=[pl.BlockSpec((ts, DTOK), rev), pl.BlockSpec((ng, PG, PG), lambda i: (0, 0, 0)),
                   pl.BlockSpec((1, DTOK), lambda i: (0, 0))],
        out_shape=[jax.ShapeDtypeStruct((S, DTOK), F32), jax.ShapeDtypeStruct((ng, PG, PG), F32),
                   jax.ShapeDtypeStruct((1, DTOK), F32)],
        scratch_shapes=[pltpu.VMEM((HALO, DTOK), F32)],
        compiler_params=_cparams(("arbitrary",)),
    )(proj, proj, dtok, pw, ps)


def _tri(strict):
    r = lax.broadcasted_iota(jnp.int32, (QB, QB), 0)
    c = lax.broadcasted_iota(jnp.int32, (QB, QB), 1)
    return jnp.where(r > c if strict else r >= c, 1.0, 0.0).astype(_MM)


def _sb_terms(z, valid):
    e = jnp.exp(-jnp.abs(z))
    sp = jnp.log(1.0 + e)
    ls = jnp.minimum(z, 0.0) - sp
    lf = jnp.where(valid, jnp.minimum(-z, 0.0) - sp, 0.0)
    return ls, lf, e


def _sb_alive(cs):
    top = cs[0]
    for c in cs[1:]:
        top = jnp.maximum(top, c)
    return (jnp.max(top) > NEG_CUT).astype(jnp.int32)


def _sb_split(x2, head_a):
    zero = jnp.zeros_like(x2)
    return jnp.where(head_a, x2, zero), jnp.where(head_a, zero, x2)


SB_GROUPS_FWD = 4
SB_GROUPS_BWD = 4


def _sb_fwd(proj):
    S = proj.shape[0]
    G = min(SB_GROUPS_FWD, S // QB)
    nstep = S // (G * QB)
    npair = DTOK // 128
    nch = 2 * G

    def body(q_ref, k_ref, v_ref, tri_ref, o_ref):
        qb = pl.program_id(1)
        head_a = lax.broadcasted_iota(jnp.int32, (QB, 128), 1) < HD
        qs = []
        for g in range(G):
            qs.extend(_sb_split(q_ref[g * QB:(g + 1) * QB, :], head_a))
        row = lax.broadcasted_iota(jnp.int32, (QB, QB), 0)
        col = lax.broadcasted_iota(jnp.int32, (QB, QB), 1)
        tri = tri_ref[...]
        last = G * qb + G - 1

        def step(carry):
            i, _, cs, accs = carry
            causal = (col - row) < i * QB
            ks, vs, valids = [], [], []
            for g in range(G):
                kb = G * qb + g - i
                off = pl.multiple_of(jnp.maximum(kb, 0) * QB, QB)
                ks.append(k_ref[pl.ds(off, QB), :])
                vs.append(v_ref[pl.ds(off, QB), :])
                valids.append(jnp.logical_and(causal, kb >= 0))
            zs = [_mm_nt(qs[ch], ks[ch // 2]) * SCALE for ch in range(nch)]
            terms = [_sb_terms(zs[ch], valids[ch // 2]) for ch in range(nch)]
            withins = [_mm2(terms[ch][1], tri) for ch in range(nch)]
            new_c, new_acc = [], []
            for ch in range(nch):
                ls, lf, _ = terms[ch]
                a = jnp.where(valids[ch // 2], jnp.exp(ls + withins[ch] + cs[ch]), 0.0)
                new_acc.append(accs[ch] + _mm2(a, vs[ch // 2]))
                new_c.append(cs[ch] + jnp.sum(lf, axis=-1, keepdims=True))
            return i + 1, _sb_alive(new_c), tuple(new_c), tuple(new_acc)

        zc = jnp.zeros((QB, 1), F32)
        za = jnp.zeros((QB, 128), F32)
        more = lambda cr: jnp.logical_and(cr[0] <= last, cr[1] > 0)
        accs = lax.while_loop(more, step, (0, 1, (zc,) * nch, (za,) * nch))[3]
        for g in range(G):
            o_ref[g * QB:(g + 1) * QB, :] = jnp.where(head_a, accs[2 * g], accs[2 * g + 1])

    return pl.pallas_call(
        body, name="sb_fwd", grid=(npair, nstep),
        in_specs=[pl.BlockSpec((G * QB, 128), lambda p, i: (i, p)),
                  pl.BlockSpec((S, 128), lambda p, i: (0, npair + p)),
                  pl.BlockSpec((S, 128), lambda p, i: (0, 2 * npair + p)),
                  pl.BlockSpec((QB, QB), lambda p, i: (0, 0))],
        out_specs=pl.BlockSpec((G * QB, 128), lambda p, i: (i, p)),
        out_shape=jax.ShapeDtypeStruct((S, DTOK), F32),
        compiler_params=_cparams(("parallel", "parallel")),
    )(proj, proj, proj, _tri(True))


def _sb_bwd(proj, o, do):
    S = proj.shape[0]
    G = min(SB_GROUPS_BWD, S // QB)
    nstep = S // (G * QB)
    npair = DTOK // 128
    nch = 2 * G

    def body(q_ref, k_ref, v_ref, o_ref, do_ref, tri_ref, tri2_ref, dq_ref, dk_ref, dv_ref):
        qb = pl.program_id(1)

        @pl.when(qb == 0)
        def _():
            dk_ref[...] = jnp.zeros_like(dk_ref)
            dv_ref[...] = jnp.zeros_like(dv_ref)

        head_a = lax.broadcasted_iota(jnp.int32, (QB, 128), 1) < HD
        qs, dos, gs = [], [], []
        for g in range(G):
            rows = slice(g * QB, (g + 1) * QB)
            qs.extend(_sb_split(q_ref[rows, :], head_a))
            dob = do_ref[rows, :].astype(_MM)
            dos.extend(_sb_split(dob, head_a))
            go = dob.astype(F32) * o_ref[rows, :]
            gs.append(jnp.sum(jnp.where(head_a, go, 0.0), axis=-1, keepdims=True))
            gs.append(jnp.sum(jnp.where(head_a, 0.0, go), axis=-1, keepdims=True))
        row = lax.broadcasted_iota(jnp.int32, (QB, QB), 0)
        col = lax.broadcasted_iota(jnp.int32, (QB, QB), 1)
        tri = tri_ref[...]
        tri2 = tri2_ref[...]
        last = G * qb + G - 1

        def step(carry):
            i, _, cs, rs, dqs = carry
            causal = (col - row) < i * QB
            offs, ks, vs, valids = [], [], [], []
            for g in range(G):
                kb = G * qb + g - i
                off = pl.multiple_of(jnp.maximum(kb, 0) * QB, QB)
                offs.append(off)
                ks.append(k_ref[pl.ds(off, QB), :])
                vs.append(v_ref[pl.ds(off, QB), :])
                valids.append(jnp.logical_and(causal, kb >= 0))
            zs = [_mm_nt(qs[ch], ks[ch // 2]) * SCALE for ch in range(nch)]
            das = [_mm_nt(dos[ch], vs[ch // 2]) for ch in range(nch)]
            terms = [_sb_terms(zs[ch], valids[ch // 2]) for ch in range(nch)]
            withins = [_mm2(terms[ch][1], tri) for ch in range(nch)]
            a_s, dlws = [], []
            for ch in range(nch):
                a = jnp.where(valids[ch // 2], jnp.exp(terms[ch][0] + withins[ch] + cs[ch]), 0.0)
                a_s.append(a)
                dlws.append(das[ch] * a)
            sfx = [_mm2(dlws[ch], tri2) for ch in range(nch)]
            dzs = []
            for ch in range(nch):
                e = terms[ch][2]
                inv = 1.0 / (1.0 + e)
                pos = zs[ch] >= 0.0
                beta = jnp.where(pos, 1.0, e) * inv
                omb = jnp.where(pos, e, 1.0) * inv
                prefix = gs[ch] - rs[ch] - sfx[ch]
                dzs.append((jnp.where(valids[ch // 2], dlws[ch] * omb - beta * prefix, 0.0) * SCALE).astype(_MM))
            new_dq = [dqs[ch] + _mm(dzs[ch], ks[ch // 2]) for ch in range(nch)]
            for g in range(G):
                dk_ref[pl.ds(offs[g], QB), :] += (_mm_tn(dzs[2 * g], qs[2 * g]) + _mm_tn(dzs[2 * g + 1], qs[2 * g + 1]))
                dv_ref[pl.ds(offs[g], QB), :] += (_mm_tn(a_s[2 * g], dos[2 * g]) + _mm_tn(a_s[2 * g + 1], dos[2 * g + 1]))
            new_c = [cs[ch] + jnp.sum(terms[ch][1], axis=-1, keepdims=True) for ch in range(nch)]
            new_r = [rs[ch] + jnp.sum(dlws[ch], axis=-1, keepdims=True) for ch in range(nch)]
            return i + 1, _sb_alive(new_c), tuple(new_c), tuple(new_r), tuple(new_dq)

        zc = jnp.zeros((QB, 1), F32)
        za = jnp.zeros((QB, 128), F32)
        more = lambda cr: jnp.logical_and(cr[0] <= last, cr[1] > 0)
        dqs = lax.while_loop(more, step, (0, 1, (zc,) * nch, (zc,) * nch, (za,) * nch))[4]
        for g in range(G):
            dq_ref[g * QB:(g + 1) * QB, :] = jnp.where(head_a, dqs[2 * g], dqs[2 * g + 1])

    qspec = pl.BlockSpec((G * QB, 128), lambda p, i: (i, p))
    full = lambda base: pl.BlockSpec((S, 128), lambda p, i: (0, base + p))
    tspec = pl.BlockSpec((QB, QB), lambda p, i: (0, 0))
    return pl.pallas_call(
        body, name="sb_bwd", grid=(npair, nstep),
        in_specs=[qspec, full(npair), full(2 * npair), qspec, qspec, tspec, tspec],
        out_specs=[qspec, full(0), full(0)],
        out_shape=[jax.ShapeDtypeStruct((S, DTOK), F32)] * 3,
        compiler_params=_cparams(("parallel", "arbitrary")),
    )(proj, proj, proj, o, do, _tri(True), _tri(False))


def _memkv_fwd(mem, gm, wkv):
    nk, kw, _ = wkv.shape

    def body(m_ref, g_ref, w_ref, kv_ref):
        m = m_ref[...]
        mn = (m * _rstd(m) * g_ref[...]).astype(_MM)
        acc = _mm(mn[:, :kw], w_ref[0])
        for j in range(1, nk):
            acc = acc + _mm(mn[:, j * kw:(j + 1) * kw], w_ref[j])
        kv_ref[...] = acc.astype(kv_ref.dtype)

    return pl.pallas_call(
        body, name="memkv_fwd", out_shape=jax.ShapeDtypeStruct((MEM_LEN, 2 * DMEM), _MM),
        compiler_params=pltpu.CompilerParams(vmem_limit_bytes=VMEM_LIMIT),
    )(mem, gm, wkv)


def _memkv_bwd(mem, gm, wkv, dkv):
    nk, kw, _ = wkv.shape

    def body(m_ref, g_ref, w_ref, dkv_ref, dw_ref, dg_ref):
        m = m_ref[...]
        mh = m * _rstd(m)
        mn = (mh * g_ref[...]).astype(_MM)
        dkv = dkv_ref[...].astype(_MM)
        dw_ref[...] = _mm_tn(mn, dkv)
        dmn = jnp.concatenate([_mm_nt(dkv, w_ref[j]) for j in range(nk)], axis=-1)
        dg_ref[...] = jnp.sum(dmn * mh, axis=0, keepdims=True)

    return pl.pallas_call(
        body, name="memkv_bwd",
        out_shape=[jax.ShapeDtypeStruct((D, 2 * DMEM), F32), jax.ShapeDtypeStruct((1, D), F32)],
        compiler_params=pltpu.CompilerParams(vmem_limit_bytes=VMEM_LIMIT),
    )(mem, gm, wkv, dkv)


def _mem_probs(qh, kp):
    s = _mm_nt(qh, kp) * SCALE
    p = jnp.exp(s - jnp.max(s, axis=-1, keepdims=True))
    return p / jnp.sum(p, axis=-1, keepdims=True)


def _mixout_fwd(h, tok, proj, kv, wo, g, qm_block):
    S = h.shape[0]
    ts = _row_tile(S, 256)

    def body(h_ref, tok_ref, qm_ref, kv_ref, wo_ref, g_ref, ho_ref, cat_ref, mix_ref):
        head_a = lax.broadcasted_iota(jnp.int32, (ts, 128), 1) < HD
        qm = qm_ref[...].astype(_MM)
        zq = jnp.zeros((ts, 128), _MM)
        mos = []
        for pi in range(DMEM // 128):
            cs = slice(pi * 128, (pi + 1) * 128)
            qp = qm[:, cs]
            kp = kv_ref[:, cs]
            vp = kv_ref[:, DMEM + pi * 128:DMEM + (pi + 1) * 128]
            pa = _mem_probs(jnp.where(head_a, qp, zq), kp)
            pb = _mem_probs(jnp.where(head_a, zq, qp), kp)
            mos.append(jnp.where(head_a, _mm(pa, vp), _mm(pb, vp)))
        cat = jnp.concatenate([tok_ref[...].astype(_MM)] + [m.astype(_MM) for m in mos], axis=-1)
        cat_ref[...] = cat
        mix = jnp.concatenate([_mm(cat, wo_ref[j]) for j in range(N_CHIPS)], axis=-1)
        mix_ref[...] = mix
        ho_ref[...] = h_ref[...] + mix * _rstd(mix) * g_ref[...]

    row = lambda i: (i, 0)
    return pl.pallas_call(
        body, name="mixout_fwd", grid=(S // ts,),
        in_specs=[pl.BlockSpec((ts, D), row), pl.BlockSpec((ts, DTOK), row),
                  pl.BlockSpec((ts, DMEM), lambda i: (i, qm_block)),
                  pl.BlockSpec((MEM_LEN, 2 * DMEM), lambda i: (0, 0)),
                  pl.BlockSpec((N_CHIPS, DMIX, D // N_CHIPS), lambda i: (0, 0, 0)), pl.BlockSpec((1, D), lambda i: (0, 0))],
        out_specs=[pl.BlockSpec((ts, D), row), pl.BlockSpec((ts, DMIX), row), pl.BlockSpec((ts, D), row)],
        out_shape=[jax.ShapeDtypeStruct((S, D), F32), jax.ShapeDtypeStruct((S, DMIX), _MM),
                   jax.ShapeDtypeStruct((S, D), F32)],
        compiler_params=_cparams(("parallel",)),
    )(h, tok, proj, kv, wo, g)


def _mixout_bwd(dho, mix, proj, kv, wo, g, qm_block):
    S = dho.shape[0]
    ts = _row_tile(S, 256)

    def body(dho_ref, mix_ref, qm_ref, kv_ref, wo_ref, g_ref, dmix_ref, dtok_ref, dqm_ref, dkv_ref, dg_ref):
        @pl.when(pl.program_id(0) == 0)
        def _():
            dkv_ref[...] = jnp.zeros_like(dkv_ref)
            dg_ref[...] = jnp.zeros_like(dg_ref)

        dmix, dg = _rms_bwd(mix_ref[...], g_ref[...], dho_ref[...])
        dg_ref[...] += dg
        dmb = dmix.astype(_MM)
        dmix_ref[...] = dmb
        cw = D // N_CHIPS
        dcat = _mm_nt(dmb[:, :cw], wo_ref[0])
        for j in range(1, N_CHIPS):
            dcat = dcat + _mm_nt(dmb[:, j * cw:(j + 1) * cw], wo_ref[j])
        dtok_ref[...] = dcat[:, :DTOK]
        head_a = lax.broadcasted_iota(jnp.int32, (ts, 128), 1) < HD
        qm = qm_ref[...].astype(_MM)
        zq = jnp.zeros((ts, 128), _MM)
        for pi in range(DMEM // 128):
            cs = slice(pi * 128, (pi + 1) * 128)
            vs = slice(DMEM + pi * 128, DMEM + (pi + 1) * 128)
            kp = kv_ref[:, cs]
            vp = kv_ref[:, vs]
            dmo = dcat[:, DTOK + pi * 128:DTOK + (pi + 1) * 128].astype(_MM)
            dqs = []
            dk_acc = jnp.zeros((MEM_LEN, 128), F32)
            dv_acc = jnp.zeros((MEM_LEN, 128), F32)
            for hd in range(2):
                qh = jnp.where(head_a, qm[:, cs], zq) if hd == 0 else jnp.where(head_a, zq, qm[:, cs])
                dmh = jnp.where(head_a, dmo, zq) if hd == 0 else jnp.where(head_a, zq, dmo)
                p = _mem_probs(qh, kp)
                dp = _mm_nt(dmh, vp)
                ds = (p * (dp - jnp.sum(dp * p, axis=-1, keepdims=True)) * SCALE).astype(_MM)
                dqs.append(_mm(ds, kp))
                dk_acc = dk_acc + _mm_tn(ds, qh)
                dv_acc = dv_acc + _mm_tn(p, dmh)
            dqm_ref[:, cs] = jnp.where(head_a, dqs[0], dqs[1]).astype(_MM)
            dkv_ref[:, cs] += dk_acc
            dkv_ref[:, vs] += dv_acc

    row = lambda i: (i, 0)
    fix = lambda i: (0, 0)
    return pl.pallas_call(
        body, name="mixout_bwd", grid=(S // ts,),
        in_specs=[pl.BlockSpec((ts, D), row), pl.BlockSpec((ts, D), row),
                  pl.BlockSpec((ts, DMEM), lambda i: (i, qm_block)),
                  pl.BlockSpec((MEM_LEN, 2 * DMEM), fix),
                  pl.BlockSpec((N_CHIPS, DMIX, D // N_CHIPS), lambda i: (0, 0, 0)), pl.BlockSpec((1, D), fix)],
        out_specs=[pl.BlockSpec((ts, D), row), pl.BlockSpec((ts, DTOK), row), pl.BlockSpec((ts, DMEM), row),
                   pl.BlockSpec((MEM_LEN, 2 * DMEM), fix), pl.BlockSpec((1, D), fix)],
        out_shape=[jax.ShapeDtypeStruct((S, D), _MM), jax.ShapeDtypeStruct((S, DTOK), F32),
                   jax.ShapeDtypeStruct((S, DMEM), _MM), jax.ShapeDtypeStruct((MEM_LEN, 2 * DMEM), F32),
                   jax.ShapeDtypeStruct((1, D), F32)],
        compiler_params=_cparams(("arbitrary",)),
    )(dho, mix, proj, kv, wo, g)


def _loss_grad(y, target):
    S = y.shape[0]
    ts = _row_tile(S, 512)

    def body(y_ref, t_ref, acc_ref, dy_ref):
        @pl.when(pl.program_id(0) == 0)
        def _():
            acc_ref[...] = jnp.zeros_like(acc_ref)

        err = y_ref[...] - t_ref[...]
        acc_ref[...] += jnp.sum(err * err, axis=0, keepdims=True)
        dy_ref[...] = err * (1.0 / D)

    row = lambda i: (i, 0)
    return pl.pallas_call(
        body, name="loss_grad", grid=(S // ts,),
        in_specs=[pl.BlockSpec((ts, D), row), pl.BlockSpec((ts, D), row)],
        out_specs=[pl.BlockSpec((1, D), lambda i: (0, 0)), pl.BlockSpec((ts, D), row)],
        out_shape=[jax.ShapeDtypeStruct((1, D), F32), jax.ShapeDtypeStruct((S, D), F32)],
        compiler_params=_cparams(("arbitrary",)),
    )(y, target)


LAYER_TENSORS = ("ffn1_gate", "ffn1_up", "ffn1_down", "ffn2_gate", "ffn2_up", "ffn2_down", "w_in", "w_mem_kv", "w_out")
SPLITS = {"ffn1_gate": "c", "ffn1_up": "c", "ffn1_down": "r", "ffn2_gate": "c", "ffn2_up": "c", "ffn2_down": "r",
          "w_in_pool": "r", "w_in_sb": "c", "w_mem_kv": "r", "w_out": "c"}


def _is_pool(i):
    return i % 2 == 0


def _qm_block(i):
    return (DTOK if _is_pool(i) else 3 * DTOK) // DMEM


FIRST_FFN = LAYER_TENSORS[:3]
REST = LAYER_TENSORS[3:]


def _layer_forward(h, mem, lw, i, rest_shards, next_shards):
    g_at = lambda g, k: g[k].reshape(1, D)
    sv = {"h0": h}
    (h, sv["a1"], sv["b1"], sv["f1"]), rest = _ffn_fwd(
        h, g_at(lw["g_pre"], 0), g_at(lw["g_post"], 0), lw["ffn1_gate"], lw["ffn1_up"], lw["ffn1_down"],
        _GatherExchange(rest_shards))
    lw.update(zip(REST, rest))
    if not _is_pool(i):
        lw["w_in"] = jnp.transpose(lw["w_in"], (1, 0, 2)).reshape(1, D, DSB)
    sv["h1"] = h
    sv["kv"] = _memkv_fwd(mem, lw["g_mem"], lw["w_mem_kv"])
    if _is_pool(i):
        sv["u"], sv["proj"] = _proj_fwd(h, g_at(lw["g_pre"], 1), lw["w_in"], F32)
        tok = _pool_fwd(sv["proj"], lw["pool_w"], lw["pool_scale"])
    else:
        sv["u"], sv["proj"] = _proj_fwd(h, g_at(lw["g_pre"], 1), lw["w_in"], _MM)
        tok = sv["o"] = _sb_fwd(sv["proj"])
    h, sv["cat"], sv["mix"] = _mixout_fwd(h, tok, sv["proj"], sv["kv"], lw["w_out"], g_at(lw["g_post"], 1),
                                          _qm_block(i))
    sv["h2"] = h
    ffn2 = (h, g_at(lw["g_pre"], 2), g_at(lw["g_post"], 2), lw["ffn2_gate"], lw["ffn2_up"], lw["ffn2_down"])
    if next_shards is None:
        h, sv["a2"], sv["b2"], sv["f2"] = _ffn_fwd(*ffn2)
        return h, sv, None
    (h, sv["a2"], sv["b2"], sv["f2"]), nxt = _ffn_fwd(*ffn2, _GatherExchange(next_shards))
    return h, sv, nxt


def _layer_backward(dh, mem, lw, sv, i, rides=None):
    g_at = lambda g, k: g[k].reshape(1, D)
    big, small = {}, {}
    dgpre, dgpost = [None] * 3, [None] * 3
    rode = {}
    rides = rides or {}

    def hosted(host, fn, *args):
        ride = rides[host](big, rode) if host in rides else None
        if ride is None:
            return fn(*args)
        outs, rode[host] = fn(*args, ride)
        return outs

    def ffn_back(which, dh, h_in):
        k = 0 if which == 1 else 2
        n = f"ffn{which}"
        dh, nb, sb, da, db, df, dgpre[k], dgpost[k] = hosted(
            n, _ffn_bwd, h_in, sv[f"f{which}"], sv[f"a{which}"], sv[f"b{which}"], dh, g_at(lw["g_pre"], k),
            g_at(lw["g_post"], k), lw[n + "_gate"], lw[n + "_up"], lw[n + "_down"])
        big[n + "_gate"] = hosted(n + "_gate", _wgrad, nb, da, N_CHIPS)
        big[n + "_up"] = hosted(n + "_up", _wgrad, nb, db, N_CHIPS)
        big[n + "_down"] = hosted(n + "_down", _wgrad, sb, df, 0).reshape(N_CHIPS, FF // N_CHIPS, D)
        return dh

    dh = ffn_back(2, dh, sv["h2"])
    dmix, dtok, dqm, dkv, dgpost[1] = _mixout_bwd(dh, sv["mix"], sv["proj"], sv["kv"], lw["w_out"],
                                                  g_at(lw["g_post"], 1), _qm_block(i))
    big["w_out"] = _wgrad(sv["cat"], dmix, N_CHIPS)
    dwkv, small["g_mem"] = _memkv_bwd(mem, lw["g_mem"], lw["w_mem_kv"], dkv)
    big["w_mem_kv"] = dwkv.reshape(N_CHIPS, D // N_CHIPS, 2 * DMEM)
    if _is_pool(i):
        dx, small["pool_w"], small["pool_scale"] = _pool_bwd(sv["proj"], dtok, lw["pool_w"], lw["pool_scale"])
        dh, dproj, dgpre[1] = _proj_bwd(sv["h1"], dh, g_at(lw["g_pre"], 1), lw["w_in"], [dx, dqm])
        big["w_in"] = _wgrad(sv["u"], dproj).reshape(N_CHIPS, D // N_CHIPS, DMIX)
    else:
        dq, dk, dv = _sb_bwd(sv["proj"], sv["o"], dtok)
        dh, dproj, dgpre[1] = _proj_bwd(sv["h1"], dh, g_at(lw["g_pre"], 1), lw["w_in"], [dq, dk, dv, dqm])
        big["w_in"] = jnp.transpose(_wgrad(sv["u"], dproj).reshape(D, N_CHIPS, DSB // N_CHIPS), (1, 0, 2))
    dh = ffn_back(1, dh, sv["h0"])
    small["g_pre"] = jnp.concatenate(dgpre, axis=0)
    small["g_post"] = jnp.concatenate(dgpost, axis=0)
    return dh, big, small, rode


ANY = pl.BlockSpec(memory_space=pl.ANY)


def _place():
    return lax.axis_index("x"), lax.axis_index("y"), lax.axis_index("c")


def _flip(v, bit):
    return 1 - v if bit else v


def _small_allgather(buf):
    R = buf.shape[0]

    def body(x_ref, o_ref, send_sems, recv_sems, loc_sem):
        x, y, c = _place()
        me = 4 * x + 2 * y + c
        loc = pltpu.make_async_copy(x_ref, o_ref.at[me], loc_sem)
        loc.start()
        sends = []
        for k in range(1, N_DEV):
            peer = (_flip(x, k & 4), _flip(y, k & 2), _flip(c, k & 1))
            cp = pltpu.make_async_remote_copy(src_ref=x_ref, dst_ref=o_ref.at[me], send_sem=send_sems.at[k - 1],
                                              recv_sem=recv_sems.at[k - 1], device_id=peer, device_id_type=MESH)
            cp.start()
            sends.append(cp)
        for k in range(1, N_DEV):
            peer = (_flip(x, k & 4), _flip(y, k & 2), _flip(c, k & 1))
            slot = 4 * peer[0] + 2 * peer[1] + peer[2]
            pltpu.make_async_remote_copy(src_ref=x_ref, dst_ref=o_ref.at[slot], send_sem=send_sems.at[k - 1],
                                         recv_sem=recv_sems.at[k - 1], device_id=peer, device_id_type=MESH).wait_recv()
        for cp in sends:
            cp.wait_send()
        loc.wait()

    return pl.pallas_call(
        body, name="small_allgather", in_specs=[ANY], out_specs=ANY,
        out_shape=jax.ShapeDtypeStruct((N_DEV, R, 128), buf.dtype),
        scratch_shapes=[pltpu.SemaphoreType.DMA((N_DEV - 1,)), pltpu.SemaphoreType.DMA((N_DEV - 1,)),
                        pltpu.SemaphoreType.DMA],
    )(buf)


def _dma_sems(m):
    return pltpu.SemaphoreType.DMA((m,))


class _GatherExchange:
    name = "gather_weights"

    def __init__(self, shards):
        n = len(shards)
        self.n, self.inputs = n, list(shards)
        self.halves = [s.shape[0] // 2 for s in shards]
        self.out_shapes = [jax.ShapeDtypeStruct((N_CHIPS,) + s.shape, s.dtype) for s in shards]
        self.sems = [_dma_sems(3 * n), _dma_sems(3 * n), _dma_sems(3 * n), _dma_sems(3 * n), _dma_sems(2 * n)]

    def _copies(self, ins, outs, sems):
        ici_send, ici_recv, d2d_send, d2d_recv, loc_sems = sems
        x, y, c = _place()
        me = 2 * x + y
        sib = (x, y, 1 - c)
        chips = [(_flip(x, k & 2), _flip(y, k & 1)) for k in (1, 2, 3)]
        half = lambda t, slot, hc: outs[t].at[slot, pl.ds(hc * self.halves[t], self.halves[t])]
        locs, ici_out, ici_in, fwd_out, fwd_in = [], [], [], [], []
        for t in range(self.n):
            for hc in range(2):
                rows = pl.ds(hc * self.halves[t], self.halves[t])
                locs.append(pltpu.make_async_copy(ins[t].at[rows], outs[t].at[me, rows], loc_sems.at[2 * t + hc]))
            for k, chip in enumerate(chips):
                idx = 2 * chip[0] + chip[1]
                peer = (chip[0], chip[1], c)
                ici = dict(send_sem=ici_send.at[3 * t + k], recv_sem=ici_recv.at[3 * t + k], device_id=peer,
                           device_id_type=MESH)
                d2d = dict(send_sem=d2d_send.at[3 * t + k], recv_sem=d2d_recv.at[3 * t + k], device_id=sib,
                           device_id_type=MESH)
                ici_out.append(pltpu.make_async_remote_copy(
                    src_ref=ins[t].at[pl.ds(c * self.halves[t], self.halves[t])], dst_ref=half(t, me, c), **ici))
                ici_in.append(pltpu.make_async_remote_copy(src_ref=half(t, idx, c), dst_ref=half(t, idx, c), **ici))
                fwd_out.append(pltpu.make_async_remote_copy(src_ref=half(t, idx, c), dst_ref=half(t, idx, c), **d2d))
                fwd_in.append(pltpu.make_async_remote_copy(
                    src_ref=half(t, idx, 1 - c), dst_ref=half(t, idx, 1 - c), **d2d))
        return locs, ici_out, ici_in, fwd_out, fwd_in

    def start(self, ins, outs, sems):
        locs, ici_out, _, _, _ = self._copies(ins, outs, sems)
        for cp in ici_out + locs:
            cp.start()

    def mid(self, ins, outs, sems):
        _, _, ici_in, fwd_out, _ = self._copies(ins, outs, sems)
        for arrived, onward in zip(ici_in, fwd_out):
            arrived.wait_recv()
            onward.start()

    def finish(self, ins, outs, sems):
        locs, ici_out, _, fwd_out, fwd_in = self._copies(ins, outs, sems)
        for cp in fwd_in:
            cp.wait_recv()
        for cp in ici_out + fwd_out:
            cp.wait_send()
        for cp in locs:
            cp.wait()


class _SiblingExchange:
    name = "rs_to_sibling"

    def __init__(self, gs):
        n = len(gs)
        self.n, self.inputs = n, list(gs)
        self.halves = [g.shape[1] // 2 for g in gs]
        self.out_shapes = [jax.ShapeDtypeStruct((N_CHIPS, g.shape[1] // 2) + g.shape[2:], g.dtype) for g in gs]
        self.sems = [_dma_sems(n), _dma_sems(n)]

    def _copies(self, ins, outs, sems):
        x, y, c = _place()
        return [pltpu.make_async_remote_copy(
            src_ref=ins[t].at[:, pl.ds((1 - c) * self.halves[t], self.halves[t])], dst_ref=outs[t],
            send_sem=sems[0].at[t], recv_sem=sems[1].at[t], device_id=(x, y, 1 - c), device_id_type=MESH)
            for t in range(self.n)]

    def start(self, ins, outs, sems):
        for cp in self._copies(ins, outs, sems):
            cp.start()

    def mid(self, ins, outs, sems):
        pass

    def finish(self, ins, outs, sems):
        for cp in self._copies(ins, outs, sems):
            cp.wait()


class _ChipsExchange:
    name = "rs_to_chips"

    def __init__(self, ss):
        n = len(ss)
        self.n, self.inputs = n, list(ss)
        self.out_shapes = [jax.ShapeDtypeStruct((3,) + s.shape[1:], s.dtype) for s in ss]
        self.sems = [_dma_sems(3 * n), _dma_sems(3 * n)]

    def _copies(self, ins, outs, sems):
        x, y, c = _place()
        chips = [(_flip(x, k & 2), _flip(y, k & 1)) for k in (1, 2, 3)]
        return [pltpu.make_async_remote_copy(
            src_ref=ins[t].at[2 * chip[0] + chip[1]], dst_ref=outs[t].at[k], send_sem=sems[0].at[3 * t + k],
            recv_sem=sems[1].at[3 * t + k], device_id=(chip[0], chip[1], c), device_id_type=MESH)
            for t in range(self.n) for k, chip in enumerate(chips)]

    def start(self, ins, outs, sems):
        for cp in self._copies(ins, outs, sems):
            cp.start()

    def mid(self, ins, outs, sems):
        pass

    def finish(self, ins, outs, sems):
        for cp in self._copies(ins, outs, sems):
            cp.wait()


class _Both:
    def __init__(self, first, second):
        self.parts = (first, second)
        self.name = first.name + "_" + second.name
        self.inputs = first.inputs + second.inputs
        self.out_shapes = first.out_shapes + second.out_shapes
        self.sems = first.sems + second.sems

    def _each(self, ins, outs, sems):
        a = self.parts[0]
        ni, no, ns = len(a.inputs), len(a.out_shapes), len(a.sems)
        return ((a, ins[:ni], outs[:no], sems[:ns]), (self.parts[1], ins[ni:], outs[no:], sems[ns:]))

    def start(self, ins, outs, sems):
        for ex, i, o, s in self._each(ins, outs, sems):
            ex.start(i, o, s)

    def mid(self, ins, outs, sems):
        for ex, i, o, s in self._each(ins, outs, sems):
            ex.mid(i, o, s)

    def finish(self, ins, outs, sems):
        for ex, i, o, s in self._each(ins, outs, sems):
            ex.finish(i, o, s)

    def split(self, outs):
        no = len(self.parts[0].out_shapes)
        return outs[:no], outs[no:]


def _run_exchange(ex):
    ni, no = len(ex.inputs), len(ex.out_shapes)

    def body(*refs):
        ins, outs, sems = refs[:ni], refs[ni:ni + no], refs[ni + no:]
        ex.start(ins, outs, sems)
        ex.mid(ins, outs, sems)
        ex.finish(ins, outs, sems)

    return pl.pallas_call(
        body, name=ex.name, in_specs=[ANY] * ni, out_specs=[ANY] * no, out_shape=ex.out_shapes,
        scratch_shapes=ex.sems,
    )(*ex.inputs)


def _call(body, name, nsteps, in_specs, out_specs, out_shape, args, ride=None):
    if ride is None:
        return pl.pallas_call(body, name=name, grid=(nsteps,), in_specs=in_specs, out_specs=out_specs,
                              out_shape=out_shape, compiler_params=_cparams(("arbitrary",)))(*args)
    ni, no = len(args), len(out_shape)
    ri, ro = len(ride.inputs), len(ride.out_shapes)
    mid_step = (3 * nsteps) // 4 if nsteps > 1 else 0

    def hosted(*refs):
        a, ra = refs[:ni], refs[ni:ni + ri]
        o, ro_refs = refs[ni + ri:ni + ri + no], refs[ni + ri + no:ni + ri + no + ro]
        sems = refs[ni + ri + no + ro:]
        step = pl.program_id(0)

        @pl.when(step == 0)
        def _():
            ride.start(ra, ro_refs, sems)

        body(*a, *o)

        @pl.when(step == mid_step)
        def _():
            ride.mid(ra, ro_refs, sems)

        @pl.when(step == nsteps - 1)
        def _():
            ride.finish(ra, ro_refs, sems)

    outs = pl.pallas_call(
        hosted, name=name + "_" + ride.name, grid=(nsteps,), in_specs=list(in_specs) + [ANY] * ri,
        out_specs=list(out_specs) + [ANY] * ro, out_shape=list(out_shape) + ride.out_shapes,
        scratch_shapes=ride.sems, compiler_params=_cparams(("arbitrary",)),
    )(*args, *ride.inputs)
    return outs[:no], outs[no:]


def _share_halves(ts):
    n = len(ts)

    def body(*refs):
        outs = refs[n:2 * n]
        send_sems, recv_sems = refs[2 * n:]
        x, y, c = _place()
        cps = []
        for t in range(n):
            cp = pltpu.make_async_remote_copy(
                src_ref=outs[t].at[c], dst_ref=outs[t].at[c], send_sem=send_sems.at[t], recv_sem=recv_sems.at[t],
                device_id=(x, y, 1 - c), device_id_type=MESH)
            cp.start()
            cps.append(cp)
        for t in range(n):
            pltpu.make_async_remote_copy(
                src_ref=outs[t].at[1 - c], dst_ref=outs[t].at[1 - c], send_sem=send_sems.at[t],
                recv_sem=recv_sems.at[t], device_id=(x, y, 1 - c), device_id_type=MESH).wait_recv()
        for cp in cps:
            cp.wait_send()

    dma = lambda m: pltpu.SemaphoreType.DMA((m,))
    return pl.pallas_call(
        body, name="share_halves", in_specs=[ANY] * n, out_specs=[ANY] * n,
        out_shape=[jax.ShapeDtypeStruct(t.shape, t.dtype) for t in ts],
        input_output_aliases={t: t for t in range(n)},
        scratch_shapes=[dma(n), dma(n)],
    )(*ts)


def _rows(a, lead):
    return a.reshape(a.shape[:lead] + (-1, a.shape[-1]))


def _add_sibling(g, r, c):
    L_ = g.shape[1]
    g3 = _rows(g.reshape((N_CHIPS, 2, L_ // 2) + g.shape[2:]), 2)
    r3 = _rows(r, 1)
    R_, C = r3.shape[1:]
    tr = _row_tile(R_, 512)

    def body(c_ref, g_ref, r_ref, o_ref):
        o_ref[...] = (g_ref[...] + r_ref[...]).astype(o_ref.dtype)

    out = pl.pallas_call(
        body, name="add_sibling",
        grid_spec=pltpu.PrefetchScalarGridSpec(
            num_scalar_prefetch=1, grid=(N_CHIPS, R_ // tr),
            in_specs=[pl.BlockSpec((None, None, tr, C), lambda s, i, c_ref: (s, c_ref[0], i, 0)),
                      pl.BlockSpec((None, tr, C), lambda s, i, c_ref: (s, i, 0))],
            out_specs=pl.BlockSpec((None, tr, C), lambda s, i, c_ref: (s, i, 0))),
        out_shape=jax.ShapeDtypeStruct((N_CHIPS, R_, C), _MM),
        compiler_params=_cparams(("parallel", "parallel")),
    )(c.reshape(1), g3, r3)
    return out.reshape((N_CHIPS, L_ // 2) + g.shape[2:])


def _add_chips(s, r, me, c):
    s3, r3 = _rows(s, 1), _rows(r, 1)
    R_, C = s3.shape[1:]
    tr = _row_tile(R_, 512)

    def body(at_ref, s_ref, r0_ref, r1_ref, r2_ref, o_ref):
        f = lambda ref: ref[...].astype(F32)
        o_ref[...] = ((f(s_ref) + f(r0_ref)) + f(r1_ref)) + f(r2_ref)

    rk = lambda k: pl.BlockSpec((None, tr, C), lambda i, at_ref: (k, i, 0))
    out = pl.pallas_call(
        body, name="add_chips",
        grid_spec=pltpu.PrefetchScalarGridSpec(
            num_scalar_prefetch=1, grid=(R_ // tr,),
            in_specs=[pl.BlockSpec((None, tr, C), lambda i, at_ref: (at_ref[0], i, 0)), rk(0), rk(1), rk(2)],
            out_specs=pl.BlockSpec((None, tr, C), lambda i, at_ref: (at_ref[1], i, 0))),
        out_shape=jax.ShapeDtypeStruct((2, R_, C), F32),
        compiler_params=_cparams(("parallel",)),
    )(jnp.stack([me, c]), s3, r3, r3, r3)
    return out.reshape((2,) + s.shape[1:])


def _sum_devices(a):
    R_ = a.shape[1]

    def body(a_ref, o_ref):
        acc = a_ref[0]
        for k in range(1, N_DEV):
            acc = acc + a_ref[k]
        o_ref[...] = acc

    return pl.pallas_call(
        body, name="sum_devices", out_shape=jax.ShapeDtypeStruct((R_, 128), F32),
        compiler_params=pltpu.CompilerParams(vmem_limit_bytes=VMEM_LIMIT),
    )(a)


def _adamw(w, g, m, v):
    shape = w.shape
    to2 = lambda a: a.reshape(-1, shape[-1])
    w2, g2, m2, v2 = to2(w), to2(g), to2(m), to2(v)
    R_, C = w2.shape
    tr = 256 if R_ % 256 == 0 else R_

    def body(w_ref, g_ref, m_ref, v_ref, d_ref, mo_ref, vo_ref):
        gv = g_ref[...]
        mn = ADAM_B1 * m_ref[...] + (1.0 - ADAM_B1) * gv
        vn = ADAM_B2 * v_ref[...] + (1.0 - ADAM_B2) * (gv * gv)
        m_hat = mn / (1.0 - ADAM_B1 ** ADAM_STEP)
        v_hat = vn / (1.0 - ADAM_B2 ** ADAM_STEP)
        d_ref[...] = -ADAM_LR * (m_hat / (jnp.sqrt(v_hat) + ADAM_EPS) + ADAM_WD * w_ref[...])
        mo_ref[...] = mn
        vo_ref[...] = vn

    spec = pl.BlockSpec((tr, C), lambda i: (i, 0))
    outs = pl.pallas_call(
        body, name="adamw", grid=(R_ // tr,), in_specs=[spec] * 4, out_specs=[spec] * 3,
        out_shape=[jax.ShapeDtypeStruct((R_, C), F32)] * 3,
        compiler_params=_cparams(("parallel",)),
    )(w2, g2, m2, v2)
    return [o.reshape(shape) for o in outs]


SHARD_AXIS = {"g_pre": 2, "g_post": 2, "g_mem": None, "ffn1_gate": 2, "ffn1_up": 2, "ffn1_down": 1,
              "ffn2_gate": 2, "ffn2_up": 2, "ffn2_down": 1, "w_in_pool": 1, "pool_w": None, "pool_scale": None,
              "w_in_sb": 2, "w_mem_kv": 1, "w_out": 2}
WEIGHTS = list(SHARD_AXIS)
BIG = [k for k in WEIGHTS if SHARD_AXIS[k] is not None and k not in ("g_pre", "g_post")]
SMALL = [k for k in WEIGHTS if k not in BIG]


def _unshard(g, axis):
    _, L_, r, N = g.shape
    if axis == 1:
        return jnp.transpose(g, (1, 0, 2, 3)).reshape(L_, N_CHIPS * r, N)
    return jnp.transpose(g, (1, 2, 0, 3)).reshape(L_, r, N_CHIPS * N)


def _to_shards(a, axis):
    L_, r, N = a.shape
    if axis == 1:
        return jnp.transpose(a.reshape(L_, N_CHIPS, r // N_CHIPS, N), (1, 0, 2, 3))
    return jnp.transpose(a.reshape(L_, r, N_CHIPS, N // N_CHIPS), (2, 0, 1, 3))


def _pack128(arrs):
    flat = jnp.concatenate([a.reshape(-1) for a in arrs])
    pad = (-flat.shape[0]) % (8 * 128)
    return jnp.pad(flat, (0, pad)).reshape(-1, 128)


def _unpack128(buf, shapes):
    flat = buf.reshape(-1)
    out, off = [], 0
    for shp in shapes:
        size = 1
        for d in shp:
            size *= d
        out.append(flat[off:off + size].reshape(shp))
        off += size
    return out


def kernel(x, mem, g_pre, g_post, g_mem, ffn1_gate, ffn1_up, ffn1_down, ffn2_gate, ffn2_up, ffn2_down, w_in_pool, pool_w, pool_scale, w_in_sb, w_mem_kv, w_out, loss_target, m_g_pre, m_g_post, m_g_mem, m_ffn1_gate, m_ffn1_up, m_ffn1_down, m_ffn2_gate, m_ffn2_up, m_ffn2_down, m_w_in_pool, m_pool_w, m_pool_scale, m_w_in_sb, m_w_mem_kv, m_w_out, v_g_pre, v_g_post, v_g_mem, v_ffn1_gate, v_ffn1_up, v_ffn1_down, v_ffn2_gate, v_ffn2_up, v_ffn2_down, v_w_in_pool, v_pool_w, v_pool_scale, v_w_in_sb, v_w_mem_kv, v_w_out):
    w = dict(g_pre=g_pre, g_post=g_post, g_mem=g_mem, ffn1_gate=ffn1_gate, ffn1_up=ffn1_up, ffn1_down=ffn1_down,
             ffn2_gate=ffn2_gate, ffn2_up=ffn2_up, ffn2_down=ffn2_down, w_in_pool=w_in_pool, pool_w=pool_w,
             pool_scale=pool_scale, w_in_sb=w_in_sb, w_mem_kv=w_mem_kv, w_out=w_out)
    m = dict(g_pre=m_g_pre, g_post=m_g_post, g_mem=m_g_mem, ffn1_gate=m_ffn1_gate, ffn1_up=m_ffn1_up,
             ffn1_down=m_ffn1_down, ffn2_gate=m_ffn2_gate, ffn2_up=m_ffn2_up, ffn2_down=m_ffn2_down,
             w_in_pool=m_w_in_pool, pool_w=m_pool_w, pool_scale=m_pool_scale, w_in_sb=m_w_in_sb,
             w_mem_kv=m_w_mem_kv, w_out=m_w_out)
    v = dict(g_pre=v_g_pre, g_post=v_g_post, g_mem=v_g_mem, ffn1_gate=v_ffn1_gate, ffn1_up=v_ffn1_up,
             ffn1_down=v_ffn1_down, ffn2_gate=v_ffn2_gate, ffn2_up=v_ffn2_up, ffn2_down=v_ffn2_down,
             w_in_pool=v_w_in_pool, pool_w=v_pool_w, pool_scale=v_pool_scale, w_in_sb=v_w_in_sb,
             w_mem_kv=v_w_mem_kv, w_out=v_w_out)
    cx, cy, cc = _place()
    chip = (2 * cx + cy).astype(jnp.int32)
    core = cc.astype(jnp.int32)

    gains = _small_allgather(_pack128([g_pre, g_post]))
    per_chip = [_unpack128(gains[2 * j], [g_pre.shape, g_post.shape]) for j in range(N_CHIPS)]
    full_pre = jnp.concatenate([p[0] for p in per_chip], axis=2)
    full_post = jnp.concatenate([p[1] for p in per_chip], axis=2)

    def shard_name(k, i):
        return ("w_in_pool" if _is_pool(i) else "w_in_sb") if k == "w_in" else k

    def own(i, names):
        return [w[shard_name(k, i)][i // 2 if k == "w_in" else i].astype(_MM) for k in names]

    xs, mems = x[0], mem[0]
    h, lws, saved = xs, [], []
    first = _run_exchange(_GatherExchange(own(0, FIRST_FFN)))
    for i in range(DEPTH):
        lw = dict(zip(FIRST_FFN, first))
        lw["g_pre"], lw["g_post"], lw["g_mem"] = full_pre[i], full_post[i], g_mem[i].reshape(1, D)
        if _is_pool(i):
            lw["pool_w"], lw["pool_scale"] = pool_w[i // 2], pool_scale[i // 2].reshape(1, DTOK)
        h, sv, first = _layer_forward(h, mems, lw, i, own(i, REST), own(i + 1, FIRST_FFN) if i + 1 < DEPTH else None)
        lws.append(lw)
        saved.append(sv)
    acc, dh = _loss_grad(h, loss_target[0])
    loss = lax.psum(0.5 / D * jnp.sum(acc), ("x", "y", "c"))

    mine, smalls = [None] * DEPTH, [None] * DEPTH
    above, sums = None, {}
    nt = len(LAYER_TENSORS)
    early = ("ffn2_gate", "w_mem_kv", "ffn2_up", "w_out", "ffn2_down", "w_in")
    add_sib = lambda gs, from_sib: [_add_sibling(g, r, core) for g, r in zip(gs, from_sib)]
    add_chp = lambda ss, from_chips: [_add_chips(s, r, chip, core) for s, r in zip(ss, from_chips)]

    for i in reversed(range(DEPTH)):
        rides = {}
        if above is not None:
            up = [above[k] for k in LAYER_TENSORS]
            rides["ffn2"] = lambda big, rode, up=up: _SiblingExchange(up)

            def on_ffn1(big, rode, up=up, layer=i + 1, bottom=(i == 0)):
                sums[layer] = add_sib(up, rode["ffn2"])
                ex = _ChipsExchange(sums[layer])
                return _Both(ex, _SiblingExchange([big[k] for k in early])) if bottom else ex

            def on_wgrad(big, rode, j):
                if "early" not in sums:
                    sums["early"] = add_sib([big[k] for k in early], rode["ffn1"][nt:])
                return _ChipsExchange(sums["early"][2 * j:2 * j + 2])

            rides["ffn1"] = on_ffn1
            if i == 0:
                for j, host in enumerate(FIRST_FFN):
                    rides[host] = lambda big, rode, j=j: on_wgrad(big, rode, j)
        dh, gs, smalls[i], rode = _layer_backward(dh, mems, lws[i], saved[i], i, rides)
        if above is not None:
            mine[i + 1] = add_chp(sums[i + 1], rode["ffn1"][:nt])
        above = gs
    early_mine = add_chp(sums["early"], [t for host in FIRST_FFN for t in rode[host]])
    late = [above[k] for k in FIRST_FFN]
    late_sums = add_sib(late, _run_exchange(_SiblingExchange(late)))
    late_mine = add_chp(late_sums, _run_exchange(_ChipsExchange(late_sums)))
    by_name = dict(zip(early + FIRST_FFN, early_mine + late_mine))
    mine[0] = [by_name[k] for k in LAYER_TENSORS]
    shared = _share_halves([t for layer in mine for t in layer])
    reduced = [dict(zip(LAYER_TENSORS, [s.reshape(-1, s.shape[-1]) for s in shared[nt * i:nt * (i + 1)]]))
               for i in range(DEPTH)]
    dx = dh
    red = {}
    for k in BIG:
        if k.startswith("w_in_"):
            layers = [i for i in range(DEPTH) if _is_pool(i) == (k == "w_in_pool")]
            red[k] = jnp.stack([reduced[i]["w_in"] for i in layers])
        else:
            red[k] = jnp.stack([reduced[i][k] for i in range(DEPTH)])

    grads = {"g_pre": jnp.stack([s["g_pre"] for s in smalls]), "g_post": jnp.stack([s["g_post"] for s in smalls]),
             "g_mem": jnp.concatenate([s["g_mem"] for s in smalls], axis=0),
             "pool_w": jnp.stack([smalls[i]["pool_w"] for i in range(DEPTH) if _is_pool(i)]),
             "pool_scale": jnp.concatenate([smalls[i]["pool_scale"] for i in range(DEPTH) if _is_pool(i)], axis=0)}
    small_shapes = [grads[k].shape for k in SMALL]
    summed = _unpack128(_sum_devices(_small_allgather(_pack128([grads[k] for k in SMALL]))), small_shapes)
    for k, s in zip(SMALL, summed):
        if SHARD_AXIS[k] is None:
            red[k] = s
        else:
            red[k] = lax.dynamic_slice_in_dim(s, chip * w[k].shape[2], w[k].shape[2], axis=2)

    deltas, new_m, new_v = {}, {}, {}
    for k in WEIGHTS:
        deltas[k], new_m[k], new_v[k] = _adamw(w[k], red[k], m[k], v[k])
    return (loss, dx.reshape(x.shape), *[red[k] for k in WEIGHTS], *[deltas[k] for k in WEIGHTS],
            *[new_m[k] for k in WEIGHTS], *[new_v[k] for k in WEIGHTS])
```

```python
import jax
import jax.numpy as jnp
from jax import lax
from jax.experimental import pallas as pl
from jax.experimental.pallas import tpu as pltpu

F32 = jnp.float32
BF16 = jnp.bfloat16
_MM = jnp.bfloat16

D = 1024
FF = 2048
DTOK = 512
DMEM = 256
DMIX = DTOK + DMEM
DSB = 3 * DTOK + DMEM
HD = 64
MEM_LEN = 256
DEPTH = 4
EPS = 1e-6
WINDOWS = (2, 4, 8, 16)
HALO = 16
PG = 128
QB = 128
SCALE = HD ** -0.5
FC = 512
NEG_CUT = -104.0

ADAM_LR, ADAM_B1, ADAM_B2, ADAM_EPS, ADAM_WD, ADAM_STEP = 0.001, 0.9, 0.999, 1e-08, 0.01, 10

VMEM_LIMIT = 56 * 2 ** 20
MESH = pl.DeviceIdType.MESH
N_CHIPS = 4
N_DEV = 8


def _cparams(sem):
    return pltpu.CompilerParams(dimension_semantics=sem, vmem_limit_bytes=VMEM_LIMIT)


def _mm(a, b):
    return jnp.dot(a.astype(_MM), b.astype(_MM), preferred_element_type=F32)


def _mm_nt(a, b):
    return lax.dot_general(a.astype(_MM), b.astype(_MM), (((1,), (1,)), ((), ())), preferred_element_type=F32)


def _mm_tn(a, b):
    return lax.dot_general(a.astype(_MM), b.astype(_MM), (((0,), (0,)), ((), ())), preferred_element_type=F32)


def _mm2(a, b):
    hi = a.astype(_MM)
    lo = (a - hi.astype(F32)).astype(_MM)
    return jnp.dot(hi, b, preferred_element_type=F32) + jnp.dot(lo, b, preferred_element_type=F32)


def _rstd(x):
    return lax.rsqrt(jnp.mean(x * x, axis=-1, keepdims=True) + EPS)


def _rms_bwd(x, g, dy):
    r = _rstd(x)
    xh = x * r
    t = dy * g
    dx = r * (t - xh * jnp.mean(t * xh, axis=-1, keepdims=True))
    return dx, jnp.sum(dy * xh, axis=0, keepdims=True)


def _sigmoid(a):
    return 1.0 / (1.0 + jnp.exp(-a))


def _row_tile(n, want):
    t = min(n, want)
    assert n % t == 0, (n, t)
    return t


def _ffn_fwd(h, g1, g2, wg, wu, wd, ride=None):
    S = h.shape[0]
    ts = _row_tile(S, 512)

    def body(h_ref, g1_ref, g2_ref, wg_ref, wu_ref, wd_ref, ho_ref, a_ref, b_ref, f_ref):
        hv = h_ref[...]
        n = (hv * _rstd(hv) * g1_ref[...]).astype(_MM)
        f = jnp.zeros((ts, D), F32)
        for j in range(FF // FC):
            cs = slice(j * FC, (j + 1) * FC)
            a = _mm(n, wg_ref[j])
            b = _mm(n, wu_ref[j])
            a_ref[:, cs] = a.astype(a_ref.dtype)
            b_ref[:, cs] = b.astype(b_ref.dtype)
            f = f + _mm(a * _sigmoid(a) * b, wd_ref[j])
        f_ref[...] = f
        ho_ref[...] = hv + 0.5 * (f * _rstd(f) * g2_ref[...])

    row = lambda i: (i, 0)
    fix = lambda i: (0, 0)
    all3 = lambda i: (0, 0, 0)
    ns = FF // FC
    return _call(
        body, "ffn_fwd", S // ts,
        [pl.BlockSpec((ts, D), row), pl.BlockSpec((1, D), fix), pl.BlockSpec((1, D), fix),
         pl.BlockSpec((ns, D, FC), all3), pl.BlockSpec((ns, D, FC), all3), pl.BlockSpec((ns, FC, D), all3)],
        [pl.BlockSpec((ts, D), row), pl.BlockSpec((ts, FF), row), pl.BlockSpec((ts, FF), row),
         pl.BlockSpec((ts, D), row)],
        [jax.ShapeDtypeStruct((S, D), F32), jax.ShapeDtypeStruct((S, FF), _MM),
         jax.ShapeDtypeStruct((S, FF), _MM), jax.ShapeDtypeStruct((S, D), F32)],
        (h, g1, g2, wg, wu, wd), ride)


def _ffn_bwd(h, f, a, b, dho, g1, g2, wg, wu, wd, ride=None):
    S = h.shape[0]
    ts = _row_tile(S, 256)

    def body(h_ref, f_ref, a_ref, b_ref, dho_ref, g1_ref, g2_ref, wg_ref, wu_ref, wd_ref,
             dh_ref, n_ref, s_ref, da_ref, db_ref, df_ref, dg1_ref, dg2_ref):
        @pl.when(pl.program_id(0) == 0)
        def _():
            dg1_ref[...] = jnp.zeros_like(dg1_ref)
            dg2_ref[...] = jnp.zeros_like(dg2_ref)

        hv = h_ref[...]
        dho = dho_ref[...]
        df, dg2 = _rms_bwd(f_ref[...], g2_ref[...], 0.5 * dho)
        dg2_ref[...] += dg2
        dfb = df.astype(_MM)
        df_ref[...] = dfb
        rh = _rstd(hv)
        hh = hv * rh
        g1 = g1_ref[...]
        n_ref[...] = (hh * g1).astype(_MM)
        dss = [_mm_nt(dfb, wd_ref[j]) for j in range(FF // FC)]
        das, dbs = [], []
        for j in range(FF // FC):
            cs = slice(j * FC, (j + 1) * FC)
            av = a_ref[:, cs].astype(F32)
            bv = b_ref[:, cs].astype(F32)
            sig = _sigmoid(av)
            sl = av * sig
            s_ref[:, cs] = (sl * bv).astype(_MM)
            da = (dss[j] * bv * (sig * (1.0 + av * (1.0 - sig)))).astype(_MM)
            db = (dss[j] * sl).astype(_MM)
            da_ref[:, cs] = da
            db_ref[:, cs] = db
            das.append(da)
            dbs.append(db)
        dn = jnp.zeros((ts, D), F32)
        for j in range(FF // FC):
            dn = dn + _mm_nt(das[j], wg_ref[j]) + _mm_nt(dbs[j], wu_ref[j])
        t = dn * g1
        dh_ref[...] = dho + rh * (t - hh * jnp.mean(t * hh, axis=-1, keepdims=True))
        dg1_ref[...] += jnp.sum(dn * hh, axis=0, keepdims=True)

    row = lambda i: (i, 0)
    fix = lambda i: (0, 0)
    all3 = lambda i: (0, 0, 0)
    ns = FF // FC
    td = pl.BlockSpec((ts, D), row)
    tf = pl.BlockSpec((ts, FF), row)
    gd = pl.BlockSpec((1, D), fix)
    return _call(
        body, "ffn_bwd", S // ts,
        [td, td, tf, tf, td, gd, gd,
         pl.BlockSpec((ns, D, FC), all3), pl.BlockSpec((ns, D, FC), all3), pl.BlockSpec((ns, FC, D), all3)],
        [td, td, tf, tf, tf, td, gd, gd],
        [jax.ShapeDtypeStruct((S, D), F32), jax.ShapeDtypeStruct((S, D), _MM),
         jax.ShapeDtypeStruct((S, FF), _MM), jax.ShapeDtypeStruct((S, FF), _MM),
         jax.ShapeDtypeStruct((S, FF), _MM), jax.ShapeDtypeStruct((S, D), _MM),
         jax.ShapeDtypeStruct((1, D), F32), jax.ShapeDtypeStruct((1, D), F32)],
        (h, f, a, b, dho, g1, g2, wg, wu, wd), ride)


def _wgrad(x, y, nslot=0, ride=None):
    T, M = x.shape
    N = y.shape[1]
    tt = _row_tile(T, 1024)
    cw = N // nslot if nslot else (512 if N % 512 == 0 else 256)
    nc = N // cw
    assert cw * nc == N and cw % 128 == 0, (N, nc)

    def body(x_ref, y_ref, o_ref):
        @pl.when(pl.program_id(0) == 0)
        def _():
            o_ref[...] = jnp.zeros_like(o_ref)

        xt = x_ref[...].T
        for j in range(nc):
            part = jnp.dot(xt, y_ref[:, j * cw:(j + 1) * cw], preferred_element_type=F32)
            if nslot:
                o_ref[j] += part
            else:
                o_ref[:, j * cw:(j + 1) * cw] += part

    oshape = (nslot, M, cw) if nslot else (M, N)
    res = _call(body, "wgrad", T // tt,
                [pl.BlockSpec((tt, M), lambda t: (t, 0)), pl.BlockSpec((tt, N), lambda t: (t, 0))],
                [pl.BlockSpec(oshape, lambda t: (0,) * len(oshape))], [jax.ShapeDtypeStruct(oshape, F32)], (x, y), ride)
    return res[0] if ride is None else (res[0][0], res[1])


def _proj_fwd(h, g, w, out_dtype):
    S = h.shape[0]
    nk, kw, N = w.shape
    ts = _row_tile(S, 256)

    def body(h_ref, g_ref, w_ref, u_ref, p_ref):
        hv = h_ref[...]
        u = (hv * _rstd(hv) * g_ref[...]).astype(_MM)
        u_ref[...] = u
        acc = _mm(u[:, :kw], w_ref[0])
        for j in range(1, nk):
            acc = acc + _mm(u[:, j * kw:(j + 1) * kw], w_ref[j])
        p_ref[...] = acc.astype(p_ref.dtype)

    return pl.pallas_call(
        body, name="proj_fwd", grid=(S // ts,),
        in_specs=[pl.BlockSpec((ts, D), lambda i: (i, 0)), pl.BlockSpec((1, D), lambda i: (0, 0)),
                  pl.BlockSpec((nk, kw, N), lambda i: (0, 0, 0))],
        out_specs=[pl.BlockSpec((ts, D), lambda i: (i, 0)), pl.BlockSpec((ts, N), lambda i: (i, 0))],
        out_shape=[jax.ShapeDtypeStruct((S, D), _MM), jax.ShapeDtypeStruct((S, N), out_dtype)],
        compiler_params=_cparams(("parallel",)),
    )(h, g, w)


def _proj_bwd(h, dho, g, w, pieces):
    S = h.shape[0]
    nk, kw, N = w.shape
    ts = _row_tile(S, 256)
    widths = [p.shape[1] for p in pieces]
    assert sum(widths) == N
    npc = len(pieces)

    def body(*refs):
        h_ref, dho_ref, g_ref, w_ref = refs[:4]
        p_refs = refs[4:4 + npc]
        dh_ref, dp_ref, dg_ref = refs[4 + npc:]

        @pl.when(pl.program_id(0) == 0)
        def _():
            dg_ref[...] = jnp.zeros_like(dg_ref)

        dus = [jnp.zeros((ts, kw), F32) for _ in range(nk)]
        off = 0
        for p_ref, n in zip(p_refs, widths):
            pv = p_ref[...].astype(_MM)
            dp_ref[:, off:off + n] = pv
            for j in range(nk):
                dus[j] = dus[j] + _mm_nt(pv, w_ref[j, :, off:off + n])
            off += n
        du = dus[0] if nk == 1 else jnp.concatenate(dus, axis=-1)
        dx, dg = _rms_bwd(h_ref[...], g_ref[...], du)
        dh_ref[...] = dho_ref[...] + dx
        dg_ref[...] += dg

    row = lambda i: (i, 0)
    return pl.pallas_call(
        body, name="proj_bwd", grid=(S // ts,),
        in_specs=[pl.BlockSpec((ts, D), row), pl.BlockSpec((ts, D), row), pl.BlockSpec((1, D), lambda i: (0, 0)),
                  pl.BlockSpec((nk, kw, N), lambda i: (0, 0, 0))] + [pl.BlockSpec((ts, n), row) for n in widths],
        out_specs=[pl.BlockSpec((ts, D), row), pl.BlockSpec((ts, N), row), pl.BlockSpec((1, D), lambda i: (0, 0))],
        out_shape=[jax.ShapeDtypeStruct((S, D), F32), jax.ShapeDtypeStruct((S, N), _MM),
                   jax.ShapeDtypeStruct((1, D), F32)],
        compiler_params=_cparams(("arbitrary",)),
    )(h, dho, g, w, *pieces)


def _pool_counts(first_row, ts):
    pos = (first_row + lax.broadcasted_iota(jnp.int32, (ts, 1), 0) + 1).astype(F32)
    return [jnp.minimum(pos, float(w)) for w in WINDOWS]


def _pool_delta(x, prev, cnts, ts):
    xe = jnp.concatenate([prev, x], axis=0)
    sums = []
    cur = xe
    for sh in (1, 2, 4, 8):
        cur = cur + pltpu.roll(cur, sh, 0)
        sums.append(cur)
    return [sums[gi][HALO:, gi * PG:(gi + 1) * PG] / cnts[gi] - x[:, gi * PG:(gi + 1) * PG]
            for gi in range(len(WINDOWS))]


def _pool_fwd(proj, pw, ps):
    S = proj.shape[0]
    ts = _row_tile(S, 256)

    def body(x_ref, pw_ref, ps_ref, tok_ref, carry_ref):
        i = pl.program_id(0)

        @pl.when(i == 0)
        def _():
            carry_ref[...] = jnp.zeros_like(carry_ref)

        x = x_ref[...]
        ds = _pool_delta(x, carry_ref[...], _pool_counts(i * ts, ts), ts)
        y = jnp.concatenate([_mm(ds[gi], pw_ref[gi]) for gi in range(len(WINDOWS))], axis=-1)
        tok_ref[...] = (y * ps_ref[...]).astype(tok_ref.dtype)
        carry_ref[...] = x[ts - HALO:, :]

    return pl.pallas_call(
        body, name="pool_fwd", grid=(S // ts,),
        in_specs=[pl.BlockSpec((ts, DTOK), lambda i: (i, 0)), pl.BlockSpec((len(WINDOWS), PG, PG), lambda i: (0, 0, 0)),
                  pl.BlockSpec((1, DTOK), lambda i: (0, 0))],
        out_specs=pl.BlockSpec((ts, DTOK), lambda i: (i, 0)),
        out_shape=jax.ShapeDtypeStruct((S, DTOK), _MM),
        scratch_shapes=[pltpu.VMEM((HALO, DTOK), F32)],
        compiler_params=_cparams(("arbitrary",)),
    )(proj, pw, ps)


def _pool_bwd(proj, dtok, pw, ps):
    S = proj.shape[0]
    ts = _row_tile(S, 256)
    nt = S // ts
    per = ts // HALO
    ng = len(WINDOWS)

    def body(x_ref, xp_ref, dt_ref, pw_ref, ps_ref, dx_ref, dpw_ref, dps_ref, carry_ref):
        i = pl.program_id(0)
        idx = nt - 1 - i

        @pl.when(i == 0)
        def _():
            carry_ref[...] = jnp.zeros_like(carry_ref)
            dpw_ref[...] = jnp.zeros_like(dpw_ref)
            dps_ref[...] = jnp.zeros_like(dps_ref)

        x = x_ref[...]
        prev = jnp.where(idx > 0, xp_ref[...], 0.0)
        cnts = _pool_counts(idx * ts, ts)
        ds = _pool_delta(x, prev, cnts, ts)
        dt = dt_ref[...]
        y = jnp.concatenate([_mm(ds[gi], pw_ref[gi]) for gi in range(ng)], axis=-1)
        dps_ref[...] += jnp.sum(dt * y, axis=0, keepdims=True)
        dy = (dt * ps_ref[...]).astype(_MM)
        dds = []
        for gi in range(ng):
            dyg = dy[:, gi * PG:(gi + 1) * PG]
            dpw_ref[gi] += _mm_tn(ds[gi], dyg)
            dds.append(_mm_nt(dyg, pw_ref[gi]))
        e = jnp.concatenate([dds[gi] / cnts[gi] for gi in range(ng)], axis=-1)
        ee = jnp.concatenate([e, carry_ref[...]], axis=0)
        rows = ts + HALO
        cur = ee
        outs = []
        for gi, sh in enumerate((1, 2, 4, 8)):
            cur = cur + pltpu.roll(cur, rows - sh, 0)
            outs.append(cur[:ts, gi * PG:(gi + 1) * PG] - dds[gi])
        dx_ref[...] = jnp.concatenate(outs, axis=-1)
        carry_ref[...] = e[:HALO, :]

    rev = lambda i: (nt - 1 - i, 0)
    return pl.pallas_call(
        body, name="pool_bwd", grid=(nt,),
        in_specs=[pl.BlockSpec((ts, DTOK), rev),
                  pl.BlockSpec((HALO, DTOK), lambda i: (jnp.maximum((nt - 1 - i) * per - 1, 0), 0)),
                  pl.BlockSpec((ts, DTOK), rev), pl.BlockSpec((ng, PG, PG), lambda i: (0, 0, 0)),
                  pl.BlockSpec((1, DTOK), lambda i: (0, 0))],
        out_specs=[pl.BlockSpec((ts, DTOK), rev), pl.BlockSpec((ng, PG, PG), lambda i: (0, 0, 0)),
                   pl.BlockSpec((1, DTOK), lambda i: (0, 0))],
        out_shape=[jax.ShapeDtypeStruct((S, DTOK), F32), jax.ShapeDtypeStruct((ng, PG, PG), F32),
                   jax.ShapeDtypeStruct((1, DTOK), F32)],
        scratch_shapes=[pltpu.VMEM((HALO, DTOK), F32)],
        compiler_params=_cparams(("arbitrary",)),
    )(proj, proj, dtok, pw, ps)


def _tri(strict):
    r = lax.broadcasted_iota(jnp.int32, (QB, QB), 0)
    c = lax.broadcasted_iota(jnp.int32, (QB, QB), 1)
    return jnp.where(r > c if strict else r >= c, 1.0, 0.0).astype(_MM)


def _sb_terms(z, valid):
    e = jnp.exp(-jnp.abs(z))
    sp = jnp.log(1.0 + e)
    ls = jnp.minimum(z, 0.0) - sp
    lf = jnp.where(valid, jnp.minimum(-z, 0.0) - sp, 0.0)
    return ls, lf, e


def _sb_alive(cs):
    top = cs[0]
    for c in cs[1:]:
        top = jnp.maximum(top, c)
    return (jnp.max(top) > NEG_CUT).astype(jnp.int32)


def _sb_split(x2, head_a):
    zero = jnp.zeros_like(x2)
    return jnp.where(head_a, x2, zero), jnp.where(head_a, zero, x2)


SB_GROUPS_FWD = 4
SB_GROUPS_BWD = 4


def _sb_fwd(proj):
    S = proj.shape[0]
    G = min(SB_GROUPS_FWD, S // QB)
    nstep = S // (G * QB)
    npair = DTOK // 128
    nch = 2 * G

    def body(q_ref, k_ref, v_ref, tri_ref, o_ref):
        qb = pl.program_id(1)
        head_a = lax.broadcasted_iota(jnp.int32, (QB, 128), 1) < HD
        qs = []
        for g in range(G):
            qs.extend(_sb_split(q_ref[g * QB:(g + 1) * QB, :], head_a))
        row = lax.broadcasted_iota(jnp.int32, (QB, QB), 0)
        col = lax.broadcasted_iota(jnp.int32, (QB, QB), 1)
        tri = tri_ref[...]
        last = G * qb + G - 1

        def step(carry):
            i, _, cs, accs = carry
            causal = (col - row) < i * QB
            ks, vs, valids = [], [], []
            for g in range(G):
                kb = G * qb + g - i
                off = pl.multiple_of(jnp.maximum(kb, 0) * QB, QB)
                ks.append(k_ref[pl.ds(off, QB), :])
                vs.append(v_ref[pl.ds(off, QB), :])
                valids.append(jnp.logical_and(causal, kb >= 0))
            zs = [_mm_nt(qs[ch], ks[ch // 2]) * SCALE for ch in range(nch)]
            terms = [_sb_terms(zs[ch], valids[ch // 2]) for ch in range(nch)]
            withins = [_mm2(terms[ch][1], tri) for ch in range(nch)]
            new_c, new_acc = [], []
            for ch in range(nch):
                ls, lf, _ = terms[ch]
                a = jnp.where(valids[ch // 2], jnp.exp(ls + withins[ch] + cs[ch]), 0.0)
                new_acc.append(accs[ch] + _mm2(a, vs[ch // 2]))
                new_c.append(cs[ch] + jnp.sum(lf, axis=-1, keepdims=True))
            return i + 1, _sb_alive(new_c), tuple(new_c), tuple(new_acc)

        zc = jnp.zeros((QB, 1), F32)
        za = jnp.zeros((QB, 128), F32)
        more = lambda cr: jnp.logical_and(cr[0] <= last, cr[1] > 0)
        accs = lax.while_loop(more, step, (0, 1, (zc,) * nch, (za,) * nch))[3]
        for g in range(G):
            o_ref[g * QB:(g + 1) * QB, :] = jnp.where(head_a, accs[2 * g], accs[2 * g + 1])

    return pl.pallas_call(
        body, name="sb_fwd", grid=(npair, nstep),
        in_specs=[pl.BlockSpec((G * QB, 128), lambda p, i: (i, p)),
                  pl.BlockSpec((S, 128), lambda p, i: (0, npair + p)),
                  pl.BlockSpec((S, 128), lambda p, i: (0, 2 * npair + p)),
                  pl.BlockSpec((QB, QB), lambda p, i: (0, 0))],
        out_specs=pl.BlockSpec((G * QB, 128), lambda p, i: (i, p)),
        out_shape=jax.ShapeDtypeStruct((S, DTOK), F32),
        compiler_params=_cparams(("parallel", "parallel")),
    )(proj, proj, proj, _tri(True))


def _sb_bwd(proj, o, do):
    S = proj.shape[0]
    G = min(SB_GROUPS_BWD, S // QB)
    nstep = S // (G * QB)
    npair = DTOK // 128
    nch = 2 * G

    def body(q_ref, k_ref, v_ref, o_ref, do_ref, tri_ref, tri2_ref, dq_ref, dk_ref, dv_ref):
        qb = pl.program_id(1)

        @pl.when(qb == 0)
        def _():
            dk_ref[...] = jnp.zeros_like(dk_ref)
            dv_ref[...] = jnp.zeros_like(dv_ref)

        head_a = lax.broadcasted_iota(jnp.int32, (QB, 128), 1) < HD
        qs, dos, gs = [], [], []
        for g in range(G):
            rows = slice(g * QB, (g + 1) * QB)
            qs.extend(_sb_split(q_ref[rows, :], head_a))
            dob = do_ref[rows, :].astype(_MM)
            dos.extend(_sb_split(dob, head_a))
            go = dob.astype(F32) * o_ref[rows, :]
            gs.append(jnp.sum(jnp.where(head_a, go, 0.0), axis=-1, keepdims=True))
            gs.append(jnp.sum(jnp.where(head_a, 0.0, go), axis=-1, keepdims=True))
        row = lax.broadcasted_iota(jnp.int32, (QB, QB), 0)
        col = lax.broadcasted_iota(jnp.int32, (QB, QB), 1)
        tri = tri_ref[...]
        tri2 = tri2_ref[...]
        last = G * qb + G - 1

        def step(carry):
            i, _, cs, rs, dqs = carry
            causal = (col - row) < i * QB
            offs, ks, vs, valids = [], [], [], []
            for g in range(G):
                kb = G * qb + g - i
                off = pl.multiple_of(jnp.maximum(kb, 0) * QB, QB)
                offs.append(off)
                ks.append(k_ref[pl.ds(off, QB), :])
                vs.append(v_ref[pl.ds(off, QB), :])
                valids.append(jnp.logical_and(causal, kb >= 0))
            zs = [_mm_nt(qs[ch], ks[ch // 2]) * SCALE for ch in range(nch)]
            das = [_mm_nt(dos[ch], vs[ch // 2]) for ch in range(nch)]
            terms = [_sb_terms(zs[ch], valids[ch // 2]) for ch in range(nch)]
            withins = [_mm2(terms[ch][1], tri) for ch in range(nch)]
            a_s, dlws = [], []
            for ch in range(nch):
                a = jnp.where(valids[ch // 2], jnp.exp(terms[ch][0] + withins[ch] + cs[ch]), 0.0)
                a_s.append(a)
                dlws.append(das[ch] * a)
            sfx = [_mm2(dlws[ch], tri2) for ch in range(nch)]
            dzs = []
            for ch in range(nch):
                e = terms[ch][2]
                inv = 1.0 / (1.0 + e)
                pos = zs[ch] >= 0.0
                beta = jnp.where(pos, 1.0, e) * inv
                omb = jnp.where(pos, e, 1.0) * inv
                prefix = gs[ch] - rs[ch] - sfx[ch]
                dzs.append((jnp.where(valids[ch // 2], dlws[ch] * omb - beta * prefix, 0.0) * SCALE).astype(_MM))
            new_dq = [dqs[ch] + _mm(dzs[ch], ks[ch // 2]) for ch in range(nch)]
            for g in range(G):
                dk_ref[pl.ds(offs[g], QB), :] += (_mm_tn(dzs[2 * g], qs[2 * g]) + _mm_tn(dzs[2 * g + 1], qs[2 * g + 1]))
                dv_ref[pl.ds(offs[g], QB), :] += (_mm_tn(a_s[2 * g], dos[2 * g]) + _mm_tn(a_s[2 * g + 1], dos[2 * g + 1]))
            new_c = [cs[ch] + jnp.sum(terms[ch][1], axis=-1, keepdims=True) for ch in range(nch)]
            new_r = [rs[ch] + jnp.sum(dlws[ch], axis=-1, keepdims=True) for ch in range(nch)]
            return i + 1, _sb_alive(new_c), tuple(new_c), tuple(new_r), tuple(new_dq)

        zc = jnp.zeros((QB, 1), F32)
        za = jnp.zeros((QB, 128), F32)
        more = lambda cr: jnp.logical_and(cr[0] <= last, cr[1] > 0)
        dqs = lax.while_loop(more, step, (0, 1, (zc,) * nch, (zc,) * nch, (za,) * nch))[4]
        for g in range(G):
            dq_ref[g * QB:(g + 1) * QB, :] = jnp.where(head_a, dqs[2 * g], dqs[2 * g + 1])

    qspec = pl.BlockSpec((G * QB, 128), lambda p, i: (i, p))
    full = lambda base: pl.BlockSpec((S, 128), lambda p, i: (0, base + p))
    tspec = pl.BlockSpec((QB, QB), lambda p, i: (0, 0))
    return pl.pallas_call(
        body, name="sb_bwd", grid=(npair, nstep),
        in_specs=[qspec, full(npair), full(2 * npair), qspec, qspec, tspec, tspec],
        out_specs=[qspec, full(0), full(0)],
        out_shape=[jax.ShapeDtypeStruct((S, DTOK), F32)] * 3,
        compiler_params=_cparams(("parallel", "arbitrary")),
    )(proj, proj, proj, o, do, _tri(True), _tri(False))


def _memkv_fwd(mem, gm, wkv):
    nk, kw, _ = wkv.shape

    def body(m_ref, g_ref, w_ref, kv_ref):
        m = m_ref[...]
        mn = (m * _rstd(m) * g_ref[...]).astype(_MM)
        acc = _mm(mn[:, :kw], w_ref[0])
        for j in range(1, nk):
            acc = acc + _mm(mn[:, j * kw:(j + 1) * kw], w_ref[j])
        kv_ref[...] = acc.astype(kv_ref.dtype)

    return pl.pallas_call(
        body, name="memkv_fwd", out_shape=jax.ShapeDtypeStruct((MEM_LEN, 2 * DMEM), _MM),
        compiler_params=pltpu.CompilerParams(vmem_limit_bytes=VMEM_LIMIT),
    )(mem, gm, wkv)


def _memkv_bwd(mem, gm, wkv, dkv):
    nk, kw, _ = wkv.shape

    def body(m_ref, g_ref, w_ref, dkv_ref, dw_ref, dg_ref):
        m = m_ref[...]
        mh = m * _rstd(m)
        mn = (mh * g_ref[...]).astype(_MM)
        dkv = dkv_ref[...].astype(_MM)
        dw_ref[...] = _mm_tn(mn, dkv)
        dmn = jnp.concatenate([_mm_nt(dkv, w_ref[j]) for j in range(nk)], axis=-1)
        dg_ref[...] = jnp.sum(dmn * mh, axis=0, keepdims=True)

    return pl.pallas_call(
        body, name="memkv_bwd",
        out_shape=[jax.ShapeDtypeStruct((D, 2 * DMEM), F32), jax.ShapeDtypeStruct((1, D), F32)],
        compiler_params=pltpu.CompilerParams(vmem_limit_bytes=VMEM_LIMIT),
    )(mem, gm, wkv, dkv)


def _mixout_fwd(h, tok, proj, kv, wo, g, qm_block):
    S = h.shape[0]
    ts = _row_tile(S, 256)

    def body(h_ref, tok_ref, qm_ref, kv_ref, wo_ref, g_ref, ho_ref, cat_ref, mix_ref):
        head_a = lax.broadcasted_iota(jnp.int32, (ts, 128), 1) < HD
        qm = qm_ref[...].astype(_MM)
        zq = jnp.zeros((ts, 128), _MM)
        npair = DMEM // 128
        qhs = []
        for pi in range(npair):
            qp = qm[:, pi * 128:(pi + 1) * 128]
            qhs += [jnp.where(head_a, qp, zq), jnp.where(head_a, zq, qp)]
        scores = [_mm_nt(qhs[h], kv_ref[:, (h // 2) * 128:(h // 2 + 1) * 128]) * SCALE for h in range(2 * npair)]
        ps = []
        for s in scores:
            e = jnp.exp(s - jnp.max(s, axis=-1, keepdims=True))
            ps.append(e / jnp.sum(e, axis=-1, keepdims=True))
        pvs = [_mm(ps[h], kv_ref[:, DMEM + (h // 2) * 128:DMEM + (h // 2 + 1) * 128]) for h in range(2 * npair)]
        mos = [jnp.where(head_a, pvs[2 * pi], pvs[2 * pi + 1]) for pi in range(npair)]
        cat = jnp.concatenate([tok_ref[...].astype(_MM)] + [m.astype(_MM) for m in mos], axis=-1)
        cat_ref[...] = cat
        mix = jnp.concatenate([_mm(cat, wo_ref[j]) for j in range(N_CHIPS)], axis=-1)
        mix_ref[...] = mix
        ho_ref[...] = h_ref[...] + mix * _rstd(mix) * g_ref[...]

    row = lambda i: (i, 0)
    return pl.pallas_call(
        body, name="mixout_fwd", grid=(S // ts,),
        in_specs=[pl.BlockSpec((ts, D), row), pl.BlockSpec((ts, DTOK), row),
                  pl.BlockSpec((ts, DMEM), lambda i: (i, qm_block)),
                  pl.BlockSpec((MEM_LEN, 2 * DMEM), lambda i: (0, 0)),
                  pl.BlockSpec((N_CHIPS, DMIX, D // N_CHIPS), lambda i: (0, 0, 0)), pl.BlockSpec((1, D), lambda i: (0, 0))],
        out_specs=[pl.BlockSpec((ts, D), row), pl.BlockSpec((ts, DMIX), row), pl.BlockSpec((ts, D), row)],
        out_shape=[jax.ShapeDtypeStruct((S, D), F32), jax.ShapeDtypeStruct((S, DMIX), _MM),
                   jax.ShapeDtypeStruct((S, D), F32)],
        compiler_params=_cparams(("parallel",)),
    )(h, tok, proj, kv, wo, g)


def _mixout_bwd(dho, mix, proj, kv, wo, g, qm_block):
    S = dho.shape[0]
    ts = _row_tile(S, 256)

    def body(dho_ref, mix_ref, qm_ref, kv_ref, wo_ref, g_ref, dmix_ref, dtok_ref, dqm_ref, dkv_ref, dg_ref):
        @pl.when(pl.program_id(0) == 0)
        def _():
            dkv_ref[...] = jnp.zeros_like(dkv_ref)
            dg_ref[...] = jnp.zeros_like(dg_ref)

        dmix, dg = _rms_bwd(mix_ref[...], g_ref[...], dho_ref[...])
        dg_ref[...] += dg
        dmb = dmix.astype(_MM)
        dmix_ref[...] = dmb
        cw = D // N_CHIPS
        dcat = _mm_nt(dmb[:, :cw], wo_ref[0])
        for j in range(1, N_CHIPS):
            dcat = dcat + _mm_nt(dmb[:, j * cw:(j + 1) * cw], wo_ref[j])
        dtok_ref[...] = dcat[:, :DTOK]
        head_a = lax.broadcasted_iota(jnp.int32, (ts, 128), 1) < HD
        qm = qm_ref[...].astype(_MM)
        zq = jnp.zeros((ts, 128), _MM)
        npair = DMEM // 128
        kps = [kv_ref[:, pi * 128:(pi + 1) * 128] for pi in range(npair)]
        vps = [kv_ref[:, DMEM + pi * 128:DMEM + (pi + 1) * 128] for pi in range(npair)]
        qhs, dmhs = [], []
        for pi in range(npair):
            cs = slice(pi * 128, (pi + 1) * 128)
            dmo = dcat[:, DTOK + pi * 128:DTOK + (pi + 1) * 128].astype(_MM)
            qhs += [jnp.where(head_a, qm[:, cs], zq), jnp.where(head_a, zq, qm[:, cs])]
            dmhs += [jnp.where(head_a, dmo, zq), jnp.where(head_a, zq, dmo)]
        nh = 2 * npair
        scores = [_mm_nt(qhs[h], kps[h // 2]) * SCALE for h in range(nh)]
        dps = [_mm_nt(dmhs[h], vps[h // 2]) for h in range(nh)]
        ps, dss = [], []
        for h in range(nh):
            e = jnp.exp(scores[h] - jnp.max(scores[h], axis=-1, keepdims=True))
            p = e / jnp.sum(e, axis=-1, keepdims=True)
            ps.append(p)
            dss.append((p * (dps[h] - jnp.sum(dps[h] * p, axis=-1, keepdims=True)) * SCALE).astype(_MM))
        dqs = [_mm(dss[h], kps[h // 2]) for h in range(nh)]
        for pi in range(npair):
            cs = slice(pi * 128, (pi + 1) * 128)
            a, b = 2 * pi, 2 * pi + 1
            dqm_ref[:, cs] = jnp.where(head_a, dqs[a], dqs[b]).astype(_MM)
            dkv_ref[:, cs] += _mm_tn(dss[a], qhs[a]) + _mm_tn(dss[b], qhs[b])
            dkv_ref[:, DMEM + pi * 128:DMEM + (pi + 1) * 128] += _mm_tn(ps[a], dmhs[a]) + _mm_tn(ps[b], dmhs[b])

    row = lambda i: (i, 0)
    fix = lambda i: (0, 0)
    return pl.pallas_call(
        body, name="mixout_bwd", grid=(S // ts,),
        in_specs=[pl.BlockSpec((ts, D), row), pl.BlockSpec((ts, D), row),
                  pl.BlockSpec((ts, DMEM), lambda i: (i, qm_block)),
                  pl.BlockSpec((MEM_LEN, 2 * DMEM), fix),
                  pl.BlockSpec((N_CHIPS, DMIX, D // N_CHIPS), lambda i: (0, 0, 0)), pl.BlockSpec((1, D), fix)],
        out_specs=[pl.BlockSpec((ts, D), row), pl.BlockSpec((ts, DTOK), row), pl.BlockSpec((ts, DMEM), row),
                   pl.BlockSpec((MEM_LEN, 2 * DMEM), fix), pl.BlockSpec((1, D), fix)],
        out_shape=[jax.ShapeDtypeStruct((S, D), _MM), jax.ShapeDtypeStruct((S, DTOK), F32),
                   jax.ShapeDtypeStruct((S, DMEM), _MM), jax.ShapeDtypeStruct((MEM_LEN, 2 * DMEM), F32),
                   jax.ShapeDtypeStruct((1, D), F32)],
        compiler_params=_cparams(("arbitrary",)),
    )(dho, mix, proj, kv, wo, g)


def _loss_grad(y, target):
    S = y.shape[0]
    ts = _row_tile(S, 512)

    def body(y_ref, t_ref, acc_ref, dy_ref):
        @pl.when(pl.program_id(0) == 0)
        def _():
            acc_ref[...] = jnp.zeros_like(acc_ref)

        err = y_ref[...] - t_ref[...]
        acc_ref[...] += jnp.sum(err * err, axis=0, keepdims=True)
        dy_ref[...] = err * (1.0 / D)

    row = lambda i: (i, 0)
    return pl.pallas_call(
        body, name="loss_grad", grid=(S // ts,),
        in_specs=[pl.BlockSpec((ts, D), row), pl.BlockSpec((ts, D), row)],
        out_specs=[pl.BlockSpec((1, D), lambda i: (0, 0)), pl.BlockSpec((ts, D), row)],
        out_shape=[jax.ShapeDtypeStruct((1, D), F32), jax.ShapeDtypeStruct((S, D), F32)],
        compiler_params=_cparams(("arbitrary",)),
    )(y, target)


LAYER_TENSORS = ("ffn1_gate", "ffn1_up", "ffn1_down", "ffn2_gate", "ffn2_up", "ffn2_down", "w_in", "w_mem_kv", "w_out")
SPLITS = {"ffn1_gate": "c", "ffn1_up": "c", "ffn1_down": "r", "ffn2_gate": "c", "ffn2_up": "c", "ffn2_down": "r",
          "w_in_pool": "r", "w_in_sb": "c", "w_mem_kv": "r", "w_out": "c"}


def _is_pool(i):
    return i % 2 == 0


def _qm_block(i):
    return (DTOK if _is_pool(i) else 3 * DTOK) // DMEM


FIRST_FFN = LAYER_TENSORS[:3]
REST = LAYER_TENSORS[3:]


def _layer_forward(h, mem, lw, i, rest_shards, next_shards):
    g_at = lambda g, k: g[k].reshape(1, D)
    sv = {"h0": h}
    (h, sv["a1"], sv["b1"], sv["f1"]), rest = _ffn_fwd(
        h, g_at(lw["g_pre"], 0), g_at(lw["g_post"], 0), lw["ffn1_gate"], lw["ffn1_up"], lw["ffn1_down"],
        _GatherExchange(rest_shards))
    lw.update(zip(REST, rest))
    if not _is_pool(i):
        lw["w_in"] = jnp.transpose(lw["w_in"], (1, 0, 2)).reshape(1, D, DSB)
    sv["h1"] = h
    sv["kv"] = _memkv_fwd(mem, lw["g_mem"], lw["w_mem_kv"])
    if _is_pool(i):
        sv["u"], sv["proj"] = _proj_fwd(h, g_at(lw["g_pre"], 1), lw["w_in"], F32)
        tok = _pool_fwd(sv["proj"], lw["pool_w"], lw["pool_scale"])
    else:
        sv["u"], sv["proj"] = _proj_fwd(h, g_at(lw["g_pre"], 1), lw["w_in"], _MM)
        tok = sv["o"] = _sb_fwd(sv["proj"])
    h, sv["cat"], sv["mix"] = _mixout_fwd(h, tok, sv["proj"], sv["kv"], lw["w_out"], g_at(lw["g_post"], 1),
                                          _qm_block(i))
    sv["h2"] = h
    ffn2 = (h, g_at(lw["g_pre"], 2), g_at(lw["g_post"], 2), lw["ffn2_gate"], lw["ffn2_up"], lw["ffn2_down"])
    if next_shards is None:
        h, sv["a2"], sv["b2"], sv["f2"] = _ffn_fwd(*ffn2)
        return h, sv, None
    (h, sv["a2"], sv["b2"], sv["f2"]), nxt = _ffn_fwd(*ffn2, _GatherExchange(next_shards))
    return h, sv, nxt


def _layer_backward(dh, mem, lw, sv, i, rides=None):
    g_at = lambda g, k: g[k].reshape(1, D)
    big, small = {}, {}
    dgpre, dgpost = [None] * 3, [None] * 3
    rode = {}
    rides = rides or {}

    def hosted(host, fn, *args):
        ride = rides[host](big, rode) if host in rides else None
        if ride is None:
            return fn(*args)
        outs, rode[host] = fn(*args, ride)
        return outs

    def ffn_back(which, dh, h_in):
        k = 0 if which == 1 else 2
        n = f"ffn{which}"
        dh, nb, sb, da, db, df, dgpre[k], dgpost[k] = hosted(
            n, _ffn_bwd, h_in, sv[f"f{which}"], sv[f"a{which}"], sv[f"b{which}"], dh, g_at(lw["g_pre"], k),
            g_at(lw["g_post"], k), lw[n + "_gate"], lw[n + "_up"], lw[n + "_down"])
        big[n + "_gate"] = hosted(n + "_gate", _wgrad, nb, da, N_CHIPS)
        big[n + "_up"] = hosted(n + "_up", _wgrad, nb, db, N_CHIPS)
        big[n + "_down"] = hosted(n + "_down", _wgrad, sb, df, 0).reshape(N_CHIPS, FF // N_CHIPS, D)
        return dh

    dh = ffn_back(2, dh, sv["h2"])
    dmix, dtok, dqm, dkv, dgpost[1] = _mixout_bwd(dh, sv["mix"], sv["proj"], sv["kv"], lw["w_out"],
                                                  g_at(lw["g_post"], 1), _qm_block(i))
    big["w_out"] = _wgrad(sv["cat"], dmix, N_CHIPS)
    dwkv, small["g_mem"] = _memkv_bwd(mem, lw["g_mem"], lw["w_mem_kv"], dkv)
    big["w_mem_kv"] = dwkv.reshape(N_CHIPS, D // N_CHIPS, 2 * DMEM)
    if _is_pool(i):
        dx, small["pool_w"], small["pool_scale"] = _pool_bwd(sv["proj"], dtok, lw["pool_w"], lw["pool_scale"])
        dh, dproj, dgpre[1] = _proj_bwd(sv["h1"], dh, g_at(lw["g_pre"], 1), lw["w_in"], [dx, dqm])
        big["w_in"] = _wgrad(sv["u"], dproj).reshape(N_CHIPS, D // N_CHIPS, DMIX)
    else:
        dq, dk, dv = _sb_bwd(sv["proj"], sv["o"], dtok)
        dh, dproj, dgpre[1] = _proj_bwd(sv["h1"], dh, g_at(lw["g_pre"], 1), lw["w_in"], [dq, dk, dv, dqm])
        big["w_in"] = jnp.transpose(_wgrad(sv["u"], dproj).reshape(D, N_CHIPS, DSB // N_CHIPS), (1, 0, 2))
    dh = ffn_back(1, dh, sv["h0"])
    small["g_pre"] = jnp.concatenate(dgpre, axis=0)
    small["g_post"] = jnp.concatenate(dgpost, axis=0)
    return dh, big, small, rode


ANY = pl.BlockSpec(memory_space=pl.ANY)


def _place():
    return lax.axis_index("x"), lax.axis_index("y"), lax.axis_index("c")


def _flip(v, bit):
    return 1 - v if bit else v


def _small_allgather(buf):
    R = buf.shape[0]

    def body(x_ref, o_ref, send_sems, recv_sems, loc_sem):
        x, y, c = _place()
        me = 4 * x + 2 * y + c
        loc = pltpu.make_async_copy(x_ref, o_ref.at[me], loc_sem)
        loc.start()
        sends = []
        for k in range(1, N_DEV):
            peer = (_flip(x, k & 4), _flip(y, k & 2), _flip(c, k & 1))
            cp = pltpu.make_async_remote_copy(src_ref=x_ref, dst_ref=o_ref.at[me], send_sem=send_sems.at[k - 1],
                                              recv_sem=recv_sems.at[k - 1], device_id=peer, device_id_type=MESH)
            cp.start()
            sends.append(cp)
        for k in range(1, N_DEV):
            peer = (_flip(x, k & 4), _flip(y, k & 2), _flip(c, k & 1))
            slot = 4 * peer[0] + 2 * peer[1] + peer[2]
            pltpu.make_async_remote_copy(src_ref=x_ref, dst_ref=o_ref.at[slot], send_sem=send_sems.at[k - 1],
                                         recv_sem=recv_sems.at[k - 1], device_id=peer, device_id_type=MESH).wait_recv()
        for cp in sends:
            cp.wait_send()
        loc.wait()

    return pl.pallas_call(
        body, name="small_allgather", in_specs=[ANY], out_specs=ANY,
        out_shape=jax.ShapeDtypeStruct((N_DEV, R, 128), buf.dtype),
        scratch_shapes=[pltpu.SemaphoreType.DMA((N_DEV - 1,)), pltpu.SemaphoreType.DMA((N_DEV - 1,)),
                        pltpu.SemaphoreType.DMA],
    )(buf)


def _dma_sems(m):
    return pltpu.SemaphoreType.DMA((m,))


class _GatherExchange:
    name = "gather_weights"

    def __init__(self, shards):
        n = len(shards)
        self.n, self.inputs = n, list(shards)
        self.halves = [s.shape[0] // 2 for s in shards]
        self.out_shapes = [jax.ShapeDtypeStruct((N_CHIPS,) + s.shape, s.dtype) for s in shards]
        self.sems = [_dma_sems(3 * n), _dma_sems(3 * n), _dma_sems(3 * n), _dma_sems(3 * n), _dma_sems(2 * n)]

    def _copies(self, ins, outs, sems):
        ici_send, ici_recv, d2d_send, d2d_recv, loc_sems = sems
        x, y, c = _place()
        me = 2 * x + y
        sib = (x, y, 1 - c)
        chips = [(_flip(x, k & 2), _flip(y, k & 1)) for k in (1, 2, 3)]
        half = lambda t, slot, hc: outs[t].at[slot, pl.ds(hc * self.halves[t], self.halves[t])]
        locs, ici_out, ici_in, fwd_out, fwd_in = [], [], [], [], []
        for t in range(self.n):
            for hc in range(2):
                rows = pl.ds(hc * self.halves[t], self.halves[t])
                locs.append(pltpu.make_async_copy(ins[t].at[rows], outs[t].at[me, rows], loc_sems.at[2 * t + hc]))
            for k, chip in enumerate(chips):
                idx = 2 * chip[0] + chip[1]
                peer = (chip[0], chip[1], c)
                ici = dict(send_sem=ici_send.at[3 * t + k], recv_sem=ici_recv.at[3 * t + k], device_id=peer,
                           device_id_type=MESH)
                d2d = dict(send_sem=d2d_send.at[3 * t + k], recv_sem=d2d_recv.at[3 * t + k], device_id=sib,
                           device_id_type=MESH)
                ici_out.append(pltpu.make_async_remote_copy(
                    src_ref=ins[t].at[pl.ds(c * self.halves[t], self.halves[t])], dst_ref=half(t, me, c), **ici))
                ici_in.append(pltpu.make_async_remote_copy(src_ref=half(t, idx, c), dst_ref=half(t, idx, c), **ici))
                fwd_out.append(pltpu.make_async_remote_copy(src_ref=half(t, idx, c), dst_ref=half(t, idx, c), **d2d))
                fwd_in.append(pltpu.make_async_remote_copy(
                    src_ref=half(t, idx, 1 - c), dst_ref=half(t, idx, 1 - c), **d2d))
        return locs, ici_out, ici_in, fwd_out, fwd_in

    def start(self, ins, outs, sems):
        locs, ici_out, _, _, _ = self._copies(ins, outs, sems)
        for cp in ici_out + locs:
            cp.start()

    def mid(self, ins, outs, sems):
        _, _, ici_in, fwd_out, _ = self._copies(ins, outs, sems)
        for arrived, onward in zip(ici_in, fwd_out):
            arrived.wait_recv()
            onward.start()

    def finish(self, ins, outs, sems):
        locs, ici_out, _, fwd_out, fwd_in = self._copies(ins, outs, sems)
        for cp in fwd_in:
            cp.wait_recv()
        for cp in ici_out + fwd_out:
            cp.wait_send()
        for cp in locs:
            cp.wait()


class _SiblingExchange:
    name = "rs_to_sibling"

    def __init__(self, gs):
        n = len(gs)
        self.n, self.inputs = n, list(gs)
        self.halves = [g.shape[1] // 2 for g in gs]
        self.out_shapes = [jax.ShapeDtypeStruct((N_CHIPS, g.shape[1] // 2) + g.shape[2:], g.dtype) for g in gs]
        self.sems = [_dma_sems(n), _dma_sems(n)]

    def _copies(self, ins, outs, sems):
        x, y, c = _place()
        return [pltpu.make_async_remote_copy(
            src_ref=ins[t].at[:, pl.ds((1 - c) * self.halves[t], self.halves[t])], dst_ref=outs[t],
            send_sem=sems[0].at[t], recv_sem=sems[1].at[t], device_id=(x, y, 1 - c), device_id_type=MESH)
            for t in range(self.n)]

    def start(self, ins, outs, sems):
        for cp in self._copies(ins, outs, sems):
            cp.start()

    def mid(self, ins, outs, sems):
        pass

    def finish(self, ins, outs, sems):
        for cp in self._copies(ins, outs, sems):
            cp.wait()


class _ChipsExchange:
    name = "rs_to_chips"

    def __init__(self, ss):
        n = len(ss)
        self.n, self.inputs = n, list(ss)
        self.out_shapes = [jax.ShapeDtypeStruct((3,) + s.shape[1:], s.dtype) for s in ss]
        self.sems = [_dma_sems(3 * n), _dma_sems(3 * n)]

    def _copies(self, ins, outs, sems):
        x, y, c = _place()
        chips = [(_flip(x, k & 2), _flip(y, k & 1)) for k in (1, 2, 3)]
        return [pltpu.make_async_remote_copy(
            src_ref=ins[t].at[2 * chip[0] + chip[1]], dst_ref=outs[t].at[k], send_sem=sems[0].at[3 * t + k],
            recv_sem=sems[1].at[3 * t + k], device_id=(chip[0], chip[1], c), device_id_type=MESH)
            for t in range(self.n) for k, chip in enumerate(chips)]

    def start(self, ins, outs, sems):
        for cp in self._copies(ins, outs, sems):
            cp.start()

    def mid(self, ins, outs, sems):
        pass

    def finish(self, ins, outs, sems):
        for cp in self._copies(ins, outs, sems):
            cp.wait()


class _Both:
    def __init__(self, first, second):
        self.parts = (first, second)
        self.name = first.name + "_" + second.name
        self.inputs = first.inputs + second.inputs
        self.out_shapes = first.out_shapes + second.out_shapes
        self.sems = first.sems + second.sems

    def _each(self, ins, outs, sems):
        a = self.parts[0]
        ni, no, ns = len(a.inputs), len(a.out_shapes), len(a.sems)
        return ((a, ins[:ni], outs[:no], sems[:ns]), (self.parts[1], ins[ni:], outs[no:], sems[ns:]))

    def start(self, ins, outs, sems):
        for ex, i, o, s in self._each(ins, outs, sems):
            ex.start(i, o, s)

    def mid(self, ins, outs, sems):
        for ex, i, o, s in self._each(ins, outs, sems):
            ex.mid(i, o, s)

    def finish(self, ins, outs, sems):
        for ex, i, o, s in self._each(ins, outs, sems):
            ex.finish(i, o, s)

    def split(self, outs):
        no = len(self.parts[0].out_shapes)
        return outs[:no], outs[no:]


def _run_exchange(ex):
    ni, no = len(ex.inputs), len(ex.out_shapes)

    def body(*refs):
        ins, outs, sems = refs[:ni], refs[ni:ni + no], refs[ni + no:]
        ex.start(ins, outs, sems)
        ex.mid(ins, outs, sems)
        ex.finish(ins, outs, sems)

    return pl.pallas_call(
        body, name=ex.name, in_specs=[ANY] * ni, out_specs=[ANY] * no, out_shape=ex.out_shapes,
        scratch_shapes=ex.sems,
    )(*ex.inputs)


def _call(body, name, nsteps, in_specs, out_specs, out_shape, args, ride=None):
    if ride is None:
        return pl.pallas_call(body, name=name, grid=(nsteps,), in_specs=in_specs, out_specs=out_specs,
                              out_shape=out_shape, compiler_params=_cparams(("arbitrary",)))(*args)
    ni, no = len(args), len(out_shape)
    ri, ro = len(ride.inputs), len(ride.out_shapes)
    mid_step = (3 * nsteps) // 4 if nsteps > 1 else 0

    def hosted(*refs):
        a, ra = refs[:ni], refs[ni:ni + ri]
        o, ro_refs = refs[ni + ri:ni + ri + no], refs[ni + ri + no:ni + ri + no + ro]
        sems = refs[ni + ri + no + ro:]
        step = pl.program_id(0)

        @pl.when(step == 0)
        def _():
            ride.start(ra, ro_refs, sems)

        body(*a, *o)

        @pl.when(step == mid_step)
        def _():
            ride.mid(ra, ro_refs, sems)

        @pl.when(step == nsteps - 1)
        def _():
            ride.finish(ra, ro_refs, sems)

    outs = pl.pallas_call(
        hosted, name=name + "_" + ride.name, grid=(nsteps,), in_specs=list(in_specs) + [ANY] * ri,
        out_specs=list(out_specs) + [ANY] * ro, out_shape=list(out_shape) + ride.out_shapes,
        scratch_shapes=ride.sems, compiler_params=_cparams(("arbitrary",)),
    )(*args, *ride.inputs)
    return outs[:no], outs[no:]


def _share_halves(ts):
    n = len(ts)

    def body(*refs):
        outs = refs[n:2 * n]
        send_sems, recv_sems = refs[2 * n:]
        x, y, c = _place()
        cps = []
        for t in range(n):
            cp = pltpu.make_async_remote_copy(
                src_ref=outs[t].at[c], dst_ref=outs[t].at[c], send_sem=send_sems.at[t], recv_sem=recv_sems.at[t],
                device_id=(x, y, 1 - c), device_id_type=MESH)
            cp.start()
            cps.append(cp)
        for t in range(n):
            pltpu.make_async_remote_copy(
                src_ref=outs[t].at[1 - c], dst_ref=outs[t].at[1 - c], send_sem=send_sems.at[t],
                recv_sem=recv_sems.at[t], device_id=(x, y, 1 - c), device_id_type=MESH).wait_recv()
        for cp in cps:
            cp.wait_send()

    dma = lambda m: pltpu.SemaphoreType.DMA((m,))
    return pl.pallas_call(
        body, name="share_halves", in_specs=[ANY] * n, out_specs=[ANY] * n,
        out_shape=[jax.ShapeDtypeStruct(t.shape, t.dtype) for t in ts],
        input_output_aliases={t: t for t in range(n)},
        scratch_shapes=[dma(n), dma(n)],
    )(*ts)


def _by_shape(fn, *lists):
    groups = {}
    for idx, a in enumerate(lists[0]):
        groups.setdefault(a.shape, []).append(idx)
    out = [None] * len(lists[0])
    for idxs in groups.values():
        for idx, res in zip(idxs, fn(*[[lst[idx] for idx in idxs] for lst in lists])):
            out[idx] = res
    return out


def _add_sibling(gs, rs, c):
    m = len(gs)
    _, r_, n_ = gs[0].shape
    h = r_ // 2
    tr = _row_tile(h, 512)

    def body(c_ref, *refs):
        for g_ref, r_ref, o_ref in zip(refs[:m], refs[m:2 * m], refs[2 * m:]):
            o_ref[...] = (g_ref[...] + r_ref[...]).astype(o_ref.dtype)

    gspec = pl.BlockSpec((None, None, tr, n_), lambda s, i, c_ref: (s, c_ref[0], i, 0))
    hspec = pl.BlockSpec((None, tr, n_), lambda s, i, c_ref: (s, i, 0))
    return pl.pallas_call(
        body, name="add_sibling",
        grid_spec=pltpu.PrefetchScalarGridSpec(
            num_scalar_prefetch=1, grid=(N_CHIPS, h // tr), in_specs=[gspec] * m + [hspec] * m, out_specs=[hspec] * m),
        out_shape=[jax.ShapeDtypeStruct((N_CHIPS, h, n_), _MM)] * m,
        compiler_params=_cparams(("parallel", "parallel")),
    )(c.reshape(1), *[g.reshape(N_CHIPS, 2, h, n_) for g in gs], *rs)


def _add_chips(ss, rs, me, c):
    m = len(ss)
    _, h, n_ = ss[0].shape
    tr = _row_tile(h, 512)

    def body(at_ref, *refs):
        f = lambda ref: ref[...].astype(F32)
        for t in range(m):
            s_ref, (r0, r1, r2), o_ref = refs[t], refs[m + 3 * t:m + 3 * t + 3], refs[4 * m + t]
            o_ref[...] = ((f(s_ref) + f(r0)) + f(r1)) + f(r2)

    rk = lambda k: pl.BlockSpec((None, tr, n_), lambda i, at_ref: (k, i, 0))
    return pl.pallas_call(
        body, name="add_chips",
        grid_spec=pltpu.PrefetchScalarGridSpec(
            num_scalar_prefetch=1, grid=(h // tr,),
            in_specs=[pl.BlockSpec((None, tr, n_), lambda i, at_ref: (at_ref[0], i, 0))] * m + [rk(0), rk(1), rk(2)] * m,
            out_specs=[pl.BlockSpec((None, tr, n_), lambda i, at_ref: (at_ref[1], i, 0))] * m),
        out_shape=[jax.ShapeDtypeStruct((2, h, n_), F32)] * m,
        compiler_params=_cparams(("parallel",)),
    )(jnp.stack([me, c]), *ss, *[r for r in rs for _ in range(3)])


def _sum_devices(a):
    R_ = a.shape[1]

    def body(a_ref, o_ref):
        acc = a_ref[0]
        for k in range(1, N_DEV):
            acc = acc + a_ref[k]
        o_ref[...] = acc

    return pl.pallas_call(
        body, name="sum_devices", out_shape=jax.ShapeDtypeStruct((R_, 128), F32),
        compiler_params=pltpu.CompilerParams(vmem_limit_bytes=VMEM_LIMIT),
    )(a)


def _adamw(w, g, m, v):
    shape = w.shape
    to2 = lambda a: a.reshape(-1, shape[-1])
    w2, g2, m2, v2 = to2(w), to2(g), to2(m), to2(v)
    R_, C = w2.shape
    tr = 256 if R_ % 256 == 0 else R_

    def body(w_ref, g_ref, m_ref, v_ref, d_ref, mo_ref, vo_ref):
        gv = g_ref[...]
        mn = ADAM_B1 * m_ref[...] + (1.0 - ADAM_B1) * gv
        vn = ADAM_B2 * v_ref[...] + (1.0 - ADAM_B2) * (gv * gv)
        m_hat = mn / (1.0 - ADAM_B1 ** ADAM_STEP)
        v_hat = vn / (1.0 - ADAM_B2 ** ADAM_STEP)
        d_ref[...] = -ADAM_LR * (m_hat / (jnp.sqrt(v_hat) + ADAM_EPS) + ADAM_WD * w_ref[...])
        mo_ref[...] = mn
        vo_ref[...] = vn

    spec = pl.BlockSpec((tr, C), lambda i: (i, 0))
    outs = pl.pallas_call(
        body, name="adamw", grid=(R_ // tr,), in_specs=[spec] * 4, out_specs=[spec] * 3,
        out_shape=[jax.ShapeDtypeStruct((R_, C), F32)] * 3,
        compiler_params=_cparams(("parallel",)),
    )(w2, g2, m2, v2)
    return [o.reshape(shape) for o in outs]


SHARD_AXIS = {"g_pre": 2, "g_post": 2, "g_mem": None, "ffn1_gate": 2, "ffn1_up": 2, "ffn1_down": 1,
              "ffn2_gate": 2, "ffn2_up": 2, "ffn2_down": 1, "w_in_pool": 1, "pool_w": None, "pool_scale": None,
              "w_in_sb": 2, "w_mem_kv": 1, "w_out": 2}
WEIGHTS = list(SHARD_AXIS)
BIG = [k for k in WEIGHTS if SHARD_AXIS[k] is not None and k not in ("g_pre", "g_post")]
SMALL = [k for k in WEIGHTS if k not in BIG]


def _pack128(arrs):
    flat = jnp.concatenate([a.reshape(-1) for a in arrs])
    pad = (-flat.shape[0]) % (8 * 128)
    return jnp.pad(flat, (0, pad)).reshape(-1, 128)


def _unpack128(buf, shapes):
    flat = buf.reshape(-1)
    out, off = [], 0
    for shp in shapes:
        size = 1
        for d in shp:
            size *= d
        out.append(flat[off:off + size].reshape(shp))
        off += size
    return out


def kernel(x, mem, g_pre, g_post, g_mem, ffn1_gate, ffn1_up, ffn1_down, ffn2_gate, ffn2_up, ffn2_down, w_in_pool, pool_w, pool_scale, w_in_sb, w_mem_kv, w_out, loss_target, m_g_pre, m_g_post, m_g_mem, m_ffn1_gate, m_ffn1_up, m_ffn1_down, m_ffn2_gate, m_ffn2_up, m_ffn2_down, m_w_in_pool, m_pool_w, m_pool_scale, m_w_in_sb, m_w_mem_kv, m_w_out, v_g_pre, v_g_post, v_g_mem, v_ffn1_gate, v_ffn1_up, v_ffn1_down, v_ffn2_gate, v_ffn2_up, v_ffn2_down, v_w_in_pool, v_pool_w, v_pool_scale, v_w_in_sb, v_w_mem_kv, v_w_out):
    w = dict(g_pre=g_pre, g_post=g_post, g_mem=g_mem, ffn1_gate=ffn1_gate, ffn1_up=ffn1_up, ffn1_down=ffn1_down,
             ffn2_gate=ffn2_gate, ffn2_up=ffn2_up, ffn2_down=ffn2_down, w_in_pool=w_in_pool, pool_w=pool_w,
             pool_scale=pool_scale, w_in_sb=w_in_sb, w_mem_kv=w_mem_kv, w_out=w_out)
    m = dict(g_pre=m_g_pre, g_post=m_g_post, g_mem=m_g_mem, ffn1_gate=m_ffn1_gate, ffn1_up=m_ffn1_up,
             ffn1_down=m_ffn1_down, ffn2_gate=m_ffn2_gate, ffn2_up=m_ffn2_up, ffn2_down=m_ffn2_down,
             w_in_pool=m_w_in_pool, pool_w=m_pool_w, pool_scale=m_pool_scale, w_in_sb=m_w_in_sb,
             w_mem_kv=m_w_mem_kv, w_out=m_w_out)
    v = dict(g_pre=v_g_pre, g_post=v_g_post, g_mem=v_g_mem, ffn1_gate=v_ffn1_gate, ffn1_up=v_ffn1_up,
             ffn1_down=v_ffn1_down, ffn2_gate=v_ffn2_gate, ffn2_up=v_ffn2_up, ffn2_down=v_ffn2_down,
             w_in_pool=v_w_in_pool, pool_w=v_pool_w, pool_scale=v_pool_scale, w_in_sb=v_w_in_sb,
             w_mem_kv=v_w_mem_kv, w_out=v_w_out)
    cx, cy, cc = _place()
    chip = (2 * cx + cy).astype(jnp.int32)
    core = cc.astype(jnp.int32)

    gains = _small_allgather(_pack128([g_pre, g_post]))
    per_chip = [_unpack128(gains[2 * j], [g_pre.shape, g_post.shape]) for j in range(N_CHIPS)]
    full_pre = jnp.concatenate([p[0] for p in per_chip], axis=2)
    full_post = jnp.concatenate([p[1] for p in per_chip], axis=2)

    def shard_name(k, i):
        return ("w_in_pool" if _is_pool(i) else "w_in_sb") if k == "w_in" else k

    def own(i, names):
        return [w[shard_name(k, i)][i // 2 if k == "w_in" else i].astype(_MM) for k in names]

    xs, mems = x[0], mem[0]
    h, lws, saved = xs, [], []
    first = _run_exchange(_GatherExchange(own(0, FIRST_FFN)))
    for i in range(DEPTH):
        lw = dict(zip(FIRST_FFN, first))
        lw["g_pre"], lw["g_post"], lw["g_mem"] = full_pre[i], full_post[i], g_mem[i].reshape(1, D)
        if _is_pool(i):
            lw["pool_w"], lw["pool_scale"] = pool_w[i // 2], pool_scale[i // 2].reshape(1, DTOK)
        h, sv, first = _layer_forward(h, mems, lw, i, own(i, REST), own(i + 1, FIRST_FFN) if i + 1 < DEPTH else None)
        lws.append(lw)
        saved.append(sv)
    acc, dh = _loss_grad(h, loss_target[0])
    loss = lax.psum(0.5 / D * jnp.sum(acc), ("x", "y", "c"))

    mine, smalls = [None] * DEPTH, [None] * DEPTH
    above, sums = None, {}
    nt = len(LAYER_TENSORS)
    early = ("ffn2_gate", "w_mem_kv", "ffn2_up", "w_out", "ffn2_down", "w_in")
    add_sib = lambda gs, from_sib: _by_shape(lambda g, r: _add_sibling(g, r, core), gs, from_sib)
    add_chp = lambda ss, from_chips: _by_shape(lambda s, r: _add_chips(s, r, chip, core), ss, from_chips)

    for i in reversed(range(DEPTH)):
        rides = {}
        if above is not None:
            up = [above[k] for k in LAYER_TENSORS]
            rides["ffn2"] = lambda big, rode, up=up: _SiblingExchange(up)

            def on_ffn1(big, rode, up=up, layer=i + 1, bottom=(i == 0)):
                sums[layer] = add_sib(up, rode["ffn2"])
                ex = _ChipsExchange(sums[layer])
                return _Both(ex, _SiblingExchange([big[k] for k in early])) if bottom else ex

            def on_wgrad(big, rode, j):
                if "early" not in sums:
                    sums["early"] = add_sib([big[k] for k in early], rode["ffn1"][nt:])
                return _ChipsExchange(sums["early"][2 * j:2 * j + 2])

            rides["ffn1"] = on_ffn1
            if i == 0:
                for j, host in enumerate(FIRST_FFN):
                    rides[host] = lambda big, rode, j=j: on_wgrad(big, rode, j)
        dh, gs, smalls[i], rode = _layer_backward(dh, mems, lws[i], saved[i], i, rides)
        if above is not None:
            mine[i + 1] = add_chp(sums[i + 1], rode["ffn1"][:nt])
        above = gs
    early_mine = add_chp(sums["early"], [t for host in FIRST_FFN for t in rode[host]])
    late = [above[k] for k in FIRST_FFN]
    late_sums = add_sib(late, _run_exchange(_SiblingExchange(late)))
    late_mine = add_chp(late_sums, _run_exchange(_ChipsExchange(late_sums)))
    by_name = dict(zip(early + FIRST_FFN, early_mine + late_mine))
    mine[0] = [by_name[k] for k in LAYER_TENSORS]
    shared = _share_halves([t for layer in mine for t in layer])
    reduced = [dict(zip(LAYER_TENSORS, [s.reshape(-1, s.shape[-1]) for s in shared[nt * i:nt * (i + 1)]]))
               for i in range(DEPTH)]
    dx = dh
    red = {}
    for k in BIG:
        if k.startswith("w_in_"):
            layers = [i for i in range(DEPTH) if _is_pool(i) == (k == "w_in_pool")]
            red[k] = jnp.stack([reduced[i]["w_in"] for i in layers])
        else:
            red[k] = jnp.stack([reduced[i][k] for i in range(DEPTH)])

    grads = {"g_pre": jnp.stack([s["g_pre"] for s in smalls]), "g_post": jnp.stack([s["g_post"] for s in smalls]),
             "g_mem": jnp.concatenate([s["g_mem"] for s in smalls], axis=0),
             "pool_w": jnp.stack([smalls[i]["pool_w"] for i in range(DEPTH) if _is_pool(i)]),
             "pool_scale": jnp.concatenate([smalls[i]["pool_scale"] for i in range(DEPTH) if _is_pool(i)], axis=0)}
    small_shapes = [grads[k].shape for k in SMALL]
    summed = _unpack128(_sum_devices(_small_allgather(_pack128([grads[k] for k in SMALL]))), small_shapes)
    for k, s in zip(SMALL, summed):
        if SHARD_AXIS[k] is None:
            red[k] = s
        else:
            red[k] = lax.dynamic_slice_in_dim(s, chip * w[k].shape[2], w[k].shape[2], axis=2)

    deltas, new_m, new_v = {}, {}, {}
    for k in WEIGHTS:
        deltas[k], new_m[k], new_v[k] = _adamw(w[k], red[k], m[k], v[k])
    return (loss, dx.reshape(x.shape), *[red[k] for k in WEIGHTS], *[deltas[k] for k in WEIGHTS],
            *[new_m[k] for k in WEIGHTS], *[new_v[k] for k in WEIGHTS])
```

```python
import jax
import jax.numpy as jnp
from jax import lax
from jax.experimental import pallas as pl
from jax.experimental.pallas import tpu as pltpu

F32 = jnp.float32
BF16 = jnp.bfloat16
_MM = jnp.bfloat16

D = 1024
FF = 2048
DTOK = 512
DMEM = 256
DMIX = DTOK + DMEM
DSB = 3 * DTOK + DMEM
HD = 64
MEM_LEN = 256
DEPTH = 4
EPS = 1e-6
WINDOWS = (2, 4, 8, 16)
HALO = 16
PG = 128
QB = 128
SCALE = HD ** -0.5
FC = 512
NEG_CUT = -104.0

ADAM_LR, ADAM_B1, ADAM_B2, ADAM_EPS, ADAM_WD, ADAM_STEP = 0.001, 0.9, 0.999, 1e-08, 0.01, 10

VMEM_LIMIT = 56 * 2 ** 20
MESH = pl.DeviceIdType.MESH
N_CHIPS = 4
N_DEV = 8


def _cparams(sem):
    return pltpu.CompilerParams(dimension_semantics=sem, vmem_limit_bytes=VMEM_LIMIT)


def _mm(a, b):
    return jnp.dot(a.astype(_MM), b.astype(_MM), preferred_element_type=F32)


def _mm_nt(a, b):
    return lax.dot_general(a.astype(_MM), b.astype(_MM), (((1,), (1,)), ((), ())), preferred_element_type=F32)


def _mm_tn(a, b):
    return lax.dot_general(a.astype(_MM), b.astype(_MM), (((0,), (0,)), ((), ())), preferred_element_type=F32)


def _mm2(a, b):
    hi = a.astype(_MM)
    lo = (a - hi.astype(F32)).astype(_MM)
    return jnp.dot(hi, b, preferred_element_type=F32) + jnp.dot(lo, b, preferred_element_type=F32)


def _rstd(x):
    return lax.rsqrt(jnp.mean(x * x, axis=-1, keepdims=True) + EPS)


def _rms_bwd(x, g, dy):
    r = _rstd(x)
    xh = x * r
    t = dy * g
    dx = r * (t - xh * jnp.mean(t * xh, axis=-1, keepdims=True))
    return dx, jnp.sum(dy * xh, axis=0, keepdims=True)


def _sigmoid(a):
    return 1.0 / (1.0 + jnp.exp(-a))


def _row_tile(n, want):
    t = min(n, want)
    assert n % t == 0, (n, t)
    return t


def _ffn_fwd(h, g1, g2, wg, wu, wd, ride=None):
    S = h.shape[0]
    ts = _row_tile(S, 512)

    def body(h_ref, g1_ref, g2_ref, wg_ref, wu_ref, wd_ref, ho_ref, a_ref, b_ref, f_ref):
        hv = h_ref[...]
        n = (hv * _rstd(hv) * g1_ref[...]).astype(_MM)
        f = jnp.zeros((ts, D), F32)
        for j in range(FF // FC):
            cs = slice(j * FC, (j + 1) * FC)
            a = _mm(n, wg_ref[j])
            b = _mm(n, wu_ref[j])
            a_ref[:, cs] = a.astype(a_ref.dtype)
            b_ref[:, cs] = b.astype(b_ref.dtype)
            f = f + _mm(a * _sigmoid(a) * b, wd_ref[j])
        f_ref[...] = f
        ho_ref[...] = hv + 0.5 * (f * _rstd(f) * g2_ref[...])

    row = lambda i: (i, 0)
    fix = lambda i: (0, 0)
    all3 = lambda i: (0, 0, 0)
    ns = FF // FC
    return _call(
        body, "ffn_fwd", S // ts,
        [pl.BlockSpec((ts, D), row), pl.BlockSpec((1, D), fix), pl.BlockSpec((1, D), fix),
         pl.BlockSpec((ns, D, FC), all3), pl.BlockSpec((ns, D, FC), all3), pl.BlockSpec((ns, FC, D), all3)],
        [pl.BlockSpec((ts, D), row), pl.BlockSpec((ts, FF), row), pl.BlockSpec((ts, FF), row),
         pl.BlockSpec((ts, D), row)],
        [jax.ShapeDtypeStruct((S, D), F32), jax.ShapeDtypeStruct((S, FF), _MM),
         jax.ShapeDtypeStruct((S, FF), _MM), jax.ShapeDtypeStruct((S, D), F32)],
        (h, g1, g2, wg, wu, wd), ride)


def _ffn_bwd(h, f, a, b, dho, g1, g2, wg, wu, wd, ride=None):
    S = h.shape[0]
    ts = _row_tile(S, 256)

    def body(h_ref, f_ref, a_ref, b_ref, dho_ref, g1_ref, g2_ref, wg_ref, wu_ref, wd_ref,
             dh_ref, n_ref, s_ref, da_ref, db_ref, df_ref, dg1_ref, dg2_ref):
        @pl.when(pl.program_id(0) == 0)
        def _():
            dg1_ref[...] = jnp.zeros_like(dg1_ref)
            dg2_ref[...] = jnp.zeros_like(dg2_ref)

        hv = h_ref[...]
        dho = dho_ref[...]
        df, dg2 = _rms_bwd(f_ref[...], g2_ref[...], 0.5 * dho)
        dg2_ref[...] += dg2
        dfb = df.astype(_MM)
        df_ref[...] = dfb
        rh = _rstd(hv)
        hh = hv * rh
        g1 = g1_ref[...]
        n_ref[...] = (hh * g1).astype(_MM)
        dss = [_mm_nt(dfb, wd_ref[j]) for j in range(FF // FC)]
        das, dbs = [], []
        for j in range(FF // FC):
            cs = slice(j * FC, (j + 1) * FC)
            av = a_ref[:, cs].astype(F32)
            bv = b_ref[:, cs].astype(F32)
            sig = _sigmoid(av)
            sl = av * sig
            s_ref[:, cs] = (sl * bv).astype(_MM)
            da = (dss[j] * bv * (sig * (1.0 + av * (1.0 - sig)))).astype(_MM)
            db = (dss[j] * sl).astype(_MM)
            da_ref[:, cs] = da
            db_ref[:, cs] = db
            das.append(da)
            dbs.append(db)
        dn = jnp.zeros((ts, D), F32)
        for j in range(FF // FC):
            dn = dn + _mm_nt(das[j], wg_ref[j]) + _mm_nt(dbs[j], wu_ref[j])
        t = dn * g1
        dh_ref[...] = dho + rh * (t - hh * jnp.mean(t * hh, axis=-1, keepdims=True))
        dg1_ref[...] += jnp.sum(dn * hh, axis=0, keepdims=True)

    row = lambda i: (i, 0)
    fix = lambda i: (0, 0)
    all3 = lambda i: (0, 0, 0)
    ns = FF // FC
    td = pl.BlockSpec((ts, D), row)
    tf = pl.BlockSpec((ts, FF), row)
    gd = pl.BlockSpec((1, D), fix)
    return _call(
        body, "ffn_bwd", S // ts,
        [td, td, tf, tf, td, gd, gd,
         pl.BlockSpec((ns, D, FC), all3), pl.BlockSpec((ns, D, FC), all3), pl.BlockSpec((ns, FC, D), all3)],
        [td, td, tf, tf, tf, td, gd, gd],
        [jax.ShapeDtypeStruct((S, D), F32), jax.ShapeDtypeStruct((S, D), _MM),
         jax.ShapeDtypeStruct((S, FF), _MM), jax.ShapeDtypeStruct((S, FF), _MM),
         jax.ShapeDtypeStruct((S, FF), _MM), jax.ShapeDtypeStruct((S, D), _MM),
         jax.ShapeDtypeStruct((1, D), F32), jax.ShapeDtypeStruct((1, D), F32)],
        (h, f, a, b, dho, g1, g2, wg, wu, wd), ride)


def _wgrad(x, y, nslot=0, ride=None):
    T, M = x.shape
    N = y.shape[1]
    tt = _row_tile(T, 1024)
    cw = N // nslot if nslot else (512 if N % 512 == 0 else 256)
    nc = N // cw
    assert cw * nc == N and cw % 128 == 0, (N, nc)

    def body(x_ref, y_ref, o_ref):
        @pl.when(pl.program_id(0) == 0)
        def _():
            o_ref[...] = jnp.zeros_like(o_ref)

        xt = x_ref[...].T
        for j in range(nc):
            part = jnp.dot(xt, y_ref[:, j * cw:(j + 1) * cw], preferred_element_type=F32)
            if nslot:
                o_ref[j] += part
            else:
                o_ref[:, j * cw:(j + 1) * cw] += part

    oshape = (nslot, M, cw) if nslot else (M, N)
    res = _call(body, "wgrad", T // tt,
                [pl.BlockSpec((tt, M), lambda t: (t, 0)), pl.BlockSpec((tt, N), lambda t: (t, 0))],
                [pl.BlockSpec(oshape, lambda t: (0,) * len(oshape))], [jax.ShapeDtypeStruct(oshape, F32)], (x, y), ride)
    return res[0] if ride is None else (res[0][0], res[1])


def _proj_fwd(h, g, w, out_dtype):
    S = h.shape[0]
    nk, kw, N = w.shape
    ts = _row_tile(S, 256)

    def body(h_ref, g_ref, w_ref, u_ref, p_ref):
        hv = h_ref[...]
        u = (hv * _rstd(hv) * g_ref[...]).astype(_MM)
        u_ref[...] = u
        acc = _mm(u[:, :kw], w_ref[0])
        for j in range(1, nk):
            acc = acc + _mm(u[:, j * kw:(j + 1) * kw], w_ref[j])
        p_ref[...] = acc.astype(p_ref.dtype)

    return pl.pallas_call(
        body, name="proj_fwd", grid=(S // ts,),
        in_specs=[pl.BlockSpec((ts, D), lambda i: (i, 0)), pl.BlockSpec((1, D), lambda i: (0, 0)),
                  pl.BlockSpec((nk, kw, N), lambda i: (0, 0, 0))],
        out_specs=[pl.BlockSpec((ts, D), lambda i: (i, 0)), pl.BlockSpec((ts, N), lambda i: (i, 0))],
        out_shape=[jax.ShapeDtypeStruct((S, D), _MM), jax.ShapeDtypeStruct((S, N), out_dtype)],
        compiler_params=_cparams(("parallel",)),
    )(h, g, w)


def _proj_bwd(h, dho, g, w, pieces):
    S = h.shape[0]
    nk, kw, N = w.shape
    ts = _row_tile(S, 256)
    widths = [p.shape[1] for p in pieces]
    assert sum(widths) == N
    npc = len(pieces)

    def body(*refs):
        h_ref, dho_ref, g_ref, w_ref = refs[:4]
        p_refs = refs[4:4 + npc]
        dh_ref, dp_ref, dg_ref = refs[4 + npc:]

        @pl.when(pl.program_id(0) == 0)
        def _():
            dg_ref[...] = jnp.zeros_like(dg_ref)

        dus = [jnp.zeros((ts, kw), F32) for _ in range(nk)]
        off = 0
        for p_ref, n in zip(p_refs, widths):
            pv = p_ref[...].astype(_MM)
            dp_ref[:, off:off + n] = pv
            for j in range(nk):
                dus[j] = dus[j] + _mm_nt(pv, w_ref[j, :, off:off + n])
            off += n
        du = dus[0] if nk == 1 else jnp.concatenate(dus, axis=-1)
        dx, dg = _rms_bwd(h_ref[...], g_ref[...], du)
        dh_ref[...] = dho_ref[...] + dx
        dg_ref[...] += dg

    row = lambda i: (i, 0)
    return pl.pallas_call(
        body, name="proj_bwd", grid=(S // ts,),
        in_specs=[pl.BlockSpec((ts, D), row), pl.BlockSpec((ts, D), row), pl.BlockSpec((1, D), lambda i: (0, 0)),
                  pl.BlockSpec((nk, kw, N), lambda i: (0, 0, 0))] + [pl.BlockSpec((ts, n), row) for n in widths],
        out_specs=[pl.BlockSpec((ts, D), row), pl.BlockSpec((ts, N), row), pl.BlockSpec((1, D), lambda i: (0, 0))],
        out_shape=[jax.ShapeDtypeStruct((S, D), F32), jax.ShapeDtypeStruct((S, N), _MM),
                   jax.ShapeDtypeStruct((1, D), F32)],
        compiler_params=_cparams(("arbitrary",)),
    )(h, dho, g, w, *pieces)


def _pool_counts(first_row, ts):
    pos = (first_row + lax.broadcasted_iota(jnp.int32, (ts, 1), 0) + 1).astype(F32)
    return [jnp.minimum(pos, float(w)) for w in WINDOWS]


def _pool_delta(x, prev, cnts, ts):
    xe = jnp.concatenate([prev, x], axis=0)
    sums = []
    cur = xe
    for sh in (1, 2, 4, 8):
        cur = cur + pltpu.roll(cur, sh, 0)
        sums.append(cur)
    return [sums[gi][HALO:, gi * PG:(gi + 1) * PG] / cnts[gi] - x[:, gi * PG:(gi + 1) * PG]
            for gi in range(len(WINDOWS))]


def _pool_fwd(proj, pw, ps):
    S = proj.shape[0]
    ts = _row_tile(S, 256)

    def body(x_ref, pw_ref, ps_ref, tok_ref, carry_ref):
        i = pl.program_id(0)

        @pl.when(i == 0)
        def _():
            carry_ref[...] = jnp.zeros_like(carry_ref)

        x = x_ref[...]
        ds = _pool_delta(x, carry_ref[...], _pool_counts(i * ts, ts), ts)
        y = jnp.concatenate([_mm(ds[gi], pw_ref[gi]) for gi in range(len(WINDOWS))], axis=-1)
        tok_ref[...] = (y * ps_ref[...]).astype(tok_ref.dtype)
        carry_ref[...] = x[ts - HALO:, :]

    return pl.pallas_call(
        body, name="pool_fwd", grid=(S // ts,),
        in_specs=[pl.BlockSpec((ts, DTOK), lambda i: (i, 0)), pl.BlockSpec((len(WINDOWS), PG, PG), lambda i: (0, 0, 0)),
                  pl.BlockSpec((1, DTOK), lambda i: (0, 0))],
        out_specs=pl.BlockSpec((ts, DTOK), lambda i: (i, 0)),
        out_shape=jax.ShapeDtypeStruct((S, DTOK), _MM),
        scratch_shapes=[pltpu.VMEM((HALO, DTOK), F32)],
        compiler_params=_cparams(("arbitrary",)),
    )(proj, pw, ps)


def _pool_bwd(proj, dtok, pw, ps):
    S = proj.shape[0]
    ts = _row_tile(S, 256)
    nt = S // ts
    per = ts // HALO
    ng = len(WINDOWS)

    def body(x_ref, xp_ref, dt_ref, pw_ref, ps_ref, dx_ref, dpw_ref, dps_ref, carry_ref):
        i = pl.program_id(0)
        idx = nt - 1 - i

        @pl.when(i == 0)
        def _():
            carry_ref[...] = jnp.zeros_like(carry_ref)
            dpw_ref[...] = jnp.zeros_like(dpw_ref)
            dps_ref[...] = jnp.zeros_like(dps_ref)

        x = x_ref[...]
        prev = jnp.where(idx > 0, xp_ref[...], 0.0)
        cnts = _pool_counts(idx * ts, ts)
        ds = _pool_delta(x, prev, cnts, ts)
        dt = dt_ref[...]
        y = jnp.concatenate([_mm(ds[gi], pw_ref[gi]) for gi in range(ng)], axis=-1)
        dps_ref[...] += jnp.sum(dt * y, axis=0, keepdims=True)
        dy = (dt * ps_ref[...]).astype(_MM)
        dds = []
        for gi in range(ng):
            dyg = dy[:, gi * PG:(gi + 1) * PG]
            dpw_ref[gi] += _mm_tn(ds[gi], dyg)
            dds.append(_mm_nt(dyg, pw_ref[gi]))
        e = jnp.concatenate([dds[gi] / cnts[gi] for gi in range(ng)], axis=-1)
        ee = jnp.concatenate([e, carry_ref[...]], axis=0)
        rows = ts + HALO
        cur = ee
        outs = []
        for gi, sh in enumerate((1, 2, 4, 8)):
            cur = cur + pltpu.roll(cur, rows - sh, 0)
            outs.append(cur[:ts, gi * PG:(gi + 1) * PG] - dds[gi])
        dx_ref[...] = jnp.concatenate(outs, axis=-1)
        carry_ref[...] = e[:HALO, :]

    rev = lambda i: (nt - 1 - i, 0)
    return pl.pallas_call(
        body, name="pool_bwd", grid=(nt,),
        in_specs=[pl.BlockSpec((ts, DTOK), rev),
                  pl.BlockSpec((HALO, DTOK), lambda i: (jnp.maximum((nt - 1 - i) * per - 1, 0), 0)),
                  pl.BlockSpec((ts, DTOK), rev), pl.BlockSpec((ng, PG, PG), lambda i: (0, 0, 0)),
                  pl.BlockSpec((1, DTOK), lambda i: (0, 0))],
        out_specs=[pl.BlockSpec((ts, DTOK), rev), pl.BlockSpec((ng, PG, PG), lambda i: (0, 0, 0)),
                   pl.BlockSpec((1, DTOK), lambda i: (0, 0))],
        out_shape=[jax.ShapeDtypeStruct((S, DTOK), F32), jax.ShapeDtypeStruct((ng, PG, PG), F32),
                   jax.ShapeDtypeStruct((1, DTOK), F32)],
        scratch_shapes=[pltpu.VMEM((HALO, DTOK), F32)],
        compiler_params=_cparams(("arbitrary",)),
    )(proj, proj, dtok, pw, ps)


def _tri(strict):
    r = lax.broadcasted_iota(jnp.int32, (QB, QB), 0)
    c = lax.broadcasted_iota(jnp.int32, (QB, QB), 1)
    return jnp.where(r > c if strict else r >= c, 1.0, 0.0).astype(_MM)


def _sb_terms(z, valid):
    e = jnp.exp(-jnp.abs(z))
    sp = jnp.log(1.0 + e)
    ls = jnp.minimum(z, 0.0) - sp
    lf = jnp.where(valid, jnp.minimum(-z, 0.0) - sp, 0.0)
    return ls, lf, e


def _sb_alive(cs, lo=0):
    top = cs[0][lo:]
    for c in cs[1:]:
        top = jnp.maximum(top, c[lo:])
    return (jnp.max(top) > NEG_CUT).astype(jnp.int32)


def _sb_split(x2, head_a):
    zero = jnp.zeros_like(x2)
    return jnp.where(head_a, x2, zero), jnp.where(head_a, zero, x2)


SB_GROUPS_FWD = 4
SB_GROUPS_BWD = 4
SB_TOP = 48


def _sb_fwd(proj):
    S = proj.shape[0]
    G = min(SB_GROUPS_FWD, S // QB)
    nstep = S // (G * QB)
    npair = DTOK // 128
    nch = 2 * G

    def body(q_ref, k_ref, v_ref, tri_ref, o_ref):
        qb = pl.program_id(1)
        head_a = lax.broadcasted_iota(jnp.int32, (QB, 128), 1) < HD
        qs = []
        for g in range(G):
            qs.extend(_sb_split(q_ref[g * QB:(g + 1) * QB, :], head_a))
        tri = tri_ref[...]
        last = G * qb + G - 1

        def trips(rows):
            qr = [q[:rows] for q in qs]
            row = lax.broadcasted_iota(jnp.int32, (rows, QB), 0)
            col = lax.broadcasted_iota(jnp.int32, (rows, QB), 1)

            def step(carry):
                i, _, cs, accs = carry
                causal = (col - row) < i * QB
                ks, vs, valids = [], [], []
                for g in range(G):
                    kb = G * qb + g - i
                    off = pl.multiple_of(jnp.maximum(kb, 0) * QB, QB)
                    ks.append(k_ref[pl.ds(off, QB), :])
                    vs.append(v_ref[pl.ds(off, QB), :])
                    valids.append(jnp.logical_and(causal, kb >= 0))
                zs = [_mm_nt(qr[ch], ks[ch // 2]) * SCALE for ch in range(nch)]
                terms = [_sb_terms(zs[ch], valids[ch // 2]) for ch in range(nch)]
                withins = [_mm2(terms[ch][1], tri) for ch in range(nch)]
                new_c, new_acc = [], []
                for ch in range(nch):
                    ls, lf, _ = terms[ch]
                    a = jnp.where(valids[ch // 2], jnp.exp(ls + withins[ch] + cs[ch]), 0.0)
                    new_acc.append(accs[ch] + _mm2(a, vs[ch // 2]))
                    new_c.append(cs[ch] + jnp.sum(lf, axis=-1, keepdims=True))
                return i + 1, _sb_alive(new_c, SB_TOP if rows == QB else 0), tuple(new_c), tuple(new_acc)

            return step

        zc = jnp.zeros((QB, 1), F32)
        za = jnp.zeros((QB, 128), F32)
        more = lambda cr: jnp.logical_and(cr[0] <= last, cr[1] > 0)
        i, _, cs, accs = lax.while_loop(more, trips(QB), (0, 1, (zc,) * nch, (za,) * nch))
        cs_top = tuple(c[:SB_TOP] for c in cs)
        acc_top = tuple(a[:SB_TOP] for a in accs)
        acc_top = lax.while_loop(more, trips(SB_TOP), (i, _sb_alive(cs_top), cs_top, acc_top))[3]
        for g in range(G):
            o_ref[g * QB:g * QB + SB_TOP, :] = jnp.where(head_a[:SB_TOP], acc_top[2 * g], acc_top[2 * g + 1])
            o_ref[g * QB + SB_TOP:(g + 1) * QB, :] = jnp.where(head_a, accs[2 * g], accs[2 * g + 1])[SB_TOP:]

    return pl.pallas_call(
        body, name="sb_fwd", grid=(npair, nstep),
        in_specs=[pl.BlockSpec((G * QB, 128), lambda p, i: (i, p)),
                  pl.BlockSpec((S, 128), lambda p, i: (0, npair + p)),
                  pl.BlockSpec((S, 128), lambda p, i: (0, 2 * npair + p)),
                  pl.BlockSpec((QB, QB), lambda p, i: (0, 0))],
        out_specs=pl.BlockSpec((G * QB, 128), lambda p, i: (i, p)),
        out_shape=jax.ShapeDtypeStruct((S, DTOK), F32),
        compiler_params=_cparams(("parallel", "parallel")),
    )(proj, proj, proj, _tri(True))


def _sb_bwd(proj, o, do):
    S = proj.shape[0]
    G = min(SB_GROUPS_BWD, S // QB)
    nstep = S // (G * QB)
    npair = DTOK // 128
    nch = 2 * G

    def body(q_ref, k_ref, v_ref, o_ref, do_ref, tri_ref, tri2_ref, dq_ref, dk_ref, dv_ref):
        qb = pl.program_id(1)

        @pl.when(qb == 0)
        def _():
            dk_ref[...] = jnp.zeros_like(dk_ref)
            dv_ref[...] = jnp.zeros_like(dv_ref)

        head_a = lax.broadcasted_iota(jnp.int32, (QB, 128), 1) < HD
        qs, dos, gs = [], [], []
        for g in range(G):
            rows = slice(g * QB, (g + 1) * QB)
            qs.extend(_sb_split(q_ref[rows, :], head_a))
            dob = do_ref[rows, :].astype(_MM)
            dos.extend(_sb_split(dob, head_a))
            go = dob.astype(F32) * o_ref[rows, :]
            gs.append(jnp.sum(jnp.where(head_a, go, 0.0), axis=-1, keepdims=True))
            gs.append(jnp.sum(jnp.where(head_a, 0.0, go), axis=-1, keepdims=True))
        tri = tri_ref[...]
        tri2 = tri2_ref[...]
        last = G * qb + G - 1

        def trips(rows):
            qr = [q[:rows] for q in qs]
            dor = [d[:rows] for d in dos]
            gr = [g_[:rows] for g_ in gs]
            row = lax.broadcasted_iota(jnp.int32, (rows, QB), 0)
            col = lax.broadcasted_iota(jnp.int32, (rows, QB), 1)

            def step(carry):
                i, _, cs, rs, dqs = carry
                causal = (col - row) < i * QB
                offs, ks, vs, valids = [], [], [], []
                for g in range(G):
                    kb = G * qb + g - i
                    off = pl.multiple_of(jnp.maximum(kb, 0) * QB, QB)
                    offs.append(off)
                    ks.append(k_ref[pl.ds(off, QB), :])
                    vs.append(v_ref[pl.ds(off, QB), :])
                    valids.append(jnp.logical_and(causal, kb >= 0))
                zs = [_mm_nt(qr[ch], ks[ch // 2]) * SCALE for ch in range(nch)]
                das = [_mm_nt(dor[ch], vs[ch // 2]) for ch in range(nch)]
                terms = [_sb_terms(zs[ch], valids[ch // 2]) for ch in range(nch)]
                withins = [_mm2(terms[ch][1], tri) for ch in range(nch)]
                a_s, dlws = [], []
                for ch in range(nch):
                    a = jnp.where(valids[ch // 2], jnp.exp(terms[ch][0] + withins[ch] + cs[ch]), 0.0)
                    a_s.append(a)
                    dlws.append(das[ch] * a)
                sfx = [_mm2(dlws[ch], tri2) for ch in range(nch)]
                dzs = []
                for ch in range(nch):
                    e = terms[ch][2]
                    inv = 1.0 / (1.0 + e)
                    pos = zs[ch] >= 0.0
                    beta = jnp.where(pos, 1.0, e) * inv
                    omb = jnp.where(pos, e, 1.0) * inv
                    prefix = gr[ch] - rs[ch] - sfx[ch]
                    dzs.append((jnp.where(valids[ch // 2], dlws[ch] * omb - beta * prefix, 0.0) * SCALE).astype(_MM))
                new_dq = [dqs[ch] + _mm(dzs[ch], ks[ch // 2]) for ch in range(nch)]
                for g in range(G):
                    a, b = 2 * g, 2 * g + 1
                    dk_ref[pl.ds(offs[g], QB), :] += _mm_tn(dzs[a], qr[a]) + _mm_tn(dzs[b], qr[b])
                    dv_ref[pl.ds(offs[g], QB), :] += _mm_tn(a_s[a], dor[a]) + _mm_tn(a_s[b], dor[b])
                new_c = [cs[ch] + jnp.sum(terms[ch][1], axis=-1, keepdims=True) for ch in range(nch)]
                new_r = [rs[ch] + jnp.sum(dlws[ch], axis=-1, keepdims=True) for ch in range(nch)]
                alive = _sb_alive(new_c, SB_TOP if rows == QB else 0)
                return i + 1, alive, tuple(new_c), tuple(new_r), tuple(new_dq)

            return step

        zc = jnp.zeros((QB, 1), F32)
        za = jnp.zeros((QB, 128), F32)
        more = lambda cr: jnp.logical_and(cr[0] <= last, cr[1] > 0)
        i, _, cs, rs, dqs = lax.while_loop(more, trips(QB), (0, 1, (zc,) * nch, (zc,) * nch, (za,) * nch))
        top = lambda xs: tuple(x[:SB_TOP] for x in xs)
        cs_top = top(cs)
        dq_top = lax.while_loop(more, trips(SB_TOP), (i, _sb_alive(cs_top), cs_top, top(rs), top(dqs)))[4]
        for g in range(G):
            dq_ref[g * QB:g * QB + SB_TOP, :] = jnp.where(head_a[:SB_TOP], dq_top[2 * g], dq_top[2 * g + 1])
            dq_ref[g * QB + SB_TOP:(g + 1) * QB, :] = jnp.where(head_a, dqs[2 * g], dqs[2 * g + 1])[SB_TOP:]

    qspec = pl.BlockSpec((G * QB, 128), lambda p, i: (i, p))
    full = lambda base: pl.BlockSpec((S, 128), lambda p, i: (0, base + p))
    tspec = pl.BlockSpec((QB, QB), lambda p, i: (0, 0))
    return pl.pallas_call(
        body, name="sb_bwd", grid=(npair, nstep),
        in_specs=[qspec, full(npair), full(2 * npair), qspec, qspec, tspec, tspec],
        out_specs=[qspec, full(0), full(0)],
        out_shape=[jax.ShapeDtypeStruct((S, DTOK), F32)] * 3,
        compiler_params=_cparams(("parallel", "arbitrary")),
    )(proj, proj, proj, o, do, _tri(True), _tri(False))


def _memkv_fwd(mem, gm, wkv):
    nk, kw, _ = wkv.shape

    def body(m_ref, g_ref, w_ref, kv_ref):
        m = m_ref[...]
        mn = (m * _rstd(m) * g_ref[...]).astype(_MM)
        acc = _mm(mn[:, :kw], w_ref[0])
        for j in range(1, nk):
            acc = acc + _mm(mn[:, j * kw:(j + 1) * kw], w_ref[j])
        kv_ref[...] = acc.astype(kv_ref.dtype)

    return pl.pallas_call(
        body, name="memkv_fwd", out_shape=jax.ShapeDtypeStruct((MEM_LEN, 2 * DMEM), _MM),
        compiler_params=pltpu.CompilerParams(vmem_limit_bytes=VMEM_LIMIT),
    )(mem, gm, wkv)


def _memkv_bwd(mem, gm, wkv, dkv):
    nk, kw, _ = wkv.shape

    def body(m_ref, g_ref, w_ref, dkv_ref, dw_ref, dg_ref):
        m = m_ref[...]
        mh = m * _rstd(m)
        mn = (mh * g_ref[...]).astype(_MM)
        dkv = dkv_ref[...].astype(_MM)
        dw_ref[...] = _mm_tn(mn, dkv)
        dmn = jnp.concatenate([_mm_nt(dkv, w_ref[j]) for j in range(nk)], axis=-1)
        dg_ref[...] = jnp.sum(dmn * mh, axis=0, keepdims=True)

    return pl.pallas_call(
        body, name="memkv_bwd",
        out_shape=[jax.ShapeDtypeStruct((D, 2 * DMEM), F32), jax.ShapeDtypeStruct((1, D), F32)],
        compiler_params=pltpu.CompilerParams(vmem_limit_bytes=VMEM_LIMIT),
    )(mem, gm, wkv, dkv)


def _mixout_fwd(h, tok, proj, kv, wo, g, qm_block):
    S = h.shape[0]
    ts = _row_tile(S, 256)

    def body(h_ref, tok_ref, qm_ref, kv_ref, wo_ref, g_ref, ho_ref, cat_ref, mix_ref):
        head_a = lax.broadcasted_iota(jnp.int32, (ts, 128), 1) < HD
        qm = qm_ref[...].astype(_MM)
        zq = jnp.zeros((ts, 128), _MM)
        npair = DMEM // 128
        qhs = []
        for pi in range(npair):
            qp = qm[:, pi * 128:(pi + 1) * 128]
            qhs += [jnp.where(head_a, qp, zq), jnp.where(head_a, zq, qp)]
        scores = [_mm_nt(qhs[h], kv_ref[:, (h // 2) * 128:(h // 2 + 1) * 128]) * SCALE for h in range(2 * npair)]
        ps = []
        for s in scores:
            e = jnp.exp(s - jnp.max(s, axis=-1, keepdims=True))
            ps.append(e / jnp.sum(e, axis=-1, keepdims=True))
        pvs = [_mm(ps[h], kv_ref[:, DMEM + (h // 2) * 128:DMEM + (h // 2 + 1) * 128]) for h in range(2 * npair)]
        mos = [jnp.where(head_a, pvs[2 * pi], pvs[2 * pi + 1]) for pi in range(npair)]
        cat = jnp.concatenate([tok_ref[...].astype(_MM)] + [m.astype(_MM) for m in mos], axis=-1)
        cat_ref[...] = cat
        mix = jnp.concatenate([_mm(cat, wo_ref[j]) for j in range(N_CHIPS)], axis=-1)
        mix_ref[...] = mix
        ho_ref[...] = h_ref[...] + mix * _rstd(mix) * g_ref[...]

    row = lambda i: (i, 0)
    return pl.pallas_call(
        body, name="mixout_fwd", grid=(S // ts,),
        in_specs=[pl.BlockSpec((ts, D), row), pl.BlockSpec((ts, DTOK), row),
                  pl.BlockSpec((ts, DMEM), lambda i: (i, qm_block)),
                  pl.BlockSpec((MEM_LEN, 2 * DMEM), lambda i: (0, 0)),
                  pl.BlockSpec((N_CHIPS, DMIX, D // N_CHIPS), lambda i: (0, 0, 0)), pl.BlockSpec((1, D), lambda i: (0, 0))],
        out_specs=[pl.BlockSpec((ts, D), row), pl.BlockSpec((ts, DMIX), row), pl.BlockSpec((ts, D), row)],
        out_shape=[jax.ShapeDtypeStruct((S, D), F32), jax.ShapeDtypeStruct((S, DMIX), _MM),
                   jax.ShapeDtypeStruct((S, D), F32)],
        compiler_params=_cparams(("parallel",)),
    )(h, tok, proj, kv, wo, g)


def _mixout_bwd(dho, mix, proj, kv, wo, g, qm_block):
    S = dho.shape[0]
    ts = _row_tile(S, 256)

    def body(dho_ref, mix_ref, qm_ref, kv_ref, wo_ref, g_ref, dmix_ref, dtok_ref, dqm_ref, dkv_ref, dg_ref):
        @pl.when(pl.program_id(0) == 0)
        def _():
            dkv_ref[...] = jnp.zeros_like(dkv_ref)
            dg_ref[...] = jnp.zeros_like(dg_ref)

        dmix, dg = _rms_bwd(mix_ref[...], g_ref[...], dho_ref[...])
        dg_ref[...] += dg
        dmb = dmix.astype(_MM)
        dmix_ref[...] = dmb
        cw = D // N_CHIPS
        dcat = _mm_nt(dmb[:, :cw], wo_ref[0])
        for j in range(1, N_CHIPS):
            dcat = dcat + _mm_nt(dmb[:, j * cw:(j + 1) * cw], wo_ref[j])
        dtok_ref[...] = dcat[:, :DTOK]
        head_a = lax.broadcasted_iota(jnp.int32, (ts, 128), 1) < HD
        qm = qm_ref[...].astype(_MM)
        zq = jnp.zeros((ts, 128), _MM)
        npair = DMEM // 128
        kps = [kv_ref[:, pi * 128:(pi + 1) * 128] for pi in range(npair)]
        vps = [kv_ref[:, DMEM + pi * 128:DMEM + (pi + 1) * 128] for pi in range(npair)]
        qhs, dmhs = [], []
        for pi in range(npair):
            cs = slice(pi * 128, (pi + 1) * 128)
            dmo = dcat[:, DTOK + pi * 128:DTOK + (pi + 1) * 128].astype(_MM)
            qhs += [jnp.where(head_a, qm[:, cs], zq), jnp.where(head_a, zq, qm[:, cs])]
            dmhs += [jnp.where(head_a, dmo, zq), jnp.where(head_a, zq, dmo)]
        nh = 2 * npair
        scores = [_mm_nt(qhs[h], kps[h // 2]) * SCALE for h in range(nh)]
        dps = [_mm_nt(dmhs[h], vps[h // 2]) for h in range(nh)]
        ps, dss = [], []
        for h in range(nh):
            e = jnp.exp(scores[h] - jnp.max(scores[h], axis=-1, keepdims=True))
            p = e / jnp.sum(e, axis=-1, keepdims=True)
            ps.append(p)
            dss.append((p * (dps[h] - jnp.sum(dps[h] * p, axis=-1, keepdims=True)) * SCALE).astype(_MM))
        dqs = [_mm(dss[h], kps[h // 2]) for h in range(nh)]
        for pi in range(npair):
            cs = slice(pi * 128, (pi + 1) * 128)
            a, b = 2 * pi, 2 * pi + 1
            dqm_ref[:, cs] = jnp.where(head_a, dqs[a], dqs[b]).astype(_MM)
            dkv_ref[:, cs] += _mm_tn(dss[a], qhs[a]) + _mm_tn(dss[b], qhs[b])
            dkv_ref[:, DMEM + pi * 128:DMEM + (pi + 1) * 128] += _mm_tn(ps[a], dmhs[a]) + _mm_tn(ps[b], dmhs[b])

    row = lambda i: (i, 0)
    fix = lambda i: (0, 0)
    return pl.pallas_call(
        body, name="mixout_bwd", grid=(S // ts,),
        in_specs=[pl.BlockSpec((ts, D), row), pl.BlockSpec((ts, D), row),
                  pl.BlockSpec((ts, DMEM), lambda i: (i, qm_block)),
                  pl.BlockSpec((MEM_LEN, 2 * DMEM), fix),
                  pl.BlockSpec((N_CHIPS, DMIX, D // N_CHIPS), lambda i: (0, 0, 0)), pl.BlockSpec((1, D), fix)],
        out_specs=[pl.BlockSpec((ts, D), row), pl.BlockSpec((ts, DTOK), row), pl.BlockSpec((ts, DMEM), row),
                   pl.BlockSpec((MEM_LEN, 2 * DMEM), fix), pl.BlockSpec((1, D), fix)],
        out_shape=[jax.ShapeDtypeStruct((S, D), _MM), jax.ShapeDtypeStruct((S, DTOK), F32),
                   jax.ShapeDtypeStruct((S, DMEM), _MM), jax.ShapeDtypeStruct((MEM_LEN, 2 * DMEM), F32),
                   jax.ShapeDtypeStruct((1, D), F32)],
        compiler_params=_cparams(("arbitrary",)),
    )(dho, mix, proj, kv, wo, g)


def _loss_grad(y, target):
    S = y.shape[0]
    ts = _row_tile(S, 512)

    def body(y_ref, t_ref, acc_ref, dy_ref):
        @pl.when(pl.program_id(0) == 0)
        def _():
            acc_ref[...] = jnp.zeros_like(acc_ref)

        err = y_ref[...] - t_ref[...]
        acc_ref[...] += jnp.sum(err * err, axis=0, keepdims=True)
        dy_ref[...] = err * (1.0 / D)

    row = lambda i: (i, 0)
    return pl.pallas_call(
        body, name="loss_grad", grid=(S // ts,),
        in_specs=[pl.BlockSpec((ts, D), row), pl.BlockSpec((ts, D), row)],
        out_specs=[pl.BlockSpec((1, D), lambda i: (0, 0)), pl.BlockSpec((ts, D), row)],
        out_shape=[jax.ShapeDtypeStruct((1, D), F32), jax.ShapeDtypeStruct((S, D), F32)],
        compiler_params=_cparams(("arbitrary",)),
    )(y, target)


LAYER_TENSORS = ("ffn1_gate", "ffn1_up", "ffn1_down", "ffn2_gate", "ffn2_up", "ffn2_down", "w_in", "w_mem_kv", "w_out")
SPLITS = {"ffn1_gate": "c", "ffn1_up": "c", "ffn1_down": "r", "ffn2_gate": "c", "ffn2_up": "c", "ffn2_down": "r",
          "w_in_pool": "r", "w_in_sb": "c", "w_mem_kv": "r", "w_out": "c"}


def _is_pool(i):
    return i % 2 == 0


def _qm_block(i):
    return (DTOK if _is_pool(i) else 3 * DTOK) // DMEM


FIRST_FFN = LAYER_TENSORS[:3]
REST = LAYER_TENSORS[3:]


def _layer_forward(h, mem, lw, i, rest_shards, next_shards):
    g_at = lambda g, k: g[k].reshape(1, D)
    sv = {"h0": h}
    (h, sv["a1"], sv["b1"], sv["f1"]), rest = _ffn_fwd(
        h, g_at(lw["g_pre"], 0), g_at(lw["g_post"], 0), lw["ffn1_gate"], lw["ffn1_up"], lw["ffn1_down"],
        _GatherExchange(rest_shards))
    lw.update(zip(REST, rest))
    if not _is_pool(i):
        lw["w_in"] = jnp.transpose(lw["w_in"], (1, 0, 2)).reshape(1, D, DSB)
    sv["h1"] = h
    sv["kv"] = _memkv_fwd(mem, lw["g_mem"], lw["w_mem_kv"])
    if _is_pool(i):
        sv["u"], sv["proj"] = _proj_fwd(h, g_at(lw["g_pre"], 1), lw["w_in"], F32)
        tok = _pool_fwd(sv["proj"], lw["pool_w"], lw["pool_scale"])
    else:
        sv["u"], sv["proj"] = _proj_fwd(h, g_at(lw["g_pre"], 1), lw["w_in"], _MM)
        tok = sv["o"] = _sb_fwd(sv["proj"])
    h, sv["cat"], sv["mix"] = _mixout_fwd(h, tok, sv["proj"], sv["kv"], lw["w_out"], g_at(lw["g_post"], 1),
                                          _qm_block(i))
    sv["h2"] = h
    ffn2 = (h, g_at(lw["g_pre"], 2), g_at(lw["g_post"], 2), lw["ffn2_gate"], lw["ffn2_up"], lw["ffn2_down"])
    if next_shards is None:
        h, sv["a2"], sv["b2"], sv["f2"] = _ffn_fwd(*ffn2)
        return h, sv, None
    (h, sv["a2"], sv["b2"], sv["f2"]), nxt = _ffn_fwd(*ffn2, _GatherExchange(next_shards))
    return h, sv, nxt


def _layer_backward(dh, mem, lw, sv, i, rides=None):
    g_at = lambda g, k: g[k].reshape(1, D)
    big, small = {}, {}
    dgpre, dgpost = [None] * 3, [None] * 3
    rode = {}
    rides = rides or {}

    def hosted(host, fn, *args):
        ride = rides[host](big, rode) if host in rides else None
        if ride is None:
            return fn(*args)
        outs, rode[host] = fn(*args, ride)
        return outs

    def ffn_back(which, dh, h_in):
        k = 0 if which == 1 else 2
        n = f"ffn{which}"
        dh, nb, sb, da, db, df, dgpre[k], dgpost[k] = hosted(
            n, _ffn_bwd, h_in, sv[f"f{which}"], sv[f"a{which}"], sv[f"b{which}"], dh, g_at(lw["g_pre"], k),
            g_at(lw["g_post"], k), lw[n + "_gate"], lw[n + "_up"], lw[n + "_down"])
        big[n + "_gate"] = hosted(n + "_gate", _wgrad, nb, da, N_CHIPS)
        big[n + "_up"] = hosted(n + "_up", _wgrad, nb, db, N_CHIPS)
        big[n + "_down"] = hosted(n + "_down", _wgrad, sb, df, 0).reshape(N_CHIPS, FF // N_CHIPS, D)
        return dh

    dh = ffn_back(2, dh, sv["h2"])
    dmix, dtok, dqm, dkv, dgpost[1] = _mixout_bwd(dh, sv["mix"], sv["proj"], sv["kv"], lw["w_out"],
                                                  g_at(lw["g_post"], 1), _qm_block(i))
    big["w_out"] = _wgrad(sv["cat"], dmix, N_CHIPS)
    dwkv, small["g_mem"] = _memkv_bwd(mem, lw["g_mem"], lw["w_mem_kv"], dkv)
    big["w_mem_kv"] = dwkv.reshape(N_CHIPS, D // N_CHIPS, 2 * DMEM)
    if _is_pool(i):
        dx, small["pool_w"], small["pool_scale"] = _pool_bwd(sv["proj"], dtok, lw["pool_w"], lw["pool_scale"])
        dh, dproj, dgpre[1] = _proj_bwd(sv["h1"], dh, g_at(lw["g_pre"], 1), lw["w_in"], [dx, dqm])
        big["w_in"] = _wgrad(sv["u"], dproj).reshape(N_CHIPS, D // N_CHIPS, DMIX)
    else:
        dq, dk, dv = _sb_bwd(sv["proj"], sv["o"], dtok)
        dh, dproj, dgpre[1] = _proj_bwd(sv["h1"], dh, g_at(lw["g_pre"], 1), lw["w_in"], [dq, dk, dv, dqm])
        big["w_in"] = jnp.transpose(_wgrad(sv["u"], dproj).reshape(D, N_CHIPS, DSB // N_CHIPS), (1, 0, 2))
    dh = ffn_back(1, dh, sv["h0"])
    small["g_pre"] = jnp.concatenate(dgpre, axis=0)
    small["g_post"] = jnp.concatenate(dgpost, axis=0)
    return dh, big, small, rode


ANY = pl.BlockSpec(memory_space=pl.ANY)


def _place():
    return lax.axis_index("x"), lax.axis_index("y"), lax.axis_index("c")


def _flip(v, bit):
    return 1 - v if bit else v


def _small_allgather(buf):
    R = buf.shape[0]

    def body(x_ref, o_ref, send_sems, recv_sems, loc_sem):
        x, y, c = _place()
        me = 4 * x + 2 * y + c
        loc = pltpu.make_async_copy(x_ref, o_ref.at[me], loc_sem)
        loc.start()
        sends = []
        for k in range(1, N_DEV):
            peer = (_flip(x, k & 4), _flip(y, k & 2), _flip(c, k & 1))
            cp = pltpu.make_async_remote_copy(src_ref=x_ref, dst_ref=o_ref.at[me], send_sem=send_sems.at[k - 1],
                                              recv_sem=recv_sems.at[k - 1], device_id=peer, device_id_type=MESH)
            cp.start()
            sends.append(cp)
        for k in range(1, N_DEV):
            peer = (_flip(x, k & 4), _flip(y, k & 2), _flip(c, k & 1))
            slot = 4 * peer[0] + 2 * peer[1] + peer[2]
            pltpu.make_async_remote_copy(src_ref=x_ref, dst_ref=o_ref.at[slot], send_sem=send_sems.at[k - 1],
                                         recv_sem=recv_sems.at[k - 1], device_id=peer, device_id_type=MESH).wait_recv()
        for cp in sends:
            cp.wait_send()
        loc.wait()

    return pl.pallas_call(
        body, name="small_allgather", in_specs=[ANY], out_specs=ANY,
        out_shape=jax.ShapeDtypeStruct((N_DEV, R, 128), buf.dtype),
        scratch_shapes=[pltpu.SemaphoreType.DMA((N_DEV - 1,)), pltpu.SemaphoreType.DMA((N_DEV - 1,)),
                        pltpu.SemaphoreType.DMA],
    )(buf)


def _dma_sems(m):
    return pltpu.SemaphoreType.DMA((m,))


class _GatherExchange:
    name = "gather_weights"

    def __init__(self, shards):
        n = len(shards)
        self.n, self.inputs = n, list(shards)
        self.halves = [s.shape[0] // 2 for s in shards]
        self.out_shapes = [jax.ShapeDtypeStruct((N_CHIPS,) + s.shape, s.dtype) for s in shards]
        self.sems = [_dma_sems(3 * n), _dma_sems(3 * n), _dma_sems(3 * n), _dma_sems(3 * n), _dma_sems(2 * n)]

    def _copies(self, ins, outs, sems):
        ici_send, ici_recv, d2d_send, d2d_recv, loc_sems = sems
        x, y, c = _place()
        me = 2 * x + y
        sib = (x, y, 1 - c)
        chips = [(_flip(x, k & 2), _flip(y, k & 1)) for k in (1, 2, 3)]
        half = lambda t, slot, hc: outs[t].at[slot, pl.ds(hc * self.halves[t], self.halves[t])]
        locs, ici_out, ici_in, fwd_out, fwd_in = [], [], [], [], []
        for t in range(self.n):
            for hc in range(2):
                rows = pl.ds(hc * self.halves[t], self.halves[t])
                locs.append(pltpu.make_async_copy(ins[t].at[rows], outs[t].at[me, rows], loc_sems.at[2 * t + hc]))
            for k, chip in enumerate(chips):
                idx = 2 * chip[0] + chip[1]
                peer = (chip[0], chip[1], c)
                ici = dict(send_sem=ici_send.at[3 * t + k], recv_sem=ici_recv.at[3 * t + k], device_id=peer,
                           device_id_type=MESH)
                d2d = dict(send_sem=d2d_send.at[3 * t + k], recv_sem=d2d_recv.at[3 * t + k], device_id=sib,
                           device_id_type=MESH)
                ici_out.append(pltpu.make_async_remote_copy(
                    src_ref=ins[t].at[pl.ds(c * self.halves[t], self.halves[t])], dst_ref=half(t, me, c), **ici))
                ici_in.append(pltpu.make_async_remote_copy(src_ref=half(t, idx, c), dst_ref=half(t, idx, c), **ici))
                fwd_out.append(pltpu.make_async_remote_copy(src_ref=half(t, idx, c), dst_ref=half(t, idx, c), **d2d))
                fwd_in.append(pltpu.make_async_remote_copy(
                    src_ref=half(t, idx, 1 - c), dst_ref=half(t, idx, 1 - c), **d2d))
        return locs, ici_out, ici_in, fwd_out, fwd_in

    def start(self, ins, outs, sems):
        locs, ici_out, _, _, _ = self._copies(ins, outs, sems)
        for cp in ici_out + locs:
            cp.start()

    def mid(self, ins, outs, sems):
        _, _, ici_in, fwd_out, _ = self._copies(ins, outs, sems)
        for arrived, onward in zip(ici_in, fwd_out):
            arrived.wait_recv()
            onward.start()

    def finish(self, ins, outs, sems):
        locs, ici_out, _, fwd_out, fwd_in = self._copies(ins, outs, sems)
        for cp in fwd_in:
            cp.wait_recv()
        for cp in ici_out + fwd_out:
            cp.wait_send()
        for cp in locs:
            cp.wait()


class _SiblingExchange:
    name = "rs_to_sibling"

    def __init__(self, gs):
        n = len(gs)
        self.n, self.inputs = n, list(gs)
        self.halves = [g.shape[1] // 2 for g in gs]
        self.out_shapes = [jax.ShapeDtypeStruct((N_CHIPS, g.shape[1] // 2) + g.shape[2:], g.dtype) for g in gs]
        self.sems = [_dma_sems(n), _dma_sems(n)]

    def _copies(self, ins, outs, sems):
        x, y, c = _place()
        return [pltpu.make_async_remote_copy(
            src_ref=ins[t].at[:, pl.ds((1 - c) * self.halves[t], self.halves[t])], dst_ref=outs[t],
            send_sem=sems[0].at[t], recv_sem=sems[1].at[t], device_id=(x, y, 1 - c), device_id_type=MESH)
            for t in range(self.n)]

    def start(self, ins, outs, sems):
        for cp in self._copies(ins, outs, sems):
            cp.start()

    def mid(self, ins, outs, sems):
        pass

    def finish(self, ins, outs, sems):
        for cp in self._copies(ins, outs, sems):
            cp.wait()


class _ChipsExchange:
    name = "rs_to_chips"

    def __init__(self, ss):
        n = len(ss)
        self.n, self.inputs = n, list(ss)
        self.out_shapes = [jax.ShapeDtypeStruct((3,) + s.shape[1:], s.dtype) for s in ss]
        self.sems = [_dma_sems(3 * n), _dma_sems(3 * n)]

    def _copies(self, ins, outs, sems):
        x, y, c = _place()
        chips = [(_flip(x, k & 2), _flip(y, k & 1)) for k in (1, 2, 3)]
        return [pltpu.make_async_remote_copy(
            src_ref=ins[t].at[2 * chip[0] + chip[1]], dst_ref=outs[t].at[k], send_sem=sems[0].at[3 * t + k],
            recv_sem=sems[1].at[3 * t + k], device_id=(chip[0], chip[1], c), device_id_type=MESH)
            for t in range(self.n) for k, chip in enumerate(chips)]

    def start(self, ins, outs, sems):
        for cp in self._copies(ins, outs, sems):
            cp.start()

    def mid(self, ins, outs, sems):
        pass

    def finish(self, ins, outs, sems):
        for cp in self._copies(ins, outs, sems):
            cp.wait()


class _Both:
    def __init__(self, first, second):
        self.parts = (first, second)
        self.name = first.name + "_" + second.name
        self.inputs = first.inputs + second.inputs
        self.out_shapes = first.out_shapes + second.out_shapes
        self.sems = first.sems + second.sems

    def _each(self, ins, outs, sems):
        a = self.parts[0]
        ni, no, ns = len(a.inputs), len(a.out_shapes), len(a.sems)
        return ((a, ins[:ni], outs[:no], sems[:ns]), (self.parts[1], ins[ni:], outs[no:], sems[ns:]))

    def start(self, ins, outs, sems):
        for ex, i, o, s in self._each(ins, outs, sems):
            ex.start(i, o, s)

    def mid(self, ins, outs, sems):
        for ex, i, o, s in self._each(ins, outs, sems):
            ex.mid(i, o, s)

    def finish(self, ins, outs, sems):
        for ex, i, o, s in self._each(ins, outs, sems):
            ex.finish(i, o, s)

    def split(self, outs):
        no = len(self.parts[0].out_shapes)
        return outs[:no], outs[no:]


def _run_exchange(ex):
    ni, no = len(ex.inputs), len(ex.out_shapes)

    def body(*refs):
        ins, outs, sems = refs[:ni], refs[ni:ni + no], refs[ni + no:]
        ex.start(ins, outs, sems)
        ex.mid(ins, outs, sems)
        ex.finish(ins, outs, sems)

    return pl.pallas_call(
        body, name=ex.name, in_specs=[ANY] * ni, out_specs=[ANY] * no, out_shape=ex.out_shapes,
        scratch_shapes=ex.sems,
    )(*ex.inputs)


def _call(body, name, nsteps, in_specs, out_specs, out_shape, args, ride=None):
    if ride is None:
        return pl.pallas_call(body, name=name, grid=(nsteps,), in_specs=in_specs, out_specs=out_specs,
                              out_shape=out_shape, compiler_params=_cparams(("arbitrary",)))(*args)
    ni, no = len(args), len(out_shape)
    ri, ro = len(ride.inputs), len(ride.out_shapes)
    mid_step = (13 * nsteps) // 16 if nsteps > 1 else 0

    def hosted(*refs):
        a, ra = refs[:ni], refs[ni:ni + ri]
        o, ro_refs = refs[ni + ri:ni + ri + no], refs[ni + ri + no:ni + ri + no + ro]
        sems = refs[ni + ri + no + ro:]
        step = pl.program_id(0)

        @pl.when(step == 0)
        def _():
            ride.start(ra, ro_refs, sems)

        body(*a, *o)

        @pl.when(step == mid_step)
        def _():
            ride.mid(ra, ro_refs, sems)

        @pl.when(step == nsteps - 1)
        def _():
            ride.finish(ra, ro_refs, sems)

    outs = pl.pallas_call(
        hosted, name=name + "_" + ride.name, grid=(nsteps,), in_specs=list(in_specs) + [ANY] * ri,
        out_specs=list(out_specs) + [ANY] * ro, out_shape=list(out_shape) + ride.out_shapes,
        scratch_shapes=ride.sems, compiler_params=_cparams(("arbitrary",)),
    )(*args, *ride.inputs)
    return outs[:no], outs[no:]


def _share_halves(ts):
    n = len(ts)

    def body(*refs):
        outs = refs[n:2 * n]
        send_sems, recv_sems = refs[2 * n:]
        x, y, c = _place()
        cps = []
        for t in range(n):
            cp = pltpu.make_async_remote_copy(
                src_ref=outs[t].at[c], dst_ref=outs[t].at[c], send_sem=send_sems.at[t], recv_sem=recv_sems.at[t],
                device_id=(x, y, 1 - c), device_id_type=MESH)
            cp.start()
            cps.append(cp)
        for t in range(n):
            pltpu.make_async_remote_copy(
                src_ref=outs[t].at[1 - c], dst_ref=outs[t].at[1 - c], send_sem=send_sems.at[t],
                recv_sem=recv_sems.at[t], device_id=(x, y, 1 - c), device_id_type=MESH).wait_recv()
        for cp in cps:
            cp.wait_send()

    dma = lambda m: pltpu.SemaphoreType.DMA((m,))
    return pl.pallas_call(
        body, name="share_halves", in_specs=[ANY] * n, out_specs=[ANY] * n,
        out_shape=[jax.ShapeDtypeStruct(t.shape, t.dtype) for t in ts],
        input_output_aliases={t: t for t in range(n)},
        scratch_shapes=[dma(n), dma(n)],
    )(*ts)


def _by_shape(fn, *lists):
    groups = {}
    for idx, a in enumerate(lists[0]):
        groups.setdefault(a.shape, []).append(idx)
    out = [None] * len(lists[0])
    for idxs in groups.values():
        for idx, res in zip(idxs, fn(*[[lst[idx] for idx in idxs] for lst in lists])):
            out[idx] = res
    return out


def _add_sibling(gs, rs, c):
    m = len(gs)
    _, r_, n_ = gs[0].shape
    h = r_ // 2
    tr = _row_tile(h, 512)

    def body(c_ref, *refs):
        for g_ref, r_ref, o_ref in zip(refs[:m], refs[m:2 * m], refs[2 * m:]):
            o_ref[...] = (g_ref[...] + r_ref[...]).astype(o_ref.dtype)

    gspec = pl.BlockSpec((None, None, tr, n_), lambda s, i, c_ref: (s, c_ref[0], i, 0))
    hspec = pl.BlockSpec((None, tr, n_), lambda s, i, c_ref: (s, i, 0))
    return pl.pallas_call(
        body, name="add_sibling",
        grid_spec=pltpu.PrefetchScalarGridSpec(
            num_scalar_prefetch=1, grid=(N_CHIPS, h // tr), in_specs=[gspec] * m + [hspec] * m, out_specs=[hspec] * m),
        out_shape=[jax.ShapeDtypeStruct((N_CHIPS, h, n_), _MM)] * m,
        compiler_params=_cparams(("parallel", "parallel")),
    )(c.reshape(1), *[g.reshape(N_CHIPS, 2, h, n_) for g in gs], *rs)


def _add_chips(ss, rs, me, c):
    m = len(ss)
    _, h, n_ = ss[0].shape
    tr = _row_tile(h, 512)

    def body(at_ref, *refs):
        f = lambda ref: ref[...].astype(F32)
        for t in range(m):
            s_ref, (r0, r1, r2), o_ref = refs[t], refs[m + 3 * t:m + 3 * t + 3], refs[4 * m + t]
            o_ref[...] = ((f(s_ref) + f(r0)) + f(r1)) + f(r2)

    rk = lambda k: pl.BlockSpec((None, tr, n_), lambda i, at_ref: (k, i, 0))
    return pl.pallas_call(
        body, name="add_chips",
        grid_spec=pltpu.PrefetchScalarGridSpec(
            num_scalar_prefetch=1, grid=(h // tr,),
            in_specs=[pl.BlockSpec((None, tr, n_), lambda i, at_ref: (at_ref[0], i, 0))] * m + [rk(0), rk(1), rk(2)] * m,
            out_specs=[pl.BlockSpec((None, tr, n_), lambda i, at_ref: (at_ref[1], i, 0))] * m),
        out_shape=[jax.ShapeDtypeStruct((2, h, n_), F32)] * m,
        compiler_params=_cparams(("parallel",)),
    )(jnp.stack([me, c]), *ss, *[r for r in rs for _ in range(3)])


def _sum_devices(a):
    R_ = a.shape[1]

    def body(a_ref, o_ref):
        acc = a_ref[0]
        for k in range(1, N_DEV):
            acc = acc + a_ref[k]
        o_ref[...] = acc

    return pl.pallas_call(
        body, name="sum_devices", out_shape=jax.ShapeDtypeStruct((R_, 128), F32),
        compiler_params=pltpu.CompilerParams(vmem_limit_bytes=VMEM_LIMIT),
    )(a)


def _adamw(w, g, m, v):
    shape = w.shape
    to2 = lambda a: a.reshape(-1, shape[-1])
    w2, g2, m2, v2 = to2(w), to2(g), to2(m), to2(v)
    R_, C = w2.shape
    tr = 256 if R_ % 256 == 0 else R_

    def body(w_ref, g_ref, m_ref, v_ref, d_ref, mo_ref, vo_ref):
        gv = g_ref[...]
        mn = ADAM_B1 * m_ref[...] + (1.0 - ADAM_B1) * gv
        vn = ADAM_B2 * v_ref[...] + (1.0 - ADAM_B2) * (gv * gv)
        m_hat = mn / (1.0 - ADAM_B1 ** ADAM_STEP)
        v_hat = vn / (1.0 - ADAM_B2 ** ADAM_STEP)
        d_ref[...] = -ADAM_LR * (m_hat / (jnp.sqrt(v_hat) + ADAM_EPS) + ADAM_WD * w_ref[...])
        mo_ref[...] = mn
        vo_ref[...] = vn

    spec = pl.BlockSpec((tr, C), lambda i: (i, 0))
    outs = pl.pallas_call(
        body, name="adamw", grid=(R_ // tr,), in_specs=[spec] * 4, out_specs=[spec] * 3,
        out_shape=[jax.ShapeDtypeStruct((R_, C), F32)] * 3,
        compiler_params=_cparams(("parallel",)),
    )(w2, g2, m2, v2)
    return [o.reshape(shape) for o in outs]


SHARD_AXIS = {"g_pre": 2, "g_post": 2, "g_mem": None, "ffn1_gate": 2, "ffn1_up": 2, "ffn1_down": 1,
              "ffn2_gate": 2, "ffn2_up": 2, "ffn2_down": 1, "w_in_pool": 1, "pool_w": None, "pool_scale": None,
              "w_in_sb": 2, "w_mem_kv": 1, "w_out": 2}
WEIGHTS = list(SHARD_AXIS)
BIG = [k for k in WEIGHTS if SHARD_AXIS[k] is not None and k not in ("g_pre", "g_post")]
SMALL = [k for k in WEIGHTS if k not in BIG]


def _pack128(arrs):
    flat = jnp.concatenate([a.reshape(-1) for a in arrs])
    pad = (-flat.shape[0]) % (8 * 128)
    return jnp.pad(flat, (0, pad)).reshape(-1, 128)


def _unpack128(buf, shapes):
    flat = buf.reshape(-1)
    out, off = [], 0
    for shp in shapes:
        size = 1
        for d in shp:
            size *= d
        out.append(flat[off:off + size].reshape(shp))
        off += size
    return out


def kernel(x, mem, g_pre, g_post, g_mem, ffn1_gate, ffn1_up, ffn1_down, ffn2_gate, ffn2_up, ffn2_down, w_in_pool, pool_w, pool_scale, w_in_sb, w_mem_kv, w_out, loss_target, m_g_pre, m_g_post, m_g_mem, m_ffn1_gate, m_ffn1_up, m_ffn1_down, m_ffn2_gate, m_ffn2_up, m_ffn2_down, m_w_in_pool, m_pool_w, m_pool_scale, m_w_in_sb, m_w_mem_kv, m_w_out, v_g_pre, v_g_post, v_g_mem, v_ffn1_gate, v_ffn1_up, v_ffn1_down, v_ffn2_gate, v_ffn2_up, v_ffn2_down, v_w_in_pool, v_pool_w, v_pool_scale, v_w_in_sb, v_w_mem_kv, v_w_out):
    w = dict(g_pre=g_pre, g_post=g_post, g_mem=g_mem, ffn1_gate=ffn1_gate, ffn1_up=ffn1_up, ffn1_down=ffn1_down,
             ffn2_gate=ffn2_gate, ffn2_up=ffn2_up, ffn2_down=ffn2_down, w_in_pool=w_in_pool, pool_w=pool_w,
             pool_scale=pool_scale, w_in_sb=w_in_sb, w_mem_kv=w_mem_kv, w_out=w_out)
    m = dict(g_pre=m_g_pre, g_post=m_g_post, g_mem=m_g_mem, ffn1_gate=m_ffn1_gate, ffn1_up=m_ffn1_up,
             ffn1_down=m_ffn1_down, ffn2_gate=m_ffn2_gate, ffn2_up=m_ffn2_up, ffn2_down=m_ffn2_down,
             w_in_pool=m_w_in_pool, pool_w=m_pool_w, pool_scale=m_pool_scale, w_in_sb=m_w_in_sb,
             w_mem_kv=m_w_mem_kv, w_out=m_w_out)
    v = dict(g_pre=v_g_pre, g_post=v_g_post, g_mem=v_g_mem, ffn1_gate=v_ffn1_gate, ffn1_up=v_ffn1_up,
             ffn1_down=v_ffn1_down, ffn2_gate=v_ffn2_gate, ffn2_up=v_ffn2_up, ffn2_down=v_ffn2_down,
             w_in_pool=v_w_in_pool, pool_w=v_pool_w, pool_scale=v_pool_scale, w_in_sb=v_w_in_sb,
             w_mem_kv=v_w_mem_kv, w_out=v_w_out)
    cx, cy, cc = _place()
    chip = (2 * cx + cy).astype(jnp.int32)
    core = cc.astype(jnp.int32)

    gains = _small_allgather(_pack128([g_pre, g_post]))
    per_chip = [_unpack128(gains[2 * j], [g_pre.shape, g_post.shape]) for j in range(N_CHIPS)]
    full_pre = jnp.concatenate([p[0] for p in per_chip], axis=2)
    full_post = jnp.concatenate([p[1] for p in per_chip], axis=2)

    def shard_name(k, i):
        return ("w_in_pool" if _is_pool(i) else "w_in_sb") if k == "w_in" else k

    def own(i, names):
        return [w[shard_name(k, i)][i // 2 if k == "w_in" else i].astype(_MM) for k in names]

    xs, mems = x[0], mem[0]
    h, lws, saved = xs, [], []
    first = _run_exchange(_GatherExchange(own(0, FIRST_FFN)))
    for i in range(DEPTH):
        lw = dict(zip(FIRST_FFN, first))
        lw["g_pre"], lw["g_post"], lw["g_mem"] = full_pre[i], full_post[i], g_mem[i].reshape(1, D)
        if _is_pool(i):
            lw["pool_w"], lw["pool_scale"] = pool_w[i // 2], pool_scale[i // 2].reshape(1, DTOK)
        h, sv, first = _layer_forward(h, mems, lw, i, own(i, REST), own(i + 1, FIRST_FFN) if i + 1 < DEPTH else None)
        lws.append(lw)
        saved.append(sv)
    acc, dh = _loss_grad(h, loss_target[0])
    loss = lax.psum(0.5 / D * jnp.sum(acc), ("x", "y", "c"))

    mine, smalls = [None] * DEPTH, [None] * DEPTH
    above, sums = None, {}
    nt = len(LAYER_TENSORS)
    early = ("ffn2_gate", "w_mem_kv", "ffn2_up", "w_out", "ffn2_down", "w_in")
    add_sib = lambda gs, from_sib: _by_shape(lambda g, r: _add_sibling(g, r, core), gs, from_sib)
    add_chp = lambda ss, from_chips: _by_shape(lambda s, r: _add_chips(s, r, chip, core), ss, from_chips)

    for i in reversed(range(DEPTH)):
        rides = {}
        if above is not None:
            up = [above[k] for k in LAYER_TENSORS]
            rides["ffn2"] = lambda big, rode, up=up: _SiblingExchange(up)

            def on_ffn1(big, rode, up=up, layer=i + 1, bottom=(i == 0)):
                sums[layer] = add_sib(up, rode["ffn2"])
                ex = _ChipsExchange(sums[layer])
                return _Both(ex, _SiblingExchange([big[k] for k in early])) if bottom else ex

            def on_wgrad(big, rode, j):
                if "early" not in sums:
                    sums["early"] = add_sib([big[k] for k in early], rode["ffn1"][nt:])
                return _ChipsExchange(sums["early"][2 * j:2 * j + 2])

            rides["ffn1"] = on_ffn1
            if i == 0:
                for j, host in enumerate(FIRST_FFN):
                    rides[host] = lambda big, rode, j=j: on_wgrad(big, rode, j)
        dh, gs, smalls[i], rode = _layer_backward(dh, mems, lws[i], saved[i], i, rides)
        if above is not None:
            mine[i + 1] = add_chp(sums[i + 1], rode["ffn1"][:nt])
        above = gs
    early_mine = add_chp(sums["early"], [t for host in FIRST_FFN for t in rode[host]])
    late = [above[k] for k in FIRST_FFN]
    late_sums = add_sib(late, _run_exchange(_SiblingExchange(late)))
    late_mine = add_chp(late_sums, _run_exchange(_ChipsExchange(late_sums)))
    by_name = dict(zip(early + FIRST_FFN, early_mine + late_mine))
    mine[0] = [by_name[k] for k in LAYER_TENSORS]
    shared = _share_halves([t for layer in mine for t in layer])
    reduced = [dict(zip(LAYER_TENSORS, [s.reshape(-1, s.shape[-1]) for s in shared[nt * i:nt * (i + 1)]]))
               for i in range(DEPTH)]
    dx = dh
    red = {}
    for k in BIG:
        if k.startswith("w_in_"):
            layers = [i for i in range(DEPTH) if _is_pool(i) == (k == "w_in_pool")]
            red[k] = jnp.stack([reduced[i]["w_in"] for i in layers])
        else:
            red[k] = jnp.stack([reduced[i][k] for i in range(DEPTH)])

    grads = {"g_pre": jnp.stack([s["g_pre"] for s in smalls]), "g_post": jnp.stack([s["g_post"] for s in smalls]),
             "g_mem": jnp.concatenate([s["g_mem"] for s in smalls], axis=0),
             "pool_w": jnp.stack([smalls[i]["pool_w"] for i in range(DEPTH) if _is_pool(i)]),
             "pool_scale": jnp.concatenate([smalls[i]["pool_scale"] for i in range(DEPTH) if _is_pool(i)], axis=0)}
    small_shapes = [grads[k].shape for k in SMALL]
    summed = _unpack128(_sum_devices(_small_allgather(_pack128([grads[k] for k in SMALL]))), small_shapes)
    for k, s in zip(SMALL, summed):
        if SHARD_AXIS[k] is None:
            red[k] = s
        else:
            red[k] = lax.dynamic_slice_in_dim(s, chip * w[k].shape[2], w[k].shape[2], axis=2)

    deltas, new_m, new_v = {}, {}, {}
    for k in WEIGHTS:
        deltas[k], new_m[k], new_v[k] = _adamw(w[k], red[k], m[k], v[k])
    return (loss, dx.reshape(x.shape), *[red[k] for k in WEIGHTS], *[deltas[k] for k in WEIGHTS],
            *[new_m[k] for k in WEIGHTS], *[new_v[k] for k in WEIGHTS])
```

```python
import jax
import jax.numpy as jnp
from jax import lax
from jax.experimental import pallas as pl
from jax.experimental.pallas import tpu as pltpu

F32 = jnp.float32
BF16 = jnp.bfloat16
_MM = jnp.bfloat16

D = 1024
FF = 2048
DTOK = 512
DMEM = 256
DMIX = DTOK + DMEM
DSB = 3 * DTOK + DMEM
HD = 64
MEM_LEN = 256
DEPTH = 4
EPS = 1e-6
WINDOWS = (2, 4, 8, 16)
HALO = 16
PG = 128
QB = 128
SCALE = HD ** -0.5
FC = 512
NEG_CUT = -104.0

ADAM_LR, ADAM_B1, ADAM_B2, ADAM_EPS, ADAM_WD, ADAM_STEP = 0.001, 0.9, 0.999, 1e-08, 0.01, 10

VMEM_LIMIT = 56 * 2 ** 20
MESH = pl.DeviceIdType.MESH
N_CHIPS = 4
N_DEV = 8


def _cparams(sem):
    return pltpu.CompilerParams(dimension_semantics=sem, vmem_limit_bytes=VMEM_LIMIT)


def _mm(a, b):
    return jnp.dot(a.astype(_MM), b.astype(_MM), preferred_element_type=F32)


def _mm_nt(a, b):
    return lax.dot_general(a.astype(_MM), b.astype(_MM), (((1,), (1,)), ((), ())), preferred_element_type=F32)


def _mm_tn(a, b):
    return lax.dot_general(a.astype(_MM), b.astype(_MM), (((0,), (0,)), ((), ())), preferred_element_type=F32)


def _mm2(a, b):
    hi = a.astype(_MM)
    lo = (a - hi.astype(F32)).astype(_MM)
    return jnp.dot(hi, b, preferred_element_type=F32) + jnp.dot(lo, b, preferred_element_type=F32)


def _rstd(x):
    return lax.rsqrt(jnp.mean(x * x, axis=-1, keepdims=True) + EPS)


def _rms_bwd(x, g, dy):
    r = _rstd(x)
    xh = x * r
    t = dy * g
    dx = r * (t - xh * jnp.mean(t * xh, axis=-1, keepdims=True))
    return dx, jnp.sum(dy * xh, axis=0, keepdims=True)


def _sigmoid(a):
    return 1.0 / (1.0 + jnp.exp(-a))


def _row_tile(n, want):
    t = min(n, want)
    assert n % t == 0, (n, t)
    return t


def _ffn_fwd(h, g1, g2, wg, wu, wd, ride=None):
    S = h.shape[0]
    ts = _row_tile(S, 512)

    def body(h_ref, g1_ref, g2_ref, wg_ref, wu_ref, wd_ref, ho_ref, a_ref, b_ref, f_ref):
        hv = h_ref[...]
        n = (hv * _rstd(hv) * g1_ref[...]).astype(_MM)
        f = jnp.zeros((ts, D), F32)
        for j in range(FF // FC):
            cs = slice(j * FC, (j + 1) * FC)
            a = _mm(n, wg_ref[j])
            b = _mm(n, wu_ref[j])
            a_ref[:, cs] = a.astype(a_ref.dtype)
            b_ref[:, cs] = b.astype(b_ref.dtype)
            f = f + _mm(a * _sigmoid(a) * b, wd_ref[j])
        f_ref[...] = f
        ho_ref[...] = hv + 0.5 * (f * _rstd(f) * g2_ref[...])

    row = lambda i: (i, 0)
    fix = lambda i: (0, 0)
    all3 = lambda i: (0, 0, 0)
    ns = FF // FC
    return _call(
        body, "ffn_fwd", S // ts,
        [pl.BlockSpec((ts, D), row), pl.BlockSpec((1, D), fix), pl.BlockSpec((1, D), fix),
         pl.BlockSpec((ns, D, FC), all3), pl.BlockSpec((ns, D, FC), all3), pl.BlockSpec((ns, FC, D), all3)],
        [pl.BlockSpec((ts, D), row), pl.BlockSpec((ts, FF), row), pl.BlockSpec((ts, FF), row),
         pl.BlockSpec((ts, D), row)],
        [jax.ShapeDtypeStruct((S, D), F32), jax.ShapeDtypeStruct((S, FF), _MM),
         jax.ShapeDtypeStruct((S, FF), _MM), jax.ShapeDtypeStruct((S, D), F32)],
        (h, g1, g2, wg, wu, wd), ride)


def _ffn_bwd(h, f, a, b, dho, g1, g2, wg, wu, wd, ride=None):
    S = h.shape[0]
    ts = _row_tile(S, 256)

    def body(h_ref, f_ref, a_ref, b_ref, dho_ref, g1_ref, g2_ref, wg_ref, wu_ref, wd_ref,
             dh_ref, n_ref, s_ref, da_ref, db_ref, df_ref, dg1_ref, dg2_ref):
        @pl.when(pl.program_id(0) == 0)
        def _():
            dg1_ref[...] = jnp.zeros_like(dg1_ref)
            dg2_ref[...] = jnp.zeros_like(dg2_ref)

        hv = h_ref[...]
        dho = dho_ref[...]
        df, dg2 = _rms_bwd(f_ref[...], g2_ref[...], 0.5 * dho)
        dg2_ref[...] += dg2
        dfb = df.astype(_MM)
        df_ref[...] = dfb
        rh = _rstd(hv)
        hh = hv * rh
        g1 = g1_ref[...]
        n_ref[...] = (hh * g1).astype(_MM)
        dss = [_mm_nt(dfb, wd_ref[j]) for j in range(FF // FC)]
        das, dbs = [], []
        for j in range(FF // FC):
            cs = slice(j * FC, (j + 1) * FC)
            av = a_ref[:, cs].astype(F32)
            bv = b_ref[:, cs].astype(F32)
            sig = _sigmoid(av)
            sl = av * sig
            s_ref[:, cs] = (sl * bv).astype(_MM)
            da = (dss[j] * bv * (sig * (1.0 + av * (1.0 - sig)))).astype(_MM)
            db = (dss[j] * sl).astype(_MM)
            da_ref[:, cs] = da
            db_ref[:, cs] = db
            das.append(da)
            dbs.append(db)
        dn = jnp.zeros((ts, D), F32)
        for j in range(FF // FC):
            dn = dn + _mm_nt(das[j], wg_ref[j]) + _mm_nt(dbs[j], wu_ref[j])
        t = dn * g1
        dh_ref[...] = dho + rh * (t - hh * jnp.mean(t * hh, axis=-1, keepdims=True))
        dg1_ref[...] += jnp.sum(dn * hh, axis=0, keepdims=True)

    row = lambda i: (i, 0)
    fix = lambda i: (0, 0)
    all3 = lambda i: (0, 0, 0)
    ns = FF // FC
    td = pl.BlockSpec((ts, D), row)
    tf = pl.BlockSpec((ts, FF), row)
    gd = pl.BlockSpec((1, D), fix)
    return _call(
        body, "ffn_bwd", S // ts,
        [td, td, tf, tf, td, gd, gd,
         pl.BlockSpec((ns, D, FC), all3), pl.BlockSpec((ns, D, FC), all3), pl.BlockSpec((ns, FC, D), all3)],
        [td, td, tf, tf, tf, td, gd, gd],
        [jax.ShapeDtypeStruct((S, D), F32), jax.ShapeDtypeStruct((S, D), _MM),
         jax.ShapeDtypeStruct((S, FF), _MM), jax.ShapeDtypeStruct((S, FF), _MM),
         jax.ShapeDtypeStruct((S, FF), _MM), jax.ShapeDtypeStruct((S, D), _MM),
         jax.ShapeDtypeStruct((1, D), F32), jax.ShapeDtypeStruct((1, D), F32)],
        (h, f, a, b, dho, g1, g2, wg, wu, wd), ride)


def _wgrad(x, y, nslot=0, ride=None):
    T, M = x.shape
    N = y.shape[1]
    tt = _row_tile(T, 1024)
    cw = N // nslot if nslot else (512 if N % 512 == 0 else 256)
    nc = N // cw
    assert cw * nc == N and cw % 128 == 0, (N, nc)

    def body(x_ref, y_ref, o_ref):
        @pl.when(pl.program_id(0) == 0)
        def _():
            o_ref[...] = jnp.zeros_like(o_ref)

        xt = x_ref[...].T
        for j in range(nc):
            part = jnp.dot(xt, y_ref[:, j * cw:(j + 1) * cw], preferred_element_type=F32)
            if nslot:
                o_ref[j] += part
            else:
                o_ref[:, j * cw:(j + 1) * cw] += part

    oshape = (nslot, M, cw) if nslot else (M, N)
    res = _call(body, "wgrad", T // tt,
                [pl.BlockSpec((tt, M), lambda t: (t, 0)), pl.BlockSpec((tt, N), lambda t: (t, 0))],
                [pl.BlockSpec(oshape, lambda t: (0,) * len(oshape))], [jax.ShapeDtypeStruct(oshape, F32)], (x, y), ride)
    return res[0] if ride is None else (res[0][0], res[1])


def _proj_fwd(h, g, w, out_dtype):
    S = h.shape[0]
    nk, kw, N = w.shape
    ts = _row_tile(S, 256)

    def body(h_ref, g_ref, w_ref, u_ref, p_ref):
        hv = h_ref[...]
        u = (hv * _rstd(hv) * g_ref[...]).astype(_MM)
        u_ref[...] = u
        acc = _mm(u[:, :kw], w_ref[0])
        for j in range(1, nk):
            acc = acc + _mm(u[:, j * kw:(j + 1) * kw], w_ref[j])
        p_ref[...] = acc.astype(p_ref.dtype)

    return pl.pallas_call(
        body, name="proj_fwd", grid=(S // ts,),
        in_specs=[pl.BlockSpec((ts, D), lambda i: (i, 0)), pl.BlockSpec((1, D), lambda i: (0, 0)),
                  pl.BlockSpec((nk, kw, N), lambda i: (0, 0, 0))],
        out_specs=[pl.BlockSpec((ts, D), lambda i: (i, 0)), pl.BlockSpec((ts, N), lambda i: (i, 0))],
        out_shape=[jax.ShapeDtypeStruct((S, D), _MM), jax.ShapeDtypeStruct((S, N), out_dtype)],
        compiler_params=_cparams(("parallel",)),
    )(h, g, w)


def _proj_bwd(h, dho, g, w, pieces):
    S = h.shape[0]
    nk, kw, N = w.shape
    ts = _row_tile(S, 256)
    widths = [p.shape[1] for p in pieces]
    assert sum(widths) == N
    npc = len(pieces)

    def body(*refs):
        h_ref, dho_ref, g_ref, w_ref = refs[:4]
        p_refs = refs[4:4 + npc]
        dh_ref, dp_ref, dg_ref = refs[4 + npc:]

        @pl.when(pl.program_id(0) == 0)
        def _():
            dg_ref[...] = jnp.zeros_like(dg_ref)

        dus = [jnp.zeros((ts, kw), F32) for _ in range(nk)]
        off = 0
        for p_ref, n in zip(p_refs, widths):
            pv = p_ref[...].astype(_MM)
            dp_ref[:, off:off + n] = pv
            for j in range(nk):
                dus[j] = dus[j] + _mm_nt(pv, w_ref[j, :, off:off + n])
            off += n
        du = dus[0] if nk == 1 else jnp.concatenate(dus, axis=-1)
        dx, dg = _rms_bwd(h_ref[...], g_ref[...], du)
        dh_ref[...] = dho_ref[...] + dx
        dg_ref[...] += dg

    row = lambda i: (i, 0)
    return pl.pallas_call(
        body, name="proj_bwd", grid=(S // ts,),
        in_specs=[pl.BlockSpec((ts, D), row), pl.BlockSpec((ts, D), row), pl.BlockSpec((1, D), lambda i: (0, 0)),
                  pl.BlockSpec((nk, kw, N), lambda i: (0, 0, 0))] + [pl.BlockSpec((ts, n), row) for n in widths],
        out_specs=[pl.BlockSpec((ts, D), row), pl.BlockSpec((ts, N), row), pl.BlockSpec((1, D), lambda i: (0, 0))],
        out_shape=[jax.ShapeDtypeStruct((S, D), F32), jax.ShapeDtypeStruct((S, N), _MM),
                   jax.ShapeDtypeStruct((1, D), F32)],
        compiler_params=_cparams(("arbitrary",)),
    )(h, dho, g, w, *pieces)


def _pool_counts(first_row, ts):
    pos = (first_row + lax.broadcasted_iota(jnp.int32, (ts, 1), 0) + 1).astype(F32)
    return [jnp.minimum(pos, float(w)) for w in WINDOWS]


def _pool_delta(x, prev, cnts, ts):
    xe = jnp.concatenate([prev, x], axis=0)
    sums = []
    cur = xe
    for sh in (1, 2, 4, 8):
        cur = cur + pltpu.roll(cur, sh, 0)
        sums.append(cur)
    return [sums[gi][HALO:, gi * PG:(gi + 1) * PG] / cnts[gi] - x[:, gi * PG:(gi + 1) * PG]
            for gi in range(len(WINDOWS))]


def _pool_fwd(proj, pw, ps):
    S = proj.shape[0]
    ts = _row_tile(S, 256)

    def body(x_ref, pw_ref, ps_ref, tok_ref, carry_ref):
        i = pl.program_id(0)

        @pl.when(i == 0)
        def _():
            carry_ref[...] = jnp.zeros_like(carry_ref)

        x = x_ref[...]
        ds = _pool_delta(x, carry_ref[...], _pool_counts(i * ts, ts), ts)
        y = jnp.concatenate([_mm(ds[gi], pw_ref[gi]) for gi in range(len(WINDOWS))], axis=-1)
        tok_ref[...] = (y * ps_ref[...]).astype(tok_ref.dtype)
        carry_ref[...] = x[ts - HALO:, :]

    return pl.pallas_call(
        body, name="pool_fwd", grid=(S // ts,),
        in_specs=[pl.BlockSpec((ts, DTOK), lambda i: (i, 0)), pl.BlockSpec((len(WINDOWS), PG, PG), lambda i: (0, 0, 0)),
                  pl.BlockSpec((1, DTOK), lambda i: (0, 0))],
        out_specs=pl.BlockSpec((ts, DTOK), lambda i: (i, 0)),
        out_shape=jax.ShapeDtypeStruct((S, DTOK), _MM),
        scratch_shapes=[pltpu.VMEM((HALO, DTOK), F32)],
        compiler_params=_cparams(("arbitrary",)),
    )(proj, pw, ps)


def _pool_bwd(proj, dtok, pw, ps):
    S = proj.shape[0]
    ts = _row_tile(S, 256)
    nt = S // ts
    per = ts // HALO
    ng = len(WINDOWS)

    def body(x_ref, xp_ref, dt_ref, pw_ref, ps_ref, dx_ref, dpw_ref, dps_ref, carry_ref):
        i = pl.program_id(0)
        idx = nt - 1 - i

        @pl.when(i == 0)
        def _():
            carry_ref[...] = jnp.zeros_like(carry_ref)
            dpw_ref[...] = jnp.zeros_like(dpw_ref)
            dps_ref[...] = jnp.zeros_like(dps_ref)

        x = x_ref[...]
        prev = jnp.where(idx > 0, xp_ref[...], 0.0)
        cnts = _pool_counts(idx * ts, ts)
        ds = _pool_delta(x, prev, cnts, ts)
        dt = dt_ref[...]
        y = jnp.concatenate([_mm(ds[gi], pw_ref[gi]) for gi in range(ng)], axis=-1)
        dps_ref[...] += jnp.sum(dt * y, axis=0, keepdims=True)
        dy = (dt * ps_ref[...]).astype(_MM)
        dds = []
        for gi in range(ng):
            dyg = dy[:, gi * PG:(gi + 1) * PG]
            dpw_ref[gi] += _mm_tn(ds[gi], dyg)
            dds.append(_mm_nt(dyg, pw_ref[gi]))
        e = jnp.concatenate([dds[gi] / cnts[gi] for gi in range(ng)], axis=-1)
        ee = jnp.concatenate([e, carry_ref[...]], axis=0)
        rows = ts + HALO
        cur = ee
        outs = []
        for gi, sh in enumerate((1, 2, 4, 8)):
            cur = cur + pltpu.roll(cur, rows - sh, 0)
            outs.append(cur[:ts, gi * PG:(gi + 1) * PG] - dds[gi])
        dx_ref[...] = jnp.concatenate(outs, axis=-1)
        carry_ref[...] = e[:HALO, :]

    rev = lambda i: (nt - 1 - i, 0)
    return pl.pallas_call(
        body, name="pool_bwd", grid=(nt,),
        in_specs=[pl.BlockSpec((ts, DTOK), rev),
                  pl.BlockSpec((HALO, DTOK), lambda i: (jnp.maximum((nt - 1 - i) * per - 1, 0), 0)),
                  pl.BlockSpec((ts, DTOK), rev), pl.BlockSpec((ng, PG, PG), lambda i: (0, 0, 0)),
                  pl.BlockSpec((1, DTOK), lambda i: (0, 0))],
        out_specs=[pl.BlockSpec((ts, DTOK), rev), pl.BlockSpec((ng, PG, PG), lambda i: (0, 0, 0)),
                   pl.BlockSpec((1, DTOK), lambda i: (0, 0))],
        out_shape=[jax.ShapeDtypeStruct((S, DTOK), F32), jax.ShapeDtypeStruct((ng, PG, PG), F32),
                   jax.ShapeDtypeStruct((1, DTOK), F32)],
        scratch_shapes=[pltpu.VMEM((HALO, DTOK), F32)],
        compiler_params=_cparams(("arbitrary",)),
    )(proj, proj, dtok, pw, ps)


def _tri(strict):
    r = lax.broadcasted_iota(jnp.int32, (QB, QB), 0)
    c = lax.broadcasted_iota(jnp.int32, (QB, QB), 1)
    return jnp.where(r > c if strict else r >= c, 1.0, 0.0).astype(_MM)


def _sb_terms(z, valid):
    e = jnp.exp(-jnp.abs(z))
    sp = jnp.log(1.0 + e)
    ls = jnp.minimum(z, 0.0) - sp
    lf = jnp.where(valid, jnp.minimum(-z, 0.0) - sp, 0.0)
    return ls, lf, e


def _sb_alive(cs, lo=0):
    top = cs[0][lo:]
    for c in cs[1:]:
        top = jnp.maximum(top, c[lo:])
    return (jnp.max(top) > NEG_CUT).astype(jnp.int32)


def _sb_split(x2, head_a):
    zero = jnp.zeros_like(x2)
    return jnp.where(head_a, x2, zero), jnp.where(head_a, zero, x2)


SB_GROUPS_FWD = 4
SB_GROUPS_BWD = 4
SB_TOP = 48


def _sb_fwd(proj):
    S = proj.shape[0]
    G = min(SB_GROUPS_FWD, S // QB)
    nstep = S // (G * QB)
    npair = DTOK // 128
    nch = 2 * G

    def body(q_ref, k_ref, v_ref, tri_ref, o_ref):
        qb = pl.program_id(1)
        head_a = lax.broadcasted_iota(jnp.int32, (QB, 128), 1) < HD
        qs = []
        for g in range(G):
            qs.extend(_sb_split(q_ref[g * QB:(g + 1) * QB, :], head_a))
        tri = tri_ref[...]
        last = G * qb + G - 1

        def trips(rows):
            qr = [q[:rows] for q in qs]
            row = lax.broadcasted_iota(jnp.int32, (rows, QB), 0)
            col = lax.broadcasted_iota(jnp.int32, (rows, QB), 1)

            def step(carry):
                i, _, cs, accs = carry
                causal = (col - row) < i * QB
                ks, vs, valids = [], [], []
                for g in range(G):
                    kb = G * qb + g - i
                    off = pl.multiple_of(jnp.maximum(kb, 0) * QB, QB)
                    ks.append(k_ref[pl.ds(off, QB), :])
                    vs.append(v_ref[pl.ds(off, QB), :])
                    valids.append(jnp.logical_and(causal, kb >= 0))
                zs = [_mm_nt(qr[ch], ks[ch // 2]) * SCALE for ch in range(nch)]
                terms = [_sb_terms(zs[ch], valids[ch // 2]) for ch in range(nch)]
                withins = [_mm2(terms[ch][1], tri) for ch in range(nch)]
                new_c, new_acc = [], []
                for ch in range(nch):
                    ls, lf, _ = terms[ch]
                    a = jnp.where(valids[ch // 2], jnp.exp(ls + withins[ch] + cs[ch]), 0.0)
                    new_acc.append(accs[ch] + _mm2(a, vs[ch // 2]))
                    new_c.append(cs[ch] + jnp.sum(lf, axis=-1, keepdims=True))
                return i + 1, _sb_alive(new_c, SB_TOP if rows == QB else 0), tuple(new_c), tuple(new_acc)

            return step

        zc = jnp.zeros((QB, 1), F32)
        za = jnp.zeros((QB, 128), F32)
        more = lambda cr: jnp.logical_and(cr[0] <= last, cr[1] > 0)
        i, _, cs, accs = lax.while_loop(more, trips(QB), (0, 1, (zc,) * nch, (za,) * nch))
        cs_top = tuple(c[:SB_TOP] for c in cs)
        acc_top = tuple(a[:SB_TOP] for a in accs)
        acc_top = lax.while_loop(more, trips(SB_TOP), (i, _sb_alive(cs_top), cs_top, acc_top))[3]
        for g in range(G):
            o_ref[g * QB:g * QB + SB_TOP, :] = jnp.where(head_a[:SB_TOP], acc_top[2 * g], acc_top[2 * g + 1])
            o_ref[g * QB + SB_TOP:(g + 1) * QB, :] = jnp.where(head_a, accs[2 * g], accs[2 * g + 1])[SB_TOP:]

    return pl.pallas_call(
        body, name="sb_fwd", grid=(npair, nstep),
        in_specs=[pl.BlockSpec((G * QB, 128), lambda p, i: (i, p)),
                  pl.BlockSpec((S, 128), lambda p, i: (0, npair + p)),
                  pl.BlockSpec((S, 128), lambda p, i: (0, 2 * npair + p)),
                  pl.BlockSpec((QB, QB), lambda p, i: (0, 0))],
        out_specs=pl.BlockSpec((G * QB, 128), lambda p, i: (i, p)),
        out_shape=jax.ShapeDtypeStruct((S, DTOK), F32),
        compiler_params=_cparams(("parallel", "parallel")),
    )(proj, proj, proj, _tri(True))


def _sb_bwd(proj, o, do):
    S = proj.shape[0]
    G = min(SB_GROUPS_BWD, S // QB)
    nstep = S // (G * QB)
    npair = DTOK // 128
    nch = 2 * G

    def body(q_ref, k_ref, v_ref, o_ref, do_ref, tri_ref, tri2_ref, dq_ref, dk_ref, dv_ref):
        qb = pl.program_id(1)

        @pl.when(qb == 0)
        def _():
            dk_ref[...] = jnp.zeros_like(dk_ref)
            dv_ref[...] = jnp.zeros_like(dv_ref)

        head_a = lax.broadcasted_iota(jnp.int32, (QB, 128), 1) < HD
        qs, dos, gs = [], [], []
        for g in range(G):
            rows = slice(g * QB, (g + 1) * QB)
            qs.extend(_sb_split(q_ref[rows, :], head_a))
            dob = do_ref[rows, :].astype(_MM)
            dos.extend(_sb_split(dob, head_a))
            go = dob.astype(F32) * o_ref[rows, :]
            gs.append(jnp.sum(jnp.where(head_a, go, 0.0), axis=-1, keepdims=True))
            gs.append(jnp.sum(jnp.where(head_a, 0.0, go), axis=-1, keepdims=True))
        tri = tri_ref[...]
        tri2 = tri2_ref[...]
        last = G * qb + G - 1

        def trips(rows):
            qr = [q[:rows] for q in qs]
            dor = [d[:rows] for d in dos]
            gr = [g_[:rows] for g_ in gs]
            row = lax.broadcasted_iota(jnp.int32, (rows, QB), 0)
            col = lax.broadcasted_iota(jnp.int32, (rows, QB), 1)

            def step(carry):
                i, _, cs, rs, dqs = carry
                causal = (col - row) < i * QB
                offs, ks, vs, valids = [], [], [], []
                for g in range(G):
                    kb = G * qb + g - i
                    off = pl.multiple_of(jnp.maximum(kb, 0) * QB, QB)
                    offs.append(off)
                    ks.append(k_ref[pl.ds(off, QB), :])
                    vs.append(v_ref[pl.ds(off, QB), :])
                    valids.append(jnp.logical_and(causal, kb >= 0))
                zs = [_mm_nt(qr[ch], ks[ch // 2]) * SCALE for ch in range(nch)]
                das = [_mm_nt(dor[ch], vs[ch // 2]) for ch in range(nch)]
                terms = [_sb_terms(zs[ch], valids[ch // 2]) for ch in range(nch)]
                withins = [_mm2(terms[ch][1], tri) for ch in range(nch)]
                a_s, dlws = [], []
                for ch in range(nch):
                    a = jnp.where(valids[ch // 2], jnp.exp(terms[ch][0] + withins[ch] + cs[ch]), 0.0)
                    a_s.append(a)
                    dlws.append(das[ch] * a)
                sfx = [_mm2(dlws[ch], tri2) for ch in range(nch)]
                dzs = []
                for ch in range(nch):
                    e = terms[ch][2]
                    inv = 1.0 / (1.0 + e)
                    pos = zs[ch] >= 0.0
                    beta = jnp.where(pos, 1.0, e) * inv
                    omb = jnp.where(pos, e, 1.0) * inv
                    prefix = gr[ch] - rs[ch] - sfx[ch]
                    dzs.append((jnp.where(valids[ch // 2], dlws[ch] * omb - beta * prefix, 0.0) * SCALE).astype(_MM))
                new_dq = [dqs[ch] + _mm(dzs[ch], ks[ch // 2]) for ch in range(nch)]
                for g in range(G):
                    a, b = 2 * g, 2 * g + 1
                    dk_ref[pl.ds(offs[g], QB), :] += _mm_tn(dzs[a], qr[a]) + _mm_tn(dzs[b], qr[b])
                    dv_ref[pl.ds(offs[g], QB), :] += _mm_tn(a_s[a], dor[a]) + _mm_tn(a_s[b], dor[b])
                new_c = [cs[ch] + jnp.sum(terms[ch][1], axis=-1, keepdims=True) for ch in range(nch)]
                new_r = [rs[ch] + jnp.sum(dlws[ch], axis=-1, keepdims=True) for ch in range(nch)]
                alive = _sb_alive(new_c, SB_TOP if rows == QB else 0)
                return i + 1, alive, tuple(new_c), tuple(new_r), tuple(new_dq)

            return step

        zc = jnp.zeros((QB, 1), F32)
        za = jnp.zeros((QB, 128), F32)
        more = lambda cr: jnp.logical_and(cr[0] <= last, cr[1] > 0)
        i, _, cs, rs, dqs = lax.while_loop(more, trips(QB), (0, 1, (zc,) * nch, (zc,) * nch, (za,) * nch))
        top = lambda xs: tuple(x[:SB_TOP] for x in xs)
        cs_top = top(cs)
        dq_top = lax.while_loop(more, trips(SB_TOP), (i, _sb_alive(cs_top), cs_top, top(rs), top(dqs)))[4]
        for g in range(G):
            dq_ref[g * QB:g * QB + SB_TOP, :] = jnp.where(head_a[:SB_TOP], dq_top[2 * g], dq_top[2 * g + 1])
            dq_ref[g * QB + SB_TOP:(g + 1) * QB, :] = jnp.where(head_a, dqs[2 * g], dqs[2 * g + 1])[SB_TOP:]

    qspec = pl.BlockSpec((G * QB, 128), lambda p, i: (i, p))
    full = lambda base: pl.BlockSpec((S, 128), lambda p, i: (0, base + p))
    tspec = pl.BlockSpec((QB, QB), lambda p, i: (0, 0))
    return pl.pallas_call(
        body, name="sb_bwd", grid=(npair, nstep),
        in_specs=[qspec, full(npair), full(2 * npair), qspec, qspec, tspec, tspec],
        out_specs=[qspec, full(0), full(0)],
        out_shape=[jax.ShapeDtypeStruct((S, DTOK), F32)] * 3,
        compiler_params=_cparams(("parallel", "arbitrary")),
    )(proj, proj, proj, o, do, _tri(True), _tri(False))


def _memkv_fwd(mem, gm, wkv):
    nk, kw, _ = wkv.shape

    def body(m_ref, g_ref, w_ref, kv_ref):
        m = m_ref[...]
        mn = (m * _rstd(m) * g_ref[...]).astype(_MM)
        acc = _mm(mn[:, :kw], w_ref[0])
        for j in range(1, nk):
            acc = acc + _mm(mn[:, j * kw:(j + 1) * kw], w_ref[j])
        kv_ref[...] = acc.astype(kv_ref.dtype)

    return pl.pallas_call(
        body, name="memkv_fwd", out_shape=jax.ShapeDtypeStruct((MEM_LEN, 2 * DMEM), _MM),
        compiler_params=pltpu.CompilerParams(vmem_limit_bytes=VMEM_LIMIT),
    )(mem, gm, wkv)


def _memkv_bwd(mem, gm, wkv, dkv):
    nk, kw, _ = wkv.shape

    def body(m_ref, g_ref, w_ref, dkv_ref, dw_ref, dg_ref):
        m = m_ref[...]
        mh = m * _rstd(m)
        mn = (mh * g_ref[...]).astype(_MM)
        dkv = dkv_ref[...].astype(_MM)
        dw_ref[...] = _mm_tn(mn, dkv)
        dmn = jnp.concatenate([_mm_nt(dkv, w_ref[j]) for j in range(nk)], axis=-1)
        dg_ref[...] = jnp.sum(dmn * mh, axis=0, keepdims=True)

    return pl.pallas_call(
        body, name="memkv_bwd",
        out_shape=[jax.ShapeDtypeStruct((D, 2 * DMEM), F32), jax.ShapeDtypeStruct((1, D), F32)],
        compiler_params=pltpu.CompilerParams(vmem_limit_bytes=VMEM_LIMIT),
    )(mem, gm, wkv, dkv)


def _mixout_fwd(h, tok, proj, kv, wo, g, qm_block):
    S = h.shape[0]
    ts = _row_tile(S, 256)

    def body(h_ref, tok_ref, qm_ref, kv_ref, wo_ref, g_ref, ho_ref, cat_ref, mix_ref):
        head_a = lax.broadcasted_iota(jnp.int32, (ts, 128), 1) < HD
        qm = qm_ref[...].astype(_MM)
        zq = jnp.zeros((ts, 128), _MM)
        npair = DMEM // 128
        qhs = []
        for pi in range(npair):
            qp = qm[:, pi * 128:(pi + 1) * 128]
            qhs += [jnp.where(head_a, qp, zq), jnp.where(head_a, zq, qp)]
        scores = [_mm_nt(qhs[h], kv_ref[:, (h // 2) * 128:(h // 2 + 1) * 128]) * SCALE for h in range(2 * npair)]
        ps = []
        for s in scores:
            e = jnp.exp(s - jnp.max(s, axis=-1, keepdims=True))
            ps.append(e / jnp.sum(e, axis=-1, keepdims=True))
        pvs = [_mm(ps[h], kv_ref[:, DMEM + (h // 2) * 128:DMEM + (h // 2 + 1) * 128]) for h in range(2 * npair)]
        mos = [jnp.where(head_a, pvs[2 * pi], pvs[2 * pi + 1]) for pi in range(npair)]
        cat = jnp.concatenate([tok_ref[...].astype(_MM)] + [m.astype(_MM) for m in mos], axis=-1)
        cat_ref[...] = cat
        mix = jnp.concatenate([_mm(cat, wo_ref[j]) for j in range(N_CHIPS)], axis=-1)
        mix_ref[...] = mix
        ho_ref[...] = h_ref[...] + mix * _rstd(mix) * g_ref[...]

    row = lambda i: (i, 0)
    return pl.pallas_call(
        body, name="mixout_fwd", grid=(S // ts,),
        in_specs=[pl.BlockSpec((ts, D), row), pl.BlockSpec((ts, DTOK), row),
                  pl.BlockSpec((ts, DMEM), lambda i: (i, qm_block)),
                  pl.BlockSpec((MEM_LEN, 2 * DMEM), lambda i: (0, 0)),
                  pl.BlockSpec((N_CHIPS, DMIX, D // N_CHIPS), lambda i: (0, 0, 0)), pl.BlockSpec((1, D), lambda i: (0, 0))],
        out_specs=[pl.BlockSpec((ts, D), row), pl.BlockSpec((ts, DMIX), row), pl.BlockSpec((ts, D), row)],
        out_shape=[jax.ShapeDtypeStruct((S, D), F32), jax.ShapeDtypeStruct((S, DMIX), _MM),
                   jax.ShapeDtypeStruct((S, D), F32)],
        compiler_params=_cparams(("parallel",)),
    )(h, tok, proj, kv, wo, g)


def _mixout_bwd(dho, mix, proj, kv, wo, g, qm_block):
    S = dho.shape[0]
    ts = _row_tile(S, 256)

    def body(dho_ref, mix_ref, qm_ref, kv_ref, wo_ref, g_ref, dmix_ref, dtok_ref, dqm_ref, dkv_ref, dg_ref):
        @pl.when(pl.program_id(0) == 0)
        def _():
            dkv_ref[...] = jnp.zeros_like(dkv_ref)
            dg_ref[...] = jnp.zeros_like(dg_ref)

        dmix, dg = _rms_bwd(mix_ref[...], g_ref[...], dho_ref[...])
        dg_ref[...] += dg
        dmb = dmix.astype(_MM)
        dmix_ref[...] = dmb
        cw = D // N_CHIPS
        dcat = _mm_nt(dmb[:, :cw], wo_ref[0])
        for j in range(1, N_CHIPS):
            dcat = dcat + _mm_nt(dmb[:, j * cw:(j + 1) * cw], wo_ref[j])
        dtok_ref[...] = dcat[:, :DTOK]
        head_a = lax.broadcasted_iota(jnp.int32, (ts, 128), 1) < HD
        qm = qm_ref[...].astype(_MM)
        zq = jnp.zeros((ts, 128), _MM)
        npair = DMEM // 128
        kps = [kv_ref[:, pi * 128:(pi + 1) * 128] for pi in range(npair)]
        vps = [kv_ref[:, DMEM + pi * 128:DMEM + (pi + 1) * 128] for pi in range(npair)]
        qhs, dmhs = [], []
        for pi in range(npair):
            cs = slice(pi * 128, (pi + 1) * 128)
            dmo = dcat[:, DTOK + pi * 128:DTOK + (pi + 1) * 128].astype(_MM)
            qhs += [jnp.where(head_a, qm[:, cs], zq), jnp.where(head_a, zq, qm[:, cs])]
            dmhs += [jnp.where(head_a, dmo, zq), jnp.where(head_a, zq, dmo)]
        nh = 2 * npair
        scores = [_mm_nt(qhs[h], kps[h // 2]) * SCALE for h in range(nh)]
        dps = [_mm_nt(dmhs[h], vps[h // 2]) for h in range(nh)]
        ps, dss = [], []
        for h in range(nh):
            e = jnp.exp(scores[h] - jnp.max(scores[h], axis=-1, keepdims=True))
            p = e / jnp.sum(e, axis=-1, keepdims=True)
            ps.append(p)
            dss.append((p * (dps[h] - jnp.sum(dps[h] * p, axis=-1, keepdims=True)) * SCALE).astype(_MM))
        dqs = [_mm(dss[h], kps[h // 2]) for h in range(nh)]
        for pi in range(npair):
            cs = slice(pi * 128, (pi + 1) * 128)
            a, b = 2 * pi, 2 * pi + 1
            dqm_ref[:, cs] = jnp.where(head_a, dqs[a], dqs[b]).astype(_MM)
            dkv_ref[:, cs] += _mm_tn(dss[a], qhs[a]) + _mm_tn(dss[b], qhs[b])
            dkv_ref[:, DMEM + pi * 128:DMEM + (pi + 1) * 128] += _mm_tn(ps[a], dmhs[a]) + _mm_tn(ps[b], dmhs[b])

    row = lambda i: (i, 0)
    fix = lambda i: (0, 0)
    return pl.pallas_call(
        body, name="mixout_bwd", grid=(S // ts,),
        in_specs=[pl.BlockSpec((ts, D), row), pl.BlockSpec((ts, D), row),
                  pl.BlockSpec((ts, DMEM), lambda i: (i, qm_block)),
                  pl.BlockSpec((MEM_LEN, 2 * DMEM), fix),
                  pl.BlockSpec((N_CHIPS, DMIX, D // N_CHIPS), lambda i: (0, 0, 0)), pl.BlockSpec((1, D), fix)],
        out_specs=[pl.BlockSpec((ts, D), row), pl.BlockSpec((ts, DTOK), row), pl.BlockSpec((ts, DMEM), row),
                   pl.BlockSpec((MEM_LEN, 2 * DMEM), fix), pl.BlockSpec((1, D), fix)],
        out_shape=[jax.ShapeDtypeStruct((S, D), _MM), jax.ShapeDtypeStruct((S, DTOK), F32),
                   jax.ShapeDtypeStruct((S, DMEM), _MM), jax.ShapeDtypeStruct((MEM_LEN, 2 * DMEM), F32),
                   jax.ShapeDtypeStruct((1, D), F32)],
        compiler_params=_cparams(("arbitrary",)),
    )(dho, mix, proj, kv, wo, g)


def _loss_grad(y, target):
    S = y.shape[0]
    ts = _row_tile(S, 512)

    def body(y_ref, t_ref, acc_ref, dy_ref):
        @pl.when(pl.program_id(0) == 0)
        def _():
            acc_ref[...] = jnp.zeros_like(acc_ref)

        err = y_ref[...] - t_ref[...]
        acc_ref[...] += jnp.sum(err * err, axis=0, keepdims=True)
        dy_ref[...] = err * (1.0 / D)

    row = lambda i: (i, 0)
    return pl.pallas_call(
        body, name="loss_grad", grid=(S // ts,),
        in_specs=[pl.BlockSpec((ts, D), row), pl.BlockSpec((ts, D), row)],
        out_specs=[pl.BlockSpec((1, D), lambda i: (0, 0)), pl.BlockSpec((ts, D), row)],
        out_shape=[jax.ShapeDtypeStruct((1, D), F32), jax.ShapeDtypeStruct((S, D), F32)],
        compiler_params=_cparams(("arbitrary",)),
    )(y, target)


LAYER_TENSORS = ("ffn1_gate", "ffn1_up", "ffn1_down", "ffn2_gate", "ffn2_up", "ffn2_down", "w_in", "w_mem_kv", "w_out")
SPLITS = {"ffn1_gate": "c", "ffn1_up": "c", "ffn1_down": "r", "ffn2_gate": "c", "ffn2_up": "c", "ffn2_down": "r",
          "w_in_pool": "r", "w_in_sb": "c", "w_mem_kv": "r", "w_out": "c"}


def _is_pool(i):
    return i % 2 == 0


def _qm_block(i):
    return (DTOK if _is_pool(i) else 3 * DTOK) // DMEM


FIRST_FFN = LAYER_TENSORS[:3]
REST = LAYER_TENSORS[3:]


def _layer_forward(h, mem, lw, i, rest_shards, next_shards):
    g_at = lambda g, k: g[k].reshape(1, D)
    sv = {"h0": h}
    (h, sv["a1"], sv["b1"], sv["f1"]), rest = _ffn_fwd(
        h, g_at(lw["g_pre"], 0), g_at(lw["g_post"], 0), lw["ffn1_gate"], lw["ffn1_up"], lw["ffn1_down"],
        _GatherExchange(rest_shards))
    lw.update(zip(REST, rest))
    if not _is_pool(i):
        lw["w_in"] = jnp.transpose(lw["w_in"], (1, 0, 2)).reshape(1, D, DSB)
    sv["h1"] = h
    sv["kv"] = _memkv_fwd(mem, lw["g_mem"], lw["w_mem_kv"])
    if _is_pool(i):
        sv["u"], sv["proj"] = _proj_fwd(h, g_at(lw["g_pre"], 1), lw["w_in"], F32)
        tok = _pool_fwd(sv["proj"], lw["pool_w"], lw["pool_scale"])
    else:
        sv["u"], sv["proj"] = _proj_fwd(h, g_at(lw["g_pre"], 1), lw["w_in"], _MM)
        tok = sv["o"] = _sb_fwd(sv["proj"])
    h, sv["cat"], sv["mix"] = _mixout_fwd(h, tok, sv["proj"], sv["kv"], lw["w_out"], g_at(lw["g_post"], 1),
                                          _qm_block(i))
    sv["h2"] = h
    ffn2 = (h, g_at(lw["g_pre"], 2), g_at(lw["g_post"], 2), lw["ffn2_gate"], lw["ffn2_up"], lw["ffn2_down"])
    if next_shards is None:
        h, sv["a2"], sv["b2"], sv["f2"] = _ffn_fwd(*ffn2)
        return h, sv, None
    (h, sv["a2"], sv["b2"], sv["f2"]), nxt = _ffn_fwd(*ffn2, _GatherExchange(next_shards))
    return h, sv, nxt


def _layer_backward(dh, mem, lw, sv, i, rides=None):
    g_at = lambda g, k: g[k].reshape(1, D)
    big, small = {}, {}
    dgpre, dgpost = [None] * 3, [None] * 3
    rode = {}
    rides = rides or {}

    def hosted(host, fn, *args):
        ride = rides[host](big, rode) if host in rides else None
        if ride is None:
            return fn(*args)
        outs, rode[host] = fn(*args, ride)
        return outs

    def ffn_back(which, dh, h_in):
        k = 0 if which == 1 else 2
        n = f"ffn{which}"
        dh, nb, sb, da, db, df, dgpre[k], dgpost[k] = hosted(
            n, _ffn_bwd, h_in, sv[f"f{which}"], sv[f"a{which}"], sv[f"b{which}"], dh, g_at(lw["g_pre"], k),
            g_at(lw["g_post"], k), lw[n + "_gate"], lw[n + "_up"], lw[n + "_down"])
        big[n + "_gate"] = hosted(n + "_gate", _wgrad, nb, da, N_CHIPS)
        big[n + "_up"] = hosted(n + "_up", _wgrad, nb, db, N_CHIPS)
        big[n + "_down"] = hosted(n + "_down", _wgrad, sb, df, 0).reshape(N_CHIPS, FF // N_CHIPS, D)
        return dh

    dh = ffn_back(2, dh, sv["h2"])
    dmix, dtok, dqm, dkv, dgpost[1] = _mixout_bwd(dh, sv["mix"], sv["proj"], sv["kv"], lw["w_out"],
                                                  g_at(lw["g_post"], 1), _qm_block(i))
    big["w_out"] = _wgrad(sv["cat"], dmix, N_CHIPS)
    dwkv, small["g_mem"] = _memkv_bwd(mem, lw["g_mem"], lw["w_mem_kv"], dkv)
    big["w_mem_kv"] = dwkv.reshape(N_CHIPS, D // N_CHIPS, 2 * DMEM)
    if _is_pool(i):
        dx, small["pool_w"], small["pool_scale"] = _pool_bwd(sv["proj"], dtok, lw["pool_w"], lw["pool_scale"])
        dh, dproj, dgpre[1] = _proj_bwd(sv["h1"], dh, g_at(lw["g_pre"], 1), lw["w_in"], [dx, dqm])
        big["w_in"] = _wgrad(sv["u"], dproj).reshape(N_CHIPS, D // N_CHIPS, DMIX)
    else:
        dq, dk, dv = _sb_bwd(sv["proj"], sv["o"], dtok)
        dh, dproj, dgpre[1] = _proj_bwd(sv["h1"], dh, g_at(lw["g_pre"], 1), lw["w_in"], [dq, dk, dv, dqm])
        big["w_in"] = jnp.transpose(_wgrad(sv["u"], dproj).reshape(D, N_CHIPS, DSB // N_CHIPS), (1, 0, 2))
    dh = ffn_back(1, dh, sv["h0"])
    small["g_pre"] = jnp.concatenate(dgpre, axis=0)
    small["g_post"] = jnp.concatenate(dgpost, axis=0)
    return dh, big, small, rode


ANY = pl.BlockSpec(memory_space=pl.ANY)


def _place():
    return lax.axis_index("x"), lax.axis_index("y"), lax.axis_index("c")


def _flip(v, bit):
    return 1 - v if bit else v


def _small_allgather(buf):
    R = buf.shape[0]

    def body(x_ref, o_ref, send_sems, recv_sems, loc_sem):
        x, y, c = _place()
        me = 4 * x + 2 * y + c
        loc = pltpu.make_async_copy(x_ref, o_ref.at[me], loc_sem)
        loc.start()
        sends = []
        for k in range(1, N_DEV):
            peer = (_flip(x, k & 4), _flip(y, k & 2), _flip(c, k & 1))
            cp = pltpu.make_async_remote_copy(src_ref=x_ref, dst_ref=o_ref.at[me], send_sem=send_sems.at[k - 1],
                                              recv_sem=recv_sems.at[k - 1], device_id=peer, device_id_type=MESH)
            cp.start()
            sends.append(cp)
        for k in range(1, N_DEV):
            peer = (_flip(x, k & 4), _flip(y, k & 2), _flip(c, k & 1))
            slot = 4 * peer[0] + 2 * peer[1] + peer[2]
            pltpu.make_async_remote_copy(src_ref=x_ref, dst_ref=o_ref.at[slot], send_sem=send_sems.at[k - 1],
                                         recv_sem=recv_sems.at[k - 1], device_id=peer, device_id_type=MESH).wait_recv()
        for cp in sends:
            cp.wait_send()
        loc.wait()

    return pl.pallas_call(
        body, name="small_allgather", in_specs=[ANY], out_specs=ANY,
        out_shape=jax.ShapeDtypeStruct((N_DEV, R, 128), buf.dtype),
        scratch_shapes=[pltpu.SemaphoreType.DMA((N_DEV - 1,)), pltpu.SemaphoreType.DMA((N_DEV - 1,)),
                        pltpu.SemaphoreType.DMA],
    )(buf)


def _dma_sems(m):
    return pltpu.SemaphoreType.DMA((m,))


class _GatherExchange:
    name = "gather_weights"
    aliased = True

    def __init__(self, bufs):
        n = len(bufs)
        self.n, self.inputs = n, list(bufs)
        self.halves = [b.shape[1] // 2 for b in bufs]
        self.out_shapes = [jax.ShapeDtypeStruct(b.shape, b.dtype) for b in bufs]
        self.sems = [_dma_sems(3 * n), _dma_sems(3 * n), _dma_sems(3 * n), _dma_sems(3 * n)]

    def _copies(self, outs, sems):
        ici_send, ici_recv, d2d_send, d2d_recv = sems
        x, y, c = _place()
        me = 2 * x + y
        sib = (x, y, 1 - c)
        chips = [(_flip(x, k & 2), _flip(y, k & 1)) for k in (1, 2, 3)]
        half = lambda t, slot, hc: outs[t].at[slot, pl.ds(hc * self.halves[t], self.halves[t])]
        ici_out, ici_in, fwd_out, fwd_in = [], [], [], []
        for t in range(self.n):
            for k, chip in enumerate(chips):
                idx = 2 * chip[0] + chip[1]
                peer = (chip[0], chip[1], c)
                ici = dict(send_sem=ici_send.at[3 * t + k], recv_sem=ici_recv.at[3 * t + k], device_id=peer,
                           device_id_type=MESH)
                d2d = dict(send_sem=d2d_send.at[3 * t + k], recv_sem=d2d_recv.at[3 * t + k], device_id=sib,
                           device_id_type=MESH)
                ici_out.append(pltpu.make_async_remote_copy(src_ref=half(t, me, c), dst_ref=half(t, me, c), **ici))
                ici_in.append(pltpu.make_async_remote_copy(src_ref=half(t, idx, c), dst_ref=half(t, idx, c), **ici))
                fwd_out.append(pltpu.make_async_remote_copy(src_ref=half(t, idx, c), dst_ref=half(t, idx, c), **d2d))
                fwd_in.append(pltpu.make_async_remote_copy(
                    src_ref=half(t, idx, 1 - c), dst_ref=half(t, idx, 1 - c), **d2d))
        return ici_out, ici_in, fwd_out, fwd_in

    def start(self, ins, outs, sems):
        for cp in self._copies(outs, sems)[0]:
            cp.start()

    def mid(self, ins, outs, sems):
        _, ici_in, fwd_out, _ = self._copies(outs, sems)
        for arrived, onward in zip(ici_in, fwd_out):
            arrived.wait_recv()
            onward.start()

    def finish(self, ins, outs, sems):
        ici_out, _, fwd_out, fwd_in = self._copies(outs, sems)
        for cp in fwd_in:
            cp.wait_recv()
        for cp in ici_out + fwd_out:
            cp.wait_send()


class _SiblingExchange:
    name = "rs_to_sibling"

    def __init__(self, gs):
        n = len(gs)
        self.n, self.inputs = n, list(gs)
        self.halves = [g.shape[1] // 2 for g in gs]
        self.out_shapes = [jax.ShapeDtypeStruct((N_CHIPS, g.shape[1] // 2) + g.shape[2:], g.dtype) for g in gs]
        self.sems = [_dma_sems(n), _dma_sems(n)]

    def _copies(self, ins, outs, sems):
        x, y, c = _place()
        return [pltpu.make_async_remote_copy(
            src_ref=ins[t].at[:, pl.ds((1 - c) * self.halves[t], self.halves[t])], dst_ref=outs[t],
            send_sem=sems[0].at[t], recv_sem=sems[1].at[t], device_id=(x, y, 1 - c), device_id_type=MESH)
            for t in range(self.n)]

    def start(self, ins, outs, sems):
        for cp in self._copies(ins, outs, sems):
            cp.start()

    def mid(self, ins, outs, sems):
        pass

    def finish(self, ins, outs, sems):
        for cp in self._copies(ins, outs, sems):
            cp.wait()


class _ChipsExchange:
    name = "rs_to_chips"

    def __init__(self, ss):
        n = len(ss)
        self.n, self.inputs = n, list(ss)
        self.out_shapes = [jax.ShapeDtypeStruct((3,) + s.shape[1:], s.dtype) for s in ss]
        self.sems = [_dma_sems(3 * n), _dma_sems(3 * n)]

    def _copies(self, ins, outs, sems):
        x, y, c = _place()
        chips = [(_flip(x, k & 2), _flip(y, k & 1)) for k in (1, 2, 3)]
        return [pltpu.make_async_remote_copy(
            src_ref=ins[t].at[2 * chip[0] + chip[1]], dst_ref=outs[t].at[k], send_sem=sems[0].at[3 * t + k],
            recv_sem=sems[1].at[3 * t + k], device_id=(chip[0], chip[1], c), device_id_type=MESH)
            for t in range(self.n) for k, chip in enumerate(chips)]

    def start(self, ins, outs, sems):
        for cp in self._copies(ins, outs, sems):
            cp.start()

    def mid(self, ins, outs, sems):
        pass

    def finish(self, ins, outs, sems):
        for cp in self._copies(ins, outs, sems):
            cp.wait()


class _Both:
    def __init__(self, first, second):
        self.parts = (first, second)
        self.name = first.name + "_" + second.name
        self.inputs = first.inputs + second.inputs
        self.out_shapes = first.out_shapes + second.out_shapes
        self.sems = first.sems + second.sems

    def _each(self, ins, outs, sems):
        a = self.parts[0]
        ni, no, ns = len(a.inputs), len(a.out_shapes), len(a.sems)
        return ((a, ins[:ni], outs[:no], sems[:ns]), (self.parts[1], ins[ni:], outs[no:], sems[ns:]))

    def start(self, ins, outs, sems):
        for ex, i, o, s in self._each(ins, outs, sems):
            ex.start(i, o, s)

    def mid(self, ins, outs, sems):
        for ex, i, o, s in self._each(ins, outs, sems):
            ex.mid(i, o, s)

    def finish(self, ins, outs, sems):
        for ex, i, o, s in self._each(ins, outs, sems):
            ex.finish(i, o, s)

    def split(self, outs):
        no = len(self.parts[0].out_shapes)
        return outs[:no], outs[no:]


def _run_exchange(ex):
    ni, no = len(ex.inputs), len(ex.out_shapes)

    def body(*refs):
        ins, outs, sems = refs[:ni], refs[ni:ni + no], refs[ni + no:]
        ex.start(ins, outs, sems)
        ex.mid(ins, outs, sems)
        ex.finish(ins, outs, sems)

    return pl.pallas_call(
        body, name=ex.name, in_specs=[ANY] * ni, out_specs=[ANY] * no, out_shape=ex.out_shapes,
        scratch_shapes=ex.sems, input_output_aliases={t: t for t in range(ni)} if _in_place(ex) else {},
    )(*ex.inputs)


def _in_place(ex):
    return getattr(ex, "aliased", False)


def _place_own(shard):
    L_, r, n_ = shard.shape
    tr = _row_tile(r, 256)

    def body(x_ref, *refs):
        outs, stage, sems = refs[:L_], refs[L_], refs[L_ + 1]
        x, y, _ = _place()
        rows = pl.ds(pl.multiple_of(pl.program_id(0) * tr, tr), tr)
        copies = []
        for l in range(L_):
            stage[l] = x_ref[l].astype(_MM)
            cp = pltpu.make_async_copy(stage.at[l], outs[l].at[2 * x + y, rows], sems.at[l])
            cp.start()
            copies.append(cp)
        for cp in copies:
            cp.wait()

    return pl.pallas_call(
        body, name="place_own", grid=(r // tr,),
        in_specs=[pl.BlockSpec((L_, tr, n_), lambda i: (0, i, 0))], out_specs=[ANY] * L_,
        out_shape=[jax.ShapeDtypeStruct((N_CHIPS, r, n_), _MM)] * L_,
        scratch_shapes=[pltpu.VMEM((L_, tr, n_), _MM), _dma_sems(L_)],
        compiler_params=_cparams(("arbitrary",)),
    )(shard)


def _call(body, name, nsteps, in_specs, out_specs, out_shape, args, ride=None):
    if ride is None:
        return pl.pallas_call(body, name=name, grid=(nsteps,), in_specs=in_specs, out_specs=out_specs,
                              out_shape=out_shape, compiler_params=_cparams(("arbitrary",)))(*args)
    ni, no = len(args), len(out_shape)
    ri, ro = len(ride.inputs), len(ride.out_shapes)
    mid_step = (13 * nsteps) // 16 if nsteps > 1 else 0

    def hosted(*refs):
        a, ra = refs[:ni], refs[ni:ni + ri]
        o, ro_refs = refs[ni + ri:ni + ri + no], refs[ni + ri + no:ni + ri + no + ro]
        sems = refs[ni + ri + no + ro:]
        step = pl.program_id(0)

        @pl.when(step == 0)
        def _():
            ride.start(ra, ro_refs, sems)

        body(*a, *o)

        @pl.when(step == mid_step)
        def _():
            ride.mid(ra, ro_refs, sems)

        @pl.when(step == nsteps - 1)
        def _():
            ride.finish(ra, ro_refs, sems)

    outs = pl.pallas_call(
        hosted, name=name + "_" + ride.name, grid=(nsteps,), in_specs=list(in_specs) + [ANY] * ri,
        out_specs=list(out_specs) + [ANY] * ro, out_shape=list(out_shape) + ride.out_shapes,
        scratch_shapes=ride.sems, compiler_params=_cparams(("arbitrary",)),
        input_output_aliases={ni + t: no + t for t in range(ri)} if _in_place(ride) else {},
    )(*args, *ride.inputs)
    return outs[:no], outs[no:]


def _share_halves(ts):
    n = len(ts)

    def body(*refs):
        outs = refs[n:2 * n]
        send_sems, recv_sems = refs[2 * n:]
        x, y, c = _place()
        cps = []
        for t in range(n):
            cp = pltpu.make_async_remote_copy(
                src_ref=outs[t].at[c], dst_ref=outs[t].at[c], send_sem=send_sems.at[t], recv_sem=recv_sems.at[t],
                device_id=(x, y, 1 - c), device_id_type=MESH)
            cp.start()
            cps.append(cp)
        for t in range(n):
            pltpu.make_async_remote_copy(
                src_ref=outs[t].at[1 - c], dst_ref=outs[t].at[1 - c], send_sem=send_sems.at[t],
                recv_sem=recv_sems.at[t], device_id=(x, y, 1 - c), device_id_type=MESH).wait_recv()
        for cp in cps:
            cp.wait_send()

    dma = lambda m: pltpu.SemaphoreType.DMA((m,))
    return pl.pallas_call(
        body, name="share_halves", in_specs=[ANY] * n, out_specs=[ANY] * n,
        out_shape=[jax.ShapeDtypeStruct(t.shape, t.dtype) for t in ts],
        input_output_aliases={t: t for t in range(n)},
        scratch_shapes=[dma(n), dma(n)],
    )(*ts)


def _by_shape(fn, *lists):
    groups = {}
    for idx, a in enumerate(lists[0]):
        groups.setdefault(a.shape, []).append(idx)
    out = [None] * len(lists[0])
    for idxs in groups.values():
        for idx, res in zip(idxs, fn(*[[lst[idx] for idx in idxs] for lst in lists])):
            out[idx] = res
    return out


def _add_sibling(gs, rs, c):
    m = len(gs)
    _, r_, n_ = gs[0].shape
    h = r_ // 2
    tr = _row_tile(h, 512)

    def body(c_ref, *refs):
        for g_ref, r_ref, o_ref in zip(refs[:m], refs[m:2 * m], refs[2 * m:]):
            o_ref[...] = (g_ref[...] + r_ref[...]).astype(o_ref.dtype)

    gspec = pl.BlockSpec((None, None, tr, n_), lambda s, i, c_ref: (s, c_ref[0], i, 0))
    hspec = pl.BlockSpec((None, tr, n_), lambda s, i, c_ref: (s, i, 0))
    return pl.pallas_call(
        body, name="add_sibling",
        grid_spec=pltpu.PrefetchScalarGridSpec(
            num_scalar_prefetch=1, grid=(N_CHIPS, h // tr), in_specs=[gspec] * m + [hspec] * m, out_specs=[hspec] * m),
        out_shape=[jax.ShapeDtypeStruct((N_CHIPS, h, n_), _MM)] * m,
        compiler_params=_cparams(("parallel", "parallel")),
    )(c.reshape(1), *[g.reshape(N_CHIPS, 2, h, n_) for g in gs], *rs)


def _add_chips(ss, rs, me, c):
    m = len(ss)
    _, h, n_ = ss[0].shape
    tr = _row_tile(h, 512)

    def body(at_ref, *refs):
        f = lambda ref: ref[...].astype(F32)
        for t in range(m):
            s_ref, (r0, r1, r2), o_ref = refs[t], refs[m + 3 * t:m + 3 * t + 3], refs[4 * m + t]
            o_ref[...] = ((f(s_ref) + f(r0)) + f(r1)) + f(r2)

    rk = lambda k: pl.BlockSpec((None, tr, n_), lambda i, at_ref: (k, i, 0))
    return pl.pallas_call(
        body, name="add_chips",
        grid_spec=pltpu.PrefetchScalarGridSpec(
            num_scalar_prefetch=1, grid=(h // tr,),
            in_specs=[pl.BlockSpec((None, tr, n_), lambda i, at_ref: (at_ref[0], i, 0))] * m + [rk(0), rk(1), rk(2)] * m,
            out_specs=[pl.BlockSpec((None, tr, n_), lambda i, at_ref: (at_ref[1], i, 0))] * m),
        out_shape=[jax.ShapeDtypeStruct((2, h, n_), F32)] * m,
        compiler_params=_cparams(("parallel",)),
    )(jnp.stack([me, c]), *ss, *[r for r in rs for _ in range(3)])


def _sum_devices(a):
    R_ = a.shape[1]

    def body(a_ref, o_ref):
        acc = a_ref[0]
        for k in range(1, N_DEV):
            acc = acc + a_ref[k]
        o_ref[...] = acc

    return pl.pallas_call(
        body, name="sum_devices", out_shape=jax.ShapeDtypeStruct((R_, 128), F32),
        compiler_params=pltpu.CompilerParams(vmem_limit_bytes=VMEM_LIMIT),
    )(a)


def _adamw(w, g, m, v):
    shape = w.shape
    to2 = lambda a: a.reshape(-1, shape[-1])
    w2, g2, m2, v2 = to2(w), to2(g), to2(m), to2(v)
    R_, C = w2.shape
    tr = 256 if R_ % 256 == 0 else R_

    def body(w_ref, g_ref, m_ref, v_ref, d_ref, mo_ref, vo_ref):
        gv = g_ref[...]
        mn = ADAM_B1 * m_ref[...] + (1.0 - ADAM_B1) * gv
        vn = ADAM_B2 * v_ref[...] + (1.0 - ADAM_B2) * (gv * gv)
        m_hat = mn / (1.0 - ADAM_B1 ** ADAM_STEP)
        v_hat = vn / (1.0 - ADAM_B2 ** ADAM_STEP)
        d_ref[...] = -ADAM_LR * (m_hat / (jnp.sqrt(v_hat) + ADAM_EPS) + ADAM_WD * w_ref[...])
        mo_ref[...] = mn
        vo_ref[...] = vn

    spec = pl.BlockSpec((tr, C), lambda i: (i, 0))
    outs = pl.pallas_call(
        body, name="adamw", grid=(R_ // tr,), in_specs=[spec] * 4, out_specs=[spec] * 3,
        out_shape=[jax.ShapeDtypeStruct((R_, C), F32)] * 3,
        compiler_params=_cparams(("parallel",)),
    )(w2, g2, m2, v2)
    return [o.reshape(shape) for o in outs]


SHARD_AXIS = {"g_pre": 2, "g_post": 2, "g_mem": None, "ffn1_gate": 2, "ffn1_up": 2, "ffn1_down": 1,
              "ffn2_gate": 2, "ffn2_up": 2, "ffn2_down": 1, "w_in_pool": 1, "pool_w": None, "pool_scale": None,
              "w_in_sb": 2, "w_mem_kv": 1, "w_out": 2}
WEIGHTS = list(SHARD_AXIS)
BIG = [k for k in WEIGHTS if SHARD_AXIS[k] is not None and k not in ("g_pre", "g_post")]
SMALL = [k for k in WEIGHTS if k not in BIG]


def _pack128(arrs):
    flat = jnp.concatenate([a.reshape(-1) for a in arrs])
    pad = (-flat.shape[0]) % (8 * 128)
    return jnp.pad(flat, (0, pad)).reshape(-1, 128)


def _unpack128(buf, shapes):
    flat = buf.reshape(-1)
    out, off = [], 0
    for shp in shapes:
        size = 1
        for d in shp:
            size *= d
        out.append(flat[off:off + size].reshape(shp))
        off += size
    return out


def kernel(x, mem, g_pre, g_post, g_mem, ffn1_gate, ffn1_up, ffn1_down, ffn2_gate, ffn2_up, ffn2_down, w_in_pool, pool_w, pool_scale, w_in_sb, w_mem_kv, w_out, loss_target, m_g_pre, m_g_post, m_g_mem, m_ffn1_gate, m_ffn1_up, m_ffn1_down, m_ffn2_gate, m_ffn2_up, m_ffn2_down, m_w_in_pool, m_pool_w, m_pool_scale, m_w_in_sb, m_w_mem_kv, m_w_out, v_g_pre, v_g_post, v_g_mem, v_ffn1_gate, v_ffn1_up, v_ffn1_down, v_ffn2_gate, v_ffn2_up, v_ffn2_down, v_w_in_pool, v_pool_w, v_pool_scale, v_w_in_sb, v_w_mem_kv, v_w_out):
    w = dict(g_pre=g_pre, g_post=g_post, g_mem=g_mem, ffn1_gate=ffn1_gate, ffn1_up=ffn1_up, ffn1_down=ffn1_down,
             ffn2_gate=ffn2_gate, ffn2_up=ffn2_up, ffn2_down=ffn2_down, w_in_pool=w_in_pool, pool_w=pool_w,
             pool_scale=pool_scale, w_in_sb=w_in_sb, w_mem_kv=w_mem_kv, w_out=w_out)
    m = dict(g_pre=m_g_pre, g_post=m_g_post, g_mem=m_g_mem, ffn1_gate=m_ffn1_gate, ffn1_up=m_ffn1_up,
             ffn1_down=m_ffn1_down, ffn2_gate=m_ffn2_gate, ffn2_up=m_ffn2_up, ffn2_down=m_ffn2_down,
             w_in_pool=m_w_in_pool, pool_w=m_pool_w, pool_scale=m_pool_scale, w_in_sb=m_w_in_sb,
             w_mem_kv=m_w_mem_kv, w_out=m_w_out)
    v = dict(g_pre=v_g_pre, g_post=v_g_post, g_mem=v_g_mem, ffn1_gate=v_ffn1_gate, ffn1_up=v_ffn1_up,
             ffn1_down=v_ffn1_down, ffn2_gate=v_ffn2_gate, ffn2_up=v_ffn2_up, ffn2_down=v_ffn2_down,
             w_in_pool=v_w_in_pool, pool_w=v_pool_w, pool_scale=v_pool_scale, w_in_sb=v_w_in_sb,
             w_mem_kv=v_w_mem_kv, w_out=v_w_out)
    cx, cy, cc = _place()
    chip = (2 * cx + cy).astype(jnp.int32)
    core = cc.astype(jnp.int32)

    gains = _small_allgather(_pack128([g_pre, g_post]))
    per_chip = [_unpack128(gains[2 * j], [g_pre.shape, g_post.shape]) for j in range(N_CHIPS)]
    full_pre = jnp.concatenate([p[0] for p in per_chip], axis=2)
    full_post = jnp.concatenate([p[1] for p in per_chip], axis=2)

    def shard_name(k, i):
        return ("w_in_pool" if _is_pool(i) else "w_in_sb") if k == "w_in" else k

    placed = {k: _place_own(w[k]) for k in BIG}

    def own(i, names):
        return [placed[shard_name(k, i)][i // 2 if k == "w_in" else i] for k in names]

    xs, mems = x[0], mem[0]
    h, lws, saved = xs, [], []
    first = _run_exchange(_GatherExchange(own(0, FIRST_FFN)))
    for i in range(DEPTH):
        lw = dict(zip(FIRST_FFN, first))
        lw["g_pre"], lw["g_post"], lw["g_mem"] = full_pre[i], full_post[i], g_mem[i].reshape(1, D)
        if _is_pool(i):
            lw["pool_w"], lw["pool_scale"] = pool_w[i // 2], pool_scale[i // 2].reshape(1, DTOK)
        h, sv, first = _layer_forward(h, mems, lw, i, own(i, REST), own(i + 1, FIRST_FFN) if i + 1 < DEPTH else None)
        lws.append(lw)
        saved.append(sv)
    acc, dh = _loss_grad(h, loss_target[0])
    loss = lax.psum(0.5 / D * jnp.sum(acc), ("x", "y", "c"))

    mine, smalls = [None] * DEPTH, [None] * DEPTH
    above, sums = None, {}
    nt = len(LAYER_TENSORS)
    early = ("ffn2_gate", "w_mem_kv", "ffn2_up", "w_out", "ffn2_down", "w_in")
    add_sib = lambda gs, from_sib: _by_shape(lambda g, r: _add_sibling(g, r, core), gs, from_sib)
    add_chp = lambda ss, from_chips: _by_shape(lambda s, r: _add_chips(s, r, chip, core), ss, from_chips)

    for i in reversed(range(DEPTH)):
        rides = {}
        if above is not None:
            up = [above[k] for k in LAYER_TENSORS]
            rides["ffn2"] = lambda big, rode, up=up: _SiblingExchange(up)

            def on_ffn1(big, rode, up=up, layer=i + 1, bottom=(i == 0)):
                sums[layer] = add_sib(up, rode["ffn2"])
                ex = _ChipsExchange(sums[layer])
                return _Both(ex, _SiblingExchange([big[k] for k in early])) if bottom else ex

            def on_wgrad(big, rode, j):
                if "early" not in sums:
                    sums["early"] = add_sib([big[k] for k in early], rode["ffn1"][nt:])
                return _ChipsExchange(sums["early"][2 * j:2 * j + 2])

            rides["ffn1"] = on_ffn1
            if i == 0:
                for j, host in enumerate(FIRST_FFN):
                    rides[host] = lambda big, rode, j=j: on_wgrad(big, rode, j)
        dh, gs, smalls[i], rode = _layer_backward(dh, mems, lws[i], saved[i], i, rides)
        if above is not None:
            mine[i + 1] = add_chp(sums[i + 1], rode["ffn1"][:nt])
        above = gs
    early_mine = add_chp(sums["early"], [t for host in FIRST_FFN for t in rode[host]])
    late = [above[k] for k in FIRST_FFN]
    late_sums = add_sib(late, _run_exchange(_SiblingExchange(late)))
    late_mine = add_chp(late_sums, _run_exchange(_ChipsExchange(late_sums)))
    by_name = dict(zip(early + FIRST_FFN, early_mine + late_mine))
    mine[0] = [by_name[k] for k in LAYER_TENSORS]
    shared = _share_halves([t for layer in mine for t in layer])
    reduced = [dict(zip(LAYER_TENSORS, [s.reshape(-1, s.shape[-1]) for s in shared[nt * i:nt * (i + 1)]]))
               for i in range(DEPTH)]
    dx = dh
    red = {}
    for k in BIG:
        if k.startswith("w_in_"):
            layers = [i for i in range(DEPTH) if _is_pool(i) == (k == "w_in_pool")]
            red[k] = jnp.stack([reduced[i]["w_in"] for i in layers])
        else:
            red[k] = jnp.stack([reduced[i][k] for i in range(DEPTH)])

    grads = {"g_pre": jnp.stack([s["g_pre"] for s in smalls]), "g_post": jnp.stack([s["g_post"] for s in smalls]),
             "g_mem": jnp.concatenate([s["g_mem"] for s in smalls], axis=0),
             "pool_w": jnp.stack([smalls[i]["pool_w"] for i in range(DEPTH) if _is_pool(i)]),
             "pool_scale": jnp.concatenate([smalls[i]["pool_scale"] for i in range(DEPTH) if _is_pool(i)], axis=0)}
    small_shapes = [grads[k].shape for k in SMALL]
    summed = _unpack128(_sum_devices(_small_allgather(_pack128([grads[k] for k in SMALL]))), small_shapes)
    for k, s in zip(SMALL, summed):
        if SHARD_AXIS[k] is None:
            red[k] = s
        else:
            red[k] = lax.dynamic_slice_in_dim(s, chip * w[k].shape[2], w[k].shape[2], axis=2)

    deltas, new_m, new_v = {}, {}, {}
    for k in WEIGHTS:
        deltas[k], new_m[k], new_v[k] = _adamw(w[k], red[k], m[k], v[k])
    return (loss, dx.reshape(x.shape), *[red[k] for k in WEIGHTS], *[deltas[k] for k in WEIGHTS],
            *[new_m[k] for k in WEIGHTS], *[new_v[k] for k in WEIGHTS])
```

```python
import jax
import jax.numpy as jnp
from jax import lax
from jax.experimental import pallas as pl
from jax.experimental.pallas import tpu as pltpu

F32 = jnp.float32
BF16 = jnp.bfloat16
_MM = jnp.bfloat16

D = 1024
FF = 2048
DTOK = 512
DMEM = 256
DMIX = DTOK + DMEM
DSB = 3 * DTOK + DMEM
HD = 64
MEM_LEN = 256
DEPTH = 4
EPS = 1e-6
WINDOWS = (2, 4, 8, 16)
HALO = 16
PG = 128
QB = 128
SCALE = HD ** -0.5
FC = 512
NEG_CUT = -104.0

ADAM_LR, ADAM_B1, ADAM_B2, ADAM_EPS, ADAM_WD, ADAM_STEP = 0.001, 0.9, 0.999, 1e-08, 0.01, 10

VMEM_LIMIT = 56 * 2 ** 20
MESH = pl.DeviceIdType.MESH
N_CHIPS = 4
N_DEV = 8


def _cparams(sem):
    return pltpu.CompilerParams(dimension_semantics=sem, vmem_limit_bytes=VMEM_LIMIT)


def _mm(a, b):
    return jnp.dot(a.astype(_MM), b.astype(_MM), preferred_element_type=F32)


def _mm_nt(a, b):
    return lax.dot_general(a.astype(_MM), b.astype(_MM), (((1,), (1,)), ((), ())), preferred_element_type=F32)


def _mm_tn(a, b):
    return lax.dot_general(a.astype(_MM), b.astype(_MM), (((0,), (0,)), ((), ())), preferred_element_type=F32)


def _mm2(a, b):
    hi = a.astype(_MM)
    lo = (a - hi.astype(F32)).astype(_MM)
    return jnp.dot(hi, b, preferred_element_type=F32) + jnp.dot(lo, b, preferred_element_type=F32)


def _rstd(x):
    return lax.rsqrt(jnp.mean(x * x, axis=-1, keepdims=True) + EPS)


def _rms_bwd(x, g, dy):
    r = _rstd(x)
    xh = x * r
    t = dy * g
    dx = r * (t - xh * jnp.mean(t * xh, axis=-1, keepdims=True))
    return dx, jnp.sum(dy * xh, axis=0, keepdims=True)


def _sigmoid(a):
    return 1.0 / (1.0 + jnp.exp(-a))


def _row_tile(n, want):
    t = min(n, want)
    assert n % t == 0, (n, t)
    return t


def _ffn_fwd(h, g1, g2, wg, wu, wd, ride=None):
    S = h.shape[0]
    ts = _row_tile(S, 512)

    def body(h_ref, g1_ref, g2_ref, wg_ref, wu_ref, wd_ref, ho_ref, a_ref, b_ref, f_ref):
        hv = h_ref[...]
        n = (hv * _rstd(hv) * g1_ref[...]).astype(_MM)
        f = jnp.zeros((ts, D), F32)
        for j in range(FF // FC):
            cs = slice(j * FC, (j + 1) * FC)
            a = _mm(n, wg_ref[j])
            b = _mm(n, wu_ref[j])
            a_ref[:, cs] = a.astype(a_ref.dtype)
            b_ref[:, cs] = b.astype(b_ref.dtype)
            f = f + _mm(a * _sigmoid(a) * b, wd_ref[j])
        f_ref[...] = f
        ho_ref[...] = hv + 0.5 * (f * _rstd(f) * g2_ref[...])

    row = lambda i: (i, 0)
    fix = lambda i: (0, 0)
    all3 = lambda i: (0, 0, 0)
    ns = FF // FC
    return _call(
        body, "ffn_fwd", S // ts,
        [pl.BlockSpec((ts, D), row), pl.BlockSpec((1, D), fix), pl.BlockSpec((1, D), fix),
         pl.BlockSpec((ns, D, FC), all3), pl.BlockSpec((ns, D, FC), all3), pl.BlockSpec((ns, FC, D), all3)],
        [pl.BlockSpec((ts, D), row), pl.BlockSpec((ts, FF), row), pl.BlockSpec((ts, FF), row),
         pl.BlockSpec((ts, D), row)],
        [jax.ShapeDtypeStruct((S, D), F32), jax.ShapeDtypeStruct((S, FF), _MM),
         jax.ShapeDtypeStruct((S, FF), _MM), jax.ShapeDtypeStruct((S, D), F32)],
        (h, g1, g2, wg, wu, wd), ride)


def _ffn_bwd(h, f, a, b, dho, g1, g2, wg, wu, wd, ride=None):
    S = h.shape[0]
    ts = _row_tile(S, 256)

    def body(h_ref, f_ref, a_ref, b_ref, dho_ref, g1_ref, g2_ref, wg_ref, wu_ref, wd_ref,
             dh_ref, n_ref, s_ref, da_ref, db_ref, df_ref, dg1_ref, dg2_ref):
        @pl.when(pl.program_id(0) == 0)
        def _():
            dg1_ref[...] = jnp.zeros_like(dg1_ref)
            dg2_ref[...] = jnp.zeros_like(dg2_ref)

        hv = h_ref[...]
        dho = dho_ref[...]
        df, dg2 = _rms_bwd(f_ref[...], g2_ref[...], 0.5 * dho)
        dg2_ref[...] += dg2
        dfb = df.astype(_MM)
        df_ref[...] = dfb
        rh = _rstd(hv)
        hh = hv * rh
        g1 = g1_ref[...]
        n_ref[...] = (hh * g1).astype(_MM)
        dss = [_mm_nt(dfb, wd_ref[j]) for j in range(FF // FC)]
        das, dbs = [], []
        for j in range(FF // FC):
            cs = slice(j * FC, (j + 1) * FC)
            av = a_ref[:, cs].astype(F32)
            bv = b_ref[:, cs].astype(F32)
            sig = _sigmoid(av)
            sl = av * sig
            s_ref[:, cs] = (sl * bv).astype(_MM)
            da = (dss[j] * bv * (sig * (1.0 + av * (1.0 - sig)))).astype(_MM)
            db = (dss[j] * sl).astype(_MM)
            da_ref[:, cs] = da
            db_ref[:, cs] = db
            das.append(da)
            dbs.append(db)
        dn = jnp.zeros((ts, D), F32)
        for j in range(FF // FC):
            dn = dn + _mm_nt(das[j], wg_ref[j]) + _mm_nt(dbs[j], wu_ref[j])
        t = dn * g1
        dh_ref[...] = dho + rh * (t - hh * jnp.mean(t * hh, axis=-1, keepdims=True))
        dg1_ref[...] += jnp.sum(dn * hh, axis=0, keepdims=True)

    row = lambda i: (i, 0)
    fix = lambda i: (0, 0)
    all3 = lambda i: (0, 0, 0)
    ns = FF // FC
    td = pl.BlockSpec((ts, D), row)
    tf = pl.BlockSpec((ts, FF), row)
    gd = pl.BlockSpec((1, D), fix)
    return _call(
        body, "ffn_bwd", S // ts,
        [td, td, tf, tf, td, gd, gd,
         pl.BlockSpec((ns, D, FC), all3), pl.BlockSpec((ns, D, FC), all3), pl.BlockSpec((ns, FC, D), all3)],
        [td, td, tf, tf, tf, td, gd, gd],
        [jax.ShapeDtypeStruct((S, D), F32), jax.ShapeDtypeStruct((S, D), _MM),
         jax.ShapeDtypeStruct((S, FF), _MM), jax.ShapeDtypeStruct((S, FF), _MM),
         jax.ShapeDtypeStruct((S, FF), _MM), jax.ShapeDtypeStruct((S, D), _MM),
         jax.ShapeDtypeStruct((1, D), F32), jax.ShapeDtypeStruct((1, D), F32)],
        (h, f, a, b, dho, g1, g2, wg, wu, wd), ride)


def _wgrad(x, y, nslot=0, ride=None):
    T, M = x.shape
    N = y.shape[1]
    tt = _row_tile(T, 1024)
    cw = N // nslot if nslot else (512 if N % 512 == 0 else 256)
    nc = N // cw
    assert cw * nc == N and cw % 128 == 0, (N, nc)

    def body(x_ref, y_ref, o_ref):
        @pl.when(pl.program_id(0) == 0)
        def _():
            o_ref[...] = jnp.zeros_like(o_ref)

        xt = x_ref[...].T
        for j in range(nc):
            part = jnp.dot(xt, y_ref[:, j * cw:(j + 1) * cw], preferred_element_type=F32)
            if nslot:
                o_ref[j] += part
            else:
                o_ref[:, j * cw:(j + 1) * cw] += part

    oshape = (nslot, M, cw) if nslot else (M, N)
    res = _call(body, "wgrad", T // tt,
                [pl.BlockSpec((tt, M), lambda t: (t, 0)), pl.BlockSpec((tt, N), lambda t: (t, 0))],
                [pl.BlockSpec(oshape, lambda t: (0,) * len(oshape))], [jax.ShapeDtypeStruct(oshape, F32)], (x, y), ride)
    return res[0] if ride is None else (res[0][0], res[1])


def _proj_fwd(h, g, w, out_dtype):
    S = h.shape[0]
    nk, kw, N = w.shape
    ts = _row_tile(S, 256)

    def body(h_ref, g_ref, w_ref, u_ref, p_ref):
        hv = h_ref[...]
        u = (hv * _rstd(hv) * g_ref[...]).astype(_MM)
        u_ref[...] = u
        acc = _mm(u[:, :kw], w_ref[0])
        for j in range(1, nk):
            acc = acc + _mm(u[:, j * kw:(j + 1) * kw], w_ref[j])
        p_ref[...] = acc.astype(p_ref.dtype)

    return pl.pallas_call(
        body, name="proj_fwd", grid=(S // ts,),
        in_specs=[pl.BlockSpec((ts, D), lambda i: (i, 0)), pl.BlockSpec((1, D), lambda i: (0, 0)),
                  pl.BlockSpec((nk, kw, N), lambda i: (0, 0, 0))],
        out_specs=[pl.BlockSpec((ts, D), lambda i: (i, 0)), pl.BlockSpec((ts, N), lambda i: (i, 0))],
        out_shape=[jax.ShapeDtypeStruct((S, D), _MM), jax.ShapeDtypeStruct((S, N), out_dtype)],
        compiler_params=_cparams(("parallel",)),
    )(h, g, w)


def _proj_bwd(h, dho, g, w, pieces):
    S = h.shape[0]
    nk, kw, N = w.shape
    ts = _row_tile(S, 256)
    widths = [p.shape[1] for p in pieces]
    assert sum(widths) == N
    npc = len(pieces)

    def body(*refs):
        h_ref, dho_ref, g_ref, w_ref = refs[:4]
        p_refs = refs[4:4 + npc]
        dh_ref, dp_ref, dg_ref = refs[4 + npc:]

        @pl.when(pl.program_id(0) == 0)
        def _():
            dg_ref[...] = jnp.zeros_like(dg_ref)

        dus = [jnp.zeros((ts, kw), F32) for _ in range(nk)]
        off = 0
        for p_ref, n in zip(p_refs, widths):
            pv = p_ref[...].astype(_MM)
            dp_ref[:, off:off + n] = pv
            for j in range(nk):
                dus[j] = dus[j] + _mm_nt(pv, w_ref[j, :, off:off + n])
            off += n
        du = dus[0] if nk == 1 else jnp.concatenate(dus, axis=-1)
        dx, dg = _rms_bwd(h_ref[...], g_ref[...], du)
        dh_ref[...] = dho_ref[...] + dx
        dg_ref[...] += dg

    row = lambda i: (i, 0)
    return pl.pallas_call(
        body, name="proj_bwd", grid=(S // ts,),
        in_specs=[pl.BlockSpec((ts, D), row), pl.BlockSpec((ts, D), row), pl.BlockSpec((1, D), lambda i: (0, 0)),
                  pl.BlockSpec((nk, kw, N), lambda i: (0, 0, 0))] + [pl.BlockSpec((ts, n), row) for n in widths],
        out_specs=[pl.BlockSpec((ts, D), row), pl.BlockSpec((ts, N), row), pl.BlockSpec((1, D), lambda i: (0, 0))],
        out_shape=[jax.ShapeDtypeStruct((S, D), F32), jax.ShapeDtypeStruct((S, N), _MM),
                   jax.ShapeDtypeStruct((1, D), F32)],
        compiler_params=_cparams(("arbitrary",)),
    )(h, dho, g, w, *pieces)


def _pool_counts(first_row, ts):
    pos = (first_row + lax.broadcasted_iota(jnp.int32, (ts, 1), 0) + 1).astype(F32)
    return [jnp.minimum(pos, float(w)) for w in WINDOWS]


def _pool_delta(x, prev, cnts, ts):
    xe = jnp.concatenate([prev, x], axis=0)
    sums = []
    cur = xe
    for sh in (1, 2, 4, 8):
        cur = cur + pltpu.roll(cur, sh, 0)
        sums.append(cur)
    return [sums[gi][HALO:, gi * PG:(gi + 1) * PG] / cnts[gi] - x[:, gi * PG:(gi + 1) * PG]
            for gi in range(len(WINDOWS))]


def _pool_fwd(proj, pw, ps):
    S = proj.shape[0]
    ts = _row_tile(S, 256)

    def body(x_ref, pw_ref, ps_ref, tok_ref, carry_ref):
        i = pl.program_id(0)

        @pl.when(i == 0)
        def _():
            carry_ref[...] = jnp.zeros_like(carry_ref)

        x = x_ref[...]
        ds = _pool_delta(x, carry_ref[...], _pool_counts(i * ts, ts), ts)
        y = jnp.concatenate([_mm(ds[gi], pw_ref[gi]) for gi in range(len(WINDOWS))], axis=-1)
        tok_ref[...] = (y * ps_ref[...]).astype(tok_ref.dtype)
        carry_ref[...] = x[ts - HALO:, :]

    return pl.pallas_call(
        body, name="pool_fwd", grid=(S // ts,),
        in_specs=[pl.BlockSpec((ts, DTOK), lambda i: (i, 0)), pl.BlockSpec((len(WINDOWS), PG, PG), lambda i: (0, 0, 0)),
                  pl.BlockSpec((1, DTOK), lambda i: (0, 0))],
        out_specs=pl.BlockSpec((ts, DTOK), lambda i: (i, 0)),
        out_shape=jax.ShapeDtypeStruct((S, DTOK), _MM),
        scratch_shapes=[pltpu.VMEM((HALO, DTOK), F32)],
        compiler_params=_cparams(("arbitrary",)),
    )(proj, pw, ps)


def _pool_bwd(proj, dtok, pw, ps):
    S = proj.shape[0]
    ts = _row_tile(S, 256)
    nt = S // ts
    per = ts // HALO
    ng = len(WINDOWS)

    def body(x_ref, xp_ref, dt_ref, pw_ref, ps_ref, dx_ref, dpw_ref, dps_ref, carry_ref):
        i = pl.program_id(0)
        idx = nt - 1 - i

        @pl.when(i == 0)
        def _():
            carry_ref[...] = jnp.zeros_like(carry_ref)
            dpw_ref[...] = jnp.zeros_like(dpw_ref)
            dps_ref[...] = jnp.zeros_like(dps_ref)

        x = x_ref[...]
        prev = jnp.where(idx > 0, xp_ref[...], 0.0)
        cnts = _pool_counts(idx * ts, ts)
        ds = _pool_delta(x, prev, cnts, ts)
        dt = dt_ref[...]
        y = jnp.concatenate([_mm(ds[gi], pw_ref[gi]) for gi in range(ng)], axis=-1)
        dps_ref[...] += jnp.sum(dt * y, axis=0, keepdims=True)
        dy = (dt * ps_ref[...]).astype(_MM)
        dds = []
        for gi in range(ng):
            dyg = dy[:, gi * PG:(gi + 1) * PG]
            dpw_ref[gi] += _mm_tn(ds[gi], dyg)
            dds.append(_mm_nt(dyg, pw_ref[gi]))
        e = jnp.concatenate([dds[gi] / cnts[gi] for gi in range(ng)], axis=-1)
        ee = jnp.concatenate([e, carry_ref[...]], axis=0)
        rows = ts + HALO
        cur = ee
        outs = []
        for gi, sh in enumerate((1, 2, 4, 8)):
            cur = cur + pltpu.roll(cur, rows - sh, 0)
            outs.append(cur[:ts, gi * PG:(gi + 1) * PG] - dds[gi])
        dx_ref[...] = jnp.concatenate(outs, axis=-1)
        carry_ref[...] = e[:HALO, :]

    rev = lambda i: (nt - 1 - i, 0)
    return pl.pallas_call(
        body, name="pool_bwd", grid=(nt,),
        in_specs=[pl.BlockSpec((ts, DTOK), rev),
                  pl.BlockSpec((HALO, DTOK), lambda i: (jnp.maximum((nt - 1 - i) * per - 1, 0), 0)),
                  pl.BlockSpec((ts, DTOK), rev), pl.BlockSpec((ng, PG, PG), lambda i: (0, 0, 0)),
                  pl.BlockSpec((1, DTOK), lambda i: (0, 0))],
        out_specs=[pl.BlockSpec((ts, DTOK), rev), pl.BlockSpec((ng, PG, PG), lambda i: (0, 0, 0)),
                   pl.BlockSpec((1, DTOK), lambda i: (0, 0))],
        out_shape=[jax.ShapeDtypeStruct((S, DTOK), F32), jax.ShapeDtypeStruct((ng, PG, PG), F32),
                   jax.ShapeDtypeStruct((1, DTOK), F32)],
        scratch_shapes=[pltpu.VMEM((HALO, DTOK), F32)],
        compiler_params=_cparams(("arbitrary",)),
    )(proj, proj, dtok, pw, ps)


def _tri(strict):
    r = lax.broadcasted_iota(jnp.int32, (QB, QB), 0)
    c = lax.broadcasted_iota(jnp.int32, (QB, QB), 1)
    return jnp.where(r > c if strict else r >= c, 1.0, 0.0).astype(_MM)


def _sb_terms(z, valid):
    e = jnp.exp(-jnp.abs(z))
    sp = jnp.log(1.0 + e)
    ls = jnp.minimum(z, 0.0) - sp
    lf = jnp.where(valid, jnp.minimum(-z, 0.0) - sp, 0.0)
    return ls, lf, e


def _sb_alive(cs, lo=0):
    top = cs[0][lo:]
    for c in cs[1:]:
        top = jnp.maximum(top, c[lo:])
    return (jnp.max(top) > NEG_CUT).astype(jnp.int32)


def _sb_split(x2, head_a):
    zero = jnp.zeros_like(x2)
    return jnp.where(head_a, x2, zero), jnp.where(head_a, zero, x2)


SB_GROUPS_FWD = 4
SB_GROUPS_BWD = 4
SB_TOP = 48


def _sb_fwd(proj):
    S = proj.shape[0]
    G = min(SB_GROUPS_FWD, S // QB)
    nstep = S // (G * QB)
    npair = DTOK // 128
    nch = 2 * G

    def body(q_ref, k_ref, v_ref, tri_ref, o_ref):
        qb = pl.program_id(1)
        head_a = lax.broadcasted_iota(jnp.int32, (QB, 128), 1) < HD
        qs = []
        for g in range(G):
            qs.extend(_sb_split(q_ref[g * QB:(g + 1) * QB, :], head_a))
        tri = tri_ref[...]
        last = G * qb + G - 1

        def trips(rows):
            qr = [q[:rows] for q in qs]
            row = lax.broadcasted_iota(jnp.int32, (rows, QB), 0)
            col = lax.broadcasted_iota(jnp.int32, (rows, QB), 1)

            def step(carry):
                i, _, cs, accs = carry
                causal = (col - row) < i * QB
                ks, vs, valids = [], [], []
                for g in range(G):
                    kb = G * qb + g - i
                    off = pl.multiple_of(jnp.maximum(kb, 0) * QB, QB)
                    ks.append(k_ref[pl.ds(off, QB), :])
                    vs.append(v_ref[pl.ds(off, QB), :])
                    valids.append(jnp.logical_and(causal, kb >= 0))
                zs = [_mm_nt(qr[ch], ks[ch // 2]) * SCALE for ch in range(nch)]
                terms = [_sb_terms(zs[ch], valids[ch // 2]) for ch in range(nch)]
                withins = [_mm2(terms[ch][1], tri) for ch in range(nch)]
                new_c, new_acc = [], []
                for ch in range(nch):
                    ls, lf, _ = terms[ch]
                    a = jnp.where(valids[ch // 2], jnp.exp(ls + withins[ch] + cs[ch]), 0.0)
                    new_acc.append(accs[ch] + _mm2(a, vs[ch // 2]))
                    new_c.append(cs[ch] + jnp.sum(lf, axis=-1, keepdims=True))
                return i + 1, _sb_alive(new_c, SB_TOP if rows == QB else 0), tuple(new_c), tuple(new_acc)

            return step

        zc = jnp.zeros((QB, 1), F32)
        za = jnp.zeros((QB, 128), F32)
        more = lambda cr: jnp.logical_and(cr[0] <= last, cr[1] > 0)
        i, _, cs, accs = lax.while_loop(more, trips(QB), (0, 1, (zc,) * nch, (za,) * nch))
        cs_top = tuple(c[:SB_TOP] for c in cs)
        acc_top = tuple(a[:SB_TOP] for a in accs)
        acc_top = lax.while_loop(more, trips(SB_TOP), (i, _sb_alive(cs_top), cs_top, acc_top))[3]
        for g in range(G):
            o_ref[g * QB:g * QB + SB_TOP, :] = jnp.where(head_a[:SB_TOP], acc_top[2 * g], acc_top[2 * g + 1])
            o_ref[g * QB + SB_TOP:(g + 1) * QB, :] = jnp.where(head_a, accs[2 * g], accs[2 * g + 1])[SB_TOP:]

    return pl.pallas_call(
        body, name="sb_fwd", grid=(npair, nstep),
        in_specs=[pl.BlockSpec((G * QB, 128), lambda p, i: (i, p)),
                  pl.BlockSpec((S, 128), lambda p, i: (0, npair + p)),
                  pl.BlockSpec((S, 128), lambda p, i: (0, 2 * npair + p)),
                  pl.BlockSpec((QB, QB), lambda p, i: (0, 0))],
        out_specs=pl.BlockSpec((G * QB, 128), lambda p, i: (i, p)),
        out_shape=jax.ShapeDtypeStruct((S, DTOK), F32),
        compiler_params=_cparams(("parallel", "parallel")),
    )(proj, proj, proj, _tri(True))


def _sb_bwd(proj, o, do):
    S = proj.shape[0]
    G = min(SB_GROUPS_BWD, S // QB)
    nstep = S // (G * QB)
    npair = DTOK // 128
    nch = 2 * G

    def body(q_ref, k_ref, v_ref, o_ref, do_ref, tri_ref, tri2_ref, dq_ref, dk_ref, dv_ref):
        qb = pl.program_id(1)

        @pl.when(qb == 0)
        def _():
            dk_ref[...] = jnp.zeros_like(dk_ref)
            dv_ref[...] = jnp.zeros_like(dv_ref)

        head_a = lax.broadcasted_iota(jnp.int32, (QB, 128), 1) < HD
        qs, dos, gs = [], [], []
        for g in range(G):
            rows = slice(g * QB, (g + 1) * QB)
            qs.extend(_sb_split(q_ref[rows, :], head_a))
            dob = do_ref[rows, :].astype(_MM)
            dos.extend(_sb_split(dob, head_a))
            go = dob.astype(F32) * o_ref[rows, :]
            gs.append(jnp.sum(jnp.where(head_a, go, 0.0), axis=-1, keepdims=True))
            gs.append(jnp.sum(jnp.where(head_a, 0.0, go), axis=-1, keepdims=True))
        tri = tri_ref[...]
        tri2 = tri2_ref[...]
        last = G * qb + G - 1

        def trips(rows):
            qr = [q[:rows] for q in qs]
            dor = [d[:rows] for d in dos]
            gr = [g_[:rows] for g_ in gs]
            row = lax.broadcasted_iota(jnp.int32, (rows, QB), 0)
            col = lax.broadcasted_iota(jnp.int32, (rows, QB), 1)

            def step(carry):
                i, _, cs, rs, dqs = carry
                causal = (col - row) < i * QB
                offs, ks, vs, valids = [], [], [], []
                for g in range(G):
                    kb = G * qb + g - i
                    off = pl.multiple_of(jnp.maximum(kb, 0) * QB, QB)
                    offs.append(off)
                    ks.append(k_ref[pl.ds(off, QB), :])
                    vs.append(v_ref[pl.ds(off, QB), :])
                    valids.append(jnp.logical_and(causal, kb >= 0))
                zs = [_mm_nt(qr[ch], ks[ch // 2]) * SCALE for ch in range(nch)]
                das = [_mm_nt(dor[ch], vs[ch // 2]) for ch in range(nch)]
                terms = [_sb_terms(zs[ch], valids[ch // 2]) for ch in range(nch)]
                withins = [_mm2(terms[ch][1], tri) for ch in range(nch)]
                a_s, dlws = [], []
                for ch in range(nch):
                    a = jnp.where(valids[ch // 2], jnp.exp(terms[ch][0] + withins[ch] + cs[ch]), 0.0)
                    a_s.append(a)
                    dlws.append(das[ch] * a)
                sfx = [_mm2(dlws[ch], tri2) for ch in range(nch)]
                dzs = []
                for ch in range(nch):
                    e = terms[ch][2]
                    inv = 1.0 / (1.0 + e)
                    pos = zs[ch] >= 0.0
                    beta = jnp.where(pos, 1.0, e) * inv
                    omb = jnp.where(pos, e, 1.0) * inv
                    prefix = gr[ch] - rs[ch] - sfx[ch]
                    dzs.append((jnp.where(valids[ch // 2], dlws[ch] * omb - beta * prefix, 0.0) * SCALE).astype(_MM))
                new_dq = [dqs[ch] + _mm(dzs[ch], ks[ch // 2]) for ch in range(nch)]
                for g in range(G):
                    a, b = 2 * g, 2 * g + 1
                    dk_ref[pl.ds(offs[g], QB), :] += _mm_tn(dzs[a], qr[a]) + _mm_tn(dzs[b], qr[b])
                    dv_ref[pl.ds(offs[g], QB), :] += _mm_tn(a_s[a], dor[a]) + _mm_tn(a_s[b], dor[b])
                new_c = [cs[ch] + jnp.sum(terms[ch][1], axis=-1, keepdims=True) for ch in range(nch)]
                new_r = [rs[ch] + jnp.sum(dlws[ch], axis=-1, keepdims=True) for ch in range(nch)]
                alive = _sb_alive(new_c, SB_TOP if rows == QB else 0)
                return i + 1, alive, tuple(new_c), tuple(new_r), tuple(new_dq)

            return step

        zc = jnp.zeros((QB, 1), F32)
        za = jnp.zeros((QB, 128), F32)
        more = lambda cr: jnp.logical_and(cr[0] <= last, cr[1] > 0)
        i, _, cs, rs, dqs = lax.while_loop(more, trips(QB), (0, 1, (zc,) * nch, (zc,) * nch, (za,) * nch))
        top = lambda xs: tuple(x[:SB_TOP] for x in xs)
        cs_top = top(cs)
        dq_top = lax.while_loop(more, trips(SB_TOP), (i, _sb_alive(cs_top), cs_top, top(rs), top(dqs)))[4]
        for g in range(G):
            dq_ref[g * QB:g * QB + SB_TOP, :] = jnp.where(head_a[:SB_TOP], dq_top[2 * g], dq_top[2 * g + 1])
            dq_ref[g * QB + SB_TOP:(g + 1) * QB, :] = jnp.where(head_a, dqs[2 * g], dqs[2 * g + 1])[SB_TOP:]

    qspec = pl.BlockSpec((G * QB, 128), lambda p, i: (i, p))
    full = lambda base: pl.BlockSpec((S, 128), lambda p, i: (0, base + p))
    tspec = pl.BlockSpec((QB, QB), lambda p, i: (0, 0))
    return pl.pallas_call(
        body, name="sb_bwd", grid=(npair, nstep),
        in_specs=[qspec, full(npair), full(2 * npair), qspec, qspec, tspec, tspec],
        out_specs=[qspec, full(0), full(0)],
        out_shape=[jax.ShapeDtypeStruct((S, DTOK), F32)] * 3,
        compiler_params=_cparams(("parallel", "arbitrary")),
    )(proj, proj, proj, o, do, _tri(True), _tri(False))


def _memkv_fwd(mem, gm, wkv):
    nk, kw, _ = wkv.shape

    def body(m_ref, g_ref, w_ref, kv_ref):
        m = m_ref[...]
        mn = (m * _rstd(m) * g_ref[...]).astype(_MM)
        acc = _mm(mn[:, :kw], w_ref[0])
        for j in range(1, nk):
            acc = acc + _mm(mn[:, j * kw:(j + 1) * kw], w_ref[j])
        kv_ref[...] = acc.astype(kv_ref.dtype)

    return pl.pallas_call(
        body, name="memkv_fwd", out_shape=jax.ShapeDtypeStruct((MEM_LEN, 2 * DMEM), _MM),
        compiler_params=pltpu.CompilerParams(vmem_limit_bytes=VMEM_LIMIT),
    )(mem, gm, wkv)


def _memkv_bwd(mem, gm, wkv, dkv):
    nk, kw, _ = wkv.shape

    def body(m_ref, g_ref, w_ref, dkv_ref, dw_ref, dg_ref):
        m = m_ref[...]
        mh = m * _rstd(m)
        mn = (mh * g_ref[...]).astype(_MM)
        dkv = dkv_ref[...].astype(_MM)
        dw_ref[...] = _mm_tn(mn, dkv)
        dmn = jnp.concatenate([_mm_nt(dkv, w_ref[j]) for j in range(nk)], axis=-1)
        dg_ref[...] = jnp.sum(dmn * mh, axis=0, keepdims=True)

    return pl.pallas_call(
        body, name="memkv_bwd",
        out_shape=[jax.ShapeDtypeStruct((D, 2 * DMEM), F32), jax.ShapeDtypeStruct((1, D), F32)],
        compiler_params=pltpu.CompilerParams(vmem_limit_bytes=VMEM_LIMIT),
    )(mem, gm, wkv, dkv)


def _mixout_fwd(h, tok, proj, kv, wo, g, qm_block):
    S = h.shape[0]
    ts = _row_tile(S, 256)

    def body(h_ref, tok_ref, qm_ref, kv_ref, wo_ref, g_ref, ho_ref, cat_ref, mix_ref):
        head_a = lax.broadcasted_iota(jnp.int32, (ts, 128), 1) < HD
        qm = qm_ref[...].astype(_MM)
        zq = jnp.zeros((ts, 128), _MM)
        npair = DMEM // 128
        qhs = []
        for pi in range(npair):
            qp = qm[:, pi * 128:(pi + 1) * 128]
            qhs += [jnp.where(head_a, qp, zq), jnp.where(head_a, zq, qp)]
        scores = [_mm_nt(qhs[h], kv_ref[:, (h // 2) * 128:(h // 2 + 1) * 128]) * SCALE for h in range(2 * npair)]
        ps = []
        for s in scores:
            e = jnp.exp(s - jnp.max(s, axis=-1, keepdims=True))
            ps.append(e / jnp.sum(e, axis=-1, keepdims=True))
        pvs = [_mm(ps[h], kv_ref[:, DMEM + (h // 2) * 128:DMEM + (h // 2 + 1) * 128]) for h in range(2 * npair)]
        mos = [jnp.where(head_a, pvs[2 * pi], pvs[2 * pi + 1]) for pi in range(npair)]
        cat = jnp.concatenate([tok_ref[...].astype(_MM)] + [m.astype(_MM) for m in mos], axis=-1)
        cat_ref[...] = cat
        mix = jnp.concatenate([_mm(cat, wo_ref[j]) for j in range(N_CHIPS)], axis=-1)
        mix_ref[...] = mix
        ho_ref[...] = h_ref[...] + mix * _rstd(mix) * g_ref[...]

    row = lambda i: (i, 0)
    return pl.pallas_call(
        body, name="mixout_fwd", grid=(S // ts,),
        in_specs=[pl.BlockSpec((ts, D), row), pl.BlockSpec((ts, DTOK), row),
                  pl.BlockSpec((ts, DMEM), lambda i: (i, qm_block)),
                  pl.BlockSpec((MEM_LEN, 2 * DMEM), lambda i: (0, 0)),
                  pl.BlockSpec((N_CHIPS, DMIX, D // N_CHIPS), lambda i: (0, 0, 0)), pl.BlockSpec((1, D), lambda i: (0, 0))],
        out_specs=[pl.BlockSpec((ts, D), row), pl.BlockSpec((ts, DMIX), row), pl.BlockSpec((ts, D), row)],
        out_shape=[jax.ShapeDtypeStruct((S, D), F32), jax.ShapeDtypeStruct((S, DMIX), _MM),
                   jax.ShapeDtypeStruct((S, D), F32)],
        compiler_params=_cparams(("parallel",)),
    )(h, tok, proj, kv, wo, g)


def _mixout_bwd(dho, mix, proj, kv, wo, g, qm_block):
    S = dho.shape[0]
    ts = _row_tile(S, 256)

    def body(dho_ref, mix_ref, qm_ref, kv_ref, wo_ref, g_ref, dmix_ref, dtok_ref, dqm_ref, dkv_ref, dg_ref):
        @pl.when(pl.program_id(0) == 0)
        def _():
            dkv_ref[...] = jnp.zeros_like(dkv_ref)
            dg_ref[...] = jnp.zeros_like(dg_ref)

        dmix, dg = _rms_bwd(mix_ref[...], g_ref[...], dho_ref[...])
        dg_ref[...] += dg
        dmb = dmix.astype(_MM)
        dmix_ref[...] = dmb
        cw = D // N_CHIPS
        dcat = _mm_nt(dmb[:, :cw], wo_ref[0])
        for j in range(1, N_CHIPS):
            dcat = dcat + _mm_nt(dmb[:, j * cw:(j + 1) * cw], wo_ref[j])
        dtok_ref[...] = dcat[:, :DTOK]
        head_a = lax.broadcasted_iota(jnp.int32, (ts, 128), 1) < HD
        qm = qm_ref[...].astype(_MM)
        zq = jnp.zeros((ts, 128), _MM)
        npair = DMEM // 128
        kps = [kv_ref[:, pi * 128:(pi + 1) * 128] for pi in range(npair)]
        vps = [kv_ref[:, DMEM + pi * 128:DMEM + (pi + 1) * 128] for pi in range(npair)]
        qhs, dmhs = [], []
        for pi in range(npair):
            cs = slice(pi * 128, (pi + 1) * 128)
            dmo = dcat[:, DTOK + pi * 128:DTOK + (pi + 1) * 128].astype(_MM)
            qhs += [jnp.where(head_a, qm[:, cs], zq), jnp.where(head_a, zq, qm[:, cs])]
            dmhs += [jnp.where(head_a, dmo, zq), jnp.where(head_a, zq, dmo)]
        nh = 2 * npair
        scores = [_mm_nt(qhs[h], kps[h // 2]) * SCALE for h in range(nh)]
        dps = [_mm_nt(dmhs[h], vps[h // 2]) for h in range(nh)]
        ps, dss = [], []
        for h in range(nh):
            e = jnp.exp(scores[h] - jnp.max(scores[h], axis=-1, keepdims=True))
            p = e / jnp.sum(e, axis=-1, keepdims=True)
            ps.append(p)
            dss.append((p * (dps[h] - jnp.sum(dps[h] * p, axis=-1, keepdims=True)) * SCALE).astype(_MM))
        dqs = [_mm(dss[h], kps[h // 2]) for h in range(nh)]
        for pi in range(npair):
            cs = slice(pi * 128, (pi + 1) * 128)
            a, b = 2 * pi, 2 * pi + 1
            dqm_ref[:, cs] = jnp.where(head_a, dqs[a], dqs[b]).astype(_MM)
            dkv_ref[:, cs] += _mm_tn(dss[a], qhs[a]) + _mm_tn(dss[b], qhs[b])
            dkv_ref[:, DMEM + pi * 128:DMEM + (pi + 1) * 128] += _mm_tn(ps[a], dmhs[a]) + _mm_tn(ps[b], dmhs[b])

    row = lambda i: (i, 0)
    fix = lambda i: (0, 0)
    return pl.pallas_call(
        body, name="mixout_bwd", grid=(S // ts,),
        in_specs=[pl.BlockSpec((ts, D), row), pl.BlockSpec((ts, D), row),
                  pl.BlockSpec((ts, DMEM), lambda i: (i, qm_block)),
                  pl.BlockSpec((MEM_LEN, 2 * DMEM), fix),
                  pl.BlockSpec((N_CHIPS, DMIX, D // N_CHIPS), lambda i: (0, 0, 0)), pl.BlockSpec((1, D), fix)],
        out_specs=[pl.BlockSpec((ts, D), row), pl.BlockSpec((ts, DTOK), row), pl.BlockSpec((ts, DMEM), row),
                   pl.BlockSpec((MEM_LEN, 2 * DMEM), fix), pl.BlockSpec((1, D), fix)],
        out_shape=[jax.ShapeDtypeStruct((S, D), _MM), jax.ShapeDtypeStruct((S, DTOK), F32),
                   jax.ShapeDtypeStruct((S, DMEM), _MM), jax.ShapeDtypeStruct((MEM_LEN, 2 * DMEM), F32),
                   jax.ShapeDtypeStruct((1, D), F32)],
        compiler_params=_cparams(("arbitrary",)),
    )(dho, mix, proj, kv, wo, g)


def _loss_grad(y, target):
    S = y.shape[0]
    ts = _row_tile(S, 512)

    def body(y_ref, t_ref, acc_ref, dy_ref):
        @pl.when(pl.program_id(0) == 0)
        def _():
            acc_ref[...] = jnp.zeros_like(acc_ref)

        err = y_ref[...] - t_ref[...]
        acc_ref[...] += jnp.sum(err * err, axis=0, keepdims=True)
        dy_ref[...] = err * (1.0 / D)

    row = lambda i: (i, 0)
    return pl.pallas_call(
        body, name="loss_grad", grid=(S // ts,),
        in_specs=[pl.BlockSpec((ts, D), row), pl.BlockSpec((ts, D), row)],
        out_specs=[pl.BlockSpec((1, D), lambda i: (0, 0)), pl.BlockSpec((ts, D), row)],
        out_shape=[jax.ShapeDtypeStruct((1, D), F32), jax.ShapeDtypeStruct((S, D), F32)],
        compiler_params=_cparams(("arbitrary",)),
    )(y, target)


LAYER_TENSORS = ("ffn1_gate", "ffn1_up", "ffn1_down", "ffn2_gate", "ffn2_up", "ffn2_down", "w_in", "w_mem_kv", "w_out")
SPLITS = {"ffn1_gate": "c", "ffn1_up": "c", "ffn1_down": "r", "ffn2_gate": "c", "ffn2_up": "c", "ffn2_down": "r",
          "w_in_pool": "r", "w_in_sb": "c", "w_mem_kv": "r", "w_out": "c"}


def _is_pool(i):
    return i % 2 == 0


def _qm_block(i):
    return (DTOK if _is_pool(i) else 3 * DTOK) // DMEM


FIRST_FFN = LAYER_TENSORS[:3]
REST = LAYER_TENSORS[3:]


def _layer_forward(h, mem, lw, i, rest_shards, next_shards):
    g_at = lambda g, k: g[k].reshape(1, D)
    sv = {"h0": h}
    (h, sv["a1"], sv["b1"], sv["f1"]), rest = _ffn_fwd(
        h, g_at(lw["g_pre"], 0), g_at(lw["g_post"], 0), lw["ffn1_gate"], lw["ffn1_up"], lw["ffn1_down"],
        _GatherExchange(rest_shards))
    lw.update(zip(REST, rest))
    if not _is_pool(i):
        lw["w_in"] = jnp.transpose(lw["w_in"], (1, 0, 2)).reshape(1, D, DSB)
    sv["h1"] = h
    sv["kv"] = _memkv_fwd(mem, lw["g_mem"], lw["w_mem_kv"])
    if _is_pool(i):
        sv["u"], sv["proj"] = _proj_fwd(h, g_at(lw["g_pre"], 1), lw["w_in"], F32)
        tok = _pool_fwd(sv["proj"], lw["pool_w"], lw["pool_scale"])
    else:
        sv["u"], sv["proj"] = _proj_fwd(h, g_at(lw["g_pre"], 1), lw["w_in"], _MM)
        tok = sv["o"] = _sb_fwd(sv["proj"])
    h, sv["cat"], sv["mix"] = _mixout_fwd(h, tok, sv["proj"], sv["kv"], lw["w_out"], g_at(lw["g_post"], 1),
                                          _qm_block(i))
    sv["h2"] = h
    ffn2 = (h, g_at(lw["g_pre"], 2), g_at(lw["g_post"], 2), lw["ffn2_gate"], lw["ffn2_up"], lw["ffn2_down"])
    if next_shards is None:
        h, sv["a2"], sv["b2"], sv["f2"] = _ffn_fwd(*ffn2)
        return h, sv, None
    (h, sv["a2"], sv["b2"], sv["f2"]), nxt = _ffn_fwd(*ffn2, _GatherExchange(next_shards))
    return h, sv, nxt


def _layer_backward(dh, mem, lw, sv, i, rides=None):
    g_at = lambda g, k: g[k].reshape(1, D)
    big, small = {}, {}
    dgpre, dgpost = [None] * 3, [None] * 3
    rode = {}
    rides = rides or {}

    def hosted(host, fn, *args):
        done = {"small": small, "dgpre": dgpre, "dgpost": dgpost}
        ride = rides[host](big, rode, done) if host in rides else None
        if ride is None:
            return fn(*args)
        outs, rode[host] = fn(*args, ride)
        return outs

    def ffn_back(which, dh, h_in):
        k = 0 if which == 1 else 2
        n = f"ffn{which}"
        dh, nb, sb, da, db, df, dgpre[k], dgpost[k] = hosted(
            n, _ffn_bwd, h_in, sv[f"f{which}"], sv[f"a{which}"], sv[f"b{which}"], dh, g_at(lw["g_pre"], k),
            g_at(lw["g_post"], k), lw[n + "_gate"], lw[n + "_up"], lw[n + "_down"])
        big[n + "_gate"] = hosted(n + "_gate", _wgrad, nb, da, N_CHIPS)
        big[n + "_up"] = hosted(n + "_up", _wgrad, nb, db, N_CHIPS)
        big[n + "_down"] = hosted(n + "_down", _wgrad, sb, df, 0).reshape(N_CHIPS, FF // N_CHIPS, D)
        return dh

    dh = ffn_back(2, dh, sv["h2"])
    dmix, dtok, dqm, dkv, dgpost[1] = _mixout_bwd(dh, sv["mix"], sv["proj"], sv["kv"], lw["w_out"],
                                                  g_at(lw["g_post"], 1), _qm_block(i))
    big["w_out"] = _wgrad(sv["cat"], dmix, N_CHIPS)
    dwkv, small["g_mem"] = _memkv_bwd(mem, lw["g_mem"], lw["w_mem_kv"], dkv)
    big["w_mem_kv"] = dwkv.reshape(N_CHIPS, D // N_CHIPS, 2 * DMEM)
    if _is_pool(i):
        dx, small["pool_w"], small["pool_scale"] = _pool_bwd(sv["proj"], dtok, lw["pool_w"], lw["pool_scale"])
        dh, dproj, dgpre[1] = _proj_bwd(sv["h1"], dh, g_at(lw["g_pre"], 1), lw["w_in"], [dx, dqm])
        big["w_in"] = _wgrad(sv["u"], dproj).reshape(N_CHIPS, D // N_CHIPS, DMIX)
    else:
        dq, dk, dv = _sb_bwd(sv["proj"], sv["o"], dtok)
        dh, dproj, dgpre[1] = _proj_bwd(sv["h1"], dh, g_at(lw["g_pre"], 1), lw["w_in"], [dq, dk, dv, dqm])
        big["w_in"] = jnp.transpose(_wgrad(sv["u"], dproj).reshape(D, N_CHIPS, DSB // N_CHIPS), (1, 0, 2))
    dh = ffn_back(1, dh, sv["h0"])
    small["g_pre"] = jnp.concatenate(dgpre, axis=0)
    small["g_post"] = jnp.concatenate(dgpost, axis=0)
    return dh, big, small, rode


ANY = pl.BlockSpec(memory_space=pl.ANY)


def _place():
    return lax.axis_index("x"), lax.axis_index("y"), lax.axis_index("c")


def _flip(v, bit):
    return 1 - v if bit else v


def _dma_sems(m):
    return pltpu.SemaphoreType.DMA((m,))


class _SmallExchange:
    name = "small_allgather"

    def __init__(self, buf):
        self.inputs = [buf]
        self.out_shapes = [jax.ShapeDtypeStruct((N_DEV,) + buf.shape, buf.dtype)]
        self.sems = [_dma_sems(N_DEV - 1), _dma_sems(N_DEV - 1)]

    def _copies(self, ins, outs, sems):
        x, y, c = _place()
        me = 4 * x + 2 * y + c
        out, arrive = [], []
        for k in range(1, N_DEV):
            peer = (_flip(x, k & 4), _flip(y, k & 2), _flip(c, k & 1))
            slot = 4 * peer[0] + 2 * peer[1] + peer[2]
            how = dict(send_sem=sems[0].at[k - 1], recv_sem=sems[1].at[k - 1], device_id=peer, device_id_type=MESH)
            out.append(pltpu.make_async_remote_copy(src_ref=ins[0], dst_ref=outs[0].at[me], **how))
            arrive.append(pltpu.make_async_remote_copy(src_ref=ins[0], dst_ref=outs[0].at[slot], **how))
        return out, arrive

    def start(self, ins, outs, sems):
        for cp in self._copies(ins, outs, sems)[0]:
            cp.start()

    def mid(self, ins, outs, sems):
        pass

    def finish(self, ins, outs, sems):
        out, arrive = self._copies(ins, outs, sems)
        for cp in arrive:
            cp.wait_recv()
        for cp in out:
            cp.wait_send()


def _with_own(gathered, buf, me):
    slots = lax.broadcasted_iota(jnp.int32, (N_DEV, 1, 1), 0)
    return jnp.where(slots == me, buf[None], gathered)


class _GatherExchange:
    name = "gather_weights"
    aliased = True

    def __init__(self, bufs):
        n = len(bufs)
        self.n, self.inputs = n, list(bufs)
        self.halves = [b.shape[1] // 2 for b in bufs]
        self.out_shapes = [jax.ShapeDtypeStruct(b.shape, b.dtype) for b in bufs]
        self.sems = [_dma_sems(3 * n), _dma_sems(3 * n), _dma_sems(3 * n), _dma_sems(3 * n)]

    def _copies(self, outs, sems):
        ici_send, ici_recv, d2d_send, d2d_recv = sems
        x, y, c = _place()
        me = 2 * x + y
        sib = (x, y, 1 - c)
        chips = [(_flip(x, k & 2), _flip(y, k & 1)) for k in (1, 2, 3)]
        half = lambda t, slot, hc: outs[t].at[slot, pl.ds(hc * self.halves[t], self.halves[t])]
        ici_out, ici_in, fwd_out, fwd_in = [], [], [], []
        for t in range(self.n):
            for k, chip in enumerate(chips):
                idx = 2 * chip[0] + chip[1]
                peer = (chip[0], chip[1], c)
                ici = dict(send_sem=ici_send.at[3 * t + k], recv_sem=ici_recv.at[3 * t + k], device_id=peer,
                           device_id_type=MESH)
                d2d = dict(send_sem=d2d_send.at[3 * t + k], recv_sem=d2d_recv.at[3 * t + k], device_id=sib,
                           device_id_type=MESH)
                ici_out.append(pltpu.make_async_remote_copy(src_ref=half(t, me, c), dst_ref=half(t, me, c), **ici))
                ici_in.append(pltpu.make_async_remote_copy(src_ref=half(t, idx, c), dst_ref=half(t, idx, c), **ici))
                fwd_out.append(pltpu.make_async_remote_copy(src_ref=half(t, idx, c), dst_ref=half(t, idx, c), **d2d))
                fwd_in.append(pltpu.make_async_remote_copy(
                    src_ref=half(t, idx, 1 - c), dst_ref=half(t, idx, 1 - c), **d2d))
        return ici_out, ici_in, fwd_out, fwd_in

    def start(self, ins, outs, sems):
        for cp in self._copies(outs, sems)[0]:
            cp.start()

    def mid(self, ins, outs, sems):
        _, ici_in, fwd_out, _ = self._copies(outs, sems)
        for arrived, onward in zip(ici_in, fwd_out):
            arrived.wait_recv()
            onward.start()

    def finish(self, ins, outs, sems):
        ici_out, _, fwd_out, fwd_in = self._copies(outs, sems)
        for cp in fwd_in:
            cp.wait_recv()
        for cp in ici_out + fwd_out:
            cp.wait_send()


class _SiblingExchange:
    name = "rs_to_sibling"

    def __init__(self, gs):
        n = len(gs)
        self.n, self.inputs = n, list(gs)
        self.halves = [g.shape[1] // 2 for g in gs]
        self.out_shapes = [jax.ShapeDtypeStruct((N_CHIPS, g.shape[1] // 2) + g.shape[2:], g.dtype) for g in gs]
        self.sems = [_dma_sems(n), _dma_sems(n)]

    def _copies(self, ins, outs, sems):
        x, y, c = _place()
        return [pltpu.make_async_remote_copy(
            src_ref=ins[t].at[:, pl.ds((1 - c) * self.halves[t], self.halves[t])], dst_ref=outs[t],
            send_sem=sems[0].at[t], recv_sem=sems[1].at[t], device_id=(x, y, 1 - c), device_id_type=MESH)
            for t in range(self.n)]

    def start(self, ins, outs, sems):
        for cp in self._copies(ins, outs, sems):
            cp.start()

    def mid(self, ins, outs, sems):
        pass

    def finish(self, ins, outs, sems):
        for cp in self._copies(ins, outs, sems):
            cp.wait()


class _ChipsExchange:
    name = "rs_to_chips"

    def __init__(self, ss):
        n = len(ss)
        self.n, self.inputs = n, list(ss)
        self.out_shapes = [jax.ShapeDtypeStruct((3,) + s.shape[1:], s.dtype) for s in ss]
        self.sems = [_dma_sems(3 * n), _dma_sems(3 * n)]

    def _copies(self, ins, outs, sems):
        x, y, c = _place()
        chips = [(_flip(x, k & 2), _flip(y, k & 1)) for k in (1, 2, 3)]
        return [pltpu.make_async_remote_copy(
            src_ref=ins[t].at[2 * chip[0] + chip[1]], dst_ref=outs[t].at[k], send_sem=sems[0].at[3 * t + k],
            recv_sem=sems[1].at[3 * t + k], device_id=(chip[0], chip[1], c), device_id_type=MESH)
            for t in range(self.n) for k, chip in enumerate(chips)]

    def start(self, ins, outs, sems):
        for cp in self._copies(ins, outs, sems):
            cp.start()

    def mid(self, ins, outs, sems):
        pass

    def finish(self, ins, outs, sems):
        for cp in self._copies(ins, outs, sems):
            cp.wait()


class _Both:
    def __init__(self, first, second):
        self.parts = (first, second)
        self.name = first.name + "_" + second.name
        self.inputs = first.inputs + second.inputs
        self.out_shapes = first.out_shapes + second.out_shapes
        self.sems = first.sems + second.sems

    def _each(self, ins, outs, sems):
        a = self.parts[0]
        ni, no, ns = len(a.inputs), len(a.out_shapes), len(a.sems)
        return ((a, ins[:ni], outs[:no], sems[:ns]), (self.parts[1], ins[ni:], outs[no:], sems[ns:]))

    def start(self, ins, outs, sems):
        for ex, i, o, s in self._each(ins, outs, sems):
            ex.start(i, o, s)

    def mid(self, ins, outs, sems):
        for ex, i, o, s in self._each(ins, outs, sems):
            ex.mid(i, o, s)

    def finish(self, ins, outs, sems):
        for ex, i, o, s in self._each(ins, outs, sems):
            ex.finish(i, o, s)

    def split(self, outs):
        no = len(self.parts[0].out_shapes)
        return outs[:no], outs[no:]


def _run_exchange(ex):
    ni, no = len(ex.inputs), len(ex.out_shapes)

    def body(*refs):
        ins, outs, sems = refs[:ni], refs[ni:ni + no], refs[ni + no:]
        ex.start(ins, outs, sems)
        ex.mid(ins, outs, sems)
        ex.finish(ins, outs, sems)

    return pl.pallas_call(
        body, name=ex.name, in_specs=[ANY] * ni, out_specs=[ANY] * no, out_shape=ex.out_shapes,
        scratch_shapes=ex.sems, input_output_aliases={t: t for t in range(ni)} if _in_place(ex) else {},
    )(*ex.inputs)


def _in_place(ex):
    return getattr(ex, "aliased", False)


def _place_own(shard):
    L_, r, n_ = shard.shape
    tr = _row_tile(r, 512 if r % 512 == 0 else 256)

    def body(x_ref, *refs):
        outs, stage, sems = refs[:L_], refs[L_], refs[L_ + 1]
        x, y, _ = _place()
        rows = pl.ds(pl.multiple_of(pl.program_id(0) * tr, tr), tr)
        copies = []
        for l in range(L_):
            stage[l] = x_ref[l].astype(_MM)
            cp = pltpu.make_async_copy(stage.at[l], outs[l].at[2 * x + y, rows], sems.at[l])
            cp.start()
            copies.append(cp)
        for cp in copies:
            cp.wait()

    return pl.pallas_call(
        body, name="place_own", grid=(r // tr,),
        in_specs=[pl.BlockSpec((L_, tr, n_), lambda i: (0, i, 0))], out_specs=[ANY] * L_,
        out_shape=[jax.ShapeDtypeStruct((N_CHIPS, r, n_), _MM)] * L_,
        scratch_shapes=[pltpu.VMEM((L_, tr, n_), _MM), _dma_sems(L_)],
        compiler_params=_cparams(("arbitrary",)),
    )(shard)


def _call(body, name, nsteps, in_specs, out_specs, out_shape, args, ride=None):
    if ride is None:
        return pl.pallas_call(body, name=name, grid=(nsteps,), in_specs=in_specs, out_specs=out_specs,
                              out_shape=out_shape, compiler_params=_cparams(("arbitrary",)))(*args)
    ni, no = len(args), len(out_shape)
    ri, ro = len(ride.inputs), len(ride.out_shapes)
    mid_step = (13 * nsteps) // 16 if nsteps > 1 else 0

    def hosted(*refs):
        a, ra = refs[:ni], refs[ni:ni + ri]
        o, ro_refs = refs[ni + ri:ni + ri + no], refs[ni + ri + no:ni + ri + no + ro]
        sems = refs[ni + ri + no + ro:]
        step = pl.program_id(0)

        @pl.when(step == 0)
        def _():
            ride.start(ra, ro_refs, sems)

        body(*a, *o)

        @pl.when(step == mid_step)
        def _():
            ride.mid(ra, ro_refs, sems)

        @pl.when(step == nsteps - 1)
        def _():
            ride.finish(ra, ro_refs, sems)

    outs = pl.pallas_call(
        hosted, name=name + "_" + ride.name, grid=(nsteps,), in_specs=list(in_specs) + [ANY] * ri,
        out_specs=list(out_specs) + [ANY] * ro, out_shape=list(out_shape) + ride.out_shapes,
        scratch_shapes=ride.sems, compiler_params=_cparams(("arbitrary",)),
        input_output_aliases={ni + t: no + t for t in range(ri)} if _in_place(ride) else {},
    )(*args, *ride.inputs)
    return outs[:no], outs[no:]


def _share_halves(ts):
    n = len(ts)

    def body(*refs):
        outs = refs[n:2 * n]
        send_sems, recv_sems = refs[2 * n:]
        x, y, c = _place()
        cps = []
        for t in range(n):
            cp = pltpu.make_async_remote_copy(
                src_ref=outs[t].at[c], dst_ref=outs[t].at[c], send_sem=send_sems.at[t], recv_sem=recv_sems.at[t],
                device_id=(x, y, 1 - c), device_id_type=MESH)
            cp.start()
            cps.append(cp)
        for t in range(n):
            pltpu.make_async_remote_copy(
                src_ref=outs[t].at[1 - c], dst_ref=outs[t].at[1 - c], send_sem=send_sems.at[t],
                recv_sem=recv_sems.at[t], device_id=(x, y, 1 - c), device_id_type=MESH).wait_recv()
        for cp in cps:
            cp.wait_send()

    dma = lambda m: pltpu.SemaphoreType.DMA((m,))
    return pl.pallas_call(
        body, name="share_halves", in_specs=[ANY] * n, out_specs=[ANY] * n,
        out_shape=[jax.ShapeDtypeStruct(t.shape, t.dtype) for t in ts],
        input_output_aliases={t: t for t in range(n)},
        scratch_shapes=[dma(n), dma(n)],
    )(*ts)


def _by_shape(fn, *lists):
    groups = {}
    for idx, a in enumerate(lists[0]):
        groups.setdefault(a.shape, []).append(idx)
    out = [None] * len(lists[0])
    for idxs in groups.values():
        for idx, res in zip(idxs, fn(*[[lst[idx] for idx in idxs] for lst in lists])):
            out[idx] = res
    return out


def _add_sibling(gs, rs, c):
    m = len(gs)
    _, r_, n_ = gs[0].shape
    h = r_ // 2
    tr = _row_tile(h, 512)

    def body(c_ref, *refs):
        for g_ref, r_ref, o_ref in zip(refs[:m], refs[m:2 * m], refs[2 * m:]):
            o_ref[...] = (g_ref[...] + r_ref[...]).astype(o_ref.dtype)

    gspec = pl.BlockSpec((None, None, tr, n_), lambda s, i, c_ref: (s, c_ref[0], i, 0))
    hspec = pl.BlockSpec((None, tr, n_), lambda s, i, c_ref: (s, i, 0))
    return pl.pallas_call(
        body, name="add_sibling",
        grid_spec=pltpu.PrefetchScalarGridSpec(
            num_scalar_prefetch=1, grid=(N_CHIPS, h // tr), in_specs=[gspec] * m + [hspec] * m, out_specs=[hspec] * m),
        out_shape=[jax.ShapeDtypeStruct((N_CHIPS, h, n_), _MM)] * m,
        compiler_params=_cparams(("parallel", "parallel")),
    )(c.reshape(1), *[g.reshape(N_CHIPS, 2, h, n_) for g in gs], *rs)


def _add_chips(ss, rs, me, c):
    m = len(ss)
    _, h, n_ = ss[0].shape
    tr = _row_tile(h, 512)

    def body(at_ref, *refs):
        f = lambda ref: ref[...].astype(F32)
        for t in range(m):
            s_ref, (r0, r1, r2), o_ref = refs[t], refs[m + 3 * t:m + 3 * t + 3], refs[4 * m + t]
            o_ref[...] = ((f(s_ref) + f(r0)) + f(r1)) + f(r2)

    rk = lambda k: pl.BlockSpec((None, tr, n_), lambda i, at_ref: (k, i, 0))
    return pl.pallas_call(
        body, name="add_chips",
        grid_spec=pltpu.PrefetchScalarGridSpec(
            num_scalar_prefetch=1, grid=(h // tr,),
            in_specs=[pl.BlockSpec((None, tr, n_), lambda i, at_ref: (at_ref[0], i, 0))] * m + [rk(0), rk(1), rk(2)] * m,
            out_specs=[pl.BlockSpec((None, tr, n_), lambda i, at_ref: (at_ref[1], i, 0))] * m),
        out_shape=[jax.ShapeDtypeStruct((2, h, n_), F32)] * m,
        compiler_params=_cparams(("parallel",)),
    )(jnp.stack([me, c]), *ss, *[r for r in rs for _ in range(3)])


def _sum_devices(a):
    R_ = a.shape[1]

    def body(a_ref, o_ref):
        acc = a_ref[0]
        for k in range(1, N_DEV):
            acc = acc + a_ref[k]
        o_ref[...] = acc

    return pl.pallas_call(
        body, name="sum_devices", out_shape=jax.ShapeDtypeStruct((R_, 128), F32),
        compiler_params=pltpu.CompilerParams(vmem_limit_bytes=VMEM_LIMIT),
    )(a)


def _adamw(w, g, m, v):
    shape = w.shape
    to2 = lambda a: a.reshape(-1, shape[-1])
    w2, g2, m2, v2 = to2(w), to2(g), to2(m), to2(v)
    R_, C = w2.shape
    tr = 256 if R_ % 256 == 0 else R_

    def body(w_ref, g_ref, m_ref, v_ref, d_ref, mo_ref, vo_ref):
        gv = g_ref[...]
        mn = ADAM_B1 * m_ref[...] + (1.0 - ADAM_B1) * gv
        vn = ADAM_B2 * v_ref[...] + (1.0 - ADAM_B2) * (gv * gv)
        m_hat = mn / (1.0 - ADAM_B1 ** ADAM_STEP)
        v_hat = vn / (1.0 - ADAM_B2 ** ADAM_STEP)
        d_ref[...] = -ADAM_LR * (m_hat / (jnp.sqrt(v_hat) + ADAM_EPS) + ADAM_WD * w_ref[...])
        mo_ref[...] = mn
        vo_ref[...] = vn

    spec = pl.BlockSpec((tr, C), lambda i: (i, 0))
    outs = pl.pallas_call(
        body, name="adamw", grid=(R_ // tr,), in_specs=[spec] * 4, out_specs=[spec] * 3,
        out_shape=[jax.ShapeDtypeStruct((R_, C), F32)] * 3,
        compiler_params=_cparams(("parallel",)),
    )(w2, g2, m2, v2)
    return [o.reshape(shape) for o in outs]


SHARD_AXIS = {"g_pre": 2, "g_post": 2, "g_mem": None, "ffn1_gate": 2, "ffn1_up": 2, "ffn1_down": 1,
              "ffn2_gate": 2, "ffn2_up": 2, "ffn2_down": 1, "w_in_pool": 1, "pool_w": None, "pool_scale": None,
              "w_in_sb": 2, "w_mem_kv": 1, "w_out": 2}
WEIGHTS = list(SHARD_AXIS)
BIG = [k for k in WEIGHTS if SHARD_AXIS[k] is not None and k not in ("g_pre", "g_post")]
SMALL = [k for k in WEIGHTS if k not in BIG]


def _pack128(arrs):
    flat = jnp.concatenate([a.reshape(-1) for a in arrs])
    pad = (-flat.shape[0]) % (8 * 128)
    return jnp.pad(flat, (0, pad)).reshape(-1, 128)


def _unpack128(buf, shapes):
    flat = buf.reshape(-1)
    out, off = [], 0
    for shp in shapes:
        size = 1
        for d in shp:
            size *= d
        out.append(flat[off:off + size].reshape(shp))
        off += size
    return out


def kernel(x, mem, g_pre, g_post, g_mem, ffn1_gate, ffn1_up, ffn1_down, ffn2_gate, ffn2_up, ffn2_down, w_in_pool, pool_w, pool_scale, w_in_sb, w_mem_kv, w_out, loss_target, m_g_pre, m_g_post, m_g_mem, m_ffn1_gate, m_ffn1_up, m_ffn1_down, m_ffn2_gate, m_ffn2_up, m_ffn2_down, m_w_in_pool, m_pool_w, m_pool_scale, m_w_in_sb, m_w_mem_kv, m_w_out, v_g_pre, v_g_post, v_g_mem, v_ffn1_gate, v_ffn1_up, v_ffn1_down, v_ffn2_gate, v_ffn2_up, v_ffn2_down, v_w_in_pool, v_pool_w, v_pool_scale, v_w_in_sb, v_w_mem_kv, v_w_out):
    w = dict(g_pre=g_pre, g_post=g_post, g_mem=g_mem, ffn1_gate=ffn1_gate, ffn1_up=ffn1_up, ffn1_down=ffn1_down,
             ffn2_gate=ffn2_gate, ffn2_up=ffn2_up, ffn2_down=ffn2_down, w_in_pool=w_in_pool, pool_w=pool_w,
             pool_scale=pool_scale, w_in_sb=w_in_sb, w_mem_kv=w_mem_kv, w_out=w_out)
    m = dict(g_pre=m_g_pre, g_post=m_g_post, g_mem=m_g_mem, ffn1_gate=m_ffn1_gate, ffn1_up=m_ffn1_up,
             ffn1_down=m_ffn1_down, ffn2_gate=m_ffn2_gate, ffn2_up=m_ffn2_up, ffn2_down=m_ffn2_down,
             w_in_pool=m_w_in_pool, pool_w=m_pool_w, pool_scale=m_pool_scale, w_in_sb=m_w_in_sb,
             w_mem_kv=m_w_mem_kv, w_out=m_w_out)
    v = dict(g_pre=v_g_pre, g_post=v_g_post, g_mem=v_g_mem, ffn1_gate=v_ffn1_gate, ffn1_up=v_ffn1_up,
             ffn1_down=v_ffn1_down, ffn2_gate=v_ffn2_gate, ffn2_up=v_ffn2_up, ffn2_down=v_ffn2_down,
             w_in_pool=v_w_in_pool, pool_w=v_pool_w, pool_scale=v_pool_scale, w_in_sb=v_w_in_sb,
             w_mem_kv=v_w_mem_kv, w_out=v_w_out)
    cx, cy, cc = _place()
    chip = (2 * cx + cy).astype(jnp.int32)
    core = cc.astype(jnp.int32)

    dev = (4 * cx + 2 * cy + cc).astype(jnp.int32)

    def pack_small(per_layer):
        grads = {"g_pre": jnp.stack([s["g_pre"] for s in per_layer]),
                 "g_post": jnp.stack([s["g_post"] for s in per_layer]),
                 "g_mem": jnp.concatenate([s["g_mem"] for s in per_layer], axis=0),
                 "pool_w": jnp.stack([s["pool_w"] for i, s in enumerate(per_layer) if _is_pool(i)]),
                 "pool_scale": jnp.concatenate([s["pool_scale"] for i, s in enumerate(per_layer) if _is_pool(i)], axis=0)}
        return _pack128([grads[k] for k in SMALL]), [grads[k].shape for k in SMALL]

    gain_buf = _pack128([g_pre, g_post])
    gains = _with_own(_run_exchange(_SmallExchange(gain_buf))[0], gain_buf, dev)
    per_chip = [_unpack128(gains[2 * j], [g_pre.shape, g_post.shape]) for j in range(N_CHIPS)]
    full_pre = jnp.concatenate([p[0] for p in per_chip], axis=2)
    full_post = jnp.concatenate([p[1] for p in per_chip], axis=2)

    def shard_name(k, i):
        return ("w_in_pool" if _is_pool(i) else "w_in_sb") if k == "w_in" else k

    placed = {k: _place_own(w[k]) for k in BIG}

    def own(i, names):
        return [placed[shard_name(k, i)][i // 2 if k == "w_in" else i] for k in names]

    xs, mems = x[0], mem[0]
    h, lws, saved = xs, [], []
    first = _run_exchange(_GatherExchange(own(0, FIRST_FFN)))
    for i in range(DEPTH):
        lw = dict(zip(FIRST_FFN, first))
        lw["g_pre"], lw["g_post"], lw["g_mem"] = full_pre[i], full_post[i], g_mem[i].reshape(1, D)
        if _is_pool(i):
            lw["pool_w"], lw["pool_scale"] = pool_w[i // 2], pool_scale[i // 2].reshape(1, DTOK)
        h, sv, first = _layer_forward(h, mems, lw, i, own(i, REST), own(i + 1, FIRST_FFN) if i + 1 < DEPTH else None)
        lws.append(lw)
        saved.append(sv)
    acc, dh = _loss_grad(h, loss_target[0])
    loss = lax.psum(0.5 / D * jnp.sum(acc), ("x", "y", "c"))

    mine, smalls = [None] * DEPTH, [None] * DEPTH
    above, sums = None, {}
    nt = len(LAYER_TENSORS)
    early = ("ffn2_gate", "w_mem_kv", "ffn2_up", "w_out", "ffn2_down", "w_in")
    add_sib = lambda gs, from_sib: _by_shape(lambda g, r: _add_sibling(g, r, core), gs, from_sib)
    add_chp = lambda ss, from_chips: _by_shape(lambda s, r: _add_chips(s, r, chip, core), ss, from_chips)

    for i in reversed(range(DEPTH)):
        rides = {}
        if above is not None:
            up = [above[k] for k in LAYER_TENSORS]
            rides["ffn2"] = lambda big, rode, done, up=up: _SiblingExchange(up)

            def on_ffn1(big, rode, done, up=up, layer=i + 1, bottom=(i == 0)):
                sums[layer] = add_sib(up, rode["ffn2"])
                ex = _ChipsExchange(sums[layer])
                return _Both(ex, _SiblingExchange([big[k] for k in early])) if bottom else ex

            def on_wgrad(big, rode, done, j):
                if "early" not in sums:
                    sums["early"] = add_sib([big[k] for k in early], rode["ffn1"][nt:])
                ex = _ChipsExchange(sums["early"][2 * j:2 * j + 2])
                if j == 1:
                    bottom = dict(done["small"], g_pre=jnp.concatenate(done["dgpre"], axis=0),
                                  g_post=jnp.concatenate(done["dgpost"], axis=0))
                    sums["small"] = pack_small([bottom] + smalls[1:])
                    ex = _Both(ex, _SmallExchange(sums["small"][0]))
                return ex

            rides["ffn1"] = on_ffn1
            if i == 0:
                for j, host in enumerate(FIRST_FFN):
                    rides[host] = lambda big, rode, done, j=j: on_wgrad(big, rode, done, j)
        dh, gs, smalls[i], rode = _layer_backward(dh, mems, lws[i], saved[i], i, rides)
        if above is not None:
            mine[i + 1] = add_chp(sums[i + 1], rode["ffn1"][:nt])
        above = gs
    early_mine = add_chp(sums["early"], [t for host in FIRST_FFN for t in rode[host][:2]])
    late = [above[k] for k in FIRST_FFN]
    late_sums = add_sib(late, _run_exchange(_SiblingExchange(late)))
    late_mine = add_chp(late_sums, _run_exchange(_ChipsExchange(late_sums)))
    by_name = dict(zip(early + FIRST_FFN, early_mine + late_mine))
    mine[0] = [by_name[k] for k in LAYER_TENSORS]
    shared = _share_halves([t for layer in mine for t in layer])
    reduced = [dict(zip(LAYER_TENSORS, [s.reshape(-1, s.shape[-1]) for s in shared[nt * i:nt * (i + 1)]]))
               for i in range(DEPTH)]
    dx = dh
    red = {}
    for k in BIG:
        if k.startswith("w_in_"):
            layers = [i for i in range(DEPTH) if _is_pool(i) == (k == "w_in_pool")]
            red[k] = jnp.stack([reduced[i]["w_in"] for i in layers])
        else:
            red[k] = jnp.stack([reduced[i][k] for i in range(DEPTH)])

    small_buf, small_shapes = sums["small"]
    summed = _unpack128(_sum_devices(_with_own(rode["ffn1_up"][2], small_buf, dev)), small_shapes)
    for k, s in zip(SMALL, summed):
        if SHARD_AXIS[k] is None:
            red[k] = s
        else:
            red[k] = lax.dynamic_slice_in_dim(s, chip * w[k].shape[2], w[k].shape[2], axis=2)

    deltas, new_m, new_v = {}, {}, {}
    for k in WEIGHTS:
        deltas[k], new_m[k], new_v[k] = _adamw(w[k], red[k], m[k], v[k])
    return (loss, dx.reshape(x.shape), *[red[k] for k in WEIGHTS], *[deltas[k] for k in WEIGHTS],
            *[new_m[k] for k in WEIGHTS], *[new_v[k] for k in WEIGHTS])
```

```python
import jax
import jax.numpy as jnp
from jax import lax
from jax.experimental import pallas as pl
from jax.experimental.pallas import tpu as pltpu

F32 = jnp.float32
BF16 = jnp.bfloat16
_MM = jnp.bfloat16

D = 1024
FF = 2048
DTOK = 512
DMEM = 256
DMIX = DTOK + DMEM
DSB = 3 * DTOK + DMEM
HD = 64
MEM_LEN = 256
DEPTH = 4
EPS = 1e-6
WINDOWS = (2, 4, 8, 16)
HALO = 16
PG = 128
QB = 128
SCALE = HD ** -0.5
FC = 512
NEG_CUT = -104.0

ADAM_LR, ADAM_B1, ADAM_B2, ADAM_EPS, ADAM_WD, ADAM_STEP = 0.001, 0.9, 0.999, 1e-08, 0.01, 10

VMEM_LIMIT = 56 * 2 ** 20
MESH = pl.DeviceIdType.MESH
N_CHIPS = 4
N_DEV = 8


def _cparams(sem):
    return pltpu.CompilerParams(dimension_semantics=sem, vmem_limit_bytes=VMEM_LIMIT)


def _mm(a, b):
    return jnp.dot(a.astype(_MM), b.astype(_MM), preferred_element_type=F32)


def _mm_nt(a, b):
    return lax.dot_general(a.astype(_MM), b.astype(_MM), (((1,), (1,)), ((), ())), preferred_element_type=F32)


def _mm_tn(a, b):
    return lax.dot_general(a.astype(_MM), b.astype(_MM), (((0,), (0,)), ((), ())), preferred_element_type=F32)


def _mm2(a, b):
    hi = a.astype(_MM)
    lo = (a - hi.astype(F32)).astype(_MM)
    return jnp.dot(hi, b, preferred_element_type=F32) + jnp.dot(lo, b, preferred_element_type=F32)


def _rstd(x):
    return lax.rsqrt(jnp.mean(x * x, axis=-1, keepdims=True) + EPS)


def _rms_bwd(x, g, dy):
    r = _rstd(x)
    xh = x * r
    t = dy * g
    dx = r * (t - xh * jnp.mean(t * xh, axis=-1, keepdims=True))
    return dx, jnp.sum(dy * xh, axis=0, keepdims=True)


def _sigmoid(a):
    return 1.0 / (1.0 + jnp.exp(-a))


def _row_tile(n, want):
    t = min(n, want)
    assert n % t == 0, (n, t)
    return t


def _ffn_fwd(h, g1, g2, wg, wu, wd, ride=None):
    S = h.shape[0]
    ts = _row_tile(S, 512)

    def body(h_ref, g1_ref, g2_ref, wg_ref, wu_ref, wd_ref, ho_ref, a_ref, b_ref, f_ref):
        hv = h_ref[...]
        n = (hv * _rstd(hv) * g1_ref[...]).astype(_MM)
        f = jnp.zeros((ts, D), F32)
        for j in range(FF // FC):
            cs = slice(j * FC, (j + 1) * FC)
            a = _mm(n, wg_ref[j])
            b = _mm(n, wu_ref[j])
            a_ref[:, cs] = a.astype(a_ref.dtype)
            b_ref[:, cs] = b.astype(b_ref.dtype)
            f = f + _mm(a * _sigmoid(a) * b, wd_ref[j])
        f_ref[...] = f
        ho_ref[...] = hv + 0.5 * (f * _rstd(f) * g2_ref[...])

    row = lambda i: (i, 0)
    fix = lambda i: (0, 0)
    all3 = lambda i: (0, 0, 0)
    ns = FF // FC
    return _call(
        body, "ffn_fwd", S // ts,
        [pl.BlockSpec((ts, D), row), pl.BlockSpec((1, D), fix), pl.BlockSpec((1, D), fix),
         pl.BlockSpec((ns, D, FC), all3), pl.BlockSpec((ns, D, FC), all3), pl.BlockSpec((ns, FC, D), all3)],
        [pl.BlockSpec((ts, D), row), pl.BlockSpec((ts, FF), row), pl.BlockSpec((ts, FF), row),
         pl.BlockSpec((ts, D), row)],
        [jax.ShapeDtypeStruct((S, D), F32), jax.ShapeDtypeStruct((S, FF), _MM),
         jax.ShapeDtypeStruct((S, FF), _MM), jax.ShapeDtypeStruct((S, D), F32)],
        (h, g1, g2, wg, wu, wd), ride)


def _ffn_bwd(h, f, a, b, dho, g1, g2, wg, wu, wd, ride=None):
    S = h.shape[0]
    ts = _row_tile(S, 256)

    def body(h_ref, f_ref, a_ref, b_ref, dho_ref, g1_ref, g2_ref, wg_ref, wu_ref, wd_ref,
             dh_ref, n_ref, s_ref, da_ref, db_ref, df_ref, dg1_ref, dg2_ref):
        @pl.when(pl.program_id(0) == 0)
        def _():
            dg1_ref[...] = jnp.zeros_like(dg1_ref)
            dg2_ref[...] = jnp.zeros_like(dg2_ref)

        hv = h_ref[...]
        dho = dho_ref[...]
        df, dg2 = _rms_bwd(f_ref[...], g2_ref[...], 0.5 * dho)
        dg2_ref[...] += dg2
        dfb = df.astype(_MM)
        df_ref[...] = dfb
        rh = _rstd(hv)
        hh = hv * rh
        g1 = g1_ref[...]
        n_ref[...] = (hh * g1).astype(_MM)
        dss = [_mm_nt(dfb, wd_ref[j]) for j in range(FF // FC)]
        das, dbs = [], []
        for j in range(FF // FC):
            cs = slice(j * FC, (j + 1) * FC)
            av = a_ref[:, cs].astype(F32)
            bv = b_ref[:, cs].astype(F32)
            sig = _sigmoid(av)
            sl = av * sig
            s_ref[:, cs] = (sl * bv).astype(_MM)
            da = (dss[j] * bv * (sig * (1.0 + av * (1.0 - sig)))).astype(_MM)
            db = (dss[j] * sl).astype(_MM)
            da_ref[:, cs] = da
            db_ref[:, cs] = db
            das.append(da)
            dbs.append(db)
        dn = jnp.zeros((ts, D), F32)
        for j in range(FF // FC):
            dn = dn + _mm_nt(das[j], wg_ref[j]) + _mm_nt(dbs[j], wu_ref[j])
        t = dn * g1
        dh_ref[...] = dho + rh * (t - hh * jnp.mean(t * hh, axis=-1, keepdims=True))
        dg1_ref[...] += jnp.sum(dn * hh, axis=0, keepdims=True)

    row = lambda i: (i, 0)
    fix = lambda i: (0, 0)
    all3 = lambda i: (0, 0, 0)
    ns = FF // FC
    td = pl.BlockSpec((ts, D), row)
    tf = pl.BlockSpec((ts, FF), row)
    gd = pl.BlockSpec((1, D), fix)
    return _call(
        body, "ffn_bwd", S // ts,
        [td, td, tf, tf, td, gd, gd,
         pl.BlockSpec((ns, D, FC), all3), pl.BlockSpec((ns, D, FC), all3), pl.BlockSpec((ns, FC, D), all3)],
        [td, td, tf, tf, tf, td, gd, gd],
        [jax.ShapeDtypeStruct((S, D), F32), jax.ShapeDtypeStruct((S, D), _MM),
         jax.ShapeDtypeStruct((S, FF), _MM), jax.ShapeDtypeStruct((S, FF), _MM),
         jax.ShapeDtypeStruct((S, FF), _MM), jax.ShapeDtypeStruct((S, D), _MM),
         jax.ShapeDtypeStruct((1, D), F32), jax.ShapeDtypeStruct((1, D), F32)],
        (h, f, a, b, dho, g1, g2, wg, wu, wd), ride)


def _wgrad(x, y, nslot=0, ride=None):
    T, M = x.shape
    N = y.shape[1]
    tt = _row_tile(T, 1024)
    cw = N // nslot if nslot else (512 if N % 512 == 0 else 256)
    nc = N // cw
    assert cw * nc == N and cw % 128 == 0, (N, nc)

    def body(x_ref, y_ref, o_ref):
        @pl.when(pl.program_id(0) == 0)
        def _():
            o_ref[...] = jnp.zeros_like(o_ref)

        xt = x_ref[...].T
        for j in range(nc):
            part = jnp.dot(xt, y_ref[:, j * cw:(j + 1) * cw], preferred_element_type=F32)
            if nslot:
                o_ref[j] += part
            else:
                o_ref[:, j * cw:(j + 1) * cw] += part

    oshape = (nslot, M, cw) if nslot else (M, N)
    res = _call(body, "wgrad", T // tt,
                [pl.BlockSpec((tt, M), lambda t: (t, 0)), pl.BlockSpec((tt, N), lambda t: (t, 0))],
                [pl.BlockSpec(oshape, lambda t: (0,) * len(oshape))], [jax.ShapeDtypeStruct(oshape, F32)], (x, y), ride)
    return res[0] if ride is None else (res[0][0], res[1])


def _proj_fwd(h, g, w, out_dtype):
    S = h.shape[0]
    nk, kw, N = w.shape
    ts = _row_tile(S, 256)

    def body(h_ref, g_ref, w_ref, u_ref, p_ref):
        hv = h_ref[...]
        u = (hv * _rstd(hv) * g_ref[...]).astype(_MM)
        u_ref[...] = u
        acc = _mm(u[:, :kw], w_ref[0])
        for j in range(1, nk):
            acc = acc + _mm(u[:, j * kw:(j + 1) * kw], w_ref[j])
        p_ref[...] = acc.astype(p_ref.dtype)

    return pl.pallas_call(
        body, name="proj_fwd", grid=(S // ts,),
        in_specs=[pl.BlockSpec((ts, D), lambda i: (i, 0)), pl.BlockSpec((1, D), lambda i: (0, 0)),
                  pl.BlockSpec((nk, kw, N), lambda i: (0, 0, 0))],
        out_specs=[pl.BlockSpec((ts, D), lambda i: (i, 0)), pl.BlockSpec((ts, N), lambda i: (i, 0))],
        out_shape=[jax.ShapeDtypeStruct((S, D), _MM), jax.ShapeDtypeStruct((S, N), out_dtype)],
        compiler_params=_cparams(("parallel",)),
    )(h, g, w)


def _proj_bwd(h, dho, g, w, pieces):
    S = h.shape[0]
    nk, kw, N = w.shape
    ts = _row_tile(S, 256)
    widths = [p.shape[1] for p in pieces]
    assert sum(widths) == N
    npc = len(pieces)

    def body(*refs):
        h_ref, dho_ref, g_ref, w_ref = refs[:4]
        p_refs = refs[4:4 + npc]
        dh_ref, dp_ref, dg_ref = refs[4 + npc:]

        @pl.when(pl.program_id(0) == 0)
        def _():
            dg_ref[...] = jnp.zeros_like(dg_ref)

        dus = [jnp.zeros((ts, kw), F32) for _ in range(nk)]
        off = 0
        for p_ref, n in zip(p_refs, widths):
            pv = p_ref[...].astype(_MM)
            dp_ref[:, off:off + n] = pv
            for j in range(nk):
                dus[j] = dus[j] + _mm_nt(pv, w_ref[j, :, off:off + n])
            off += n
        du = dus[0] if nk == 1 else jnp.concatenate(dus, axis=-1)
        dx, dg = _rms_bwd(h_ref[...], g_ref[...], du)
        dh_ref[...] = dho_ref[...] + dx
        dg_ref[...] += dg

    row = lambda i: (i, 0)
    return pl.pallas_call(
        body, name="proj_bwd", grid=(S // ts,),
        in_specs=[pl.BlockSpec((ts, D), row), pl.BlockSpec((ts, D), row), pl.BlockSpec((1, D), lambda i: (0, 0)),
                  pl.BlockSpec((nk, kw, N), lambda i: (0, 0, 0))] + [pl.BlockSpec((ts, n), row) for n in widths],
        out_specs=[pl.BlockSpec((ts, D), row), pl.BlockSpec((ts, N), row), pl.BlockSpec((1, D), lambda i: (0, 0))],
        out_shape=[jax.ShapeDtypeStruct((S, D), F32), jax.ShapeDtypeStruct((S, N), _MM),
                   jax.ShapeDtypeStruct((1, D), F32)],
        compiler_params=_cparams(("arbitrary",)),
    )(h, dho, g, w, *pieces)


def _pool_counts(first_row, ts):
    pos = (first_row + lax.broadcasted_iota(jnp.int32, (ts, 1), 0) + 1).astype(F32)
    return [jnp.minimum(pos, float(w)) for w in WINDOWS]


def _pool_delta(x, prev, cnts, ts):
    xe = jnp.concatenate([prev, x], axis=0)
    sums = []
    cur = xe
    for sh in (1, 2, 4, 8):
        cur = cur + pltpu.roll(cur, sh, 0)
        sums.append(cur)
    return [sums[gi][HALO:, gi * PG:(gi + 1) * PG] / cnts[gi] - x[:, gi * PG:(gi + 1) * PG]
            for gi in range(len(WINDOWS))]


def _pool_fwd(proj, pw, ps):
    S = proj.shape[0]
    ts = _row_tile(S, 256)

    def body(x_ref, pw_ref, ps_ref, tok_ref, carry_ref):
        i = pl.program_id(0)

        @pl.when(i == 0)
        def _():
            carry_ref[...] = jnp.zeros_like(carry_ref)

        x = x_ref[...]
        ds = _pool_delta(x, carry_ref[...], _pool_counts(i * ts, ts), ts)
        y = jnp.concatenate([_mm(ds[gi], pw_ref[gi]) for gi in range(len(WINDOWS))], axis=-1)
        tok_ref[...] = (y * ps_ref[...]).astype(tok_ref.dtype)
        carry_ref[...] = x[ts - HALO:, :]

    return pl.pallas_call(
        body, name="pool_fwd", grid=(S // ts,),
        in_specs=[pl.BlockSpec((ts, DTOK), lambda i: (i, 0)), pl.BlockSpec((len(WINDOWS), PG, PG), lambda i: (0, 0, 0)),
                  pl.BlockSpec((1, DTOK), lambda i: (0, 0))],
        out_specs=pl.BlockSpec((ts, DTOK), lambda i: (i, 0)),
        out_shape=jax.ShapeDtypeStruct((S, DTOK), _MM),
        scratch_shapes=[pltpu.VMEM((HALO, DTOK), F32)],
        compiler_params=_cparams(("arbitrary",)),
    )(proj, pw, ps)


def _pool_bwd(proj, dtok, pw, ps):
    S = proj.shape[0]
    ts = _row_tile(S, 256)
    nt = S // ts
    per = ts // HALO
    ng = len(WINDOWS)

    def body(x_ref, xp_ref, dt_ref, pw_ref, ps_ref, dx_ref, dpw_ref, dps_ref, carry_ref):
        i = pl.program_id(0)
        idx = nt - 1 - i

        @pl.when(i == 0)
        def _():
            carry_ref[...] = jnp.zeros_like(carry_ref)
            dpw_ref[...] = jnp.zeros_like(dpw_ref)
            dps_ref[...] = jnp.zeros_like(dps_ref)

        x = x_ref[...]
        prev = jnp.where(idx > 0, xp_ref[...], 0.0)
        cnts = _pool_counts(idx * ts, ts)
        ds = _pool_delta(x, prev, cnts, ts)
        dt = dt_ref[...]
        y = jnp.concatenate([_mm(ds[gi], pw_ref[gi]) for gi in range(ng)], axis=-1)
        dps_ref[...] += jnp.sum(dt * y, axis=0, keepdims=True)
        dy = (dt * ps_ref[...]).astype(_MM)
        dds = []
        for gi in range(ng):
            dyg = dy[:, gi * PG:(gi + 1) * PG]
            dpw_ref[gi] += _mm_tn(ds[gi], dyg)
            dds.append(_mm_nt(dyg, pw_ref[gi]))
        e = jnp.concatenate([dds[gi] / cnts[gi] for gi in range(ng)], axis=-1)
        ee = jnp.concatenate([e, carry_ref[...]], axis=0)
        rows = ts + HALO
        cur = ee
        outs = []
        for gi, sh in enumerate((1, 2, 4, 8)):
            cur = cur + pltpu.roll(cur, rows - sh, 0)
            outs.append(cur[:ts, gi * PG:(gi + 1) * PG] - dds[gi])
        dx_ref[...] = jnp.concatenate(outs, axis=-1)
        carry_ref[...] = e[:HALO, :]

    rev = lambda i: (nt - 1 - i, 0)
    return pl.pallas_call(
        body, name="pool_bwd", grid=(nt,),
        in_specs=[pl.BlockSpec((ts, DTOK), rev),
                  pl.BlockSpec((HALO, DTOK), lambda i: (jnp.maximum((nt - 1 - i) * per - 1, 0), 0)),
                  pl.BlockSpec((ts, DTOK), rev), pl.BlockSpec((ng, PG, PG), lambda i: (0, 0, 0)),
                  pl.BlockSpec((1, DTOK), lambda i: (0, 0))],
        out_specs=[pl.BlockSpec((ts, DTOK), rev), pl.BlockSpec((ng, PG, PG), lambda i: (0, 0, 0)),
                   pl.BlockSpec((1, DTOK), lambda i: (0, 0))],
        out_shape=[jax.ShapeDtypeStruct((S, DTOK), F32), jax.ShapeDtypeStruct((ng, PG, PG), F32),
                   jax.ShapeDtypeStruct((1, DTOK), F32)],
        scratch_shapes=[pltpu.VMEM((HALO, DTOK), F32)],
        compiler_params=_cparams(("arbitrary",)),
    )(proj, proj, dtok, pw, ps)


def _tri(strict):
    r = lax.broadcasted_iota(jnp.int32, (QB, QB), 0)
    c = lax.broadcasted_iota(jnp.int32, (QB, QB), 1)
    return jnp.where(r > c if strict else r >= c, 1.0, 0.0).astype(_MM)


def _sb_terms(z, valid):
    e = jnp.exp(-jnp.abs(z))
    sp = jnp.log(1.0 + e)
    ls = jnp.minimum(z, 0.0) - sp
    lf = jnp.where(valid, jnp.minimum(-z, 0.0) - sp, 0.0)
    return ls, lf, e


def _sb_alive(cs, lo=0):
    top = cs[0][lo:]
    for c in cs[1:]:
        top = jnp.maximum(top, c[lo:])
    return (jnp.max(top) > NEG_CUT).astype(jnp.int32)


def _sb_split(x2, head_a):
    zero = jnp.zeros_like(x2)
    return jnp.where(head_a, x2, zero), jnp.where(head_a, zero, x2)


SB_GROUPS_FWD = 4
SB_GROUPS_BWD = 4
SB_TOP = 48


def _sb_fwd(proj):
    S = proj.shape[0]
    G = min(SB_GROUPS_FWD, S // QB)
    nstep = S // (G * QB)
    npair = DTOK // 128
    nch = 2 * G

    def body(q_ref, k_ref, v_ref, tri_ref, o_ref):
        qb = pl.program_id(1)
        head_a = lax.broadcasted_iota(jnp.int32, (QB, 128), 1) < HD
        qs = []
        for g in range(G):
            qs.extend(_sb_split(q_ref[g * QB:(g + 1) * QB, :], head_a))
        tri = tri_ref[...]
        last = G * qb + G - 1

        def trips(rows):
            qr = [q[:rows] for q in qs]
            row = lax.broadcasted_iota(jnp.int32, (rows, QB), 0)
            col = lax.broadcasted_iota(jnp.int32, (rows, QB), 1)

            def step(carry):
                i, _, cs, accs = carry
                causal = (col - row) < i * QB
                ks, vs, valids = [], [], []
                for g in range(G):
                    kb = G * qb + g - i
                    off = pl.multiple_of(jnp.maximum(kb, 0) * QB, QB)
                    ks.append(k_ref[pl.ds(off, QB), :])
                    vs.append(v_ref[pl.ds(off, QB), :])
                    valids.append(jnp.logical_and(causal, kb >= 0))
                zs = [_mm_nt(qr[ch], ks[ch // 2]) * SCALE for ch in range(nch)]
                terms = [_sb_terms(zs[ch], valids[ch // 2]) for ch in range(nch)]
                withins = [_mm2(terms[ch][1], tri) for ch in range(nch)]
                new_c, new_acc = [], []
                for ch in range(nch):
                    ls, lf, _ = terms[ch]
                    a = jnp.where(valids[ch // 2], jnp.exp(ls + withins[ch] + cs[ch]), 0.0)
                    new_acc.append(accs[ch] + _mm2(a, vs[ch // 2]))
                    new_c.append(cs[ch] + jnp.sum(lf, axis=-1, keepdims=True))
                return i + 1, _sb_alive(new_c, SB_TOP if rows == QB else 0), tuple(new_c), tuple(new_acc)

            return step

        zc = jnp.zeros((QB, 1), F32)
        za = jnp.zeros((QB, 128), F32)
        more = lambda cr: jnp.logical_and(cr[0] <= last, cr[1] > 0)
        i, _, cs, accs = lax.while_loop(more, trips(QB), (0, 1, (zc,) * nch, (za,) * nch))
        cs_top = tuple(c[:SB_TOP] for c in cs)
        acc_top = tuple(a[:SB_TOP] for a in accs)
        acc_top = lax.while_loop(more, trips(SB_TOP), (i, _sb_alive(cs_top), cs_top, acc_top))[3]
        for g in range(G):
            o_ref[g * QB:g * QB + SB_TOP, :] = jnp.where(head_a[:SB_TOP], acc_top[2 * g], acc_top[2 * g + 1])
            o_ref[g * QB + SB_TOP:(g + 1) * QB, :] = jnp.where(head_a, accs[2 * g], accs[2 * g + 1])[SB_TOP:]

    return pl.pallas_call(
        body, name="sb_fwd", grid=(npair, nstep),
        in_specs=[pl.BlockSpec((G * QB, 128), lambda p, i: (i, p)),
                  pl.BlockSpec((S, 128), lambda p, i: (0, npair + p)),
                  pl.BlockSpec((S, 128), lambda p, i: (0, 2 * npair + p)),
                  pl.BlockSpec((QB, QB), lambda p, i: (0, 0))],
        out_specs=pl.BlockSpec((G * QB, 128), lambda p, i: (i, p)),
        out_shape=jax.ShapeDtypeStruct((S, DTOK), F32),
        compiler_params=_cparams(("parallel", "parallel")),
    )(proj, proj, proj, _tri(True))


def _sb_bwd(proj, o, do):
    S = proj.shape[0]
    G = min(SB_GROUPS_BWD, S // QB)
    nstep = S // (G * QB)
    npair = DTOK // 128
    nch = 2 * G

    def body(q_ref, k_ref, v_ref, o_ref, do_ref, tri_ref, tri2_ref, dq_ref, dk_ref, dv_ref):
        qb = pl.program_id(1)

        @pl.when(qb == 0)
        def _():
            dk_ref[...] = jnp.zeros_like(dk_ref)
            dv_ref[...] = jnp.zeros_like(dv_ref)

        head_a = lax.broadcasted_iota(jnp.int32, (QB, 128), 1) < HD
        qs, dos, gs = [], [], []
        for g in range(G):
            rows = slice(g * QB, (g + 1) * QB)
            qs.extend(_sb_split(q_ref[rows, :], head_a))
            dob = do_ref[rows, :].astype(_MM)
            dos.extend(_sb_split(dob, head_a))
            go = dob.astype(F32) * o_ref[rows, :]
            gs.append(jnp.sum(jnp.where(head_a, go, 0.0), axis=-1, keepdims=True))
            gs.append(jnp.sum(jnp.where(head_a, 0.0, go), axis=-1, keepdims=True))
        tri = tri_ref[...]
        tri2 = tri2_ref[...]
        last = G * qb + G - 1

        def trips(rows):
            qr = [q[:rows] for q in qs]
            dor = [d[:rows] for d in dos]
            gr = [g_[:rows] for g_ in gs]
            row = lax.broadcasted_iota(jnp.int32, (rows, QB), 0)
            col = lax.broadcasted_iota(jnp.int32, (rows, QB), 1)

            def step(carry):
                i, _, cs, rs, dqs = carry
                causal = (col - row) < i * QB
                offs, ks, vs, valids = [], [], [], []
                for g in range(G):
                    kb = G * qb + g - i
                    off = pl.multiple_of(jnp.maximum(kb, 0) * QB, QB)
                    offs.append(off)
                    ks.append(k_ref[pl.ds(off, QB), :])
                    vs.append(v_ref[pl.ds(off, QB), :])
                    valids.append(jnp.logical_and(causal, kb >= 0))
                zs = [_mm_nt(qr[ch], ks[ch // 2]) * SCALE for ch in range(nch)]
                das = [_mm_nt(dor[ch], vs[ch // 2]) for ch in range(nch)]
                terms = [_sb_terms(zs[ch], valids[ch // 2]) for ch in range(nch)]
                withins = [_mm2(terms[ch][1], tri) for ch in range(nch)]
                a_s, dlws = [], []
                for ch in range(nch):
                    a = jnp.where(valids[ch // 2], jnp.exp(terms[ch][0] + withins[ch] + cs[ch]), 0.0)
                    a_s.append(a)
                    dlws.append(das[ch] * a)
                sfx = [_mm2(dlws[ch], tri2) for ch in range(nch)]
                dzs = []
                for ch in range(nch):
                    e = terms[ch][2]
                    inv = 1.0 / (1.0 + e)
                    pos = zs[ch] >= 0.0
                    beta = jnp.where(pos, 1.0, e) * inv
                    omb = jnp.where(pos, e, 1.0) * inv
                    prefix = gr[ch] - rs[ch] - sfx[ch]
                    dzs.append((jnp.where(valids[ch // 2], dlws[ch] * omb - beta * prefix, 0.0) * SCALE).astype(_MM))
                new_dq = [dqs[ch] + _mm(dzs[ch], ks[ch // 2]) for ch in range(nch)]
                for g in range(G):
                    a, b = 2 * g, 2 * g + 1
                    dk_ref[pl.ds(offs[g], QB), :] += _mm_tn(dzs[a], qr[a]) + _mm_tn(dzs[b], qr[b])
                    dv_ref[pl.ds(offs[g], QB), :] += _mm_tn(a_s[a], dor[a]) + _mm_tn(a_s[b], dor[b])
                new_c = [cs[ch] + jnp.sum(terms[ch][1], axis=-1, keepdims=True) for ch in range(nch)]
                new_r = [rs[ch] + jnp.sum(dlws[ch], axis=-1, keepdims=True) for ch in range(nch)]
                alive = _sb_alive(new_c, SB_TOP if rows == QB else 0)
                return i + 1, alive, tuple(new_c), tuple(new_r), tuple(new_dq)

            return step

        zc = jnp.zeros((QB, 1), F32)
        za = jnp.zeros((QB, 128), F32)
        more = lambda cr: jnp.logical_and(cr[0] <= last, cr[1] > 0)
        i, _, cs, rs, dqs = lax.while_loop(more, trips(QB), (0, 1, (zc,) * nch, (zc,) * nch, (za,) * nch))
        top = lambda xs: tuple(x[:SB_TOP] for x in xs)
        cs_top = top(cs)
        dq_top = lax.while_loop(more, trips(SB_TOP), (i, _sb_alive(cs_top), cs_top, top(rs), top(dqs)))[4]
        for g in range(G):
            dq_ref[g * QB:g * QB + SB_TOP, :] = jnp.where(head_a[:SB_TOP], dq_top[2 * g], dq_top[2 * g + 1])
            dq_ref[g * QB + SB_TOP:(g + 1) * QB, :] = jnp.where(head_a, dqs[2 * g], dqs[2 * g + 1])[SB_TOP:]

    qspec = pl.BlockSpec((G * QB, 128), lambda p, i: (i, p))
    full = lambda base: pl.BlockSpec((S, 128), lambda p, i: (0, base + p))
    tspec = pl.BlockSpec((QB, QB), lambda p, i: (0, 0))
    return pl.pallas_call(
        body, name="sb_bwd", grid=(npair, nstep),
        in_specs=[qspec, full(npair), full(2 * npair), qspec, qspec, tspec, tspec],
        out_specs=[qspec, full(0), full(0)],
        out_shape=[jax.ShapeDtypeStruct((S, DTOK), F32)] * 3,
        compiler_params=_cparams(("parallel", "arbitrary")),
    )(proj, proj, proj, o, do, _tri(True), _tri(False))


def _memkv_fwd(mem, gm, wkv):
    nk, kw, _ = wkv.shape

    def body(m_ref, g_ref, w_ref, kv_ref):
        m = m_ref[...]
        mn = (m * _rstd(m) * g_ref[...]).astype(_MM)
        acc = _mm(mn[:, :kw], w_ref[0])
        for j in range(1, nk):
            acc = acc + _mm(mn[:, j * kw:(j + 1) * kw], w_ref[j])
        kv_ref[...] = acc.astype(kv_ref.dtype)

    return pl.pallas_call(
        body, name="memkv_fwd", out_shape=jax.ShapeDtypeStruct((MEM_LEN, 2 * DMEM), _MM),
        compiler_params=pltpu.CompilerParams(vmem_limit_bytes=VMEM_LIMIT),
    )(mem, gm, wkv)


def _memkv_bwd(mem, gm, wkv, dkv):
    nk, kw, _ = wkv.shape

    def body(m_ref, g_ref, w_ref, dkv_ref, dw_ref, dg_ref):
        m = m_ref[...]
        mh = m * _rstd(m)
        mn = (mh * g_ref[...]).astype(_MM)
        dkv = dkv_ref[...].astype(_MM)
        dw_ref[...] = _mm_tn(mn, dkv)
        dmn = jnp.concatenate([_mm_nt(dkv, w_ref[j]) for j in range(nk)], axis=-1)
        dg_ref[...] = jnp.sum(dmn * mh, axis=0, keepdims=True)

    return pl.pallas_call(
        body, name="memkv_bwd",
        out_shape=[jax.ShapeDtypeStruct((D, 2 * DMEM), F32), jax.ShapeDtypeStruct((1, D), F32)],
        compiler_params=pltpu.CompilerParams(vmem_limit_bytes=VMEM_LIMIT),
    )(mem, gm, wkv, dkv)


def _mixout_fwd(h, tok, proj, kv, wo, g, qm_block):
    S = h.shape[0]
    ts = _row_tile(S, 256)

    def body(h_ref, tok_ref, qm_ref, kv_ref, wo_ref, g_ref, ho_ref, cat_ref, mix_ref):
        head_a = lax.broadcasted_iota(jnp.int32, (ts, 128), 1) < HD
        qm = qm_ref[...].astype(_MM)
        zq = jnp.zeros((ts, 128), _MM)
        npair = DMEM // 128
        qhs = []
        for pi in range(npair):
            qp = qm[:, pi * 128:(pi + 1) * 128]
            qhs += [jnp.where(head_a, qp, zq), jnp.where(head_a, zq, qp)]
        scores = [_mm_nt(qhs[h], kv_ref[:, (h // 2) * 128:(h // 2 + 1) * 128]) * SCALE for h in range(2 * npair)]
        ps = []
        for s in scores:
            e = jnp.exp(s - jnp.max(s, axis=-1, keepdims=True))
            ps.append(e / jnp.sum(e, axis=-1, keepdims=True))
        pvs = [_mm(ps[h], kv_ref[:, DMEM + (h // 2) * 128:DMEM + (h // 2 + 1) * 128]) for h in range(2 * npair)]
        mos = [jnp.where(head_a, pvs[2 * pi], pvs[2 * pi + 1]) for pi in range(npair)]
        cat = jnp.concatenate([tok_ref[...].astype(_MM)] + [m.astype(_MM) for m in mos], axis=-1)
        cat_ref[...] = cat
        mix = jnp.concatenate([_mm(cat, wo_ref[j]) for j in range(N_CHIPS)], axis=-1)
        mix_ref[...] = mix
        ho_ref[...] = h_ref[...] + mix * _rstd(mix) * g_ref[...]

    row = lambda i: (i, 0)
    return pl.pallas_call(
        body, name="mixout_fwd", grid=(S // ts,),
        in_specs=[pl.BlockSpec((ts, D), row), pl.BlockSpec((ts, DTOK), row),
                  pl.BlockSpec((ts, DMEM), lambda i: (i, qm_block)),
                  pl.BlockSpec((MEM_LEN, 2 * DMEM), lambda i: (0, 0)),
                  pl.BlockSpec((N_CHIPS, DMIX, D // N_CHIPS), lambda i: (0, 0, 0)), pl.BlockSpec((1, D), lambda i: (0, 0))],
        out_specs=[pl.BlockSpec((ts, D), row), pl.BlockSpec((ts, DMIX), row), pl.BlockSpec((ts, D), row)],
        out_shape=[jax.ShapeDtypeStruct((S, D), F32), jax.ShapeDtypeStruct((S, DMIX), _MM),
                   jax.ShapeDtypeStruct((S, D), F32)],
        compiler_params=_cparams(("parallel",)),
    )(h, tok, proj, kv, wo, g)


def _mixout_bwd(dho, mix, proj, kv, wo, g, qm_block):
    S = dho.shape[0]
    ts = _row_tile(S, 256)

    def body(dho_ref, mix_ref, qm_ref, kv_ref, wo_ref, g_ref, dmix_ref, dtok_ref, dqm_ref, dkv_ref, dg_ref):
        @pl.when(pl.program_id(0) == 0)
        def _():
            dkv_ref[...] = jnp.zeros_like(dkv_ref)
            dg_ref[...] = jnp.zeros_like(dg_ref)

        dmix, dg = _rms_bwd(mix_ref[...], g_ref[...], dho_ref[...])
        dg_ref[...] += dg
        dmb = dmix.astype(_MM)
        dmix_ref[...] = dmb
        cw = D // N_CHIPS
        dcat = _mm_nt(dmb[:, :cw], wo_ref[0])
        for j in range(1, N_CHIPS):
            dcat = dcat + _mm_nt(dmb[:, j * cw:(j + 1) * cw], wo_ref[j])
        dtok_ref[...] = dcat[:, :DTOK]
        head_a = lax.broadcasted_iota(jnp.int32, (ts, 128), 1) < HD
        qm = qm_ref[...].astype(_MM)
        zq = jnp.zeros((ts, 128), _MM)
        npair = DMEM // 128
        kps = [kv_ref[:, pi * 128:(pi + 1) * 128] for pi in range(npair)]
        vps = [kv_ref[:, DMEM + pi * 128:DMEM + (pi + 1) * 128] for pi in range(npair)]
        qhs, dmhs = [], []
        for pi in range(npair):
            cs = slice(pi * 128, (pi + 1) * 128)
            dmo = dcat[:, DTOK + pi * 128:DTOK + (pi + 1) * 128].astype(_MM)
            qhs += [jnp.where(head_a, qm[:, cs], zq), jnp.where(head_a, zq, qm[:, cs])]
            dmhs += [jnp.where(head_a, dmo, zq), jnp.where(head_a, zq, dmo)]
        nh = 2 * npair
        scores = [_mm_nt(qhs[h], kps[h // 2]) * SCALE for h in range(nh)]
        dps = [_mm_nt(dmhs[h], vps[h // 2]) for h in range(nh)]
        ps, dss = [], []
        for h in range(nh):
            e = jnp.exp(scores[h] - jnp.max(scores[h], axis=-1, keepdims=True))
            p = e / jnp.sum(e, axis=-1, keepdims=True)
            ps.append(p)
            dss.append((p * (dps[h] - jnp.sum(dps[h] * p, axis=-1, keepdims=True)) * SCALE).astype(_MM))
        dqs = [_mm(dss[h], kps[h // 2]) for h in range(nh)]
        for pi in range(npair):
            cs = slice(pi * 128, (pi + 1) * 128)
            a, b = 2 * pi, 2 * pi + 1
            dqm_ref[:, cs] = jnp.where(head_a, dqs[a], dqs[b]).astype(_MM)
            dkv_ref[:, cs] += _mm_tn(dss[a], qhs[a]) + _mm_tn(dss[b], qhs[b])
            dkv_ref[:, DMEM + pi * 128:DMEM + (pi + 1) * 128] += _mm_tn(ps[a], dmhs[a]) + _mm_tn(ps[b], dmhs[b])

    row = lambda i: (i, 0)
    fix = lambda i: (0, 0)
    return pl.pallas_call(
        body, name="mixout_bwd", grid=(S // ts,),
        in_specs=[pl.BlockSpec((ts, D), row), pl.BlockSpec((ts, D), row),
                  pl.BlockSpec((ts, DMEM), lambda i: (i, qm_block)),
                  pl.BlockSpec((MEM_LEN, 2 * DMEM), fix),
                  pl.BlockSpec((N_CHIPS, DMIX, D // N_CHIPS), lambda i: (0, 0, 0)), pl.BlockSpec((1, D), fix)],
        out_specs=[pl.BlockSpec((ts, D), row), pl.BlockSpec((ts, DTOK), row), pl.BlockSpec((ts, DMEM), row),
                   pl.BlockSpec((MEM_LEN, 2 * DMEM), fix), pl.BlockSpec((1, D), fix)],
        out_shape=[jax.ShapeDtypeStruct((S, D), _MM), jax.ShapeDtypeStruct((S, DTOK), F32),
                   jax.ShapeDtypeStruct((S, DMEM), _MM), jax.ShapeDtypeStruct((MEM_LEN, 2 * DMEM), F32),
                   jax.ShapeDtypeStruct((1, D), F32)],
        compiler_params=_cparams(("arbitrary",)),
    )(dho, mix, proj, kv, wo, g)


def _loss_grad(y, target):
    S = y.shape[0]
    ts = _row_tile(S, 512)

    def body(y_ref, t_ref, acc_ref, dy_ref):
        @pl.when(pl.program_id(0) == 0)
        def _():
            acc_ref[...] = jnp.zeros_like(acc_ref)

        err = y_ref[...] - t_ref[...]
        acc_ref[...] += jnp.sum(err * err, axis=0, keepdims=True)
        dy_ref[...] = err * (1.0 / D)

    row = lambda i: (i, 0)
    return pl.pallas_call(
        body, name="loss_grad", grid=(S // ts,),
        in_specs=[pl.BlockSpec((ts, D), row), pl.BlockSpec((ts, D), row)],
        out_specs=[pl.BlockSpec((1, D), lambda i: (0, 0)), pl.BlockSpec((ts, D), row)],
        out_shape=[jax.ShapeDtypeStruct((1, D), F32), jax.ShapeDtypeStruct((S, D), F32)],
        compiler_params=_cparams(("arbitrary",)),
    )(y, target)


LAYER_TENSORS = ("ffn1_gate", "ffn1_up", "ffn1_down", "ffn2_gate", "ffn2_up", "ffn2_down", "w_in", "w_mem_kv", "w_out")
SPLITS = {"ffn1_gate": "c", "ffn1_up": "c", "ffn1_down": "r", "ffn2_gate": "c", "ffn2_up": "c", "ffn2_down": "r",
          "w_in_pool": "r", "w_in_sb": "c", "w_mem_kv": "r", "w_out": "c"}


def _is_pool(i):
    return i % 2 == 0


def _qm_block(i):
    return (DTOK if _is_pool(i) else 3 * DTOK) // DMEM


FIRST_FFN = LAYER_TENSORS[:3]
REST = LAYER_TENSORS[3:]


def _layer_forward(h, mem, lw, i, rest_shards, next_shards):
    g_at = lambda g, k: g[k].reshape(1, D)
    sv = {"h0": h}
    (h, sv["a1"], sv["b1"], sv["f1"]), rest = _ffn_fwd(
        h, g_at(lw["g_pre"], 0), g_at(lw["g_post"], 0), lw["ffn1_gate"], lw["ffn1_up"], lw["ffn1_down"],
        _GatherExchange(rest_shards))
    lw.update(zip(REST, rest))
    if not _is_pool(i):
        lw["w_in"] = jnp.transpose(lw["w_in"], (1, 0, 2)).reshape(1, D, DSB)
    sv["h1"] = h
    sv["kv"] = _memkv_fwd(mem, lw["g_mem"], lw["w_mem_kv"])
    if _is_pool(i):
        sv["u"], sv["proj"] = _proj_fwd(h, g_at(lw["g_pre"], 1), lw["w_in"], F32)
        tok = _pool_fwd(sv["proj"], lw["pool_w"], lw["pool_scale"])
    else:
        sv["u"], sv["proj"] = _proj_fwd(h, g_at(lw["g_pre"], 1), lw["w_in"], _MM)
        tok = sv["o"] = _sb_fwd(sv["proj"])
    h, sv["cat"], sv["mix"] = _mixout_fwd(h, tok, sv["proj"], sv["kv"], lw["w_out"], g_at(lw["g_post"], 1),
                                          _qm_block(i))
    sv["h2"] = h
    ffn2 = (h, g_at(lw["g_pre"], 2), g_at(lw["g_post"], 2), lw["ffn2_gate"], lw["ffn2_up"], lw["ffn2_down"])
    if next_shards is None:
        h, sv["a2"], sv["b2"], sv["f2"] = _ffn_fwd(*ffn2)
        return h, sv, None
    (h, sv["a2"], sv["b2"], sv["f2"]), nxt = _ffn_fwd(*ffn2, _GatherExchange(next_shards))
    return h, sv, nxt


def _layer_backward(dh, mem, lw, sv, i, rides=None):
    g_at = lambda g, k: g[k].reshape(1, D)
    big, small = {}, {}
    dgpre, dgpost = [None] * 3, [None] * 3
    rode = {}
    rides = rides or {}

    def hosted(host, fn, *args):
        ride = rides[host](big, rode) if host in rides else None
        if ride is None:
            return fn(*args)
        outs, rode[host] = fn(*args, ride)
        return outs

    def ffn_back(which, dh, h_in):
        k = 0 if which == 1 else 2
        n = f"ffn{which}"
        dh, nb, sb, da, db, df, dgpre[k], dgpost[k] = hosted(
            n, _ffn_bwd, h_in, sv[f"f{which}"], sv[f"a{which}"], sv[f"b{which}"], dh, g_at(lw["g_pre"], k),
            g_at(lw["g_post"], k), lw[n + "_gate"], lw[n + "_up"], lw[n + "_down"])
        big[n + "_gate"] = hosted(n + "_gate", _wgrad, nb, da, N_CHIPS)
        big[n + "_up"] = hosted(n + "_up", _wgrad, nb, db, N_CHIPS)
        big[n + "_down"] = hosted(n + "_down", _wgrad, sb, df, 0).reshape(N_CHIPS, FF // N_CHIPS, D)
        return dh

    dh = ffn_back(2, dh, sv["h2"])
    dmix, dtok, dqm, dkv, dgpost[1] = _mixout_bwd(dh, sv["mix"], sv["proj"], sv["kv"], lw["w_out"],
                                                  g_at(lw["g_post"], 1), _qm_block(i))
    big["w_out"] = _wgrad(sv["cat"], dmix, N_CHIPS)
    dwkv, small["g_mem"] = _memkv_bwd(mem, lw["g_mem"], lw["w_mem_kv"], dkv)
    big["w_mem_kv"] = dwkv.reshape(N_CHIPS, D // N_CHIPS, 2 * DMEM)
    if _is_pool(i):
        dx, small["pool_w"], small["pool_scale"] = _pool_bwd(sv["proj"], dtok, lw["pool_w"], lw["pool_scale"])
        dh, dproj, dgpre[1] = _proj_bwd(sv["h1"], dh, g_at(lw["g_pre"], 1), lw["w_in"], [dx, dqm])
        big["w_in"] = _wgrad(sv["u"], dproj).reshape(N_CHIPS, D // N_CHIPS, DMIX)
    else:
        dq, dk, dv = _sb_bwd(sv["proj"], sv["o"], dtok)
        dh, dproj, dgpre[1] = _proj_bwd(sv["h1"], dh, g_at(lw["g_pre"], 1), lw["w_in"], [dq, dk, dv, dqm])
        big["w_in"] = jnp.transpose(_wgrad(sv["u"], dproj).reshape(D, N_CHIPS, DSB // N_CHIPS), (1, 0, 2))
    dh = ffn_back(1, dh, sv["h0"])
    small["g_pre"] = jnp.concatenate(dgpre, axis=0)
    small["g_post"] = jnp.concatenate(dgpost, axis=0)
    return dh, big, small, rode


ANY = pl.BlockSpec(memory_space=pl.ANY)


def _place():
    return lax.axis_index("x"), lax.axis_index("y"), lax.axis_index("c")


def _flip(v, bit):
    return 1 - v if bit else v


def _dma_sems(m):
    return pltpu.SemaphoreType.DMA((m,))


class _SmallExchange:
    name = "small_allgather"

    def __init__(self, buf):
        self.inputs = [buf]
        self.out_shapes = [jax.ShapeDtypeStruct((N_DEV,) + buf.shape, buf.dtype)]
        self.sems = [_dma_sems(N_DEV - 1), _dma_sems(N_DEV - 1)]

    def _copies(self, ins, outs, sems):
        x, y, c = _place()
        me = 4 * x + 2 * y + c
        out, arrive = [], []
        for k in range(1, N_DEV):
            peer = (_flip(x, k & 4), _flip(y, k & 2), _flip(c, k & 1))
            slot = 4 * peer[0] + 2 * peer[1] + peer[2]
            how = dict(send_sem=sems[0].at[k - 1], recv_sem=sems[1].at[k - 1], device_id=peer, device_id_type=MESH)
            out.append(pltpu.make_async_remote_copy(src_ref=ins[0], dst_ref=outs[0].at[me], **how))
            arrive.append(pltpu.make_async_remote_copy(src_ref=ins[0], dst_ref=outs[0].at[slot], **how))
        return out, arrive

    def start(self, ins, outs, sems):
        for cp in self._copies(ins, outs, sems)[0]:
            cp.start()

    def mid(self, ins, outs, sems):
        pass

    def finish(self, ins, outs, sems):
        out, arrive = self._copies(ins, outs, sems)
        for cp in arrive:
            cp.wait_recv()
        for cp in out:
            cp.wait_send()


def _small_allgather(buf, me):
    gathered = _run_exchange(_SmallExchange(buf))[0]
    slots = lax.broadcasted_iota(jnp.int32, (N_DEV, 1, 1), 0)
    return jnp.where(slots == me, buf[None], gathered)


class _GatherExchange:
    name = "gather_weights"
    aliased = True

    def __init__(self, bufs):
        n = len(bufs)
        self.n, self.inputs = n, list(bufs)
        self.halves = [b.shape[1] // 2 for b in bufs]
        self.out_shapes = [jax.ShapeDtypeStruct(b.shape, b.dtype) for b in bufs]
        self.sems = [_dma_sems(3 * n), _dma_sems(3 * n), _dma_sems(3 * n), _dma_sems(3 * n)]

    def _copies(self, outs, sems):
        ici_send, ici_recv, d2d_send, d2d_recv = sems
        x, y, c = _place()
        me = 2 * x + y
        sib = (x, y, 1 - c)
        chips = [(_flip(x, k & 2), _flip(y, k & 1)) for k in (1, 2, 3)]
        half = lambda t, slot, hc: outs[t].at[slot, pl.ds(hc * self.halves[t], self.halves[t])]
        ici_out, ici_in, fwd_out, fwd_in = [], [], [], []
        for t in range(self.n):
            for k, chip in enumerate(chips):
                idx = 2 * chip[0] + chip[1]
                peer = (chip[0], chip[1], c)
                ici = dict(send_sem=ici_send.at[3 * t + k], recv_sem=ici_recv.at[3 * t + k], device_id=peer,
                           device_id_type=MESH)
                d2d = dict(send_sem=d2d_send.at[3 * t + k], recv_sem=d2d_recv.at[3 * t + k], device_id=sib,
                           device_id_type=MESH)
                ici_out.append(pltpu.make_async_remote_copy(src_ref=half(t, me, c), dst_ref=half(t, me, c), **ici))
                ici_in.append(pltpu.make_async_remote_copy(src_ref=half(t, idx, c), dst_ref=half(t, idx, c), **ici))
                fwd_out.append(pltpu.make_async_remote_copy(src_ref=half(t, idx, c), dst_ref=half(t, idx, c), **d2d))
                fwd_in.append(pltpu.make_async_remote_copy(
                    src_ref=half(t, idx, 1 - c), dst_ref=half(t, idx, 1 - c), **d2d))
        return ici_out, ici_in, fwd_out, fwd_in

    def start(self, ins, outs, sems):
        for cp in self._copies(outs, sems)[0]:
            cp.start()

    def mid(self, ins, outs, sems):
        _, ici_in, fwd_out, _ = self._copies(outs, sems)
        for arrived, onward in zip(ici_in, fwd_out):
            arrived.wait_recv()
            onward.start()

    def finish(self, ins, outs, sems):
        ici_out, _, fwd_out, fwd_in = self._copies(outs, sems)
        for cp in fwd_in:
            cp.wait_recv()
        for cp in ici_out + fwd_out:
            cp.wait_send()


class _SiblingExchange:
    name = "rs_to_sibling"

    def __init__(self, gs):
        n = len(gs)
        self.n, self.inputs = n, list(gs)
        self.halves = [g.shape[1] // 2 for g in gs]
        self.out_shapes = [jax.ShapeDtypeStruct((N_CHIPS, g.shape[1] // 2) + g.shape[2:], g.dtype) for g in gs]
        self.sems = [_dma_sems(n), _dma_sems(n)]

    def _copies(self, ins, outs, sems):
        x, y, c = _place()
        return [pltpu.make_async_remote_copy(
            src_ref=ins[t].at[:, pl.ds((1 - c) * self.halves[t], self.halves[t])], dst_ref=outs[t],
            send_sem=sems[0].at[t], recv_sem=sems[1].at[t], device_id=(x, y, 1 - c), device_id_type=MESH)
            for t in range(self.n)]

    def start(self, ins, outs, sems):
        for cp in self._copies(ins, outs, sems):
            cp.start()

    def mid(self, ins, outs, sems):
        pass

    def finish(self, ins, outs, sems):
        for cp in self._copies(ins, outs, sems):
            cp.wait()


class _ChipsExchange:
    name = "rs_to_chips"

    def __init__(self, ss):
        n = len(ss)
        self.n, self.inputs = n, list(ss)
        self.out_shapes = [jax.ShapeDtypeStruct((3,) + s.shape[1:], s.dtype) for s in ss]
        self.sems = [_dma_sems(3 * n), _dma_sems(3 * n)]

    def _copies(self, ins, outs, sems):
        x, y, c = _place()
        chips = [(_flip(x, k & 2), _flip(y, k & 1)) for k in (1, 2, 3)]
        return [pltpu.make_async_remote_copy(
            src_ref=ins[t].at[2 * chip[0] + chip[1]], dst_ref=outs[t].at[k], send_sem=sems[0].at[3 * t + k],
            recv_sem=sems[1].at[3 * t + k], device_id=(chip[0], chip[1], c), device_id_type=MESH)
            for t in range(self.n) for k, chip in enumerate(chips)]

    def start(self, ins, outs, sems):
        for cp in self._copies(ins, outs, sems):
            cp.start()

    def mid(self, ins, outs, sems):
        pass

    def finish(self, ins, outs, sems):
        for cp in self._copies(ins, outs, sems):
            cp.wait()


class _Both:
    def __init__(self, first, second):
        self.parts = (first, second)
        self.name = first.name + "_" + second.name
        self.inputs = first.inputs + second.inputs
        self.out_shapes = first.out_shapes + second.out_shapes
        self.sems = first.sems + second.sems

    def _each(self, ins, outs, sems):
        a = self.parts[0]
        ni, no, ns = len(a.inputs), len(a.out_shapes), len(a.sems)
        return ((a, ins[:ni], outs[:no], sems[:ns]), (self.parts[1], ins[ni:], outs[no:], sems[ns:]))

    def start(self, ins, outs, sems):
        for ex, i, o, s in self._each(ins, outs, sems):
            ex.start(i, o, s)

    def mid(self, ins, outs, sems):
        for ex, i, o, s in self._each(ins, outs, sems):
            ex.mid(i, o, s)

    def finish(self, ins, outs, sems):
        for ex, i, o, s in self._each(ins, outs, sems):
            ex.finish(i, o, s)

    def split(self, outs):
        no = len(self.parts[0].out_shapes)
        return outs[:no], outs[no:]


def _run_exchange(ex):
    ni, no = len(ex.inputs), len(ex.out_shapes)

    def body(*refs):
        ins, outs, sems = refs[:ni], refs[ni:ni + no], refs[ni + no:]
        ex.start(ins, outs, sems)
        ex.mid(ins, outs, sems)
        ex.finish(ins, outs, sems)

    return pl.pallas_call(
        body, name=ex.name, in_specs=[ANY] * ni, out_specs=[ANY] * no, out_shape=ex.out_shapes,
        scratch_shapes=ex.sems, input_output_aliases={t: t for t in range(ni)} if _in_place(ex) else {},
    )(*ex.inputs)


def _in_place(ex):
    return getattr(ex, "aliased", False)


def _place_own(shard):
    L_, r, n_ = shard.shape
    tr = _row_tile(r, 512 if r % 512 == 0 else 256)

    def body(x_ref, *refs):
        outs, stage, sems = refs[:L_], refs[L_], refs[L_ + 1]
        x, y, _ = _place()
        rows = pl.ds(pl.multiple_of(pl.program_id(0) * tr, tr), tr)
        copies = []
        for l in range(L_):
            stage[l] = x_ref[l].astype(_MM)
            cp = pltpu.make_async_copy(stage.at[l], outs[l].at[2 * x + y, rows], sems.at[l])
            cp.start()
            copies.append(cp)
        for cp in copies:
            cp.wait()

    return pl.pallas_call(
        body, name="place_own", grid=(r // tr,),
        in_specs=[pl.BlockSpec((L_, tr, n_), lambda i: (0, i, 0))], out_specs=[ANY] * L_,
        out_shape=[jax.ShapeDtypeStruct((N_CHIPS, r, n_), _MM)] * L_,
        scratch_shapes=[pltpu.VMEM((L_, tr, n_), _MM), _dma_sems(L_)],
        compiler_params=_cparams(("arbitrary",)),
    )(shard)


def _call(body, name, nsteps, in_specs, out_specs, out_shape, args, ride=None):
    if ride is None:
        return pl.pallas_call(body, name=name, grid=(nsteps,), in_specs=in_specs, out_specs=out_specs,
                              out_shape=out_shape, compiler_params=_cparams(("arbitrary",)))(*args)
    ni, no = len(args), len(out_shape)
    ri, ro = len(ride.inputs), len(ride.out_shapes)
    mid_step = (13 * nsteps) // 16 if nsteps > 1 else 0

    def hosted(*refs):
        a, ra = refs[:ni], refs[ni:ni + ri]
        o, ro_refs = refs[ni + ri:ni + ri + no], refs[ni + ri + no:ni + ri + no + ro]
        sems = refs[ni + ri + no + ro:]
        step = pl.program_id(0)

        @pl.when(step == 0)
        def _():
            ride.start(ra, ro_refs, sems)

        body(*a, *o)

        @pl.when(step == mid_step)
        def _():
            ride.mid(ra, ro_refs, sems)

        @pl.when(step == nsteps - 1)
        def _():
            ride.finish(ra, ro_refs, sems)

    outs = pl.pallas_call(
        hosted, name=name + "_" + ride.name, grid=(nsteps,), in_specs=list(in_specs) + [ANY] * ri,
        out_specs=list(out_specs) + [ANY] * ro, out_shape=list(out_shape) + ride.out_shapes,
        scratch_shapes=ride.sems, compiler_params=_cparams(("arbitrary",)),
        input_output_aliases={ni + t: no + t for t in range(ri)} if _in_place(ride) else {},
    )(*args, *ride.inputs)
    return outs[:no], outs[no:]


def _share_halves(ts):
    n = len(ts)

    def body(*refs):
        outs = refs[n:2 * n]
        send_sems, recv_sems = refs[2 * n:]
        x, y, c = _place()
        cps = []
        for t in range(n):
            cp = pltpu.make_async_remote_copy(
                src_ref=outs[t].at[c], dst_ref=outs[t].at[c], send_sem=send_sems.at[t], recv_sem=recv_sems.at[t],
                device_id=(x, y, 1 - c), device_id_type=MESH)
            cp.start()
            cps.append(cp)
        for t in range(n):
            pltpu.make_async_remote_copy(
                src_ref=outs[t].at[1 - c], dst_ref=outs[t].at[1 - c], send_sem=send_sems.at[t],
                recv_sem=recv_sems.at[t], device_id=(x, y, 1 - c), device_id_type=MESH).wait_recv()
        for cp in cps:
            cp.wait_send()

    dma = lambda m: pltpu.SemaphoreType.DMA((m,))
    return pl.pallas_call(
        body, name="share_halves", in_specs=[ANY] * n, out_specs=[ANY] * n,
        out_shape=[jax.ShapeDtypeStruct(t.shape, t.dtype) for t in ts],
        input_output_aliases={t: t for t in range(n)},
        scratch_shapes=[dma(n), dma(n)],
    )(*ts)


def _by_shape(fn, *lists):
    groups = {}
    for idx, a in enumerate(lists[0]):
        groups.setdefault(a.shape, []).append(idx)
    out = [None] * len(lists[0])
    for idxs in groups.values():
        for idx, res in zip(idxs, fn(*[[lst[idx] for idx in idxs] for lst in lists])):
            out[idx] = res
    return out


def _add_sibling(gs, rs, c):
    m = len(gs)
    _, r_, n_ = gs[0].shape
    h = r_ // 2
    tr = _row_tile(h, 512)

    def body(c_ref, *refs):
        for g_ref, r_ref, o_ref in zip(refs[:m], refs[m:2 * m], refs[2 * m:]):
            o_ref[...] = (g_ref[...] + r_ref[...]).astype(o_ref.dtype)

    gspec = pl.BlockSpec((None, None, tr, n_), lambda s, i, c_ref: (s, c_ref[0], i, 0))
    hspec = pl.BlockSpec((None, tr, n_), lambda s, i, c_ref: (s, i, 0))
    return pl.pallas_call(
        body, name="add_sibling",
        grid_spec=pltpu.PrefetchScalarGridSpec(
            num_scalar_prefetch=1, grid=(N_CHIPS, h // tr), in_specs=[gspec] * m + [hspec] * m, out_specs=[hspec] * m),
        out_shape=[jax.ShapeDtypeStruct((N_CHIPS, h, n_), _MM)] * m,
        compiler_params=_cparams(("parallel", "parallel")),
    )(c.reshape(1), *[g.reshape(N_CHIPS, 2, h, n_) for g in gs], *rs)


def _add_chips(ss, rs, me, c):
    m = len(ss)
    _, h, n_ = ss[0].shape
    tr = _row_tile(h, 512)

    def body(at_ref, *refs):
        f = lambda ref: ref[...].astype(F32)
        for t in range(m):
            s_ref, (r0, r1, r2), o_ref = refs[t], refs[m + 3 * t:m + 3 * t + 3], refs[4 * m + t]
            o_ref[...] = ((f(s_ref) + f(r0)) + f(r1)) + f(r2)

    rk = lambda k: pl.BlockSpec((None, tr, n_), lambda i, at_ref: (k, i, 0))
    return pl.pallas_call(
        body, name="add_chips",
        grid_spec=pltpu.PrefetchScalarGridSpec(
            num_scalar_prefetch=1, grid=(h // tr,),
            in_specs=[pl.BlockSpec((None, tr, n_), lambda i, at_ref: (at_ref[0], i, 0))] * m + [rk(0), rk(1), rk(2)] * m,
            out_specs=[pl.BlockSpec((None, tr, n_), lambda i, at_ref: (at_ref[1], i, 0))] * m),
        out_shape=[jax.ShapeDtypeStruct((2, h, n_), F32)] * m,
        compiler_params=_cparams(("parallel",)),
    )(jnp.stack([me, c]), *ss, *[r for r in rs for _ in range(3)])


def _sum_devices(a):
    R_ = a.shape[1]

    def body(a_ref, o_ref):
        acc = a_ref[0]
        for k in range(1, N_DEV):
            acc = acc + a_ref[k]
        o_ref[...] = acc

    return pl.pallas_call(
        body, name="sum_devices", out_shape=jax.ShapeDtypeStruct((R_, 128), F32),
        compiler_params=pltpu.CompilerParams(vmem_limit_bytes=VMEM_LIMIT),
    )(a)


def _adamw(w, g, m, v):
    shape = w.shape
    to2 = lambda a: a.reshape(-1, shape[-1])
    w2, g2, m2, v2 = to2(w), to2(g), to2(m), to2(v)
    R_, C = w2.shape
    tr = 256 if R_ % 256 == 0 else R_

    def body(w_ref, g_ref, m_ref, v_ref, d_ref, mo_ref, vo_ref):
        gv = g_ref[...]
        mn = ADAM_B1 * m_ref[...] + (1.0 - ADAM_B1) * gv
        vn = ADAM_B2 * v_ref[...] + (1.0 - ADAM_B2) * (gv * gv)
        m_hat = mn / (1.0 - ADAM_B1 ** ADAM_STEP)
        v_hat = vn / (1.0 - ADAM_B2 ** ADAM_STEP)
        d_ref[...] = -ADAM_LR * (m_hat / (jnp.sqrt(v_hat) + ADAM_EPS) + ADAM_WD * w_ref[...])
        mo_ref[...] = mn
        vo_ref[...] = vn

    spec = pl.BlockSpec((tr, C), lambda i: (i, 0))
    outs = pl.pallas_call(
        body, name="adamw", grid=(R_ // tr,), in_specs=[spec] * 4, out_specs=[spec] * 3,
        out_shape=[jax.ShapeDtypeStruct((R_, C), F32)] * 3,
        compiler_params=_cparams(("parallel",)),
    )(w2, g2, m2, v2)
    return [o.reshape(shape) for o in outs]


SHARD_AXIS = {"g_pre": 2, "g_post": 2, "g_mem": None, "ffn1_gate": 2, "ffn1_up": 2, "ffn1_down": 1,
              "ffn2_gate": 2, "ffn2_up": 2, "ffn2_down": 1, "w_in_pool": 1, "pool_w": None, "pool_scale": None,
              "w_in_sb": 2, "w_mem_kv": 1, "w_out": 2}
WEIGHTS = list(SHARD_AXIS)
BIG = [k for k in WEIGHTS if SHARD_AXIS[k] is not None and k not in ("g_pre", "g_post")]
SMALL = [k for k in WEIGHTS if k not in BIG]


def _pack128(arrs):
    flat = jnp.concatenate([a.reshape(-1) for a in arrs])
    pad = (-flat.shape[0]) % (8 * 128)
    return jnp.pad(flat, (0, pad)).reshape(-1, 128)


def _unpack128(buf, shapes):
    flat = buf.reshape(-1)
    out, off = [], 0
    for shp in shapes:
        size = 1
        for d in shp:
            size *= d
        out.append(flat[off:off + size].reshape(shp))
        off += size
    return out


def kernel(x, mem, g_pre, g_post, g_mem, ffn1_gate, ffn1_up, ffn1_down, ffn2_gate, ffn2_up, ffn2_down, w_in_pool, pool_w, pool_scale, w_in_sb, w_mem_kv, w_out, loss_target, m_g_pre, m_g_post, m_g_mem, m_ffn1_gate, m_ffn1_up, m_ffn1_down, m_ffn2_gate, m_ffn2_up, m_ffn2_down, m_w_in_pool, m_pool_w, m_pool_scale, m_w_in_sb, m_w_mem_kv, m_w_out, v_g_pre, v_g_post, v_g_mem, v_ffn1_gate, v_ffn1_up, v_ffn1_down, v_ffn2_gate, v_ffn2_up, v_ffn2_down, v_w_in_pool, v_pool_w, v_pool_scale, v_w_in_sb, v_w_mem_kv, v_w_out):
    w = dict(g_pre=g_pre, g_post=g_post, g_mem=g_mem, ffn1_gate=ffn1_gate, ffn1_up=ffn1_up, ffn1_down=ffn1_down,
             ffn2_gate=ffn2_gate, ffn2_up=ffn2_up, ffn2_down=ffn2_down, w_in_pool=w_in_pool, pool_w=pool_w,
             pool_scale=pool_scale, w_in_sb=w_in_sb, w_mem_kv=w_mem_kv, w_out=w_out)
    m = dict(g_pre=m_g_pre, g_post=m_g_post, g_mem=m_g_mem, ffn1_gate=m_ffn1_gate, ffn1_up=m_ffn1_up,
             ffn1_down=m_ffn1_down, ffn2_gate=m_ffn2_gate, ffn2_up=m_ffn2_up, ffn2_down=m_ffn2_down,
             w_in_pool=m_w_in_pool, pool_w=m_pool_w, pool_scale=m_pool_scale, w_in_sb=m_w_in_sb,
             w_mem_kv=m_w_mem_kv, w_out=m_w_out)
    v = dict(g_pre=v_g_pre, g_post=v_g_post, g_mem=v_g_mem, ffn1_gate=v_ffn1_gate, ffn1_up=v_ffn1_up,
             ffn1_down=v_ffn1_down, ffn2_gate=v_ffn2_gate, ffn2_up=v_ffn2_up, ffn2_down=v_ffn2_down,
             w_in_pool=v_w_in_pool, pool_w=v_pool_w, pool_scale=v_pool_scale, w_in_sb=v_w_in_sb,
             w_mem_kv=v_w_mem_kv, w_out=v_w_out)
    cx, cy, cc = _place()
    chip = (2 * cx + cy).astype(jnp.int32)
    core = cc.astype(jnp.int32)

    dev = (4 * cx + 2 * cy + cc).astype(jnp.int32)

    gains = _small_allgather(_pack128([g_pre, g_post]), dev)
    per_chip = [_unpack128(gains[2 * j], [g_pre.shape, g_post.shape]) for j in range(N_CHIPS)]
    full_pre = jnp.concatenate([p[0] for p in per_chip], axis=2)
    full_post = jnp.concatenate([p[1] for p in per_chip], axis=2)

    def shard_name(k, i):
        return ("w_in_pool" if _is_pool(i) else "w_in_sb") if k == "w_in" else k

    placed = {k: _place_own(w[k]) for k in BIG}

    def own(i, names):
        return [placed[shard_name(k, i)][i // 2 if k == "w_in" else i] for k in names]

    xs, mems = x[0], mem[0]
    h, lws, saved = xs, [], []
    first = _run_exchange(_GatherExchange(own(0, FIRST_FFN)))
    for i in range(DEPTH):
        lw = dict(zip(FIRST_FFN, first))
        lw["g_pre"], lw["g_post"], lw["g_mem"] = full_pre[i], full_post[i], g_mem[i].reshape(1, D)
        if _is_pool(i):
            lw["pool_w"], lw["pool_scale"] = pool_w[i // 2], pool_scale[i // 2].reshape(1, DTOK)
        h, sv, first = _layer_forward(h, mems, lw, i, own(i, REST), own(i + 1, FIRST_FFN) if i + 1 < DEPTH else None)
        lws.append(lw)
        saved.append(sv)
    acc, dh = _loss_grad(h, loss_target[0])
    loss = lax.psum(0.5 / D * jnp.sum(acc), ("x", "y", "c"))

    mine, smalls = [None] * DEPTH, [None] * DEPTH
    above, sums = None, {}
    nt = len(LAYER_TENSORS)
    early = ("ffn2_gate", "w_mem_kv", "ffn2_up", "w_out", "ffn2_down", "w_in")
    add_sib = lambda gs, from_sib: _by_shape(lambda g, r: _add_sibling(g, r, core), gs, from_sib)
    add_chp = lambda ss, from_chips: _by_shape(lambda s, r: _add_chips(s, r, chip, core), ss, from_chips)

    for i in reversed(range(DEPTH)):
        rides = {}
        if above is not None:
            up = [above[k] for k in LAYER_TENSORS]
            rides["ffn2"] = lambda big, rode, up=up: _SiblingExchange(up)

            def on_ffn1(big, rode, up=up, layer=i + 1, bottom=(i == 0)):
                sums[layer] = add_sib(up, rode["ffn2"])
                ex = _ChipsExchange(sums[layer])
                return _Both(ex, _SiblingExchange([big[k] for k in early])) if bottom else ex

            def on_wgrad(big, rode, j):
                if "early" not in sums:
                    sums["early"] = add_sib([big[k] for k in early], rode["ffn1"][nt:])
                return _ChipsExchange(sums["early"][2 * j:2 * j + 2])

            rides["ffn1"] = on_ffn1
            if i == 0:
                for j, host in enumerate(FIRST_FFN):
                    rides[host] = lambda big, rode, j=j: on_wgrad(big, rode, j)
        dh, gs, smalls[i], rode = _layer_backward(dh, mems, lws[i], saved[i], i, rides)
        if above is not None:
            mine[i + 1] = add_chp(sums[i + 1], rode["ffn1"][:nt])
        above = gs
    early_mine = add_chp(sums["early"], [t for host in FIRST_FFN for t in rode[host]])
    late = [above[k] for k in FIRST_FFN]
    late_sums = add_sib(late, _run_exchange(_SiblingExchange(late)))
    late_mine = add_chp(late_sums, _run_exchange(_ChipsExchange(late_sums)))
    by_name = dict(zip(early + FIRST_FFN, early_mine + late_mine))
    mine[0] = [by_name[k] for k in LAYER_TENSORS]
    shared = _share_halves([t for layer in mine for t in layer])
    reduced = [dict(zip(LAYER_TENSORS, [s.reshape(-1, s.shape[-1]) for s in shared[nt * i:nt * (i + 1)]]))
               for i in range(DEPTH)]
    dx = dh
    red = {}
    for k in BIG:
        if k.startswith("w_in_"):
            layers = [i for i in range(DEPTH) if _is_pool(i) == (k == "w_in_pool")]
            red[k] = jnp.stack([reduced[i]["w_in"] for i in layers])
        else:
            red[k] = jnp.stack([reduced[i][k] for i in range(DEPTH)])

    grads = {"g_pre": jnp.stack([s["g_pre"] for s in smalls]), "g_post": jnp.stack([s["g_post"] for s in smalls]),
             "g_mem": jnp.concatenate([s["g_mem"] for s in smalls], axis=0),
             "pool_w": jnp.stack([smalls[i]["pool_w"] for i in range(DEPTH) if _is_pool(i)]),
             "pool_scale": jnp.concatenate([smalls[i]["pool_scale"] for i in range(DEPTH) if _is_pool(i)], axis=0)}
    small_shapes = [grads[k].shape for k in SMALL]
    summed = _unpack128(_sum_devices(_small_allgather(_pack128([grads[k] for k in SMALL]), dev)), small_shapes)
    for k, s in zip(SMALL, summed):
        if SHARD_AXIS[k] is None:
            red[k] = s
        else:
            red[k] = lax.dynamic_slice_in_dim(s, chip * w[k].shape[2], w[k].shape[2], axis=2)

    deltas, new_m, new_v = {}, {}, {}
    for k in WEIGHTS:
        deltas[k], new_m[k], new_v[k] = _adamw(w[k], red[k], m[k], v[k])
    return (loss, dx.reshape(x.shape), *[red[k] for k in WEIGHTS], *[deltas[k] for k in WEIGHTS],
            *[new_m[k] for k in WEIGHTS], *[new_v[k] for k in WEIGHTS])
```

```python
import jax
import jax.numpy as jnp
from jax import lax
from jax.experimental import pallas as pl
from jax.experimental.pallas import tpu as pltpu

F32 = jnp.float32
BF16 = jnp.bfloat16
_MM = jnp.bfloat16

D = 1024
FF = 2048
DTOK = 512
DMEM = 256
DMIX = DTOK + DMEM
DSB = 3 * DTOK + DMEM
HD = 64
MEM_LEN = 256
DEPTH = 4
EPS = 1e-6
WINDOWS = (2, 4, 8, 16)
HALO = 16
PG = 128
QB = 128
SCALE = HD ** -0.5
FC = 512
NEG_CUT = -104.0

ADAM_LR, ADAM_B1, ADAM_B2, ADAM_EPS, ADAM_WD, ADAM_STEP = 0.001, 0.9, 0.999, 1e-08, 0.01, 10

VMEM_LIMIT = 56 * 2 ** 20
MESH = pl.DeviceIdType.MESH
N_CHIPS = 4
N_DEV = 8


def _cparams(sem):
    return pltpu.CompilerParams(dimension_semantics=sem, vmem_limit_bytes=VMEM_LIMIT)


def _mm(a, b):
    return jnp.dot(a.astype(_MM), b.astype(_MM), preferred_element_type=F32)


def _mm_nt(a, b):
    return lax.dot_general(a.astype(_MM), b.astype(_MM), (((1,), (1,)), ((), ())), preferred_element_type=F32)


def _mm_tn(a, b):
    return lax.dot_general(a.astype(_MM), b.astype(_MM), (((0,), (0,)), ((), ())), preferred_element_type=F32)


def _mm2(a, b):
    hi = a.astype(_MM)
    lo = (a - hi.astype(F32)).astype(_MM)
    return jnp.dot(hi, b, preferred_element_type=F32) + jnp.dot(lo, b, preferred_element_type=F32)


def _rstd(x):
    return lax.rsqrt(jnp.mean(x * x, axis=-1, keepdims=True) + EPS)


def _rms_bwd(x, g, dy):
    r = _rstd(x)
    xh = x * r
    t = dy * g
    dx = r * (t - xh * jnp.mean(t * xh, axis=-1, keepdims=True))
    return dx, jnp.sum(dy * xh, axis=0, keepdims=True)


def _sigmoid(a):
    return 1.0 / (1.0 + jnp.exp(-a))


def _row_tile(n, want):
    t = min(n, want)
    assert n % t == 0, (n, t)
    return t


def _ffn_fwd(h, g1, g2, wg, wu, wd, ride=None):
    S = h.shape[0]
    ts = _row_tile(S, 512)

    def body(h_ref, g1_ref, g2_ref, wg_ref, wu_ref, wd_ref, ho_ref, a_ref, b_ref, f_ref):
        hv = h_ref[...]
        n = (hv * _rstd(hv) * g1_ref[...]).astype(_MM)
        f = jnp.zeros((ts, D), F32)
        for j in range(FF // FC):
            cs = slice(j * FC, (j + 1) * FC)
            a = _mm(n, wg_ref[j])
            b = _mm(n, wu_ref[j])
            a_ref[:, cs] = a.astype(a_ref.dtype)
            b_ref[:, cs] = b.astype(b_ref.dtype)
            f = f + _mm(a * _sigmoid(a) * b, wd_ref[j])
        f_ref[...] = f
        ho_ref[...] = hv + 0.5 * (f * _rstd(f) * g2_ref[...])

    row = lambda i: (i, 0)
    fix = lambda i: (0, 0)
    all3 = lambda i: (0, 0, 0)
    ns = FF // FC
    return _call(
        body, "ffn_fwd", S // ts,
        [pl.BlockSpec((ts, D), row), pl.BlockSpec((1, D), fix), pl.BlockSpec((1, D), fix),
         pl.BlockSpec((ns, D, FC), all3), pl.BlockSpec((ns, D, FC), all3), pl.BlockSpec((ns, FC, D), all3)],
        [pl.BlockSpec((ts, D), row), pl.BlockSpec((ts, FF), row), pl.BlockSpec((ts, FF), row),
         pl.BlockSpec((ts, D), row)],
        [jax.ShapeDtypeStruct((S, D), F32), jax.ShapeDtypeStruct((S, FF), _MM),
         jax.ShapeDtypeStruct((S, FF), _MM), jax.ShapeDtypeStruct((S, D), F32)],
        (h, g1, g2, wg, wu, wd), ride)


def _ffn_bwd(h, f, a, b, dho, g1, g2, wg, wu, wd, ride=None):
    S = h.shape[0]
    ts = _row_tile(S, 256)

    def body(h_ref, f_ref, a_ref, b_ref, dho_ref, g1_ref, g2_ref, wg_ref, wu_ref, wd_ref,
             dh_ref, n_ref, s_ref, da_ref, db_ref, df_ref, dg1_ref, dg2_ref):
        @pl.when(pl.program_id(0) == 0)
        def _():
            dg1_ref[...] = jnp.zeros_like(dg1_ref)
            dg2_ref[...] = jnp.zeros_like(dg2_ref)

        hv = h_ref[...]
        dho = dho_ref[...]
        df, dg2 = _rms_bwd(f_ref[...], g2_ref[...], 0.5 * dho)
        dg2_ref[...] += dg2
        dfb = df.astype(_MM)
        df_ref[...] = dfb
        rh = _rstd(hv)
        hh = hv * rh
        g1 = g1_ref[...]
        n_ref[...] = (hh * g1).astype(_MM)
        dss = [_mm_nt(dfb, wd_ref[j]) for j in range(FF // FC)]
        das, dbs = [], []
        for j in range(FF // FC):
            cs = slice(j * FC, (j + 1) * FC)
            av = a_ref[:, cs].astype(F32)
            bv = b_ref[:, cs].astype(F32)
            sig = _sigmoid(av)
            sl = av * sig
            s_ref[:, cs] = (sl * bv).astype(_MM)
            da = (dss[j] * bv * (sig * (1.0 + av * (1.0 - sig)))).astype(_MM)
            db = (dss[j] * sl).astype(_MM)
            da_ref[:, cs] = da
            db_ref[:, cs] = db
            das.append(da)
            dbs.append(db)
        dn = jnp.zeros((ts, D), F32)
        for j in range(FF // FC):
            dn = dn + _mm_nt(das[j], wg_ref[j]) + _mm_nt(dbs[j], wu_ref[j])
        t = dn * g1
        dh_ref[...] = dho + rh * (t - hh * jnp.mean(t * hh, axis=-1, keepdims=True))
        dg1_ref[...] += jnp.sum(dn * hh, axis=0, keepdims=True)

    row = lambda i: (i, 0)
    fix = lambda i: (0, 0)
    all3 = lambda i: (0, 0, 0)
    ns = FF // FC
    td = pl.BlockSpec((ts, D), row)
    tf = pl.BlockSpec((ts, FF), row)
    gd = pl.BlockSpec((1, D), fix)
    return _call(
        body, "ffn_bwd", S // ts,
        [td, td, tf, tf, td, gd, gd,
         pl.BlockSpec((ns, D, FC), all3), pl.BlockSpec((ns, D, FC), all3), pl.BlockSpec((ns, FC, D), all3)],
        [td, td, tf, tf, tf, td, gd, gd],
        [jax.ShapeDtypeStruct((S, D), F32), jax.ShapeDtypeStruct((S, D), _MM),
         jax.ShapeDtypeStruct((S, FF), _MM), jax.ShapeDtypeStruct((S, FF), _MM),
         jax.ShapeDtypeStruct((S, FF), _MM), jax.ShapeDtypeStruct((S, D), _MM),
         jax.ShapeDtypeStruct((1, D), F32), jax.ShapeDtypeStruct((1, D), F32)],
        (h, f, a, b, dho, g1, g2, wg, wu, wd), ride)


def _wgrad(x, y, nslot=0, ride=None):
    T, M = x.shape
    N = y.shape[1]
    tt = _row_tile(T, 1024)
    cw = N // nslot if nslot else (512 if N % 512 == 0 else 256)
    nc = N // cw
    assert cw * nc == N and cw % 128 == 0, (N, nc)

    def body(x_ref, y_ref, o_ref):
        @pl.when(pl.program_id(0) == 0)
        def _():
            o_ref[...] = jnp.zeros_like(o_ref)

        xt = x_ref[...].T
        for j in range(nc):
            part = jnp.dot(xt, y_ref[:, j * cw:(j + 1) * cw], preferred_element_type=F32)
            if nslot:
                o_ref[j] += part
            else:
                o_ref[:, j * cw:(j + 1) * cw] += part

    oshape = (nslot, M, cw) if nslot else (M, N)
    res = _call(body, "wgrad", T // tt,
                [pl.BlockSpec((tt, M), lambda t: (t, 0)), pl.BlockSpec((tt, N), lambda t: (t, 0))],
                [pl.BlockSpec(oshape, lambda t: (0,) * len(oshape))], [jax.ShapeDtypeStruct(oshape, F32)], (x, y), ride)
    return res[0] if ride is None else (res[0][0], res[1])


def _proj_fwd(h, g, w, out_dtype):
    S = h.shape[0]
    nk, kw, N = w.shape
    ts = _row_tile(S, 256)

    def body(h_ref, g_ref, w_ref, u_ref, p_ref):
        hv = h_ref[...]
        u = (hv * _rstd(hv) * g_ref[...]).astype(_MM)
        u_ref[...] = u
        acc = _mm(u[:, :kw], w_ref[0])
        for j in range(1, nk):
            acc = acc + _mm(u[:, j * kw:(j + 1) * kw], w_ref[j])
        p_ref[...] = acc.astype(p_ref.dtype)

    return pl.pallas_call(
        body, name="proj_fwd", grid=(S // ts,),
        in_specs=[pl.BlockSpec((ts, D), lambda i: (i, 0)), pl.BlockSpec((1, D), lambda i: (0, 0)),
                  pl.BlockSpec((nk, kw, N), lambda i: (0, 0, 0))],
        out_specs=[pl.BlockSpec((ts, D), lambda i: (i, 0)), pl.BlockSpec((ts, N), lambda i: (i, 0))],
        out_shape=[jax.ShapeDtypeStruct((S, D), _MM), jax.ShapeDtypeStruct((S, N), out_dtype)],
        compiler_params=_cparams(("parallel",)),
    )(h, g, w)


def _proj_bwd(h, dho, g, w, pieces):
    S = h.shape[0]
    nk, kw, N = w.shape
    ts = _row_tile(S, 256)
    widths = [p.shape[1] for p in pieces]
    assert sum(widths) == N
    npc = len(pieces)

    def body(*refs):
        h_ref, dho_ref, g_ref, w_ref = refs[:4]
        p_refs = refs[4:4 + npc]
        dh_ref, dp_ref, dg_ref = refs[4 + npc:]

        @pl.when(pl.program_id(0) == 0)
        def _():
            dg_ref[...] = jnp.zeros_like(dg_ref)

        dus = [jnp.zeros((ts, kw), F32) for _ in range(nk)]
        off = 0
        for p_ref, n in zip(p_refs, widths):
            pv = p_ref[...].astype(_MM)
            dp_ref[:, off:off + n] = pv
            for j in range(nk):
                dus[j] = dus[j] + _mm_nt(pv, w_ref[j, :, off:off + n])
            off += n
        du = dus[0] if nk == 1 else jnp.concatenate(dus, axis=-1)
        dx, dg = _rms_bwd(h_ref[...], g_ref[...], du)
        dh_ref[...] = dho_ref[...] + dx
        dg_ref[...] += dg

    row = lambda i: (i, 0)
    return pl.pallas_call(
        body, name="proj_bwd", grid=(S // ts,),
        in_specs=[pl.BlockSpec((ts, D), row), pl.BlockSpec((ts, D), row), pl.BlockSpec((1, D), lambda i: (0, 0)),
                  pl.BlockSpec((nk, kw, N), lambda i: (0, 0, 0))] + [pl.BlockSpec((ts, n), row) for n in widths],
        out_specs=[pl.BlockSpec((ts, D), row), pl.BlockSpec((ts, N), row), pl.BlockSpec((1, D), lambda i: (0, 0))],
        out_shape=[jax.ShapeDtypeStruct((S, D), F32), jax.ShapeDtypeStruct((S, N), _MM),
                   jax.ShapeDtypeStruct((1, D), F32)],
        compiler_params=_cparams(("arbitrary",)),
    )(h, dho, g, w, *pieces)


def _pool_counts(first_row, ts):
    pos = (first_row + lax.broadcasted_iota(jnp.int32, (ts, 1), 0) + 1).astype(F32)
    return [jnp.minimum(pos, float(w)) for w in WINDOWS]


def _pool_delta(x, prev, cnts, ts):
    xe = jnp.concatenate([prev, x], axis=0)
    sums = []
    cur = xe
    for sh in (1, 2, 4, 8):
        cur = cur + pltpu.roll(cur, sh, 0)
        sums.append(cur)
    return [sums[gi][HALO:, gi * PG:(gi + 1) * PG] / cnts[gi] - x[:, gi * PG:(gi + 1) * PG]
            for gi in range(len(WINDOWS))]


def _pool_fwd(proj, pw, ps):
    S = proj.shape[0]
    ts = _row_tile(S, 256)

    def body(x_ref, pw_ref, ps_ref, tok_ref, carry_ref):
        i = pl.program_id(0)

        @pl.when(i == 0)
        def _():
            carry_ref[...] = jnp.zeros_like(carry_ref)

        x = x_ref[...]
        ds = _pool_delta(x, carry_ref[...], _pool_counts(i * ts, ts), ts)
        y = jnp.concatenate([_mm(ds[gi], pw_ref[gi]) for gi in range(len(WINDOWS))], axis=-1)
        tok_ref[...] = (y * ps_ref[...]).astype(tok_ref.dtype)
        carry_ref[...] = x[ts - HALO:, :]

    return pl.pallas_call(
        body, name="pool_fwd", grid=(S // ts,),
        in_specs=[pl.BlockSpec((ts, DTOK), lambda i: (i, 0)), pl.BlockSpec((len(WINDOWS), PG, PG), lambda i: (0, 0, 0)),
                  pl.BlockSpec((1, DTOK), lambda i: (0, 0))],
        out_specs=pl.BlockSpec((ts, DTOK), lambda i: (i, 0)),
        out_shape=jax.ShapeDtypeStruct((S, DTOK), _MM),
        scratch_shapes=[pltpu.VMEM((HALO, DTOK), F32)],
        compiler_params=_cparams(("arbitrary",)),
    )(proj, pw, ps)


def _pool_bwd(proj, dtok, pw, ps):
    S = proj.shape[0]
    ts = _row_tile(S, 256)
    nt = S // ts
    per = ts // HALO
    ng = len(WINDOWS)

    def body(x_ref, xp_ref, dt_ref, pw_ref, ps_ref, dx_ref, dpw_ref, dps_ref, carry_ref):
        i = pl.program_id(0)
        idx = nt - 1 - i

        @pl.when(i == 0)
        def _():
            carry_ref[...] = jnp.zeros_like(carry_ref)
            dpw_ref[...] = jnp.zeros_like(dpw_ref)
            dps_ref[...] = jnp.zeros_like(dps_ref)

        x = x_ref[...]
        prev = jnp.where(idx > 0, xp_ref[...], 0.0)
        cnts = _pool_counts(idx * ts, ts)
        ds = _pool_delta(x, prev, cnts, ts)
        dt = dt_ref[...]
        y = jnp.concatenate([_mm(ds[gi], pw_ref[gi]) for gi in range(ng)], axis=-1)
        dps_ref[...] += jnp.sum(dt * y, axis=0, keepdims=True)
        dy = (dt * ps_ref[...]).astype(_MM)
        dds = []
        for gi in range(ng):
            dyg = dy[:, gi * PG:(gi + 1) * PG]
            dpw_ref[gi] += _mm_tn(ds[gi], dyg)
            dds.append(_mm_nt(dyg, pw_ref[gi]))
        e = jnp.concatenate([dds[gi] / cnts[gi] for gi in range(ng)], axis=-1)
        ee = jnp.concatenate([e, carry_ref[...]], axis=0)
        rows = ts + HALO
        cur = ee
        outs = []
        for gi, sh in enumerate((1, 2, 4, 8)):
            cur = cur + pltpu.roll(cur, rows - sh, 0)
            outs.append(cur[:ts, gi * PG:(gi + 1) * PG] - dds[gi])
        dx_ref[...] = jnp.concatenate(outs, axis=-1)
        carry_ref[...] = e[:HALO, :]

    rev = lambda i: (nt - 1 - i, 0)
    return pl.pallas_call(
        body, name="pool_bwd", grid=(nt,),
        in_specs=[pl.BlockSpec((ts, DTOK), rev),
                  pl.BlockSpec((HALO, DTOK), lambda i: (jnp.maximum((nt - 1 - i) * per - 1, 0), 0)),
                  pl.BlockSpec((ts, DTOK), rev), pl.BlockSpec((ng, PG, PG), lambda i: (0, 0, 0)),
                  pl.BlockSpec((1, DTOK), lambda i: (0, 0))],
        out_specs=[pl.BlockSpec((ts, DTOK), rev), pl.BlockSpec((ng, PG, PG), lambda i: (0, 0, 0)),
                   pl.BlockSpec((1, DTOK), lambda i: (0, 0))],
        out_shape=[jax.ShapeDtypeStruct((S, DTOK), F32), jax.ShapeDtypeStruct((ng, PG, PG), F32),
                   jax.ShapeDtypeStruct((1, DTOK), F32)],
        scratch_shapes=[pltpu.VMEM((HALO, DTOK), F32)],
        compiler_params=_cparams(("arbitrary",)),
    )(proj, proj, dtok, pw, ps)


def _tri(strict):
    r = lax.broadcasted_iota(jnp.int32, (QB, QB), 0)
    c = lax.broadcasted_iota(jnp.int32, (QB, QB), 1)
    return jnp.where(r > c if strict else r >= c, 1.0, 0.0).astype(_MM)


def _sb_terms(z, valid):
    e = jnp.exp(-jnp.abs(z))
    sp = jnp.log(1.0 + e)
    ls = jnp.minimum(z, 0.0) - sp
    lf = jnp.where(valid, jnp.minimum(-z, 0.0) - sp, 0.0)
    return ls, lf, e


def _sb_alive(cs, lo=0):
    top = cs[0][lo:]
    for c in cs[1:]:
        top = jnp.maximum(top, c[lo:])
    return (jnp.max(top) > NEG_CUT).astype(jnp.int32)


def _sb_split(x2, head_a):
    zero = jnp.zeros_like(x2)
    return jnp.where(head_a, x2, zero), jnp.where(head_a, zero, x2)


SB_GROUPS_FWD = 4
SB_GROUPS_BWD = 4
SB_TOP = 64


def _sb_fwd(proj):
    S = proj.shape[0]
    G = min(SB_GROUPS_FWD, S // QB)
    nstep = S // (G * QB)
    npair = DTOK // 128
    nch = 2 * G

    def body(q_ref, k_ref, v_ref, tri_ref, o_ref):
        qb = pl.program_id(1)
        head_a = lax.broadcasted_iota(jnp.int32, (QB, 128), 1) < HD
        qs = []
        for g in range(G):
            qs.extend(_sb_split(q_ref[g * QB:(g + 1) * QB, :], head_a))
        tri = tri_ref[...]
        last = G * qb + G - 1

        def trips(rows):
            qr = [q[:rows] for q in qs]
            row = lax.broadcasted_iota(jnp.int32, (rows, QB), 0)
            col = lax.broadcasted_iota(jnp.int32, (rows, QB), 1)

            def step(carry):
                i, _, cs, accs = carry
                causal = (col - row) < i * QB
                ks, vs, valids = [], [], []
                for g in range(G):
                    kb = G * qb + g - i
                    off = pl.multiple_of(jnp.maximum(kb, 0) * QB, QB)
                    ks.append(k_ref[pl.ds(off, QB), :])
                    vs.append(v_ref[pl.ds(off, QB), :])
                    valids.append(jnp.logical_and(causal, kb >= 0))
                zs = [_mm_nt(qr[ch], ks[ch // 2]) * SCALE for ch in range(nch)]
                terms = [_sb_terms(zs[ch], valids[ch // 2]) for ch in range(nch)]
                withins = [_mm2(terms[ch][1], tri) for ch in range(nch)]
                new_c, new_acc = [], []
                for ch in range(nch):
                    ls, lf, _ = terms[ch]
                    a = jnp.where(valids[ch // 2], jnp.exp(ls + withins[ch] + cs[ch]), 0.0)
                    new_acc.append(accs[ch] + _mm2(a, vs[ch // 2]))
                    new_c.append(cs[ch] + jnp.sum(lf, axis=-1, keepdims=True))
                return i + 1, _sb_alive(new_c, SB_TOP if rows == QB else 0), tuple(new_c), tuple(new_acc)

            return step

        zc = jnp.zeros((QB, 1), F32)
        za = jnp.zeros((QB, 128), F32)
        more = lambda cr: jnp.logical_and(cr[0] <= last, cr[1] > 0)
        i, _, cs, accs = lax.while_loop(more, trips(QB), (0, 1, (zc,) * nch, (za,) * nch))
        cs_top = tuple(c[:SB_TOP] for c in cs)
        acc_top = tuple(a[:SB_TOP] for a in accs)
        acc_top = lax.while_loop(more, trips(SB_TOP), (i, _sb_alive(cs_top), cs_top, acc_top))[3]
        for g in range(G):
            o_ref[g * QB:g * QB + SB_TOP, :] = jnp.where(head_a[:SB_TOP], acc_top[2 * g], acc_top[2 * g + 1])
            o_ref[g * QB + SB_TOP:(g + 1) * QB, :] = jnp.where(head_a, accs[2 * g], accs[2 * g + 1])[SB_TOP:]

    return pl.pallas_call(
        body, name="sb_fwd", grid=(npair, nstep),
        in_specs=[pl.BlockSpec((G * QB, 128), lambda p, i: (i, p)),
                  pl.BlockSpec((S, 128), lambda p, i: (0, npair + p)),
                  pl.BlockSpec((S, 128), lambda p, i: (0, 2 * npair + p)),
                  pl.BlockSpec((QB, QB), lambda p, i: (0, 0))],
        out_specs=pl.BlockSpec((G * QB, 128), lambda p, i: (i, p)),
        out_shape=jax.ShapeDtypeStruct((S, DTOK), F32),
        compiler_params=_cparams(("parallel", "parallel")),
    )(proj, proj, proj, _tri(True))


def _sb_bwd(proj, o, do):
    S = proj.shape[0]
    G = min(SB_GROUPS_BWD, S // QB)
    nstep = S // (G * QB)
    npair = DTOK // 128
    nch = 2 * G

    def body(q_ref, k_ref, v_ref, o_ref, do_ref, tri_ref, tri2_ref, dq_ref, dk_ref, dv_ref):
        qb = pl.program_id(1)

        @pl.when(qb == 0)
        def _():
            dk_ref[...] = jnp.zeros_like(dk_ref)
            dv_ref[...] = jnp.zeros_like(dv_ref)

        head_a = lax.broadcasted_iota(jnp.int32, (QB, 128), 1) < HD
        qs, dos, gs = [], [], []
        for g in range(G):
            rows = slice(g * QB, (g + 1) * QB)
            qs.extend(_sb_split(q_ref[rows, :], head_a))
            dob = do_ref[rows, :].astype(_MM)
            dos.extend(_sb_split(dob, head_a))
            go = dob.astype(F32) * o_ref[rows, :]
            gs.append(jnp.sum(jnp.where(head_a, go, 0.0), axis=-1, keepdims=True))
            gs.append(jnp.sum(jnp.where(head_a, 0.0, go), axis=-1, keepdims=True))
        tri = tri_ref[...]
        tri2 = tri2_ref[...]
        last = G * qb + G - 1

        def trips(rows):
            qr = [q[:rows] for q in qs]
            dor = [d[:rows] for d in dos]
            gr = [g_[:rows] for g_ in gs]
            row = lax.broadcasted_iota(jnp.int32, (rows, QB), 0)
            col = lax.broadcasted_iota(jnp.int32, (rows, QB), 1)

            def step(carry):
                i, _, cs, rs, dqs = carry
                causal = (col - row) < i * QB
                offs, ks, vs, valids = [], [], [], []
                for g in range(G):
                    kb = G * qb + g - i
                    off = pl.multiple_of(jnp.maximum(kb, 0) * QB, QB)
                    offs.append(off)
                    ks.append(k_ref[pl.ds(off, QB), :])
                    vs.append(v_ref[pl.ds(off, QB), :])
                    valids.append(jnp.logical_and(causal, kb >= 0))
                zs = [_mm_nt(qr[ch], ks[ch // 2]) * SCALE for ch in range(nch)]
                das = [_mm_nt(dor[ch], vs[ch // 2]) for ch in range(nch)]
                terms = [_sb_terms(zs[ch], valids[ch // 2]) for ch in range(nch)]
                withins = [_mm2(terms[ch][1], tri) for ch in range(nch)]
                a_s, dlws = [], []
                for ch in range(nch):
                    a = jnp.where(valids[ch // 2], jnp.exp(terms[ch][0] + withins[ch] + cs[ch]), 0.0)
                    a_s.append(a)
                    dlws.append(das[ch] * a)
                sfx = [_mm2(dlws[ch], tri2) for ch in range(nch)]
                dzs = []
                for ch in range(nch):
                    e = terms[ch][2]
                    inv = 1.0 / (1.0 + e)
                    pos = zs[ch] >= 0.0
                    beta = jnp.where(pos, 1.0, e) * inv
                    omb = jnp.where(pos, e, 1.0) * inv
                    prefix = gr[ch] - rs[ch] - sfx[ch]
                    dzs.append((jnp.where(valids[ch // 2], dlws[ch] * omb - beta * prefix, 0.0) * SCALE).astype(_MM))
                new_dq = [dqs[ch] + _mm(dzs[ch], ks[ch // 2]) for ch in range(nch)]
                for g in range(G):
                    a, b = 2 * g, 2 * g + 1
                    dk_ref[pl.ds(offs[g], QB), :] += _mm_tn(dzs[a], qr[a]) + _mm_tn(dzs[b], qr[b])
                    dv_ref[pl.ds(offs[g], QB), :] += _mm_tn(a_s[a], dor[a]) + _mm_tn(a_s[b], dor[b])
                new_c = [cs[ch] + jnp.sum(terms[ch][1], axis=-1, keepdims=True) for ch in range(nch)]
                new_r = [rs[ch] + jnp.sum(dlws[ch], axis=-1, keepdims=True) for ch in range(nch)]
                alive = _sb_alive(new_c, SB_TOP if rows == QB else 0)
                return i + 1, alive, tuple(new_c), tuple(new_r), tuple(new_dq)

            return step

        zc = jnp.zeros((QB, 1), F32)
        za = jnp.zeros((QB, 128), F32)
        more = lambda cr: jnp.logical_and(cr[0] <= last, cr[1] > 0)
        i, _, cs, rs, dqs = lax.while_loop(more, trips(QB), (0, 1, (zc,) * nch, (zc,) * nch, (za,) * nch))
        top = lambda xs: tuple(x[:SB_TOP] for x in xs)
        cs_top = top(cs)
        dq_top = lax.while_loop(more, trips(SB_TOP), (i, _sb_alive(cs_top), cs_top, top(rs), top(dqs)))[4]
        for g in range(G):
            dq_ref[g * QB:g * QB + SB_TOP, :] = jnp.where(head_a[:SB_TOP], dq_top[2 * g], dq_top[2 * g + 1])
            dq_ref[g * QB + SB_TOP:(g + 1) * QB, :] = jnp.where(head_a, dqs[2 * g], dqs[2 * g + 1])[SB_TOP:]

    qspec = pl.BlockSpec((G * QB, 128), lambda p, i: (i, p))
    full = lambda base: pl.BlockSpec((S, 128), lambda p, i: (0, base + p))
    tspec = pl.BlockSpec((QB, QB), lambda p, i: (0, 0))
    return pl.pallas_call(
        body, name="sb_bwd", grid=(npair, nstep),
        in_specs=[qspec, full(npair), full(2 * npair), qspec, qspec, tspec, tspec],
        out_specs=[qspec, full(0), full(0)],
        out_shape=[jax.ShapeDtypeStruct((S, DTOK), F32)] * 3,
        compiler_params=_cparams(("parallel", "arbitrary")),
    )(proj, proj, proj, o, do, _tri(True), _tri(False))


def _memkv_fwd(mem, gm, wkv):
    nk, kw, _ = wkv.shape

    def body(m_ref, g_ref, w_ref, kv_ref):
        m = m_ref[...]
        mn = (m * _rstd(m) * g_ref[...]).astype(_MM)
        acc = _mm(mn[:, :kw], w_ref[0])
        for j in range(1, nk):
            acc = acc + _mm(mn[:, j * kw:(j + 1) * kw], w_ref[j])
        kv_ref[...] = acc.astype(kv_ref.dtype)

    return pl.pallas_call(
        body, name="memkv_fwd", out_shape=jax.ShapeDtypeStruct((MEM_LEN, 2 * DMEM), _MM),
        compiler_params=pltpu.CompilerParams(vmem_limit_bytes=VMEM_LIMIT),
    )(mem, gm, wkv)


def _memkv_bwd(mem, gm, wkv, dkv):
    nk, kw, _ = wkv.shape

    def body(m_ref, g_ref, w_ref, dkv_ref, dw_ref, dg_ref):
        m = m_ref[...]
        mh = m * _rstd(m)
        mn = (mh * g_ref[...]).astype(_MM)
        dkv = dkv_ref[...].astype(_MM)
        dw_ref[...] = _mm_tn(mn, dkv)
        dmn = jnp.concatenate([_mm_nt(dkv, w_ref[j]) for j in range(nk)], axis=-1)
        dg_ref[...] = jnp.sum(dmn * mh, axis=0, keepdims=True)

    return pl.pallas_call(
        body, name="memkv_bwd",
        out_shape=[jax.ShapeDtypeStruct((D, 2 * DMEM), F32), jax.ShapeDtypeStruct((1, D), F32)],
        compiler_params=pltpu.CompilerParams(vmem_limit_bytes=VMEM_LIMIT),
    )(mem, gm, wkv, dkv)


def _mixout_fwd(h, tok, proj, kv, wo, g, qm_block):
    S = h.shape[0]
    ts = _row_tile(S, 256)

    def body(h_ref, tok_ref, qm_ref, kv_ref, wo_ref, g_ref, ho_ref, cat_ref, mix_ref):
        head_a = lax.broadcasted_iota(jnp.int32, (ts, 128), 1) < HD
        qm = qm_ref[...].astype(_MM)
        zq = jnp.zeros((ts, 128), _MM)
        npair = DMEM // 128
        qhs = []
        for pi in range(npair):
            qp = qm[:, pi * 128:(pi + 1) * 128]
            qhs += [jnp.where(head_a, qp, zq), jnp.where(head_a, zq, qp)]
        scores = [_mm_nt(qhs[h], kv_ref[:, (h // 2) * 128:(h // 2 + 1) * 128]) * SCALE for h in range(2 * npair)]
        ps = []
        for s in scores:
            e = jnp.exp(s - jnp.max(s, axis=-1, keepdims=True))
            ps.append(e / jnp.sum(e, axis=-1, keepdims=True))
        pvs = [_mm(ps[h], kv_ref[:, DMEM + (h // 2) * 128:DMEM + (h // 2 + 1) * 128]) for h in range(2 * npair)]
        mos = [jnp.where(head_a, pvs[2 * pi], pvs[2 * pi + 1]) for pi in range(npair)]
        cat = jnp.concatenate([tok_ref[...].astype(_MM)] + [m.astype(_MM) for m in mos], axis=-1)
        cat_ref[...] = cat
        mix = jnp.concatenate([_mm(cat, wo_ref[j]) for j in range(N_CHIPS)], axis=-1)
        mix_ref[...] = mix
        ho_ref[...] = h_ref[...] + mix * _rstd(mix) * g_ref[...]

    row = lambda i: (i, 0)
    return pl.pallas_call(
        body, name="mixout_fwd", grid=(S // ts,),
        in_specs=[pl.BlockSpec((ts, D), row), pl.BlockSpec((ts, DTOK), row),
                  pl.BlockSpec((ts, DMEM), lambda i: (i, qm_block)),
                  pl.BlockSpec((MEM_LEN, 2 * DMEM), lambda i: (0, 0)),
                  pl.BlockSpec((N_CHIPS, DMIX, D // N_CHIPS), lambda i: (0, 0, 0)), pl.BlockSpec((1, D), lambda i: (0, 0))],
        out_specs=[pl.BlockSpec((ts, D), row), pl.BlockSpec((ts, DMIX), row), pl.BlockSpec((ts, D), row)],
        out_shape=[jax.ShapeDtypeStruct((S, D), F32), jax.ShapeDtypeStruct((S, DMIX), _MM),
                   jax.ShapeDtypeStruct((S, D), F32)],
        compiler_params=_cparams(("parallel",)),
    )(h, tok, proj, kv, wo, g)


def _mixout_bwd(dho, mix, proj, kv, wo, g, qm_block):
    S = dho.shape[0]
    ts = _row_tile(S, 256)

    def body(dho_ref, mix_ref, qm_ref, kv_ref, wo_ref, g_ref, dmix_ref, dtok_ref, dqm_ref, dkv_ref, dg_ref):
        @pl.when(pl.program_id(0) == 0)
        def _():
            dkv_ref[...] = jnp.zeros_like(dkv_ref)
            dg_ref[...] = jnp.zeros_like(dg_ref)

        dmix, dg = _rms_bwd(mix_ref[...], g_ref[...], dho_ref[...])
        dg_ref[...] += dg
        dmb = dmix.astype(_MM)
        dmix_ref[...] = dmb
        cw = D // N_CHIPS
        dcat = _mm_nt(dmb[:, :cw], wo_ref[0])
        for j in range(1, N_CHIPS):
            dcat = dcat + _mm_nt(dmb[:, j * cw:(j + 1) * cw], wo_ref[j])
        dtok_ref[...] = dcat[:, :DTOK]
        head_a = lax.broadcasted_iota(jnp.int32, (ts, 128), 1) < HD
        qm = qm_ref[...].astype(_MM)
        zq = jnp.zeros((ts, 128), _MM)
        npair = DMEM // 128
        kps = [kv_ref[:, pi * 128:(pi + 1) * 128] for pi in range(npair)]
        vps = [kv_ref[:, DMEM + pi * 128:DMEM + (pi + 1) * 128] for pi in range(npair)]
        qhs, dmhs = [], []
        for pi in range(npair):
            cs = slice(pi * 128, (pi + 1) * 128)
            dmo = dcat[:, DTOK + pi * 128:DTOK + (pi + 1) * 128].astype(_MM)
            qhs += [jnp.where(head_a, qm[:, cs], zq), jnp.where(head_a, zq, qm[:, cs])]
            dmhs += [jnp.where(head_a, dmo, zq), jnp.where(head_a, zq, dmo)]
        nh = 2 * npair
        scores = [_mm_nt(qhs[h], kps[h // 2]) * SCALE for h in range(nh)]
        dps = [_mm_nt(dmhs[h], vps[h // 2]) for h in range(nh)]
        ps, dss = [], []
        for h in range(nh):
            e = jnp.exp(scores[h] - jnp.max(scores[h], axis=-1, keepdims=True))
            p = e / jnp.sum(e, axis=-1, keepdims=True)
            ps.append(p)
            dss.append((p * (dps[h] - jnp.sum(dps[h] * p, axis=-1, keepdims=True)) * SCALE).astype(_MM))
        dqs = [_mm(dss[h], kps[h // 2]) for h in range(nh)]
        for pi in range(npair):
            cs = slice(pi * 128, (pi + 1) * 128)
            a, b = 2 * pi, 2 * pi + 1
            dqm_ref[:, cs] = jnp.where(head_a, dqs[a], dqs[b]).astype(_MM)
            dkv_ref[:, cs] += _mm_tn(dss[a], qhs[a]) + _mm_tn(dss[b], qhs[b])
            dkv_ref[:, DMEM + pi * 128:DMEM + (pi + 1) * 128] += _mm_tn(ps[a], dmhs[a]) + _mm_tn(ps[b], dmhs[b])

    row = lambda i: (i, 0)
    fix = lambda i: (0, 0)
    return pl.pallas_call(
        body, name="mixout_bwd", grid=(S // ts,),
        in_specs=[pl.BlockSpec((ts, D), row), pl.BlockSpec((ts, D), row),
                  pl.BlockSpec((ts, DMEM), lambda i: (i, qm_block)),
                  pl.BlockSpec((MEM_LEN, 2 * DMEM), fix),
                  pl.BlockSpec((N_CHIPS, DMIX, D // N_CHIPS), lambda i: (0, 0, 0)), pl.BlockSpec((1, D), fix)],
        out_specs=[pl.BlockSpec((ts, D), row), pl.BlockSpec((ts, DTOK), row), pl.BlockSpec((ts, DMEM), row),
                   pl.BlockSpec((MEM_LEN, 2 * DMEM), fix), pl.BlockSpec((1, D), fix)],
        out_shape=[jax.ShapeDtypeStruct((S, D), _MM), jax.ShapeDtypeStruct((S, DTOK), F32),
                   jax.ShapeDtypeStruct((S, DMEM), _MM), jax.ShapeDtypeStruct((MEM_LEN, 2 * DMEM), F32),
                   jax.ShapeDtypeStruct((1, D), F32)],
        compiler_params=_cparams(("arbitrary",)),
    )(dho, mix, proj, kv, wo, g)


def _loss_grad(y, target):
    S = y.shape[0]
    ts = _row_tile(S, 512)

    def body(y_ref, t_ref, acc_ref, dy_ref):
        @pl.when(pl.program_id(0) == 0)
        def _():
            acc_ref[...] = jnp.zeros_like(acc_ref)

        err = y_ref[...] - t_ref[...]
        acc_ref[...] += jnp.sum(err * err, axis=0, keepdims=True)
        dy_ref[...] = err * (1.0 / D)

    row = lambda i: (i, 0)
    return pl.pallas_call(
        body, name="loss_grad", grid=(S // ts,),
        in_specs=[pl.BlockSpec((ts, D), row), pl.BlockSpec((ts, D), row)],
        out_specs=[pl.BlockSpec((1, D), lambda i: (0, 0)), pl.BlockSpec((ts, D), row)],
        out_shape=[jax.ShapeDtypeStruct((1, D), F32), jax.ShapeDtypeStruct((S, D), F32)],
        compiler_params=_cparams(("arbitrary",)),
    )(y, target)


LAYER_TENSORS = ("ffn1_gate", "ffn1_up", "ffn1_down", "ffn2_gate", "ffn2_up", "ffn2_down", "w_in", "w_mem_kv", "w_out")
SPLITS = {"ffn1_gate": "c", "ffn1_up": "c", "ffn1_down": "r", "ffn2_gate": "c", "ffn2_up": "c", "ffn2_down": "r",
          "w_in_pool": "r", "w_in_sb": "c", "w_mem_kv": "r", "w_out": "c"}


def _is_pool(i):
    return i % 2 == 0


def _qm_block(i):
    return (DTOK if _is_pool(i) else 3 * DTOK) // DMEM


FIRST_FFN = LAYER_TENSORS[:3]
REST = LAYER_TENSORS[3:]


def _layer_forward(h, mem, lw, i, rest_shards, next_shards):
    g_at = lambda g, k: g[k].reshape(1, D)
    sv = {"h0": h}
    (h, sv["a1"], sv["b1"], sv["f1"]), rest = _ffn_fwd(
        h, g_at(lw["g_pre"], 0), g_at(lw["g_post"], 0), lw["ffn1_gate"], lw["ffn1_up"], lw["ffn1_down"],
        _GatherExchange(rest_shards))
    lw.update(zip(REST, rest))
    if not _is_pool(i):
        lw["w_in"] = jnp.transpose(lw["w_in"], (1, 0, 2)).reshape(1, D, DSB)
    sv["h1"] = h
    sv["kv"] = _memkv_fwd(mem, lw["g_mem"], lw["w_mem_kv"])
    if _is_pool(i):
        sv["u"], sv["proj"] = _proj_fwd(h, g_at(lw["g_pre"], 1), lw["w_in"], F32)
        tok = _pool_fwd(sv["proj"], lw["pool_w"], lw["pool_scale"])
    else:
        sv["u"], sv["proj"] = _proj_fwd(h, g_at(lw["g_pre"], 1), lw["w_in"], _MM)
        tok = sv["o"] = _sb_fwd(sv["proj"])
    h, sv["cat"], sv["mix"] = _mixout_fwd(h, tok, sv["proj"], sv["kv"], lw["w_out"], g_at(lw["g_post"], 1),
                                          _qm_block(i))
    sv["h2"] = h
    ffn2 = (h, g_at(lw["g_pre"], 2), g_at(lw["g_post"], 2), lw["ffn2_gate"], lw["ffn2_up"], lw["ffn2_down"])
    if next_shards is None:
        h, sv["a2"], sv["b2"], sv["f2"] = _ffn_fwd(*ffn2)
        return h, sv, None
    (h, sv["a2"], sv["b2"], sv["f2"]), nxt = _ffn_fwd(*ffn2, _GatherExchange(next_shards))
    return h, sv, nxt


def _layer_backward(dh, mem, lw, sv, i, rides=None):
    g_at = lambda g, k: g[k].reshape(1, D)
    big, small = {}, {}
    dgpre, dgpost = [None] * 3, [None] * 3
    rode = {}
    rides = rides or {}

    def hosted(host, fn, *args):
        ride = rides[host](big, rode) if host in rides else None
        if ride is None:
            return fn(*args)
        outs, rode[host] = fn(*args, ride)
        return outs

    def ffn_back(which, dh, h_in):
        k = 0 if which == 1 else 2
        n = f"ffn{which}"
        dh, nb, sb, da, db, df, dgpre[k], dgpost[k] = hosted(
            n, _ffn_bwd, h_in, sv[f"f{which}"], sv[f"a{which}"], sv[f"b{which}"], dh, g_at(lw["g_pre"], k),
            g_at(lw["g_post"], k), lw[n + "_gate"], lw[n + "_up"], lw[n + "_down"])
        big[n + "_gate"] = hosted(n + "_gate", _wgrad, nb, da, N_CHIPS)
        big[n + "_up"] = hosted(n + "_up", _wgrad, nb, db, N_CHIPS)
        big[n + "_down"] = hosted(n + "_down", _wgrad, sb, df, 0).reshape(N_CHIPS, FF // N_CHIPS, D)
        return dh

    dh = ffn_back(2, dh, sv["h2"])
    dmix, dtok, dqm, dkv, dgpost[1] = _mixout_bwd(dh, sv["mix"], sv["proj"], sv["kv"], lw["w_out"],
                                                  g_at(lw["g_post"], 1), _qm_block(i))
    big["w_out"] = _wgrad(sv["cat"], dmix, N_CHIPS)
    dwkv, small["g_mem"] = _memkv_bwd(mem, lw["g_mem"], lw["w_mem_kv"], dkv)
    big["w_mem_kv"] = dwkv.reshape(N_CHIPS, D // N_CHIPS, 2 * DMEM)
    if _is_pool(i):
        dx, small["pool_w"], small["pool_scale"] = _pool_bwd(sv["proj"], dtok, lw["pool_w"], lw["pool_scale"])
        dh, dproj, dgpre[1] = _proj_bwd(sv["h1"], dh, g_at(lw["g_pre"], 1), lw["w_in"], [dx, dqm])
        big["w_in"] = _wgrad(sv["u"], dproj).reshape(N_CHIPS, D // N_CHIPS, DMIX)
    else:
        dq, dk, dv = _sb_bwd(sv["proj"], sv["o"], dtok)
        dh, dproj, dgpre[1] = _proj_bwd(sv["h1"], dh, g_at(lw["g_pre"], 1), lw["w_in"], [dq, dk, dv, dqm])
        big["w_in"] = jnp.transpose(_wgrad(sv["u"], dproj).reshape(D, N_CHIPS, DSB // N_CHIPS), (1, 0, 2))
    dh = ffn_back(1, dh, sv["h0"])
    small["g_pre"] = jnp.concatenate(dgpre, axis=0)
    small["g_post"] = jnp.concatenate(dgpost, axis=0)
    return dh, big, small, rode


ANY = pl.BlockSpec(memory_space=pl.ANY)


def _place():
    return lax.axis_index("x"), lax.axis_index("y"), lax.axis_index("c")


def _flip(v, bit):
    return 1 - v if bit else v


def _dma_sems(m):
    return pltpu.SemaphoreType.DMA((m,))


class _SmallExchange:
    name = "small_allgather"

    def __init__(self, buf):
        self.inputs = [buf]
        self.out_shapes = [jax.ShapeDtypeStruct((N_DEV,) + buf.shape, buf.dtype)]
        self.sems = [_dma_sems(N_DEV - 1), _dma_sems(N_DEV - 1)]

    def _copies(self, ins, outs, sems):
        x, y, c = _place()
        me = 4 * x + 2 * y + c
        out, arrive = [], []
        for k in range(1, N_DEV):
            peer = (_flip(x, k & 4), _flip(y, k & 2), _flip(c, k & 1))
            slot = 4 * peer[0] + 2 * peer[1] + peer[2]
            how = dict(send_sem=sems[0].at[k - 1], recv_sem=sems[1].at[k - 1], device_id=peer, device_id_type=MESH)
            out.append(pltpu.make_async_remote_copy(src_ref=ins[0], dst_ref=outs[0].at[me], **how))
            arrive.append(pltpu.make_async_remote_copy(src_ref=ins[0], dst_ref=outs[0].at[slot], **how))
        return out, arrive

    def start(self, ins, outs, sems):
        for cp in self._copies(ins, outs, sems)[0]:
            cp.start()

    def mid(self, ins, outs, sems):
        pass

    def finish(self, ins, outs, sems):
        out, arrive = self._copies(ins, outs, sems)
        for cp in arrive:
            cp.wait_recv()
        for cp in out:
            cp.wait_send()


def _small_allgather(buf, me):
    gathered = _run_exchange(_SmallExchange(buf))[0]
    slots = lax.broadcasted_iota(jnp.int32, (N_DEV, 1, 1), 0)
    return jnp.where(slots == me, buf[None], gathered)


class _GatherExchange:
    name = "gather_weights"
    aliased = True

    def __init__(self, bufs):
        n = len(bufs)
        self.n, self.inputs = n, list(bufs)
        self.halves = [b.shape[1] // 2 for b in bufs]
        self.out_shapes = [jax.ShapeDtypeStruct(b.shape, b.dtype) for b in bufs]
        self.sems = [_dma_sems(3 * n), _dma_sems(3 * n), _dma_sems(3 * n), _dma_sems(3 * n)]

    def _copies(self, outs, sems):
        ici_send, ici_recv, d2d_send, d2d_recv = sems
        x, y, c = _place()
        me = 2 * x + y
        sib = (x, y, 1 - c)
        chips = [(_flip(x, k & 2), _flip(y, k & 1)) for k in (1, 2, 3)]
        half = lambda t, slot, hc: outs[t].at[slot, pl.ds(hc * self.halves[t], self.halves[t])]
        ici_out, ici_in, fwd_out, fwd_in = [], [], [], []
        for t in range(self.n):
            for k, chip in enumerate(chips):
                idx = 2 * chip[0] + chip[1]
                peer = (chip[0], chip[1], c)
                ici = dict(send_sem=ici_send.at[3 * t + k], recv_sem=ici_recv.at[3 * t + k], device_id=peer,
                           device_id_type=MESH)
                d2d = dict(send_sem=d2d_send.at[3 * t + k], recv_sem=d2d_recv.at[3 * t + k], device_id=sib,
                           device_id_type=MESH)
                ici_out.append(pltpu.make_async_remote_copy(src_ref=half(t, me, c), dst_ref=half(t, me, c), **ici))
                ici_in.append(pltpu.make_async_remote_copy(src_ref=half(t, idx, c), dst_ref=half(t, idx, c), **ici))
                fwd_out.append(pltpu.make_async_remote_copy(src_ref=half(t, idx, c), dst_ref=half(t, idx, c), **d2d))
                fwd_in.append(pltpu.make_async_remote_copy(
                    src_ref=half(t, idx, 1 - c), dst_ref=half(t, idx, 1 - c), **d2d))
        return ici_out, ici_in, fwd_out, fwd_in

    def start(self, ins, outs, sems):
        for cp in self._copies(outs, sems)[0]:
            cp.start()

    def mid(self, ins, outs, sems):
        _, ici_in, fwd_out, _ = self._copies(outs, sems)
        for arrived, onward in zip(ici_in, fwd_out):
            arrived.wait_recv()
            onward.start()

    def finish(self, ins, outs, sems):
        ici_out, _, fwd_out, fwd_in = self._copies(outs, sems)
        for cp in fwd_in:
            cp.wait_recv()
        for cp in ici_out + fwd_out:
            cp.wait_send()


class _SiblingExchange:
    name = "rs_to_sibling"

    def __init__(self, gs):
        n = len(gs)
        self.n, self.inputs = n, list(gs)
        self.halves = [g.shape[1] // 2 for g in gs]
        self.out_shapes = [jax.ShapeDtypeStruct((N_CHIPS, g.shape[1] // 2) + g.shape[2:], g.dtype) for g in gs]
        self.sems = [_dma_sems(n), _dma_sems(n)]

    def _copies(self, ins, outs, sems):
        x, y, c = _place()
        return [pltpu.make_async_remote_copy(
            src_ref=ins[t].at[:, pl.ds((1 - c) * self.halves[t], self.halves[t])], dst_ref=outs[t],
            send_sem=sems[0].at[t], recv_sem=sems[1].at[t], device_id=(x, y, 1 - c), device_id_type=MESH)
            for t in range(self.n)]

    def start(self, ins, outs, sems):
        for cp in self._copies(ins, outs, sems):
            cp.start()

    def mid(self, ins, outs, sems):
        pass

    def finish(self, ins, outs, sems):
        for cp in self._copies(ins, outs, sems):
            cp.wait()


class _ChipsExchange:
    name = "rs_to_chips"

    def __init__(self, ss):
        n = len(ss)
        self.n, self.inputs = n, list(ss)
        self.out_shapes = [jax.ShapeDtypeStruct((3,) + s.shape[1:], s.dtype) for s in ss]
        self.sems = [_dma_sems(3 * n), _dma_sems(3 * n)]

    def _copies(self, ins, outs, sems):
        x, y, c = _place()
        chips = [(_flip(x, k & 2), _flip(y, k & 1)) for k in (1, 2, 3)]
        return [pltpu.make_async_remote_copy(
            src_ref=ins[t].at[2 * chip[0] + chip[1]], dst_ref=outs[t].at[k], send_sem=sems[0].at[3 * t + k],
            recv_sem=sems[1].at[3 * t + k], device_id=(chip[0], chip[1], c), device_id_type=MESH)
            for t in range(self.n) for k, chip in enumerate(chips)]

    def start(self, ins, outs, sems):
        for cp in self._copies(ins, outs, sems):
            cp.start()

    def mid(self, ins, outs, sems):
        pass

    def finish(self, ins, outs, sems):
        for cp in self._copies(ins, outs, sems):
            cp.wait()


class _Both:
    def __init__(self, first, second):
        self.parts = (first, second)
        self.name = first.name + "_" + second.name
        self.inputs = first.inputs + second.inputs
        self.out_shapes = first.out_shapes + second.out_shapes
        self.sems = first.sems + second.sems

    def _each(self, ins, outs, sems):
        a = self.parts[0]
        ni, no, ns = len(a.inputs), len(a.out_shapes), len(a.sems)
        return ((a, ins[:ni], outs[:no], sems[:ns]), (self.parts[1], ins[ni:], outs[no:], sems[ns:]))

    def start(self, ins, outs, sems):
        for ex, i, o, s in self._each(ins, outs, sems):
            ex.start(i, o, s)

    def mid(self, ins, outs, sems):
        for ex, i, o, s in self._each(ins, outs, sems):
            ex.mid(i, o, s)

    def finish(self, ins, outs, sems):
        for ex, i, o, s in self._each(ins, outs, sems):
            ex.finish(i, o, s)

    def split(self, outs):
        no = len(self.parts[0].out_shapes)
        return outs[:no], outs[no:]


def _run_exchange(ex):
    ni, no = len(ex.inputs), len(ex.out_shapes)

    def body(*refs):
        ins, outs, sems = refs[:ni], refs[ni:ni + no], refs[ni + no:]
        ex.start(ins, outs, sems)
        ex.mid(ins, outs, sems)
        ex.finish(ins, outs, sems)

    return pl.pallas_call(
        body, name=ex.name, in_specs=[ANY] * ni, out_specs=[ANY] * no, out_shape=ex.out_shapes,
        scratch_shapes=ex.sems, input_output_aliases={t: t for t in range(ni)} if _in_place(ex) else {},
    )(*ex.inputs)


def _in_place(ex):
    return getattr(ex, "aliased", False)


def _place_own(shard):
    L_, r, n_ = shard.shape
    tr = _row_tile(r, 512 if r % 512 == 0 else 256)

    def body(x_ref, *refs):
        outs, stage, sems = refs[:L_], refs[L_], refs[L_ + 1]
        x, y, _ = _place()
        rows = pl.ds(pl.multiple_of(pl.program_id(0) * tr, tr), tr)
        copies = []
        for l in range(L_):
            stage[l] = x_ref[l].astype(_MM)
            cp = pltpu.make_async_copy(stage.at[l], outs[l].at[2 * x + y, rows], sems.at[l])
            cp.start()
            copies.append(cp)
        for cp in copies:
            cp.wait()

    return pl.pallas_call(
        body, name="place_own", grid=(r // tr,),
        in_specs=[pl.BlockSpec((L_, tr, n_), lambda i: (0, i, 0))], out_specs=[ANY] * L_,
        out_shape=[jax.ShapeDtypeStruct((N_CHIPS, r, n_), _MM)] * L_,
        scratch_shapes=[pltpu.VMEM((L_, tr, n_), _MM), _dma_sems(L_)],
        compiler_params=_cparams(("arbitrary",)),
    )(shard)


def _call(body, name, nsteps, in_specs, out_specs, out_shape, args, ride=None):
    if ride is None:
        return pl.pallas_call(body, name=name, grid=(nsteps,), in_specs=in_specs, out_specs=out_specs,
                              out_shape=out_shape, compiler_params=_cparams(("arbitrary",)))(*args)
    ni, no = len(args), len(out_shape)
    ri, ro = len(ride.inputs), len(ride.out_shapes)
    mid_step = (13 * nsteps) // 16 if nsteps > 1 else 0

    def hosted(*refs):
        a, ra = refs[:ni], refs[ni:ni + ri]
        o, ro_refs = refs[ni + ri:ni + ri + no], refs[ni + ri + no:ni + ri + no + ro]
        sems = refs[ni + ri + no + ro:]
        step = pl.program_id(0)

        @pl.when(step == 0)
        def _():
            ride.start(ra, ro_refs, sems)

        body(*a, *o)

        @pl.when(step == mid_step)
        def _():
            ride.mid(ra, ro_refs, sems)

        @pl.when(step == nsteps - 1)
        def _():
            ride.finish(ra, ro_refs, sems)

    outs = pl.pallas_call(
        hosted, name=name + "_" + ride.name, grid=(nsteps,), in_specs=list(in_specs) + [ANY] * ri,
        out_specs=list(out_specs) + [ANY] * ro, out_shape=list(out_shape) + ride.out_shapes,
        scratch_shapes=ride.sems, compiler_params=_cparams(("arbitrary",)),
        input_output_aliases={ni + t: no + t for t in range(ri)} if _in_place(ride) else {},
    )(*args, *ride.inputs)
    return outs[:no], outs[no:]


def _share_halves(ts):
    n = len(ts)

    def body(*refs):
        outs = refs[n:2 * n]
        send_sems, recv_sems = refs[2 * n:]
        x, y, c = _place()
        cps = []
        for t in range(n):
            cp = pltpu.make_async_remote_copy(
                src_ref=outs[t].at[c], dst_ref=outs[t].at[c], send_sem=send_sems.at[t], recv_sem=recv_sems.at[t],
                device_id=(x, y, 1 - c), device_id_type=MESH)
            cp.start()
            cps.append(cp)
        for t in range(n):
            pltpu.make_async_remote_copy(
                src_ref=outs[t].at[1 - c], dst_ref=outs[t].at[1 - c], send_sem=send_sems.at[t],
                recv_sem=recv_sems.at[t], device_id=(x, y, 1 - c), device_id_type=MESH).wait_recv()
        for cp in cps:
            cp.wait_send()

    dma = lambda m: pltpu.SemaphoreType.DMA((m,))
    return pl.pallas_call(
        body, name="share_halves", in_specs=[ANY] * n, out_specs=[ANY] * n,
        out_shape=[jax.ShapeDtypeStruct(t.shape, t.dtype) for t in ts],
        input_output_aliases={t: t for t in range(n)},
        scratch_shapes=[dma(n), dma(n)],
    )(*ts)


def _by_shape(fn, *lists):
    groups = {}
    for idx, a in enumerate(lists[0]):
        groups.setdefault(a.shape, []).append(idx)
    out = [None] * len(lists[0])
    for idxs in groups.values():
        for idx, res in zip(idxs, fn(*[[lst[idx] for idx in idxs] for lst in lists])):
            out[idx] = res
    return out


def _add_sibling(gs, rs, c):
    m = len(gs)
    _, r_, n_ = gs[0].shape
    h = r_ // 2
    tr = _row_tile(h, 512)

    def body(c_ref, *refs):
        for g_ref, r_ref, o_ref in zip(refs[:m], refs[m:2 * m], refs[2 * m:]):
            o_ref[...] = (g_ref[...] + r_ref[...]).astype(o_ref.dtype)

    gspec = pl.BlockSpec((None, None, tr, n_), lambda s, i, c_ref: (s, c_ref[0], i, 0))
    hspec = pl.BlockSpec((None, tr, n_), lambda s, i, c_ref: (s, i, 0))
    return pl.pallas_call(
        body, name="add_sibling",
        grid_spec=pltpu.PrefetchScalarGridSpec(
            num_scalar_prefetch=1, grid=(N_CHIPS, h // tr), in_specs=[gspec] * m + [hspec] * m, out_specs=[hspec] * m),
        out_shape=[jax.ShapeDtypeStruct((N_CHIPS, h, n_), _MM)] * m,
        compiler_params=_cparams(("parallel", "parallel")),
    )(c.reshape(1), *[g.reshape(N_CHIPS, 2, h, n_) for g in gs], *rs)


def _add_chips(ss, rs, me, c):
    m = len(ss)
    _, h, n_ = ss[0].shape
    tr = _row_tile(h, 512)

    def body(at_ref, *refs):
        f = lambda ref: ref[...].astype(F32)
        for t in range(m):
            s_ref, (r0, r1, r2), o_ref = refs[t], refs[m + 3 * t:m + 3 * t + 3], refs[4 * m + t]
            o_ref[...] = ((f(s_ref) + f(r0)) + f(r1)) + f(r2)

    rk = lambda k: pl.BlockSpec((None, tr, n_), lambda i, at_ref: (k, i, 0))
    return pl.pallas_call(
        body, name="add_chips",
        grid_spec=pltpu.PrefetchScalarGridSpec(
            num_scalar_prefetch=1, grid=(h // tr,),
            in_specs=[pl.BlockSpec((None, tr, n_), lambda i, at_ref: (at_ref[0], i, 0))] * m + [rk(0), rk(1), rk(2)] * m,
            out_specs=[pl.BlockSpec((None, tr, n_), lambda i, at_ref: (at_ref[1], i, 0))] * m),
        out_shape=[jax.ShapeDtypeStruct((2, h, n_), F32)] * m,
        compiler_params=_cparams(("parallel",)),
    )(jnp.stack([me, c]), *ss, *[r for r in rs for _ in range(3)])


def _sum_devices(a):
    R_ = a.shape[1]

    def body(a_ref, o_ref):
        acc = a_ref[0]
        for k in range(1, N_DEV):
            acc = acc + a_ref[k]
        o_ref[...] = acc

    return pl.pallas_call(
        body, name="sum_devices", out_shape=jax.ShapeDtypeStruct((R_, 128), F32),
        compiler_params=pltpu.CompilerParams(vmem_limit_bytes=VMEM_LIMIT),
    )(a)


def _adamw(w, g, m, v):
    shape = w.shape
    to2 = lambda a: a.reshape(-1, shape[-1])
    w2, g2, m2, v2 = to2(w), to2(g), to2(m), to2(v)
    R_, C = w2.shape
    tr = 256 if R_ % 256 == 0 else R_

    def body(w_ref, g_ref, m_ref, v_ref, d_ref, mo_ref, vo_ref):
        gv = g_ref[...]
        mn = ADAM_B1 * m_ref[...] + (1.0 - ADAM_B1) * gv
        vn = ADAM_B2 * v_ref[...] + (1.0 - ADAM_B2) * (gv * gv)
        m_hat = mn / (1.0 - ADAM_B1 ** ADAM_STEP)
        v_hat = vn / (1.0 - ADAM_B2 ** ADAM_STEP)
        d_ref[...] = -ADAM_LR * (m_hat / (jnp.sqrt(v_hat) + ADAM_EPS) + ADAM_WD * w_ref[...])
        mo_ref[...] = mn
        vo_ref[...] = vn

    spec = pl.BlockSpec((tr, C), lambda i: (i, 0))
    outs = pl.pallas_call(
        body, name="adamw", grid=(R_ // tr,), in_specs=[spec] * 4, out_specs=[spec] * 3,
        out_shape=[jax.ShapeDtypeStruct((R_, C), F32)] * 3,
        compiler_params=_cparams(("parallel",)),
    )(w2, g2, m2, v2)
    return [o.reshape(shape) for o in outs]


SHARD_AXIS = {"g_pre": 2, "g_post": 2, "g_mem": None, "ffn1_gate": 2, "ffn1_up": 2, "ffn1_down": 1,
              "ffn2_gate": 2, "ffn2_up": 2, "ffn2_down": 1, "w_in_pool": 1, "pool_w": None, "pool_scale": None,
              "w_in_sb": 2, "w_mem_kv": 1, "w_out": 2}
WEIGHTS = list(SHARD_AXIS)
BIG = [k for k in WEIGHTS if SHARD_AXIS[k] is not None and k not in ("g_pre", "g_post")]
SMALL = [k for k in WEIGHTS if k not in BIG]


def _pack128(arrs):
    flat = jnp.concatenate([a.reshape(-1) for a in arrs])
    pad = (-flat.shape[0]) % (8 * 128)
    return jnp.pad(flat, (0, pad)).reshape(-1, 128)


def _unpack128(buf, shapes):
    flat = buf.reshape(-1)
    out, off = [], 0
    for shp in shapes:
        size = 1
        for d in shp:
            size *= d
        out.append(flat[off:off + size].reshape(shp))
        off += size
    return out


def kernel(x, mem, g_pre, g_post, g_mem, ffn1_gate, ffn1_up, ffn1_down, ffn2_gate, ffn2_up, ffn2_down, w_in_pool, pool_w, pool_scale, w_in_sb, w_mem_kv, w_out, loss_target, m_g_pre, m_g_post, m_g_mem, m_ffn1_gate, m_ffn1_up, m_ffn1_down, m_ffn2_gate, m_ffn2_up, m_ffn2_down, m_w_in_pool, m_pool_w, m_pool_scale, m_w_in_sb, m_w_mem_kv, m_w_out, v_g_pre, v_g_post, v_g_mem, v_ffn1_gate, v_ffn1_up, v_ffn1_down, v_ffn2_gate, v_ffn2_up, v_ffn2_down, v_w_in_pool, v_pool_w, v_pool_scale, v_w_in_sb, v_w_mem_kv, v_w_out):
    w = dict(g_pre=g_pre, g_post=g_post, g_mem=g_mem, ffn1_gate=ffn1_gate, ffn1_up=ffn1_up, ffn1_down=ffn1_down,
             ffn2_gate=ffn2_gate, ffn2_up=ffn2_up, ffn2_down=ffn2_down, w_in_pool=w_in_pool, pool_w=pool_w,
             pool_scale=pool_scale, w_in_sb=w_in_sb, w_mem_kv=w_mem_kv, w_out=w_out)
    m = dict(g_pre=m_g_pre, g_post=m_g_post, g_mem=m_g_mem, ffn1_gate=m_ffn1_gate, ffn1_up=m_ffn1_up,
             ffn1_down=m_ffn1_down, ffn2_gate=m_ffn2_gate, ffn2_up=m_ffn2_up, ffn2_down=m_ffn2_down,
             w_in_pool=m_w_in_pool, pool_w=m_pool_w, pool_scale=m_pool_scale, w_in_sb=m_w_in_sb,
             w_mem_kv=m_w_mem_kv, w_out=m_w_out)
    v = dict(g_pre=v_g_pre, g_post=v_g_post, g_mem=v_g_mem, ffn1_gate=v_ffn1_gate, ffn1_up=v_ffn1_up,
             ffn1_down=v_ffn1_down, ffn2_gate=v_ffn2_gate, ffn2_up=v_ffn2_up, ffn2_down=v_ffn2_down,
             w_in_pool=v_w_in_pool, pool_w=v_pool_w, pool_scale=v_pool_scale, w_in_sb=v_w_in_sb,
             w_mem_kv=v_w_mem_kv, w_out=v_w_out)
    cx, cy, cc = _place()
    chip = (2 * cx + cy).astype(jnp.int32)
    core = cc.astype(jnp.int32)

    dev = (4 * cx + 2 * cy + cc).astype(jnp.int32)

    gains = _small_allgather(_pack128([g_pre, g_post]), dev)
    per_chip = [_unpack128(gains[2 * j], [g_pre.shape, g_post.shape]) for j in range(N_CHIPS)]
    full_pre = jnp.concatenate([p[0] for p in per_chip], axis=2)
    full_post = jnp.concatenate([p[1] for p in per_chip], axis=2)

    def shard_name(k, i):
        return ("w_in_pool" if _is_pool(i) else "w_in_sb") if k == "w_in" else k

    placed = {k: _place_own(w[k]) for k in BIG}

    def own(i, names):
        return [placed[shard_name(k, i)][i // 2 if k == "w_in" else i] for k in names]

    xs, mems = x[0], mem[0]
    h, lws, saved = xs, [], []
    first = _run_exchange(_GatherExchange(own(0, FIRST_FFN)))
    for i in range(DEPTH):
        lw = dict(zip(FIRST_FFN, first))
        lw["g_pre"], lw["g_post"], lw["g_mem"] = full_pre[i], full_post[i], g_mem[i].reshape(1, D)
        if _is_pool(i):
            lw["pool_w"], lw["pool_scale"] = pool_w[i // 2], pool_scale[i // 2].reshape(1, DTOK)
        h, sv, first = _layer_forward(h, mems, lw, i, own(i, REST), own(i + 1, FIRST_FFN) if i + 1 < DEPTH else None)
        lws.append(lw)
        saved.append(sv)
    acc, dh = _loss_grad(h, loss_target[0])
    loss = lax.psum(0.5 / D * jnp.sum(acc), ("x", "y", "c"))

    mine, smalls = [None] * DEPTH, [None] * DEPTH
    above, sums = None, {}
    nt = len(LAYER_TENSORS)
    early = ("ffn2_gate", "w_mem_kv", "ffn2_up", "w_out", "ffn2_down", "w_in")
    add_sib = lambda gs, from_sib: _by_shape(lambda g, r: _add_sibling(g, r, core), gs, from_sib)
    add_chp = lambda ss, from_chips: _by_shape(lambda s, r: _add_chips(s, r, chip, core), ss, from_chips)

    for i in reversed(range(DEPTH)):
        rides = {}
        if above is not None:
            up = [above[k] for k in LAYER_TENSORS]
            rides["ffn2"] = lambda big, rode, up=up: _SiblingExchange(up)

            def on_ffn1(big, rode, up=up, layer=i + 1, bottom=(i == 0)):
                sums[layer] = add_sib(up, rode["ffn2"])
                ex = _ChipsExchange(sums[layer])
                return _Both(ex, _SiblingExchange([big[k] for k in early])) if bottom else ex

            def on_wgrad(big, rode, j):
                if "early" not in sums:
                    sums["early"] = add_sib([big[k] for k in early], rode["ffn1"][nt:])
                return _ChipsExchange(sums["early"][2 * j:2 * j + 2])

            rides["ffn1"] = on_ffn1
            if i == 0:
                for j, host in enumerate(FIRST_FFN):
                    rides[host] = lambda big, rode, j=j: on_wgrad(big, rode, j)
        dh, gs, smalls[i], rode = _layer_backward(dh, mems, lws[i], saved[i], i, rides)
        if above is not None:
            mine[i + 1] = add_chp(sums[i + 1], rode["ffn1"][:nt])
        above = gs
    early_mine = add_chp(sums["early"], [t for host in FIRST_FFN for t in rode[host]])
    late = [above[k] for k in FIRST_FFN]
    late_sums = add_sib(late, _run_exchange(_SiblingExchange(late)))
    late_mine = add_chp(late_sums, _run_exchange(_ChipsExchange(late_sums)))
    by_name = dict(zip(early + FIRST_FFN, early_mine + late_mine))
    mine[0] = [by_name[k] for k in LAYER_TENSORS]
    shared = _share_halves([t for layer in mine for t in layer])
    reduced = [dict(zip(LAYER_TENSORS, [s.reshape(-1, s.shape[-1]) for s in shared[nt * i:nt * (i + 1)]]))
               for i in range(DEPTH)]
    dx = dh
    red = {}
    for k in BIG:
        if k.startswith("w_in_"):
            layers = [i for i in range(DEPTH) if _is_pool(i) == (k == "w_in_pool")]
            red[k] = jnp.stack([reduced[i]["w_in"] for i in layers])
        else:
            red[k] = jnp.stack([reduced[i][k] for i in range(DEPTH)])

    grads = {"g_pre": jnp.stack([s["g_pre"] for s in smalls]), "g_post": jnp.stack([s["g_post"] for s in smalls]),
             "g_mem": jnp.concatenate([s["g_mem"] for s in smalls], axis=0),
             "pool_w": jnp.stack([smalls[i]["pool_w"] for i in range(DEPTH) if _is_pool(i)]),
             "pool_scale": jnp.concatenate([smalls[i]["pool_scale"] for i in range(DEPTH) if _is_pool(i)], axis=0)}
    small_shapes = [grads[k].shape for k in SMALL]
    summed = _unpack128(_sum_devices(_small_allgather(_pack128([grads[k] for k in SMALL]), dev)), small_shapes)
    for k, s in zip(SMALL, summed):
        if SHARD_AXIS[k] is None:
            red[k] = s
        else:
            red[k] = lax.dynamic_slice_in_dim(s, chip * w[k].shape[2], w[k].shape[2], axis=2)

    deltas, new_m, new_v = {}, {}, {}
    for k in WEIGHTS:
        deltas[k], new_m[k], new_v[k] = _adamw(w[k], red[k], m[k], v[k])
    return (loss, dx.reshape(x.shape), *[red[k] for k in WEIGHTS], *[deltas[k] for k in WEIGHTS],
            *[new_m[k] for k in WEIGHTS], *[new_v[k] for k in WEIGHTS])
```

```python
import jax
import jax.numpy as jnp
from jax import lax
from jax.experimental import pallas as pl
from jax.experimental.pallas import tpu as pltpu

F32 = jnp.float32
BF16 = jnp.bfloat16
_MM = jnp.bfloat16

D = 1024
FF = 2048
DTOK = 512
DMEM = 256
DMIX = DTOK + DMEM
DSB = 3 * DTOK + DMEM
HD = 64
MEM_LEN = 256
DEPTH = 4
EPS = 1e-6
WINDOWS = (2, 4, 8, 16)
HALO = 16
PG = 128
QB = 128
SCALE = HD ** -0.5
FC = 512
NEG_CUT = -104.0

ADAM_LR, ADAM_B1, ADAM_B2, ADAM_EPS, ADAM_WD, ADAM_STEP = 0.001, 0.9, 0.999, 1e-08, 0.01, 10

VMEM_LIMIT = 56 * 2 ** 20
MESH = pl.DeviceIdType.MESH
N_CHIPS = 4
N_DEV = 8


def _cparams(sem):
    return pltpu.CompilerParams(dimension_semantics=sem, vmem_limit_bytes=VMEM_LIMIT)


def _mm(a, b):
    return jnp.dot(a.astype(_MM), b.astype(_MM), preferred_element_type=F32)


def _mm_nt(a, b):
    return lax.dot_general(a.astype(_MM), b.astype(_MM), (((1,), (1,)), ((), ())), preferred_element_type=F32)


def _mm_tn(a, b):
    return lax.dot_general(a.astype(_MM), b.astype(_MM), (((0,), (0,)), ((), ())), preferred_element_type=F32)


def _mm2(a, b):
    hi = a.astype(_MM)
    lo = (a - hi.astype(F32)).astype(_MM)
    return jnp.dot(hi, b, preferred_element_type=F32) + jnp.dot(lo, b, preferred_element_type=F32)


def _rstd(x):
    return lax.rsqrt(jnp.mean(x * x, axis=-1, keepdims=True) + EPS)


def _rms_bwd(x, g, dy):
    r = _rstd(x)
    xh = x * r
    t = dy * g
    dx = r * (t - xh * jnp.mean(t * xh, axis=-1, keepdims=True))
    return dx, jnp.sum(dy * xh, axis=0, keepdims=True)


def _sigmoid(a):
    return 1.0 / (1.0 + jnp.exp(-a))


def _row_tile(n, want):
    t = min(n, want)
    assert n % t == 0, (n, t)
    return t


def _ffn_fwd(h, g1, g2, wg, wu, wd, ride=None):
    S = h.shape[0]
    ts = _row_tile(S, 512)

    def body(h_ref, g1_ref, g2_ref, wg_ref, wu_ref, wd_ref, ho_ref, a_ref, b_ref, f_ref):
        hv = h_ref[...]
        n = (hv * _rstd(hv) * g1_ref[...]).astype(_MM)
        f = jnp.zeros((ts, D), F32)
        for j in range(FF // FC):
            cs = slice(j * FC, (j + 1) * FC)
            a = _mm(n, wg_ref[j])
            b = _mm(n, wu_ref[j])
            a_ref[:, cs] = a.astype(a_ref.dtype)
            b_ref[:, cs] = b.astype(b_ref.dtype)
            f = f + _mm(a * _sigmoid(a) * b, wd_ref[j])
        f_ref[...] = f
        ho_ref[...] = hv + 0.5 * (f * _rstd(f) * g2_ref[...])

    row = lambda i: (i, 0)
    fix = lambda i: (0, 0)
    all3 = lambda i: (0, 0, 0)
    ns = FF // FC
    return _call(
        body, "ffn_fwd", S // ts,
        [pl.BlockSpec((ts, D), row), pl.BlockSpec((1, D), fix), pl.BlockSpec((1, D), fix),
         pl.BlockSpec((ns, D, FC), all3), pl.BlockSpec((ns, D, FC), all3), pl.BlockSpec((ns, FC, D), all3)],
        [pl.BlockSpec((ts, D), row), pl.BlockSpec((ts, FF), row), pl.BlockSpec((ts, FF), row),
         pl.BlockSpec((ts, D), row)],
        [jax.ShapeDtypeStruct((S, D), F32), jax.ShapeDtypeStruct((S, FF), _MM),
         jax.ShapeDtypeStruct((S, FF), _MM), jax.ShapeDtypeStruct((S, D), F32)],
        (h, g1, g2, wg, wu, wd), ride)


def _ffn_bwd(h, f, a, b, dho, g1, g2, wg, wu, wd, ride=None):
    S = h.shape[0]
    ts = _row_tile(S, 256)

    def body(h_ref, f_ref, a_ref, b_ref, dho_ref, g1_ref, g2_ref, wg_ref, wu_ref, wd_ref,
             dh_ref, n_ref, s_ref, da_ref, db_ref, df_ref, dg1_ref, dg2_ref):
        @pl.when(pl.program_id(0) == 0)
        def _():
            dg1_ref[...] = jnp.zeros_like(dg1_ref)
            dg2_ref[...] = jnp.zeros_like(dg2_ref)

        hv = h_ref[...]
        dho = dho_ref[...]
        df, dg2 = _rms_bwd(f_ref[...], g2_ref[...], 0.5 * dho)
        dg2_ref[...] += dg2
        dfb = df.astype(_MM)
        df_ref[...] = dfb
        rh = _rstd(hv)
        hh = hv * rh
        g1 = g1_ref[...]
        n_ref[...] = (hh * g1).astype(_MM)
        dss = [_mm_nt(dfb, wd_ref[j]) for j in range(FF // FC)]
        das, dbs = [], []
        for j in range(FF // FC):
            cs = slice(j * FC, (j + 1) * FC)
            av = a_ref[:, cs].astype(F32)
            bv = b_ref[:, cs].astype(F32)
            sig = _sigmoid(av)
            sl = av * sig
            s_ref[:, cs] = (sl * bv).astype(_MM)
            da = (dss[j] * bv * (sig * (1.0 + av * (1.0 - sig)))).astype(_MM)
            db = (dss[j] * sl).astype(_MM)
            da_ref[:, cs] = da
            db_ref[:, cs] = db
            das.append(da)
            dbs.append(db)
        dn = jnp.zeros((ts, D), F32)
        for j in range(FF // FC):
            dn = dn + _mm_nt(das[j], wg_ref[j]) + _mm_nt(dbs[j], wu_ref[j])
        t = dn * g1
        dh_ref[...] = dho + rh * (t - hh * jnp.mean(t * hh, axis=-1, keepdims=True))
        dg1_ref[...] += jnp.sum(dn * hh, axis=0, keepdims=True)

    row = lambda i: (i, 0)
    fix = lambda i: (0, 0)
    all3 = lambda i: (0, 0, 0)
    ns = FF // FC
    td = pl.BlockSpec((ts, D), row)
    tf = pl.BlockSpec((ts, FF), row)
    gd = pl.BlockSpec((1, D), fix)
    return _call(
        body, "ffn_bwd", S // ts,
        [td, td, tf, tf, td, gd, gd,
         pl.BlockSpec((ns, D, FC), all3), pl.BlockSpec((ns, D, FC), all3), pl.BlockSpec((ns, FC, D), all3)],
        [td, td, tf, tf, tf, td, gd, gd],
        [jax.ShapeDtypeStruct((S, D), F32), jax.ShapeDtypeStruct((S, D), _MM),
         jax.ShapeDtypeStruct((S, FF), _MM), jax.ShapeDtypeStruct((S, FF), _MM),
         jax.ShapeDtypeStruct((S, FF), _MM), jax.ShapeDtypeStruct((S, D), _MM),
         jax.ShapeDtypeStruct((1, D), F32), jax.ShapeDtypeStruct((1, D), F32)],
        (h, f, a, b, dho, g1, g2, wg, wu, wd), ride)


def _wgrad(x, y, nslot=0, ride=None):
    T, M = x.shape
    N = y.shape[1]
    tt = _row_tile(T, 1024)
    cw = N // nslot if nslot else (512 if N % 512 == 0 else 256)
    nc = N // cw
    assert cw * nc == N and cw % 128 == 0, (N, nc)

    def body(x_ref, y_ref, o_ref):
        @pl.when(pl.program_id(0) == 0)
        def _():
            o_ref[...] = jnp.zeros_like(o_ref)

        xt = x_ref[...].T
        for j in range(nc):
            part = jnp.dot(xt, y_ref[:, j * cw:(j + 1) * cw], preferred_element_type=F32)
            if nslot:
                o_ref[j] += part
            else:
                o_ref[:, j * cw:(j + 1) * cw] += part

    oshape = (nslot, M, cw) if nslot else (M, N)
    res = _call(body, "wgrad", T // tt,
                [pl.BlockSpec((tt, M), lambda t: (t, 0)), pl.BlockSpec((tt, N), lambda t: (t, 0))],
                [pl.BlockSpec(oshape, lambda t: (0,) * len(oshape))], [jax.ShapeDtypeStruct(oshape, F32)], (x, y), ride)
    return res[0] if ride is None else (res[0][0], res[1])


def _proj_fwd(h, g, w, out_dtype):
    S = h.shape[0]
    nk, kw, N = w.shape
    ts = _row_tile(S, 256)

    def body(h_ref, g_ref, w_ref, u_ref, p_ref):
        hv = h_ref[...]
        u = (hv * _rstd(hv) * g_ref[...]).astype(_MM)
        u_ref[...] = u
        acc = _mm(u[:, :kw], w_ref[0])
        for j in range(1, nk):
            acc = acc + _mm(u[:, j * kw:(j + 1) * kw], w_ref[j])
        p_ref[...] = acc.astype(p_ref.dtype)

    return pl.pallas_call(
        body, name="proj_fwd", grid=(S // ts,),
        in_specs=[pl.BlockSpec((ts, D), lambda i: (i, 0)), pl.BlockSpec((1, D), lambda i: (0, 0)),
                  pl.BlockSpec((nk, kw, N), lambda i: (0, 0, 0))],
        out_specs=[pl.BlockSpec((ts, D), lambda i: (i, 0)), pl.BlockSpec((ts, N), lambda i: (i, 0))],
        out_shape=[jax.ShapeDtypeStruct((S, D), _MM), jax.ShapeDtypeStruct((S, N), out_dtype)],
        compiler_params=_cparams(("parallel",)),
    )(h, g, w)


def _proj_bwd(h, dho, g, w, pieces):
    S = h.shape[0]
    nk, kw, N = w.shape
    ts = _row_tile(S, 256)
    widths = [p.shape[1] for p in pieces]
    assert sum(widths) == N
    npc = len(pieces)

    def body(*refs):
        h_ref, dho_ref, g_ref, w_ref = refs[:4]
        p_refs = refs[4:4 + npc]
        dh_ref, dp_ref, dg_ref = refs[4 + npc:]

        @pl.when(pl.program_id(0) == 0)
        def _():
            dg_ref[...] = jnp.zeros_like(dg_ref)

        dus = [jnp.zeros((ts, kw), F32) for _ in range(nk)]
        off = 0
        for p_ref, n in zip(p_refs, widths):
            pv = p_ref[...].astype(_MM)
            dp_ref[:, off:off + n] = pv
            for j in range(nk):
                dus[j] = dus[j] + _mm_nt(pv, w_ref[j, :, off:off + n])
            off += n
        du = dus[0] if nk == 1 else jnp.concatenate(dus, axis=-1)
        dx, dg = _rms_bwd(h_ref[...], g_ref[...], du)
        dh_ref[...] = dho_ref[...] + dx
        dg_ref[...] += dg

    row = lambda i: (i, 0)
    return pl.pallas_call(
        body, name="proj_bwd", grid=(S // ts,),
        in_specs=[pl.BlockSpec((ts, D), row), pl.BlockSpec((ts, D), row), pl.BlockSpec((1, D), lambda i: (0, 0)),
                  pl.BlockSpec((nk, kw, N), lambda i: (0, 0, 0))] + [pl.BlockSpec((ts, n), row) for n in widths],
        out_specs=[pl.BlockSpec((ts, D), row), pl.BlockSpec((ts, N), row), pl.BlockSpec((1, D), lambda i: (0, 0))],
        out_shape=[jax.ShapeDtypeStruct((S, D), F32), jax.ShapeDtypeStruct((S, N), _MM),
                   jax.ShapeDtypeStruct((1, D), F32)],
        compiler_params=_cparams(("arbitrary",)),
    )(h, dho, g, w, *pieces)


def _pool_counts(first_row, ts):
    pos = (first_row + lax.broadcasted_iota(jnp.int32, (ts, 1), 0) + 1).astype(F32)
    return [jnp.minimum(pos, float(w)) for w in WINDOWS]


def _pool_delta(x, prev, cnts, ts):
    xe = jnp.concatenate([prev, x], axis=0)
    sums = []
    cur = xe
    for sh in (1, 2, 4, 8):
        cur = cur + pltpu.roll(cur, sh, 0)
        sums.append(cur)
    return [sums[gi][HALO:, gi * PG:(gi + 1) * PG] / cnts[gi] - x[:, gi * PG:(gi + 1) * PG]
            for gi in range(len(WINDOWS))]


def _pool_fwd(proj, pw, ps):
    S = proj.shape[0]
    ts = _row_tile(S, 256)

    def body(x_ref, pw_ref, ps_ref, tok_ref, carry_ref):
        i = pl.program_id(0)

        @pl.when(i == 0)
        def _():
            carry_ref[...] = jnp.zeros_like(carry_ref)

        x = x_ref[...]
        ds = _pool_delta(x, carry_ref[...], _pool_counts(i * ts, ts), ts)
        y = jnp.concatenate([_mm(ds[gi], pw_ref[gi]) for gi in range(len(WINDOWS))], axis=-1)
        tok_ref[...] = (y * ps_ref[...]).astype(tok_ref.dtype)
        carry_ref[...] = x[ts - HALO:, :]

    return pl.pallas_call(
        body, name="pool_fwd", grid=(S // ts,),
        in_specs=[pl.BlockSpec((ts, DTOK), lambda i: (i, 0)), pl.BlockSpec((len(WINDOWS), PG, PG), lambda i: (0, 0, 0)),
                  pl.BlockSpec((1, DTOK), lambda i: (0, 0))],
        out_specs=pl.BlockSpec((ts, DTOK), lambda i: (i, 0)),
        out_shape=jax.ShapeDtypeStruct((S, DTOK), _MM),
        scratch_shapes=[pltpu.VMEM((HALO, DTOK), F32)],
        compiler_params=_cparams(("arbitrary",)),
    )(proj, pw, ps)


def _pool_bwd(proj, dtok, pw, ps):
    S = proj.shape[0]
    ts = _row_tile(S, 256)
    nt = S // ts
    per = ts // HALO
    ng = len(WINDOWS)

    def body(x_ref, xp_ref, dt_ref, pw_ref, ps_ref, dx_ref, dpw_ref, dps_ref, carry_ref):
        i = pl.program_id(0)
        idx = nt - 1 - i

        @pl.when(i == 0)
        def _():
            carry_ref[...] = jnp.zeros_like(carry_ref)
            dpw_ref[...] = jnp.zeros_like(dpw_ref)
            dps_ref[...] = jnp.zeros_like(dps_ref)

        x = x_ref[...]
        prev = jnp.where(idx > 0, xp_ref[...], 0.0)
        cnts = _pool_counts(idx * ts, ts)
        ds = _pool_delta(x, prev, cnts, ts)
        dt = dt_ref[...]
        y = jnp.concatenate([_mm(ds[gi], pw_ref[gi]) for gi in range(ng)], axis=-1)
        dps_ref[...] += jnp.sum(dt * y, axis=0, keepdims=True)
        dy = (dt * ps_ref[...]).astype(_MM)
        dds = []
        for gi in range(ng):
            dyg = dy[:, gi * PG:(gi + 1) * PG]
            dpw_ref[gi] += _mm_tn(ds[gi], dyg)
            dds.append(_mm_nt(dyg, pw_ref[gi]))
        e = jnp.concatenate([dds[gi] / cnts[gi] for gi in range(ng)], axis=-1)
        ee = jnp.concatenate([e, carry_ref[...]], axis=0)
        rows = ts + HALO
        cur = ee
        outs = []
        for gi, sh in enumerate((1, 2, 4, 8)):
            cur = cur + pltpu.roll(cur, rows - sh, 0)
            outs.append(cur[:ts, gi * PG:(gi + 1) * PG] - dds[gi])
        dx_ref[...] = jnp.concatenate(outs, axis=-1)
        carry_ref[...] = e[:HALO, :]

    rev = lambda i: (nt - 1 - i, 0)
    return pl.pallas_call(
        body, name="pool_bwd", grid=(nt,),
        in_specs=[pl.BlockSpec((ts, DTOK), rev),
                  pl.BlockSpec((HALO, DTOK), lambda i: (jnp.maximum((nt - 1 - i) * per - 1, 0), 0)),
                  pl.BlockSpec((ts, DTOK), rev), pl.BlockSpec((ng, PG, PG), lambda i: (0, 0, 0)),
                  pl.BlockSpec((1, DTOK), lambda i: (0, 0))],
        out_specs=[pl.BlockSpec((ts, DTOK), rev), pl.BlockSpec((ng, PG, PG), lambda i: (0, 0, 0)),
                   pl.BlockSpec((1, DTOK), lambda i: (0, 0))],
        out_shape=[jax.ShapeDtypeStruct((S, DTOK), F32), jax.ShapeDtypeStruct((ng, PG, PG), F32),
                   jax.ShapeDtypeStruct((1, DTOK), F32)],
        scratch_shapes=[pltpu.VMEM((HALO, DTOK), F32)],
        compiler_params=_cparams(("arbitrary",)),
    )(proj, proj, dtok, pw, ps)


def _tri(strict):
    r = lax.broadcasted_iota(jnp.int32, (QB, QB), 0)
    c = lax.broadcasted_iota(jnp.int32, (QB, QB), 1)
    return jnp.where(r > c if strict else r >= c, 1.0, 0.0).astype(_MM)


def _sb_terms(z, valid):
    e = jnp.exp(-jnp.abs(z))
    sp = jnp.log(1.0 + e)
    ls = jnp.minimum(z, 0.0) - sp
    lf = jnp.where(valid, jnp.minimum(-z, 0.0) - sp, 0.0)
    return ls, lf, e


def _sb_alive(cs, lo=0):
    top = cs[0][lo:]
    for c in cs[1:]:
        top = jnp.maximum(top, c[lo:])
    return (jnp.max(top) > NEG_CUT).astype(jnp.int32)


def _sb_split(x2, head_a):
    zero = jnp.zeros_like(x2)
    return jnp.where(head_a, x2, zero), jnp.where(head_a, zero, x2)


SB_GROUPS_FWD = 4
SB_GROUPS_BWD = 4
SB_TOP = 48


def _sb_fwd(proj):
    S = proj.shape[0]
    G = min(SB_GROUPS_FWD, S // QB)
    nstep = S // (G * QB)
    npair = DTOK // 128
    nch = 2 * G

    def body(q_ref, k_ref, v_ref, tri_ref, o_ref):
        qb = pl.program_id(1)
        head_a = lax.broadcasted_iota(jnp.int32, (QB, 128), 1) < HD
        qs = []
        for g in range(G):
            qs.extend(_sb_split(q_ref[g * QB:(g + 1) * QB, :], head_a))
        tri = tri_ref[...]
        last = G * qb + G - 1

        def trips(rows):
            qr = [q[:rows] for q in qs]
            row = lax.broadcasted_iota(jnp.int32, (rows, QB), 0)
            col = lax.broadcasted_iota(jnp.int32, (rows, QB), 1)

            def step(carry):
                i, _, cs, accs = carry
                causal = (col - row) < i * QB
                ks, vs, valids = [], [], []
                for g in range(G):
                    kb = G * qb + g - i
                    off = pl.multiple_of(jnp.maximum(kb, 0) * QB, QB)
                    ks.append(k_ref[pl.ds(off, QB), :])
                    vs.append(v_ref[pl.ds(off, QB), :])
                    valids.append(jnp.logical_and(causal, kb >= 0))
                zs = [_mm_nt(qr[ch], ks[ch // 2]) * SCALE for ch in range(nch)]
                terms = [_sb_terms(zs[ch], valids[ch // 2]) for ch in range(nch)]
                withins = [_mm2(terms[ch][1], tri) for ch in range(nch)]
                new_c, new_acc = [], []
                for ch in range(nch):
                    ls, lf, _ = terms[ch]
                    a = jnp.where(valids[ch // 2], jnp.exp(ls + withins[ch] + cs[ch]), 0.0)
                    new_acc.append(accs[ch] + _mm2(a, vs[ch // 2]))
                    new_c.append(cs[ch] + jnp.sum(lf, axis=-1, keepdims=True))
                return i + 1, _sb_alive(new_c, SB_TOP if rows == QB else 0), tuple(new_c), tuple(new_acc)

            return step

        zc = jnp.zeros((QB, 1), F32)
        za = jnp.zeros((QB, 128), F32)
        more = lambda cr: jnp.logical_and(cr[0] <= last, cr[1] > 0)
        i, _, cs, accs = lax.while_loop(more, trips(QB), (0, 1, (zc,) * nch, (za,) * nch))
        cs_top = tuple(c[:SB_TOP] for c in cs)
        acc_top = tuple(a[:SB_TOP] for a in accs)
        acc_top = lax.while_loop(more, trips(SB_TOP), (i, _sb_alive(cs_top), cs_top, acc_top))[3]
        for g in range(G):
            o_ref[g * QB:g * QB + SB_TOP, :] = jnp.where(head_a[:SB_TOP], acc_top[2 * g], acc_top[2 * g + 1])
            o_ref[g * QB + SB_TOP:(g + 1) * QB, :] = jnp.where(head_a, accs[2 * g], accs[2 * g + 1])[SB_TOP:]

    return pl.pallas_call(
        body, name="sb_fwd", grid=(npair, nstep),
        in_specs=[pl.BlockSpec((G * QB, 128), lambda p, i: (i, p)),
                  pl.BlockSpec((S, 128), lambda p, i: (0, npair + p)),
                  pl.BlockSpec((S, 128), lambda p, i: (0, 2 * npair + p)),
                  pl.BlockSpec((QB, QB), lambda p, i: (0, 0))],
        out_specs=pl.BlockSpec((G * QB, 128), lambda p, i: (i, p)),
        out_shape=jax.ShapeDtypeStruct((S, DTOK), F32),
        compiler_params=_cparams(("parallel", "parallel")),
    )(proj, proj, proj, _tri(True))


def _sb_bwd(proj, o, do):
    S = proj.shape[0]
    G = min(SB_GROUPS_BWD, S // QB)
    nstep = S // (G * QB)
    npair = DTOK // 128
    nch = 2 * G

    def body(q_ref, k_ref, v_ref, o_ref, do_ref, tri_ref, tri2_ref, dq_ref, dk_ref, dv_ref):
        qb = pl.program_id(1)

        @pl.when(qb == 0)
        def _():
            dk_ref[...] = jnp.zeros_like(dk_ref)
            dv_ref[...] = jnp.zeros_like(dv_ref)

        head_a = lax.broadcasted_iota(jnp.int32, (QB, 128), 1) < HD
        qs, dos, gs = [], [], []
        for g in range(G):
            rows = slice(g * QB, (g + 1) * QB)
            qs.extend(_sb_split(q_ref[rows, :], head_a))
            dob = do_ref[rows, :].astype(_MM)
            dos.extend(_sb_split(dob, head_a))
            go = dob.astype(F32) * o_ref[rows, :]
            gs.append(jnp.sum(jnp.where(head_a, go, 0.0), axis=-1, keepdims=True))
            gs.append(jnp.sum(jnp.where(head_a, 0.0, go), axis=-1, keepdims=True))
        tri = tri_ref[...]
        tri2 = tri2_ref[...]
        last = G * qb + G - 1

        def trips(rows):
            qr = [q[:rows] for q in qs]
            dor = [d[:rows] for d in dos]
            gr = [g_[:rows] for g_ in gs]
            row = lax.broadcasted_iota(jnp.int32, (rows, QB), 0)
            col = lax.broadcasted_iota(jnp.int32, (rows, QB), 1)

            def step(carry):
                i, _, cs, rs, dqs = carry
                causal = (col - row) < i * QB
                offs, ks, vs, valids = [], [], [], []
                for g in range(G):
                    kb = G * qb + g - i
                    off = pl.multiple_of(jnp.maximum(kb, 0) * QB, QB)
                    offs.append(off)
                    ks.append(k_ref[pl.ds(off, QB), :])
                    vs.append(v_ref[pl.ds(off, QB), :])
                    valids.append(jnp.logical_and(causal, kb >= 0))
                zs = [_mm_nt(qr[ch], ks[ch // 2]) * SCALE for ch in range(nch)]
                das = [_mm_nt(dor[ch], vs[ch // 2]) for ch in range(nch)]
                terms = [_sb_terms(zs[ch], valids[ch // 2]) for ch in range(nch)]
                withins = [_mm2(terms[ch][1], tri) for ch in range(nch)]
                a_s, dlws = [], []
                for ch in range(nch):
                    a = jnp.where(valids[ch // 2], jnp.exp(terms[ch][0] + withins[ch] + cs[ch]), 0.0)
                    a_s.append(a)
                    dlws.append(das[ch] * a)
                sfx = [_mm2(dlws[ch], tri2) for ch in range(nch)]
                dzs = []
                for ch in range(nch):
                    e = terms[ch][2]
                    inv = 1.0 / (1.0 + e)
                    pos = zs[ch] >= 0.0
                    beta = jnp.where(pos, 1.0, e) * inv
                    omb = jnp.where(pos, e, 1.0) * inv
                    prefix = gr[ch] - rs[ch] - sfx[ch]
                    dzs.append((jnp.where(valids[ch // 2], dlws[ch] * omb - beta * prefix, 0.0) * SCALE).astype(_MM))
                new_dq = [dqs[ch] + _mm(dzs[ch], ks[ch // 2]) for ch in range(nch)]
                for g in range(G):
                    a, b = 2 * g, 2 * g + 1
                    dk_ref[pl.ds(offs[g], QB), :] += _mm_tn(dzs[a], qr[a]) + _mm_tn(dzs[b], qr[b])
                    dv_ref[pl.ds(offs[g], QB), :] += _mm_tn(a_s[a], dor[a]) + _mm_tn(a_s[b], dor[b])
                new_c = [cs[ch] + jnp.sum(terms[ch][1], axis=-1, keepdims=True) for ch in range(nch)]
                new_r = [rs[ch] + jnp.sum(dlws[ch], axis=-1, keepdims=True) for ch in range(nch)]
                alive = _sb_alive(new_c, SB_TOP if rows == QB else 0)
                return i + 1, alive, tuple(new_c), tuple(new_r), tuple(new_dq)

            return step

        zc = jnp.zeros((QB, 1), F32)
        za = jnp.zeros((QB, 128), F32)
        more = lambda cr: jnp.logical_and(cr[0] <= last, cr[1] > 0)
        i, _, cs, rs, dqs = lax.while_loop(more, trips(QB), (0, 1, (zc,) * nch, (zc,) * nch, (za,) * nch))
        top = lambda xs: tuple(x[:SB_TOP] for x in xs)
        cs_top = top(cs)
        dq_top = lax.while_loop(more, trips(SB_TOP), (i, _sb_alive(cs_top), cs_top, top(rs), top(dqs)))[4]
        for g in range(G):
            dq_ref[g * QB:g * QB + SB_TOP, :] = jnp.where(head_a[:SB_TOP], dq_top[2 * g], dq_top[2 * g + 1])
            dq_ref[g * QB + SB_TOP:(g + 1) * QB, :] = jnp.where(head_a, dqs[2 * g], dqs[2 * g + 1])[SB_TOP:]

    qspec = pl.BlockSpec((G * QB, 128), lambda p, i: (i, p))
    full = lambda base: pl.BlockSpec((S, 128), lambda p, i: (0, base + p))
    tspec = pl.BlockSpec((QB, QB), lambda p, i: (0, 0))
    return pl.pallas_call(
        body, name="sb_bwd", grid=(npair, nstep),
        in_specs=[qspec, full(npair), full(2 * npair), qspec, qspec, tspec, tspec],
        out_specs=[qspec, full(0), full(0)],
        out_shape=[jax.ShapeDtypeStruct((S, DTOK), F32)] * 3,
        compiler_params=_cparams(("parallel", "arbitrary")),
    )(proj, proj, proj, o, do, _tri(True), _tri(False))


def _memkv_fwd(mem, gm, wkv):
    nk, kw, _ = wkv.shape

    def body(m_ref, g_ref, w_ref, kv_ref):
        m = m_ref[...]
        mn = (m * _rstd(m) * g_ref[...]).astype(_MM)
        acc = _mm(mn[:, :kw], w_ref[0])
        for j in range(1, nk):
            acc = acc + _mm(mn[:, j * kw:(j + 1) * kw], w_ref[j])
        kv_ref[...] = acc.astype(kv_ref.dtype)

    return pl.pallas_call(
        body, name="memkv_fwd", out_shape=jax.ShapeDtypeStruct((MEM_LEN, 2 * DMEM), _MM),
        compiler_params=pltpu.CompilerParams(vmem_limit_bytes=VMEM_LIMIT),
    )(mem, gm, wkv)


def _memkv_bwd(mem, gm, wkv, dkv):
    nk, kw, _ = wkv.shape

    def body(m_ref, g_ref, w_ref, dkv_ref, dw_ref, dg_ref):
        m = m_ref[...]
        mh = m * _rstd(m)
        mn = (mh * g_ref[...]).astype(_MM)
        dkv = dkv_ref[...].astype(_MM)
        dw_ref[...] = _mm_tn(mn, dkv)
        dmn = jnp.concatenate([_mm_nt(dkv, w_ref[j]) for j in range(nk)], axis=-1)
        dg_ref[...] = jnp.sum(dmn * mh, axis=0, keepdims=True)

    return pl.pallas_call(
        body, name="memkv_bwd",
        out_shape=[jax.ShapeDtypeStruct((D, 2 * DMEM), F32), jax.ShapeDtypeStruct((1, D), F32)],
        compiler_params=pltpu.CompilerParams(vmem_limit_bytes=VMEM_LIMIT),
    )(mem, gm, wkv, dkv)


def _mixout_fwd(h, tok, proj, kv, wo, g, qm_block):
    S = h.shape[0]
    ts = _row_tile(S, 256)

    def body(h_ref, tok_ref, qm_ref, kv_ref, wo_ref, g_ref, ho_ref, cat_ref):
        head_a = lax.broadcasted_iota(jnp.int32, (ts, 128), 1) < HD
        qm = qm_ref[...].astype(_MM)
        zq = jnp.zeros((ts, 128), _MM)
        npair = DMEM // 128
        qhs = []
        for pi in range(npair):
            qp = qm[:, pi * 128:(pi + 1) * 128]
            qhs += [jnp.where(head_a, qp, zq), jnp.where(head_a, zq, qp)]
        scores = [_mm_nt(qhs[h], kv_ref[:, (h // 2) * 128:(h // 2 + 1) * 128]) * SCALE for h in range(2 * npair)]
        ps = []
        for s in scores:
            e = jnp.exp(s - jnp.max(s, axis=-1, keepdims=True))
            ps.append(e / jnp.sum(e, axis=-1, keepdims=True))
        pvs = [_mm(ps[h], kv_ref[:, DMEM + (h // 2) * 128:DMEM + (h // 2 + 1) * 128]) for h in range(2 * npair)]
        mos = [jnp.where(head_a, pvs[2 * pi], pvs[2 * pi + 1]) for pi in range(npair)]
        cat = jnp.concatenate([tok_ref[...].astype(_MM)] + [m.astype(_MM) for m in mos], axis=-1)
        cat_ref[...] = cat
        mix = jnp.concatenate([_mm(cat, wo_ref[j]) for j in range(N_CHIPS)], axis=-1)
        ho_ref[...] = h_ref[...] + mix * _rstd(mix) * g_ref[...]

    row = lambda i: (i, 0)
    return pl.pallas_call(
        body, name="mixout_fwd", grid=(S // ts,),
        in_specs=[pl.BlockSpec((ts, D), row), pl.BlockSpec((ts, DTOK), row),
                  pl.BlockSpec((ts, DMEM), lambda i: (i, qm_block)),
                  pl.BlockSpec((MEM_LEN, 2 * DMEM), lambda i: (0, 0)),
                  pl.BlockSpec((N_CHIPS, DMIX, D // N_CHIPS), lambda i: (0, 0, 0)), pl.BlockSpec((1, D), lambda i: (0, 0))],
        out_specs=[pl.BlockSpec((ts, D), row), pl.BlockSpec((ts, DMIX), row)],
        out_shape=[jax.ShapeDtypeStruct((S, D), F32), jax.ShapeDtypeStruct((S, DMIX), _MM)],
        compiler_params=_cparams(("parallel",)),
    )(h, tok, proj, kv, wo, g)


def _mixout_bwd(dho, cat, proj, kv, wo, g, qm_block):
    S = dho.shape[0]
    ts = _row_tile(S, 256)

    def body(dho_ref, cat_ref, qm_ref, kv_ref, wo_ref, g_ref, dmix_ref, dtok_ref, dqm_ref, dkv_ref, dg_ref):
        @pl.when(pl.program_id(0) == 0)
        def _():
            dkv_ref[...] = jnp.zeros_like(dkv_ref)
            dg_ref[...] = jnp.zeros_like(dg_ref)

        catv = cat_ref[...]
        mix = jnp.concatenate([_mm(catv, wo_ref[j]) for j in range(N_CHIPS)], axis=-1)
        dmix, dg = _rms_bwd(mix, g_ref[...], dho_ref[...])
        dg_ref[...] += dg
        dmb = dmix.astype(_MM)
        dmix_ref[...] = dmb
        cw = D // N_CHIPS
        dcat = _mm_nt(dmb[:, :cw], wo_ref[0])
        for j in range(1, N_CHIPS):
            dcat = dcat + _mm_nt(dmb[:, j * cw:(j + 1) * cw], wo_ref[j])
        dtok_ref[...] = dcat[:, :DTOK]
        head_a = lax.broadcasted_iota(jnp.int32, (ts, 128), 1) < HD
        qm = qm_ref[...].astype(_MM)
        zq = jnp.zeros((ts, 128), _MM)
        npair = DMEM // 128
        kps = [kv_ref[:, pi * 128:(pi + 1) * 128] for pi in range(npair)]
        vps = [kv_ref[:, DMEM + pi * 128:DMEM + (pi + 1) * 128] for pi in range(npair)]
        qhs, dmhs = [], []
        for pi in range(npair):
            cs = slice(pi * 128, (pi + 1) * 128)
            dmo = dcat[:, DTOK + pi * 128:DTOK + (pi + 1) * 128].astype(_MM)
            qhs += [jnp.where(head_a, qm[:, cs], zq), jnp.where(head_a, zq, qm[:, cs])]
            dmhs += [jnp.where(head_a, dmo, zq), jnp.where(head_a, zq, dmo)]
        nh = 2 * npair
        scores = [_mm_nt(qhs[h], kps[h // 2]) * SCALE for h in range(nh)]
        dps = [_mm_nt(dmhs[h], vps[h // 2]) for h in range(nh)]
        ps, dss = [], []
        for h in range(nh):
            e = jnp.exp(scores[h] - jnp.max(scores[h], axis=-1, keepdims=True))
            p = e / jnp.sum(e, axis=-1, keepdims=True)
            ps.append(p)
            dss.append((p * (dps[h] - jnp.sum(dps[h] * p, axis=-1, keepdims=True)) * SCALE).astype(_MM))
        dqs = [_mm(dss[h], kps[h // 2]) for h in range(nh)]
        for pi in range(npair):
            cs = slice(pi * 128, (pi + 1) * 128)
            a, b = 2 * pi, 2 * pi + 1
            dqm_ref[:, cs] = jnp.where(head_a, dqs[a], dqs[b]).astype(_MM)
            dkv_ref[:, cs] += _mm_tn(dss[a], qhs[a]) + _mm_tn(dss[b], qhs[b])
            dkv_ref[:, DMEM + pi * 128:DMEM + (pi + 1) * 128] += _mm_tn(ps[a], dmhs[a]) + _mm_tn(ps[b], dmhs[b])

    row = lambda i: (i, 0)
    fix = lambda i: (0, 0)
    return pl.pallas_call(
        body, name="mixout_bwd", grid=(S // ts,),
        in_specs=[pl.BlockSpec((ts, D), row), pl.BlockSpec((ts, DMIX), row),
                  pl.BlockSpec((ts, DMEM), lambda i: (i, qm_block)),
                  pl.BlockSpec((MEM_LEN, 2 * DMEM), fix),
                  pl.BlockSpec((N_CHIPS, DMIX, D // N_CHIPS), lambda i: (0, 0, 0)), pl.BlockSpec((1, D), fix)],
        out_specs=[pl.BlockSpec((ts, D), row), pl.BlockSpec((ts, DTOK), row), pl.BlockSpec((ts, DMEM), row),
                   pl.BlockSpec((MEM_LEN, 2 * DMEM), fix), pl.BlockSpec((1, D), fix)],
        out_shape=[jax.ShapeDtypeStruct((S, D), _MM), jax.ShapeDtypeStruct((S, DTOK), F32),
                   jax.ShapeDtypeStruct((S, DMEM), _MM), jax.ShapeDtypeStruct((MEM_LEN, 2 * DMEM), F32),
                   jax.ShapeDtypeStruct((1, D), F32)],
        compiler_params=_cparams(("arbitrary",)),
    )(dho, cat, proj, kv, wo, g)


def _loss_grad(y, target):
    S = y.shape[0]
    ts = _row_tile(S, 512)

    def body(y_ref, t_ref, acc_ref, dy_ref):
        @pl.when(pl.program_id(0) == 0)
        def _():
            acc_ref[...] = jnp.zeros_like(acc_ref)

        err = y_ref[...] - t_ref[...]
        acc_ref[...] += jnp.sum(err * err, axis=0, keepdims=True)
        dy_ref[...] = err * (1.0 / D)

    row = lambda i: (i, 0)
    return pl.pallas_call(
        body, name="loss_grad", grid=(S // ts,),
        in_specs=[pl.BlockSpec((ts, D), row), pl.BlockSpec((ts, D), row)],
        out_specs=[pl.BlockSpec((1, D), lambda i: (0, 0)), pl.BlockSpec((ts, D), row)],
        out_shape=[jax.ShapeDtypeStruct((1, D), F32), jax.ShapeDtypeStruct((S, D), F32)],
        compiler_params=_cparams(("arbitrary",)),
    )(y, target)


LAYER_TENSORS = ("ffn1_gate", "ffn1_up", "ffn1_down", "ffn2_gate", "ffn2_up", "ffn2_down", "w_in", "w_mem_kv", "w_out")
SPLITS = {"ffn1_gate": "c", "ffn1_up": "c", "ffn1_down": "r", "ffn2_gate": "c", "ffn2_up": "c", "ffn2_down": "r",
          "w_in_pool": "r", "w_in_sb": "c", "w_mem_kv": "r", "w_out": "c"}


def _is_pool(i):
    return i % 2 == 0


def _qm_block(i):
    return (DTOK if _is_pool(i) else 3 * DTOK) // DMEM


FIRST_FFN = LAYER_TENSORS[:3]
REST = LAYER_TENSORS[3:]


def _layer_forward(h, mem, lw, i, rest_shards, next_shards):
    g_at = lambda g, k: g[k].reshape(1, D)
    sv = {"h0": h}
    (h, sv["a1"], sv["b1"], sv["f1"]), rest = _ffn_fwd(
        h, g_at(lw["g_pre"], 0), g_at(lw["g_post"], 0), lw["ffn1_gate"], lw["ffn1_up"], lw["ffn1_down"],
        _GatherExchange(rest_shards))
    lw.update(zip(REST, rest))
    if not _is_pool(i):
        lw["w_in"] = jnp.transpose(lw["w_in"], (1, 0, 2)).reshape(1, D, DSB)
    sv["h1"] = h
    sv["kv"] = _memkv_fwd(mem, lw["g_mem"], lw["w_mem_kv"])
    if _is_pool(i):
        sv["u"], sv["proj"] = _proj_fwd(h, g_at(lw["g_pre"], 1), lw["w_in"], F32)
        tok = _pool_fwd(sv["proj"], lw["pool_w"], lw["pool_scale"])
    else:
        sv["u"], sv["proj"] = _proj_fwd(h, g_at(lw["g_pre"], 1), lw["w_in"], _MM)
        tok = sv["o"] = _sb_fwd(sv["proj"])
    h, sv["cat"] = _mixout_fwd(h, tok, sv["proj"], sv["kv"], lw["w_out"], g_at(lw["g_post"], 1), _qm_block(i))
    sv["h2"] = h
    ffn2 = (h, g_at(lw["g_pre"], 2), g_at(lw["g_post"], 2), lw["ffn2_gate"], lw["ffn2_up"], lw["ffn2_down"])
    if next_shards is None:
        h, sv["a2"], sv["b2"], sv["f2"] = _ffn_fwd(*ffn2)
        return h, sv, None
    (h, sv["a2"], sv["b2"], sv["f2"]), nxt = _ffn_fwd(*ffn2, _GatherExchange(next_shards))
    return h, sv, nxt


def _layer_backward(dh, mem, lw, sv, i, rides=None):
    g_at = lambda g, k: g[k].reshape(1, D)
    big, small = {}, {}
    dgpre, dgpost = [None] * 3, [None] * 3
    rode = {}
    rides = rides or {}

    def hosted(host, fn, *args):
        ride = rides[host](big, rode) if host in rides else None
        if ride is None:
            return fn(*args)
        outs, rode[host] = fn(*args, ride)
        return outs

    def ffn_back(which, dh, h_in):
        k = 0 if which == 1 else 2
        n = f"ffn{which}"
        dh, nb, sb, da, db, df, dgpre[k], dgpost[k] = hosted(
            n, _ffn_bwd, h_in, sv[f"f{which}"], sv[f"a{which}"], sv[f"b{which}"], dh, g_at(lw["g_pre"], k),
            g_at(lw["g_post"], k), lw[n + "_gate"], lw[n + "_up"], lw[n + "_down"])
        big[n + "_gate"] = hosted(n + "_gate", _wgrad, nb, da, N_CHIPS)
        big[n + "_up"] = hosted(n + "_up", _wgrad, nb, db, N_CHIPS)
        big[n + "_down"] = hosted(n + "_down", _wgrad, sb, df, 0).reshape(N_CHIPS, FF // N_CHIPS, D)
        return dh

    dh = ffn_back(2, dh, sv["h2"])
    dmix, dtok, dqm, dkv, dgpost[1] = _mixout_bwd(dh, sv["cat"], sv["proj"], sv["kv"], lw["w_out"],
                                                  g_at(lw["g_post"], 1), _qm_block(i))
    big["w_out"] = _wgrad(sv["cat"], dmix, N_CHIPS)
    dwkv, small["g_mem"] = _memkv_bwd(mem, lw["g_mem"], lw["w_mem_kv"], dkv)
    big["w_mem_kv"] = dwkv.reshape(N_CHIPS, D // N_CHIPS, 2 * DMEM)
    if _is_pool(i):
        dx, small["pool_w"], small["pool_scale"] = _pool_bwd(sv["proj"], dtok, lw["pool_w"], lw["pool_scale"])
        dh, dproj, dgpre[1] = _proj_bwd(sv["h1"], dh, g_at(lw["g_pre"], 1), lw["w_in"], [dx, dqm])
        big["w_in"] = _wgrad(sv["u"], dproj).reshape(N_CHIPS, D // N_CHIPS, DMIX)
    else:
        dq, dk, dv = _sb_bwd(sv["proj"], sv["o"], dtok)
        dh, dproj, dgpre[1] = _proj_bwd(sv["h1"], dh, g_at(lw["g_pre"], 1), lw["w_in"], [dq, dk, dv, dqm])
        big["w_in"] = jnp.transpose(_wgrad(sv["u"], dproj).reshape(D, N_CHIPS, DSB // N_CHIPS), (1, 0, 2))
    dh = ffn_back(1, dh, sv["h0"])
    small["g_pre"] = jnp.concatenate(dgpre, axis=0)
    small["g_post"] = jnp.concatenate(dgpost, axis=0)
    return dh, big, small, rode


ANY = pl.BlockSpec(memory_space=pl.ANY)


def _place():
    return lax.axis_index("x"), lax.axis_index("y"), lax.axis_index("c")


def _flip(v, bit):
    return 1 - v if bit else v


def _dma_sems(m):
    return pltpu.SemaphoreType.DMA((m,))


class _SmallExchange:
    name = "small_allgather"

    def __init__(self, buf):
        self.inputs = [buf]
        self.out_shapes = [jax.ShapeDtypeStruct((N_DEV,) + buf.shape, buf.dtype)]
        self.sems = [_dma_sems(N_DEV - 1), _dma_sems(N_DEV - 1)]

    def _copies(self, ins, outs, sems):
        x, y, c = _place()
        me = 4 * x + 2 * y + c
        out, arrive = [], []
        for k in range(1, N_DEV):
            peer = (_flip(x, k & 4), _flip(y, k & 2), _flip(c, k & 1))
            slot = 4 * peer[0] + 2 * peer[1] + peer[2]
            how = dict(send_sem=sems[0].at[k - 1], recv_sem=sems[1].at[k - 1], device_id=peer, device_id_type=MESH)
            out.append(pltpu.make_async_remote_copy(src_ref=ins[0], dst_ref=outs[0].at[me], **how))
            arrive.append(pltpu.make_async_remote_copy(src_ref=ins[0], dst_ref=outs[0].at[slot], **how))
        return out, arrive

    def start(self, ins, outs, sems):
        for cp in self._copies(ins, outs, sems)[0]:
            cp.start()

    def mid(self, ins, outs, sems):
        pass

    def finish(self, ins, outs, sems):
        out, arrive = self._copies(ins, outs, sems)
        for cp in arrive:
            cp.wait_recv()
        for cp in out:
            cp.wait_send()


def _small_allgather(buf, me):
    gathered = _run_exchange(_SmallExchange(buf))[0]
    slots = lax.broadcasted_iota(jnp.int32, (N_DEV, 1, 1), 0)
    return jnp.where(slots == me, buf[None], gathered)


class _GatherExchange:
    name = "gather_weights"
    aliased = True

    def __init__(self, bufs):
        n = len(bufs)
        self.n, self.inputs = n, list(bufs)
        self.halves = [b.shape[1] // 2 for b in bufs]
        self.out_shapes = [jax.ShapeDtypeStruct(b.shape, b.dtype) for b in bufs]
        self.sems = [_dma_sems(3 * n), _dma_sems(3 * n), _dma_sems(3 * n), _dma_sems(3 * n)]

    def _copies(self, outs, sems):
        ici_send, ici_recv, d2d_send, d2d_recv = sems
        x, y, c = _place()
        me = 2 * x + y
        sib = (x, y, 1 - c)
        chips = [(_flip(x, k & 2), _flip(y, k & 1)) for k in (1, 2, 3)]
        half = lambda t, slot, hc: outs[t].at[slot, pl.ds(hc * self.halves[t], self.halves[t])]
        ici_out, ici_in, fwd_out, fwd_in = [], [], [], []
        for t in range(self.n):
            for k, chip in enumerate(chips):
                idx = 2 * chip[0] + chip[1]
                peer = (chip[0], chip[1], c)
                ici = dict(send_sem=ici_send.at[3 * t + k], recv_sem=ici_recv.at[3 * t + k], device_id=peer,
                           device_id_type=MESH)
                d2d = dict(send_sem=d2d_send.at[3 * t + k], recv_sem=d2d_recv.at[3 * t + k], device_id=sib,
                           device_id_type=MESH)
                ici_out.append(pltpu.make_async_remote_copy(src_ref=half(t, me, c), dst_ref=half(t, me, c), **ici))
                ici_in.append(pltpu.make_async_remote_copy(src_ref=half(t, idx, c), dst_ref=half(t, idx, c), **ici))
                fwd_out.append(pltpu.make_async_remote_copy(src_ref=half(t, idx, c), dst_ref=half(t, idx, c), **d2d))
                fwd_in.append(pltpu.make_async_remote_copy(
                    src_ref=half(t, idx, 1 - c), dst_ref=half(t, idx, 1 - c), **d2d))
        return ici_out, ici_in, fwd_out, fwd_in

    def start(self, ins, outs, sems):
        for cp in self._copies(outs, sems)[0]:
            cp.start()

    def mid(self, ins, outs, sems):
        _, ici_in, fwd_out, _ = self._copies(outs, sems)
        for arrived, onward in zip(ici_in, fwd_out):
            arrived.wait_recv()
            onward.start()

    def finish(self, ins, outs, sems):
        ici_out, _, fwd_out, fwd_in = self._copies(outs, sems)
        for cp in fwd_in:
            cp.wait_recv()
        for cp in ici_out + fwd_out:
            cp.wait_send()


class _SiblingExchange:
    name = "rs_to_sibling"

    def __init__(self, gs):
        n = len(gs)
        self.n, self.inputs = n, list(gs)
        self.halves = [g.shape[1] // 2 for g in gs]
        self.out_shapes = [jax.ShapeDtypeStruct((N_CHIPS, g.shape[1] // 2) + g.shape[2:], g.dtype) for g in gs]
        self.sems = [_dma_sems(n), _dma_sems(n)]

    def _copies(self, ins, outs, sems):
        x, y, c = _place()
        return [pltpu.make_async_remote_copy(
            src_ref=ins[t].at[:, pl.ds((1 - c) * self.halves[t], self.halves[t])], dst_ref=outs[t],
            send_sem=sems[0].at[t], recv_sem=sems[1].at[t], device_id=(x, y, 1 - c), device_id_type=MESH)
            for t in range(self.n)]

    def start(self, ins, outs, sems):
        for cp in self._copies(ins, outs, sems):
            cp.start()

    def mid(self, ins, outs, sems):
        pass

    def finish(self, ins, outs, sems):
        for cp in self._copies(ins, outs, sems):
            cp.wait()


class _ChipsExchange:
    name = "rs_to_chips"

    def __init__(self, ss):
        n = len(ss)
        self.n, self.inputs = n, list(ss)
        self.out_shapes = [jax.ShapeDtypeStruct((3,) + s.shape[1:], s.dtype) for s in ss]
        self.sems = [_dma_sems(3 * n), _dma_sems(3 * n)]

    def _copies(self, ins, outs, sems):
        x, y, c = _place()
        chips = [(_flip(x, k & 2), _flip(y, k & 1)) for k in (1, 2, 3)]
        return [pltpu.make_async_remote_copy(
            src_ref=ins[t].at[2 * chip[0] + chip[1]], dst_ref=outs[t].at[k], send_sem=sems[0].at[3 * t + k],
            recv_sem=sems[1].at[3 * t + k], device_id=(chip[0], chip[1], c), device_id_type=MESH)
            for t in range(self.n) for k, chip in enumerate(chips)]

    def start(self, ins, outs, sems):
        for cp in self._copies(ins, outs, sems):
            cp.start()

    def mid(self, ins, outs, sems):
        pass

    def finish(self, ins, outs, sems):
        for cp in self._copies(ins, outs, sems):
            cp.wait()


class _Both:
    def __init__(self, first, second):
        self.parts = (first, second)
        self.name = first.name + "_" + second.name
        self.inputs = first.inputs + second.inputs
        self.out_shapes = first.out_shapes + second.out_shapes
        self.sems = first.sems + second.sems

    def _each(self, ins, outs, sems):
        a = self.parts[0]
        ni, no, ns = len(a.inputs), len(a.out_shapes), len(a.sems)
        return ((a, ins[:ni], outs[:no], sems[:ns]), (self.parts[1], ins[ni:], outs[no:], sems[ns:]))

    def start(self, ins, outs, sems):
        for ex, i, o, s in self._each(ins, outs, sems):
            ex.start(i, o, s)

    def mid(self, ins, outs, sems):
        for ex, i, o, s in self._each(ins, outs, sems):
            ex.mid(i, o, s)

    def finish(self, ins, outs, sems):
        for ex, i, o, s in self._each(ins, outs, sems):
            ex.finish(i, o, s)

    def split(self, outs):
        no = len(self.parts[0].out_shapes)
        return outs[:no], outs[no:]


def _run_exchange(ex):
    ni, no = len(ex.inputs), len(ex.out_shapes)

    def body(*refs):
        ins, outs, sems = refs[:ni], refs[ni:ni + no], refs[ni + no:]
        ex.start(ins, outs, sems)
        ex.mid(ins, outs, sems)
        ex.finish(ins, outs, sems)

    return pl.pallas_call(
        body, name=ex.name, in_specs=[ANY] * ni, out_specs=[ANY] * no, out_shape=ex.out_shapes,
        scratch_shapes=ex.sems, input_output_aliases={t: t for t in range(ni)} if _in_place(ex) else {},
    )(*ex.inputs)


def _in_place(ex):
    return getattr(ex, "aliased", False)


def _place_own(shard):
    L_, r, n_ = shard.shape
    tr = _row_tile(r, 512 if r % 512 == 0 else 256)

    def body(x_ref, *refs):
        outs, stage, sems = refs[:L_], refs[L_], refs[L_ + 1]
        x, y, _ = _place()
        rows = pl.ds(pl.multiple_of(pl.program_id(0) * tr, tr), tr)
        copies = []
        for l in range(L_):
            stage[l] = x_ref[l].astype(_MM)
            cp = pltpu.make_async_copy(stage.at[l], outs[l].at[2 * x + y, rows], sems.at[l])
            cp.start()
            copies.append(cp)
        for cp in copies:
            cp.wait()

    return pl.pallas_call(
        body, name="place_own", grid=(r // tr,),
        in_specs=[pl.BlockSpec((L_, tr, n_), lambda i: (0, i, 0))], out_specs=[ANY] * L_,
        out_shape=[jax.ShapeDtypeStruct((N_CHIPS, r, n_), _MM)] * L_,
        scratch_shapes=[pltpu.VMEM((L_, tr, n_), _MM), _dma_sems(L_)],
        compiler_params=_cparams(("arbitrary",)),
    )(shard)


def _call(body, name, nsteps, in_specs, out_specs, out_shape, args, ride=None):
    if ride is None:
        return pl.pallas_call(body, name=name, grid=(nsteps,), in_specs=in_specs, out_specs=out_specs,
                              out_shape=out_shape, compiler_params=_cparams(("arbitrary",)))(*args)
    ni, no = len(args), len(out_shape)
    ri, ro = len(ride.inputs), len(ride.out_shapes)
    mid_step = (13 * nsteps) // 16 if nsteps > 1 else 0

    def hosted(*refs):
        a, ra = refs[:ni], refs[ni:ni + ri]
        o, ro_refs = refs[ni + ri:ni + ri + no], refs[ni + ri + no:ni + ri + no + ro]
        sems = refs[ni + ri + no + ro:]
        step = pl.program_id(0)

        @pl.when(step == 0)
        def _():
            ride.start(ra, ro_refs, sems)

        body(*a, *o)

        @pl.when(step == mid_step)
        def _():
            ride.mid(ra, ro_refs, sems)

        @pl.when(step == nsteps - 1)
        def _():
            ride.finish(ra, ro_refs, sems)

    outs = pl.pallas_call(
        hosted, name=name + "_" + ride.name, grid=(nsteps,), in_specs=list(in_specs) + [ANY] * ri,
        out_specs=list(out_specs) + [ANY] * ro, out_shape=list(out_shape) + ride.out_shapes,
        scratch_shapes=ride.sems, compiler_params=_cparams(("arbitrary",)),
        input_output_aliases={ni + t: no + t for t in range(ri)} if _in_place(ride) else {},
    )(*args, *ride.inputs)
    return outs[:no], outs[no:]


def _share_halves(ts):
    n = len(ts)

    def body(*refs):
        outs = refs[n:2 * n]
        send_sems, recv_sems = refs[2 * n:]
        x, y, c = _place()
        cps = []
        for t in range(n):
            cp = pltpu.make_async_remote_copy(
                src_ref=outs[t].at[c], dst_ref=outs[t].at[c], send_sem=send_sems.at[t], recv_sem=recv_sems.at[t],
                device_id=(x, y, 1 - c), device_id_type=MESH)
            cp.start()
            cps.append(cp)
        for t in range(n):
            pltpu.make_async_remote_copy(
                src_ref=outs[t].at[1 - c], dst_ref=outs[t].at[1 - c], send_sem=send_sems.at[t],
                recv_sem=recv_sems.at[t], device_id=(x, y, 1 - c), device_id_type=MESH).wait_recv()
        for cp in cps:
            cp.wait_send()

    dma = lambda m: pltpu.SemaphoreType.DMA((m,))
    return pl.pallas_call(
        body, name="share_halves", in_specs=[ANY] * n, out_specs=[ANY] * n,
        out_shape=[jax.ShapeDtypeStruct(t.shape, t.dtype) for t in ts],
        input_output_aliases={t: t for t in range(n)},
        scratch_shapes=[dma(n), dma(n)],
    )(*ts)


def _by_shape(fn, *lists):
    groups = {}
    for idx, a in enumerate(lists[0]):
        groups.setdefault(a.shape, []).append(idx)
    out = [None] * len(lists[0])
    for idxs in groups.values():
        for idx, res in zip(idxs, fn(*[[lst[idx] for idx in idxs] for lst in lists])):
            out[idx] = res
    return out


def _add_sibling(gs, rs, c):
    m = len(gs)
    _, r_, n_ = gs[0].shape
    h = r_ // 2
    tr = _row_tile(h, 512)

    def body(c_ref, *refs):
        for g_ref, r_ref, o_ref in zip(refs[:m], refs[m:2 * m], refs[2 * m:]):
            o_ref[...] = (g_ref[...] + r_ref[...]).astype(o_ref.dtype)

    gspec = pl.BlockSpec((None, None, tr, n_), lambda s, i, c_ref: (s, c_ref[0], i, 0))
    hspec = pl.BlockSpec((None, tr, n_), lambda s, i, c_ref: (s, i, 0))
    return pl.pallas_call(
        body, name="add_sibling",
        grid_spec=pltpu.PrefetchScalarGridSpec(
            num_scalar_prefetch=1, grid=(N_CHIPS, h // tr), in_specs=[gspec] * m + [hspec] * m, out_specs=[hspec] * m),
        out_shape=[jax.ShapeDtypeStruct((N_CHIPS, h, n_), _MM)] * m,
        compiler_params=_cparams(("parallel", "parallel")),
    )(c.reshape(1), *[g.reshape(N_CHIPS, 2, h, n_) for g in gs], *rs)


def _add_chips(ss, rs, me, c):
    m = len(ss)
    _, h, n_ = ss[0].shape
    tr = _row_tile(h, 512)

    def body(at_ref, *refs):
        f = lambda ref: ref[...].astype(F32)
        for t in range(m):
            s_ref, (r0, r1, r2), o_ref = refs[t], refs[m + 3 * t:m + 3 * t + 3], refs[4 * m + t]
            o_ref[...] = ((f(s_ref) + f(r0)) + f(r1)) + f(r2)

    rk = lambda k: pl.BlockSpec((None, tr, n_), lambda i, at_ref: (k, i, 0))
    return pl.pallas_call(
        body, name="add_chips",
        grid_spec=pltpu.PrefetchScalarGridSpec(
            num_scalar_prefetch=1, grid=(h // tr,),
            in_specs=[pl.BlockSpec((None, tr, n_), lambda i, at_ref: (at_ref[0], i, 0))] * m + [rk(0), rk(1), rk(2)] * m,
            out_specs=[pl.BlockSpec((None, tr, n_), lambda i, at_ref: (at_ref[1], i, 0))] * m),
        out_shape=[jax.ShapeDtypeStruct((2, h, n_), F32)] * m,
        compiler_params=_cparams(("parallel",)),
    )(jnp.stack([me, c]), *ss, *[r for r in rs for _ in range(3)])


def _sum_devices(a):
    R_ = a.shape[1]

    def body(a_ref, o_ref):
        acc = a_ref[0]
        for k in range(1, N_DEV):
            acc = acc + a_ref[k]
        o_ref[...] = acc

    return pl.pallas_call(
        body, name="sum_devices", out_shape=jax.ShapeDtypeStruct((R_, 128), F32),
        compiler_params=pltpu.CompilerParams(vmem_limit_bytes=VMEM_LIMIT),
    )(a)


def _adamw(w, g, m, v):
    shape = w.shape
    to2 = lambda a: a.reshape(-1, shape[-1])
    w2, g2, m2, v2 = to2(w), to2(g), to2(m), to2(v)
    R_, C = w2.shape
    tr = 256 if R_ % 256 == 0 else R_

    def body(w_ref, g_ref, m_ref, v_ref, d_ref, mo_ref, vo_ref):
        gv = g_ref[...]
        mn = ADAM_B1 * m_ref[...] + (1.0 - ADAM_B1) * gv
        vn = ADAM_B2 * v_ref[...] + (1.0 - ADAM_B2) * (gv * gv)
        m_hat = mn / (1.0 - ADAM_B1 ** ADAM_STEP)
        v_hat = vn / (1.0 - ADAM_B2 ** ADAM_STEP)
        d_ref[...] = -ADAM_LR * (m_hat / (jnp.sqrt(v_hat) + ADAM_EPS) + ADAM_WD * w_ref[...])
        mo_ref[...] = mn
        vo_ref[...] = vn

    spec = pl.BlockSpec((tr, C), lambda i: (i, 0))
    outs = pl.pallas_call(
        body, name="adamw", grid=(R_ // tr,), in_specs=[spec] * 4, out_specs=[spec] * 3,
        out_shape=[jax.ShapeDtypeStruct((R_, C), F32)] * 3,
        compiler_params=_cparams(("parallel",)),
    )(w2, g2, m2, v2)
    return [o.reshape(shape) for o in outs]


SHARD_AXIS = {"g_pre": 2, "g_post": 2, "g_mem": None, "ffn1_gate": 2, "ffn1_up": 2, "ffn1_down": 1,
              "ffn2_gate": 2, "ffn2_up": 2, "ffn2_down": 1, "w_in_pool": 1, "pool_w": None, "pool_scale": None,
              "w_in_sb": 2, "w_mem_kv": 1, "w_out": 2}
WEIGHTS = list(SHARD_AXIS)
BIG = [k for k in WEIGHTS if SHARD_AXIS[k] is not None and k not in ("g_pre", "g_post")]
SMALL = [k for k in WEIGHTS if k not in BIG]


def _pack128(arrs):
    flat = jnp.concatenate([a.reshape(-1) for a in arrs])
    pad = (-flat.shape[0]) % (8 * 128)
    return jnp.pad(flat, (0, pad)).reshape(-1, 128)


def _unpack128(buf, shapes):
    flat = buf.reshape(-1)
    out, off = [], 0
    for shp in shapes:
        size = 1
        for d in shp:
            size *= d
        out.append(flat[off:off + size].reshape(shp))
        off += size
    return out


def kernel(x, mem, g_pre, g_post, g_mem, ffn1_gate, ffn1_up, ffn1_down, ffn2_gate, ffn2_up, ffn2_down, w_in_pool, pool_w, pool_scale, w_in_sb, w_mem_kv, w_out, loss_target, m_g_pre, m_g_post, m_g_mem, m_ffn1_gate, m_ffn1_up, m_ffn1_down, m_ffn2_gate, m_ffn2_up, m_ffn2_down, m_w_in_pool, m_pool_w, m_pool_scale, m_w_in_sb, m_w_mem_kv, m_w_out, v_g_pre, v_g_post, v_g_mem, v_ffn1_gate, v_ffn1_up, v_ffn1_down, v_ffn2_gate, v_ffn2_up, v_ffn2_down, v_w_in_pool, v_pool_w, v_pool_scale, v_w_in_sb, v_w_mem_kv, v_w_out):
    w = dict(g_pre=g_pre, g_post=g_post, g_mem=g_mem, ffn1_gate=ffn1_gate, ffn1_up=ffn1_up, ffn1_down=ffn1_down,
             ffn2_gate=ffn2_gate, ffn2_up=ffn2_up, ffn2_down=ffn2_down, w_in_pool=w_in_pool, pool_w=pool_w,
             pool_scale=pool_scale, w_in_sb=w_in_sb, w_mem_kv=w_mem_kv, w_out=w_out)
    m = dict(g_pre=m_g_pre, g_post=m_g_post, g_mem=m_g_mem, ffn1_gate=m_ffn1_gate, ffn1_up=m_ffn1_up,
             ffn1_down=m_ffn1_down, ffn2_gate=m_ffn2_gate, ffn2_up=m_ffn2_up, ffn2_down=m_ffn2_down,
             w_in_pool=m_w_in_pool, pool_w=m_pool_w, pool_scale=m_pool_scale, w_in_sb=m_w_in_sb,
             w_mem_kv=m_w_mem_kv, w_out=m_w_out)
    v = dict(g_pre=v_g_pre, g_post=v_g_post, g_mem=v_g_mem, ffn1_gate=v_ffn1_gate, ffn1_up=v_ffn1_up,
             ffn1_down=v_ffn1_down, ffn2_gate=v_ffn2_gate, ffn2_up=v_ffn2_up, ffn2_down=v_ffn2_down,
             w_in_pool=v_w_in_pool, pool_w=v_pool_w, pool_scale=v_pool_scale, w_in_sb=v_w_in_sb,
             w_mem_kv=v_w_mem_kv, w_out=v_w_out)
    cx, cy, cc = _place()
    chip = (2 * cx + cy).astype(jnp.int32)
    core = cc.astype(jnp.int32)

    dev = (4 * cx + 2 * cy + cc).astype(jnp.int32)

    gains = _small_allgather(_pack128([g_pre, g_post]), dev)
    per_chip = [_unpack128(gains[2 * j], [g_pre.shape, g_post.shape]) for j in range(N_CHIPS)]
    full_pre = jnp.concatenate([p[0] for p in per_chip], axis=2)
    full_post = jnp.concatenate([p[1] for p in per_chip], axis=2)

    def shard_name(k, i):
        return ("w_in_pool" if _is_pool(i) else "w_in_sb") if k == "w_in" else k

    placed = {k: _place_own(w[k]) for k in BIG}

    def own(i, names):
        return [placed[shard_name(k, i)][i // 2 if k == "w_in" else i] for k in names]

    xs, mems = x[0], mem[0]
    h, lws, saved = xs, [], []
    first = _run_exchange(_GatherExchange(own(0, FIRST_FFN)))
    for i in range(DEPTH):
        lw = dict(zip(FIRST_FFN, first))
        lw["g_pre"], lw["g_post"], lw["g_mem"] = full_pre[i], full_post[i], g_mem[i].reshape(1, D)
        if _is_pool(i):
            lw["pool_w"], lw["pool_scale"] = pool_w[i // 2], pool_scale[i // 2].reshape(1, DTOK)
        h, sv, first = _layer_forward(h, mems, lw, i, own(i, REST), own(i + 1, FIRST_FFN) if i + 1 < DEPTH else None)
        lws.append(lw)
        saved.append(sv)
    acc, dh = _loss_grad(h, loss_target[0])
    loss = lax.psum(0.5 / D * jnp.sum(acc), ("x", "y", "c"))

    mine, smalls = [None] * DEPTH, [None] * DEPTH
    above, sums = None, {}
    nt = len(LAYER_TENSORS)
    early = ("ffn2_gate", "w_mem_kv", "ffn2_up", "w_out", "ffn2_down", "w_in")
    add_sib = lambda gs, from_sib: _by_shape(lambda g, r: _add_sibling(g, r, core), gs, from_sib)
    add_chp = lambda ss, from_chips: _by_shape(lambda s, r: _add_chips(s, r, chip, core), ss, from_chips)

    for i in reversed(range(DEPTH)):
        rides = {}
        if above is not None:
            up = [above[k] for k in LAYER_TENSORS]
            rides["ffn2"] = lambda big, rode, up=up: _SiblingExchange(up)

            def on_ffn1(big, rode, up=up, layer=i + 1, bottom=(i == 0)):
                sums[layer] = add_sib(up, rode["ffn2"])
                ex = _ChipsExchange(sums[layer])
                return _Both(ex, _SiblingExchange([big[k] for k in early])) if bottom else ex

            def on_wgrad(big, rode, j):
                if "early" not in sums:
                    sums["early"] = add_sib([big[k] for k in early], rode["ffn1"][nt:])
                return _ChipsExchange(sums["early"][2 * j:2 * j + 2])

            rides["ffn1"] = on_ffn1
            if i == 0:
                for j, host in enumerate(FIRST_FFN):
                    rides[host] = lambda big, rode, j=j: on_wgrad(big, rode, j)
        dh, gs, smalls[i], rode = _layer_backward(dh, mems, lws[i], saved[i], i, rides)
        if above is not None:
            mine[i + 1] = add_chp(sums[i + 1], rode["ffn1"][:nt])
        above = gs
    early_mine = add_chp(sums["early"], [t for host in FIRST_FFN for t in rode[host]])
    late = [above[k] for k in FIRST_FFN]
    late_sums = add_sib(late, _run_exchange(_SiblingExchange(late)))
    late_mine = add_chp(late_sums, _run_exchange(_ChipsExchange(late_sums)))
    by_name = dict(zip(early + FIRST_FFN, early_mine + late_mine))
    mine[0] = [by_name[k] for k in LAYER_TENSORS]
    shared = _share_halves([t for layer in mine for t in layer])
    reduced = [dict(zip(LAYER_TENSORS, [s.reshape(-1, s.shape[-1]) for s in shared[nt * i:nt * (i + 1)]]))
               for i in range(DEPTH)]
    dx = dh
    red = {}
    for k in BIG:
        if k.startswith("w_in_"):
            layers = [i for i in range(DEPTH) if _is_pool(i) == (k == "w_in_pool")]
            red[k] = jnp.stack([reduced[i]["w_in"] for i in layers])
        else:
            red[k] = jnp.stack([reduced[i][k] for i in range(DEPTH)])

    grads = {"g_pre": jnp.stack([s["g_pre"] for s in smalls]), "g_post": jnp.stack([s["g_post"] for s in smalls]),
             "g_mem": jnp.concatenate([s["g_mem"] for s in smalls], axis=0),
             "pool_w": jnp.stack([smalls[i]["pool_w"] for i in range(DEPTH) if _is_pool(i)]),
             "pool_scale": jnp.concatenate([smalls[i]["pool_scale"] for i in range(DEPTH) if _is_pool(i)], axis=0)}
    small_shapes = [grads[k].shape for k in SMALL]
    summed = _unpack128(_sum_devices(_small_allgather(_pack128([grads[k] for k in SMALL]), dev)), small_shapes)
    for k, s in zip(SMALL, summed):
        if SHARD_AXIS[k] is None:
            red[k] = s
        else:
            red[k] = lax.dynamic_slice_in_dim(s, chip * w[k].shape[2], w[k].shape[2], axis=2)

    deltas, new_m, new_v = {}, {}, {}
    for k in WEIGHTS:
        deltas[k], new_m[k], new_v[k] = _adamw(w[k], red[k], m[k], v[k])
    return (loss, dx.reshape(x.shape), *[red[k] for k in WEIGHTS], *[deltas[k] for k in WEIGHTS],
            *[new_m[k] for k in WEIGHTS], *[new_v[k] for k in WEIGHTS])
```
